```python
import math
import jax, jax.numpy as jnp
from jax import lax
import numpy as np

D_MODEL = 1024
BATCH = 1
SEQ = 16384
DEPTH = 2

GRID_W = 64
CTX_LEN = 256
NA_HEADS = 8
NA_HEAD_DIM = 64
NA_WIN_H = 8
NA_WIN_W = 16
DN_HEADS = 4
DN_HEAD_DIM = 128
DN_CONV = 5
DN_CHUNK = 64
FN_GROUPS = 4
FN_GROUP_DIM = 128
N_BRANCH = 3
MLP_HIDDEN = 4 * D_MODEL
ROPE_BASE = 10000.0
EPS = 1e-6

NA_W = NA_HEADS * NA_HEAD_DIM
DN_W = DN_HEADS * DN_HEAD_DIM
FN_W = FN_GROUPS * FN_GROUP_DIM
CTX_SIDE_W = 2 * NA_W + 2 * DN_W + 4 * DN_HEADS
IN_W = CTX_SIDE_W + NA_W + 2 * DN_W + FN_W + N_BRANCH * D_MODEL
SIDE_SIZES = (NA_W, NA_W, DN_W, DN_W, 2 * DN_HEADS, 2 * DN_HEADS)
REST_SIZES = (NA_W, DN_W, DN_W, FN_W, N_BRANCH * D_MODEL)

kernel_name = 'hybrid_na_deltanet_fnet_diffusion_block'


def _split(p, sizes):
    return jnp.split(p, np.cumsum(sizes)[:-1].tolist(), axis=-1)


def _rmsnorm(x, g):
    xf = x.astype(jnp.float32)
    y = xf * lax.rsqrt(jnp.mean(xf * xf, axis=-1, keepdims=True) + EPS)
    return (y * g.astype(jnp.float32)).astype(x.dtype)


def _l2norm(x):
    xf = x.astype(jnp.float32)
    return xf * lax.rsqrt(jnp.sum(xf * xf, axis=-1, keepdims=True) + EPS)


def _heads(u, d):
    return u.reshape(u.shape[:2] + (-1, d))


def _bhtd(a):
    return jnp.transpose(a, (0, 2, 1, 3))


def _short_conv(u, w):
    k, ch = w.shape
    y = lax.conv_general_dilated(u, w[:, None, :].astype(u.dtype), window_strides=(1,),
                                 padding=[(k // 2, k // 2)], dimension_numbers=('NWC', 'WIO', 'NWC'),
                                 feature_group_count=ch)
    return jax.nn.silu(y)


def _axial_rope(x):
    t_len, d = x.shape[1], x.shape[-1]
    half = d // 2
    t = jnp.arange(t_len, dtype=jnp.int32)
    pos = jnp.stack([t // GRID_W, t % GRID_W], axis=-1).astype(jnp.float32)
    inv = 1.0 / (ROPE_BASE ** (jnp.arange(0, half, 2, dtype=jnp.float32) / half))
    ang = pos[:, :, None] * inv[None, None, :]
    ang = jnp.concatenate([ang, ang], axis=-1)
    cos, sin = jnp.cos(ang)[None, :, None], jnp.sin(ang)[None, :, None]
    xa = x.reshape(x.shape[:-1] + (2, half))
    rot = jnp.concatenate([-xa[..., half // 2:], xa[..., :half // 2]], axis=-1)
    return (xa * cos + rot * sin).reshape(x.shape)


def _neighbourhood_attention(q, k, v, kc, vc, rpb):
    b, t_len, h, dh = q.shape
    rows = t_len // GRID_W
    kh, kw = min(NA_WIN_H, rows), min(NA_WIN_W, GRID_W)
    qg, kg, vg = (a.reshape(b, rows, GRID_W, h, dh) for a in (q, k, v))
    col = np.arange(GRID_W)
    col_idx = np.clip(col - kw // 2, 0, GRID_W - kw)[:, None] + np.arange(kw)[None, :]
    col_off = col_idx - col[:, None] + NA_WIN_W - 1
    rpb_c = rpb[:, :, col_off]

    def row_block(r):
        rs = jnp.clip(r - kh // 2, 0, rows - kh)
        q_r = lax.dynamic_index_in_dim(qg, r, axis=1, keepdims=False)
        k_win = lax.dynamic_slice_in_dim(kg, rs, kh, axis=1)[:, :, col_idx]
        v_win = lax.dynamic_slice_in_dim(vg, rs, kh, axis=1)[:, :, col_idx]
        bias = jnp.take(rpb_c, rs + jnp.arange(kh) - r + NA_WIN_H - 1, axis=1)
        s_loc = (jnp.einsum('bwhd,biwjhd->bhwij', q_r, k_win).astype(jnp.float32)
                 + jnp.transpose(bias, (0, 2, 1, 3)).astype(jnp.float32)[None])
        s_ctx = jnp.einsum('bwhd,bshd->bhws', q_r, kc).astype(jnp.float32)
        s = jnp.concatenate([s_loc.reshape(b, h, GRID_W, kh * kw), s_ctx], axis=-1)
        p = jax.nn.softmax(s, axis=-1).astype(v.dtype)
        p_loc = p[..., :kh * kw].reshape(b, h, GRID_W, kh, kw)
        return (jnp.einsum('bhwij,biwjhd->bwhd', p_loc, v_win)
                + jnp.einsum('bhws,bshd->bwhd', p[..., kh * kw:], vc))

    o = lax.map(row_block, jnp.arange(rows))
    return jnp.moveaxis(o, 0, 1).reshape(b, t_len, h * dh)


def _dense_attn(q, k, v):
    s = jnp.einsum('bqhd,bkhd->bhqk', q, k).astype(jnp.float32)
    p = jax.nn.softmax(s, axis=-1).astype(v.dtype)
    o = jnp.einsum('bhqk,bkhd->bqhd', p, v)
    return o.reshape(o.shape[:2] + (-1,))


def _gated_delta_chunked(q, k, v, g, beta, s0):
    b, h, t_len, dk = k.shape
    dv = v.shape[-1]
    c = DN_CHUNK
    n = t_len // c
    ch = lambda a: a.reshape((b, h, n, c) + a.shape[3:])
    k, v, g, beta = ch(k), ch(v), ch(g), ch(beta)
    g = jnp.cumsum(g, axis=-1)
    lower = jnp.tril(jnp.ones((c, c), bool))
    strict = jnp.tril(jnp.ones((c, c), bool), -1)
    decay = jnp.where(lower, jnp.exp(jnp.where(lower, g[..., :, None] - g[..., None, :], 0.0)), 0.0)
    kb = k * beta[..., None]
    a_mat = jnp.where(strict, jnp.einsum('bhncd,bhnsd->bhncs', kb, k) * decay, 0.0) + jnp.eye(c, dtype=jnp.float32)
    rhs = jnp.concatenate([v * beta[..., None], kb * jnp.exp(g)[..., None]], axis=-1)
    sol = lax.linalg.triangular_solve(a_mat, rhs, left_side=True, lower=True)
    u, w = sol[..., :dv], sol[..., dv:]
    g_last = g[..., -1]
    k_tail = k * jnp.exp(g_last[..., None] - g)[..., None]
    xs = [u, w, k_tail, jnp.exp(g_last)]
    if q is not None:
        qs = ch(q) * (dk ** -0.5)
        xs += [qs * jnp.exp(g)[..., None], jnp.einsum('bhncd,bhnsd->bhncs', qs, k) * decay]
    xs = tuple(jnp.moveaxis(a, 2, 0) for a in xs)

    def step(s, inp):
        u_c, w_c, kt_c, dl_c = inp[:4]
        v_new = u_c - jnp.einsum('bhcd,bhde->bhce', w_c, s)
        s_next = s * dl_c[..., None, None] + jnp.einsum('bhcd,bhce->bhde', kt_c, v_new)
        if q is None:
            return s_next, None
        qg_c, qk_c = inp[4:]
        o = jnp.einsum('bhcd,bhde->bhce', qg_c, s) + jnp.einsum('bhcs,bhse->bhce', qk_c, v_new)
        return s_next, o

    s_fin, o = lax.scan(step, s0, xs)
    if q is None:
        return s_fin, None
    return s_fin, jnp.moveaxis(o, 0, 2).reshape(b, h, t_len, dv)


def _delta_direction(ctx, lat, reverse):
    flip = (lambda a: a if a is None else jnp.flip(a, axis=2)) if reverse else (lambda a: a)
    s0 = jnp.zeros(ctx[1].shape[:2] + (ctx[1].shape[-1], ctx[2].shape[-1]), jnp.float32)
    s_ctx, o_ctx = _gated_delta_chunked(*map(flip, ctx), s0)
    _, o_lat = _gated_delta_chunked(*map(flip, lat), s_ctx)
    return flip(o_ctx), flip(o_lat)


def _decay_gates(a, bt, a_log, dt_bias):
    b, t_len, _ = a.shape
    a = a.astype(jnp.float32).reshape(b, t_len, 2, DN_HEADS)
    bt = bt.astype(jnp.float32).reshape(b, t_len, 2, DN_HEADS)
    g = -jnp.exp(a_log.astype(jnp.float32)) * jax.nn.softplus(a + dt_bias.astype(jnp.float32))
    beta = jax.nn.sigmoid(bt)
    return jnp.transpose(g, (2, 0, 3, 1)), jnp.transpose(beta, (2, 0, 3, 1))


def _gated_head_norm(o, z, w):
    o = jnp.transpose(o, (0, 2, 1, 3))
    o = o * lax.rsqrt(jnp.mean(o * o, axis=-1, keepdims=True) + EPS) * w.astype(jnp.float32)
    y = o * jax.nn.silu(_heads(z, DN_HEAD_DIM).astype(jnp.float32))
    return y.reshape(y.shape[:2] + (-1,)).astype(z.dtype)


def _fourier(u):
    b, t_len, _ = u.shape
    ug = u.astype(jnp.float32).reshape(b, t_len, FN_GROUPS, FN_GROUP_DIM)
    f = jnp.fft.fft2(ug, axes=(1, 3), norm='ortho').real
    return f.reshape(b, t_len, FN_W).astype(u.dtype)


def _merge(o_na, o_dn, o_fn, gate_logits, w_na_o, w_dn_o, w_fn, w_out):
    g = jax.nn.sigmoid(gate_logits.astype(jnp.float32)).astype(o_na.dtype)
    g_na, g_dn, g_fn = jnp.split(g, N_BRANCH, axis=-1)
    y = g_na * (o_na @ w_na_o) + g_dn * (o_dn @ w_dn_o) + g_fn * (o_fn @ w_fn)
    return y @ w_out


def _token_mixer(h_l, h_c, w_in, conv_w, a_log, dt_bias, dn_norm, rpb, w_na_o, w_dn_o, w_fn, w_out, ctx_out):
    (na_k, na_v, dn_k, dn_v, dn_a, dn_b, na_q, dn_q, dn_z, fn_u, gate) = _split(h_l @ w_in, SIDE_SIZES + REST_SIZES)
    cp = _split(h_c @ (w_in if ctx_out else w_in[:, :CTX_SIDE_W]), SIDE_SIZES + (REST_SIZES if ctx_out else ()))
    na_kc, na_vc, dn_kc, dn_vc, dn_ac, dn_bc = cp[:6]
    scale = NA_HEAD_DIM ** -0.5

    kc_na, vc_na = _heads(na_kc, NA_HEAD_DIM), _heads(na_vc, NA_HEAD_DIM)
    o_na = _neighbourhood_attention(_heads(na_q, NA_HEAD_DIM) * scale, _heads(na_k, NA_HEAD_DIM),
                                    _heads(na_v, NA_HEAD_DIM), kc_na, vc_na, rpb)

    q_l, k_l, v_l = _split(_short_conv(jnp.concatenate([dn_q, dn_k, dn_v], axis=-1), conv_w), (DN_W,) * 3)
    lat_qkv = (_bhtd(_axial_rope(_l2norm(_heads(q_l, DN_HEAD_DIM)))),
               _bhtd(_axial_rope(_l2norm(_heads(k_l, DN_HEAD_DIM)))),
               _bhtd(_heads(v_l, DN_HEAD_DIM).astype(jnp.float32)))
    if ctx_out:
        q_c, k_c, v_c = _split(_short_conv(jnp.concatenate([cp[7], dn_kc, dn_vc], axis=-1), conv_w), (DN_W,) * 3)
        q_c = _bhtd(_l2norm(_heads(q_c, DN_HEAD_DIM)))
    else:
        k_c, v_c = _split(_short_conv(jnp.concatenate([dn_kc, dn_vc], axis=-1), conv_w[:, DN_W:]), (DN_W,) * 2)
        q_c = None
    ctx_qkv = (q_c, _bhtd(_l2norm(_heads(k_c, DN_HEAD_DIM))), _bhtd(_heads(v_c, DN_HEAD_DIM).astype(jnp.float32)))
    g_l, be_l = _decay_gates(dn_a, dn_b, a_log, dt_bias)
    g_c, be_c = _decay_gates(dn_ac, dn_bc, a_log, dt_bias)
    oc_f, ol_f = _delta_direction(ctx_qkv + (g_c[0], be_c[0]), lat_qkv + (g_l[0], be_l[0]), reverse=False)
    oc_b, ol_b = _delta_direction(ctx_qkv + (g_c[1], be_c[1]), lat_qkv + (g_l[1], be_l[1]), reverse=True)
    o_dn = _gated_head_norm(ol_f + ol_b, dn_z, dn_norm)

    y_l = _merge(o_na, o_dn, _fourier(fn_u), gate, w_na_o, w_dn_o, w_fn, w_out)
    if not ctx_out:
        return y_l, None
    o_na_c = _dense_attn(_heads(cp[6], NA_HEAD_DIM) * scale, kc_na, vc_na)
    o_dn_c = _gated_head_norm(oc_f + oc_b, cp[8], dn_norm)
    y_c = _merge(o_na_c, o_dn_c, _fourier(cp[9]), cp[10], w_na_o, w_dn_o, w_fn, w_out)
    return y_l, y_c


def _sq_relu_mlp(h, w1, w2):
    return jnp.square(jax.nn.relu(h @ w1)) @ w2


def _layer(x, xc, cs, ccs, w_ada, b_ada, norm1, w_in, conv_w, a_log, dt_bias, dn_norm, rpb,
           w_na_o, w_dn_o, w_fn, w_out, norm2, w_mlp1, w_mlp2, ctx_out):
    d = D_MODEL
    sh1, sc1, gt1, sh2, sc2, gt2 = jnp.split((cs @ w_ada + b_ada)[:, None, :], 6, axis=-1)
    n_mod = 6 if ctx_out else 2
    cmod = jnp.split((ccs @ w_ada[:, :n_mod * d] + b_ada[:n_mod * d])[None, None, :], n_mod, axis=-1)
    h_l = _rmsnorm(x, norm1) * (1 + sc1) + sh1
    h_c = _rmsnorm(xc, norm1) * (1 + cmod[1]) + cmod[0]
    y_l, y_c = _token_mixer(h_l, h_c, w_in, conv_w, a_log, dt_bias, dn_norm, rpb,
                            w_na_o, w_dn_o, w_fn, w_out, ctx_out)
    x = x + gt1 * y_l
    x = x + gt2 * _sq_relu_mlp(_rmsnorm(x, norm2) * (1 + sc2) + sh2, w_mlp1, w_mlp2)
    if ctx_out:
        xc = xc + cmod[2] * y_c
        xc = xc + cmod[5] * _sq_relu_mlp(_rmsnorm(xc, norm2) * (1 + cmod[4]) + cmod[3], w_mlp1, w_mlp2)
    return x, xc


def setup_inputs(seed: int = 0) -> dict:
    key = jax.random.key(seed)
    ks = jax.random.split(key, 24)
    f32 = jnp.float32
    d = D_MODEL
    nrm = lambda k, shape, s: jax.random.normal(k, shape, f32) * s
    a_init = jax.random.uniform(ks[0], (DEPTH, 2, DN_HEADS), f32, 1.0, 16.0)
    dt = jnp.exp(jax.random.uniform(ks[1], (DEPTH, 2, DN_HEADS), f32, math.log(1e-3), math.log(1e-1)))
    return {
        'x': nrm(ks[2], (BATCH, SEQ, d), 1.0),
        'c': nrm(ks[3], (BATCH, d), 1.0),
        'ctx': nrm(ks[4], (BATCH, CTX_LEN, d), 1.0),
        'c_ctx': nrm(ks[5], (d,), 1.0),
        'w_ada': nrm(ks[6], (DEPTH, d, 6 * d), 0.5 * d ** -0.5),
        'b_ada': nrm(ks[7], (DEPTH, 6 * d), 0.02),
        'norm1': 1.0 + nrm(ks[8], (DEPTH, d), 0.02),
        'w_in': nrm(ks[9], (DEPTH, d, IN_W), d ** -0.5),
        'conv_w': nrm(ks[10], (DEPTH, DN_CONV, 3 * DN_W), DN_CONV ** -0.5),
        'a_log': jnp.log(a_init),
        'dt_bias': dt + jnp.log(-jnp.expm1(-dt)),
        'dn_norm': 1.0 + nrm(ks[11], (DEPTH, DN_HEAD_DIM), 0.02),
        'rpb': nrm(ks[12], (DEPTH, NA_HEADS, 2 * NA_WIN_H - 1, 2 * NA_WIN_W - 1), 0.1),
        'w_na_o': nrm(ks[13], (DEPTH, NA_W, d), NA_W ** -0.5),
        'w_dn_o': nrm(ks[14], (DEPTH, DN_W, d), DN_W ** -0.5),
        'w_fn': nrm(ks[15], (DEPTH, FN_W, d), FN_W ** -0.5),
        'w_out': nrm(ks[16], (DEPTH, d, d), d ** -0.5),
        'norm2': 1.0 + nrm(ks[17], (DEPTH, d), 0.02),
        'w_mlp1': nrm(ks[18], (DEPTH, d, MLP_HIDDEN), d ** -0.5),
        'w_mlp2': nrm(ks[19], (DEPTH, MLP_HIDDEN, d), MLP_HIDDEN ** -0.5),
        'norm_f': 1.0 + nrm(ks[20], (d,), 0.02),
    }


def reference(x, c, ctx, c_ctx, w_ada, b_ada, norm1, w_in, conv_w, a_log, dt_bias, dn_norm, rpb,
              w_na_o, w_dn_o, w_fn, w_out, norm2, w_mlp1, w_mlp2, norm_f):
    cs = jax.nn.silu(c)
    ccs = jax.nn.silu(c_ctx)
    xc = ctx
    for l in range(DEPTH):
        x, xc = _layer(x, xc, cs, ccs, w_ada[l], b_ada[l], norm1[l], w_in[l], conv_w[l], a_log[l], dt_bias[l],
                       dn_norm[l], rpb[l], w_na_o[l], w_dn_o[l], w_fn[l], w_out[l], norm2[l], w_mlp1[l], w_mlp2[l],
                       ctx_out=(l < DEPTH - 1))
    return _rmsnorm(x, norm_f)
```

```python
import functools
import math

import numpy as np
import jax
import jax.numpy as jnp
from jax import lax
from jax.experimental import pallas as pl
from jax.experimental.pallas import tpu as pltpu

F32 = jnp.float32
BF16 = jnp.bfloat16

D = 1024
T_LAT = 16384
T_CTX = 256
T_ALL = T_LAT + T_CTX
DEPTH = 2
GRID_W = 64
GRID_H = T_LAT // GRID_W
NA_HEADS = 8
NA_DH = 64
NA_WIN_H = 8
NA_WIN_W = 16
NA_W = NA_HEADS * NA_DH
DN_HEADS = 4
DN_DH = 128
DN_W = DN_HEADS * DN_DH
DN_CONV = 5
CHUNK = 64
FN_GROUPS = 4
FN_DG = 128
FN_W = FN_GROUPS * FN_DG
HID = 4 * D
ROPE_BASE = 10000.0
EPS = 1e-6
NEG = -1e30

P_W = 3 * D + 3 * NA_W + 4 * DN_W
P_GATE = 0
P_NAQ, P_NAK, P_NAV = 6, 7, 8
P_DN = 3
P_Z = 12
TN_IN = 512
GP = 128

TM_ALL = 1280
TM_LAT = 1024
TM_MERGE_ALL = 640
TM_MERGE_LAT = 512
NA_BLK = 4 * GRID_W
FFT_N = 128
VMEM_LIMIT = 56 * 1024 * 1024


def _cparams(sem):
    return pltpu.CompilerParams(dimension_semantics=sem, vmem_limit_bytes=VMEM_LIMIT)


def _dot(a, b):
    return jnp.dot(a, b, preferred_element_type=F32)


def _dot_nt(a, b):
    return lax.dot_general(a, b, (((1,), (1,)), ((), ())), preferred_element_type=F32)


def _dot_tn(a, b):
    return lax.dot_general(a, b, (((0,), (0,)), ((), ())), preferred_element_type=F32)


def _silu(x):
    return x * jax.nn.sigmoid(x)


def _ada_kernel(c_ref, w_ref, b_ref, o_ref):
    s = _silu(c_ref[...])
    o_ref[0] = jnp.dot(s, w_ref[0], preferred_element_type=F32, precision=lax.Precision.HIGHEST) + b_ref[0]


def _ada(cc, w_ada, b_ada):
    tn = 1536
    return pl.pallas_call(
        _ada_kernel,
        grid=(DEPTH, 6 * D // tn),
        in_specs=[pl.BlockSpec((8, D), lambda l, j: (0, 0)),
                  pl.BlockSpec((1, D, tn), lambda l, j: (l, 0, j)),
                  pl.BlockSpec((1, 1, tn), lambda l, j: (l, 0, j))],
        out_specs=pl.BlockSpec((1, 8, tn), lambda l, j: (l, 0, j)),
        out_shape=jax.ShapeDtypeStruct((DEPTH, 8, 6 * D), F32),
        compiler_params=_cparams(("arbitrary", "arbitrary")),
        name="ada_mod",
    )(cc, w_ada, b_ada.reshape(DEPTH, 1, 6 * D))


def _mod_rows(mod_ref, k, row0, tm, has_ctx):
    lat = mod_ref[0:1, k * D:(k + 1) * D]
    if not has_ctx:
        return lat
    ctx = mod_ref[1:2, k * D:(k + 1) * D]
    rows = row0 + lax.broadcasted_iota(jnp.int32, (tm, 1), 0)
    return jnp.where(rows >= T_LAT, ctx, lat)


def _modnorm(x, g, shift, scale):
    y = x * lax.rsqrt(jnp.mean(x * x, axis=-1, keepdims=True) + EPS) * g
    return y * (1.0 + scale) + shift


def _in_proj_kernel(x_ref, mod_ref, n_ref, w_ref, wg_ref, u_ref, p_ref, g_ref, h_scr, *, tm):
    i = pl.program_id(0)
    j = pl.program_id(1)

    @pl.when(j == 0)
    def _():
        shift = _mod_rows(mod_ref, 0, i * tm, tm, True)
        scale = _mod_rows(mod_ref, 1, i * tm, tm, True)
        h = _modnorm(x_ref[...], n_ref[...], shift, scale).astype(BF16)
        h_scr[...] = h
        g_ref[...] = _dot(h, wg_ref[...])
        u_ref[...] = _dot(h, w_ref[...]).astype(BF16)

    @pl.when(j > 0)
    def _():
        p_ref[...] = _dot(h_scr[...], w_ref[...]).astype(BF16)


def _in_proj(xs, mod, norm, wcat, wg):
    tm = TM_ALL
    nj = (FN_W + P_W) // TN_IN
    return pl.pallas_call(
        functools.partial(_in_proj_kernel, tm=tm),
        grid=(T_ALL // tm, nj),
        in_specs=[pl.BlockSpec((tm, D), lambda i, j: (i, 0)),
                  pl.BlockSpec((8, 6 * D), lambda i, j: (0, 0)),
                  pl.BlockSpec((1, D), lambda i, j: (0, 0)),
                  pl.BlockSpec((D, TN_IN), lambda i, j: (0, j)),
                  pl.BlockSpec((D, GP), lambda i, j: (0, 0))],
        out_specs=[pl.BlockSpec((tm, FN_W), lambda i, j: (i, 0)),
                   pl.BlockSpec((tm, TN_IN), lambda i, j: (i, jnp.maximum(j - 1, 0))),
                   pl.BlockSpec((tm, GP), lambda i, j: (i, 0))],
        out_shape=[jax.ShapeDtypeStruct((T_ALL, FN_W), BF16),
                   jax.ShapeDtypeStruct((T_ALL, P_W), BF16),
                   jax.ShapeDtypeStruct((T_ALL, GP), F32)],
        scratch_shapes=[pltpu.VMEM((tm, D), BF16)],
        compiler_params=_cparams(("arbitrary", "arbitrary")),
        name="in_proj",
    )(xs, mod, norm.reshape(1, D), wcat, wg)


def _na_tables(rpb):
    qr = np.arange(4)[:, None, None, None]
    qc = np.arange(GRID_W)[None, :, None, None]
    kk = np.arange(12)[None, None, :, None]
    kc = np.arange(GRID_W)[None, None, None, :]
    valid, idx_r, idx_c = [], [], []
    for b in (0, 1, GRID_H // 4 - 1):
        r = 4 * b + qr
        kr = 4 * (b - 1) + kk
        rs = np.clip(r - NA_WIN_H // 2, 0, GRID_H - NA_WIN_H)
        cs = np.clip(qc - NA_WIN_W // 2, 0, GRID_W - NA_WIN_W)
        ok = (kr >= 0) & (kr < GRID_H) & (kr >= rs) & (kr < rs + NA_WIN_H) & (kc >= cs) & (kc < cs + NA_WIN_W)
        valid.append(np.broadcast_to(ok, (4, GRID_W, 12, GRID_W)).reshape(NA_BLK, 12 * GRID_W))
        idx_r.append(np.broadcast_to(np.clip(kr - r + NA_WIN_H - 1, 0, 2 * NA_WIN_H - 2),
                                     (4, GRID_W, 12, GRID_W)).reshape(NA_BLK, 12 * GRID_W))
        idx_c.append(np.broadcast_to(np.clip(kc - qc + NA_WIN_W - 1, 0, 2 * NA_WIN_W - 2),
                                     (4, GRID_W, 12, GRID_W)).reshape(NA_BLK, 12 * GRID_W))
    valid.append(np.zeros_like(valid[0]))
    idx_r.append(np.zeros_like(idx_r[0]))
    idx_c.append(np.zeros_like(idx_c[0]))
    valid, idx_r, idx_c = np.stack(valid), np.stack(idx_r), np.stack(idx_c)
    bias = rpb[:, idx_r, idx_c]
    tab = jnp.where(valid[None], bias, NEG)
    return jnp.transpose(tab, (1, 0, 2, 3)).astype(BF16)


def _na_kernel(q_ref, k0_ref, k1_ref, k2_ref, v0_ref, v1_ref, v2_ref, kc_ref, vc_ref, tab_ref, o_ref):
    q = q_ref[...]
    kl = jnp.concatenate([k0_ref[...], k1_ref[...], k2_ref[...]], axis=0)
    vl = jnp.concatenate([v0_ref[...], v1_ref[...], v2_ref[...]], axis=0)
    kc = kc_ref[...]
    vc = vc_ref[...]
    outs = []
    for h in range(NA_HEADS):
        sl = slice(h * NA_DH, (h + 1) * NA_DH)
        qh = q[:, sl] * (NA_DH ** -0.5)
        s_loc = _dot_nt(qh, kl[:, sl]) + tab_ref[0, h].astype(F32)
        s_ctx = _dot_nt(qh, kc[:, sl])
        m = jnp.maximum(jnp.max(s_loc, axis=-1, keepdims=True), jnp.max(s_ctx, axis=-1, keepdims=True))
        p_loc = jnp.exp(s_loc - m)
        p_ctx = jnp.exp(s_ctx - m)
        denom = jnp.sum(p_loc, axis=-1, keepdims=True) + jnp.sum(p_ctx, axis=-1, keepdims=True)
        o = _dot(p_loc.astype(BF16), vl[:, sl]) + _dot(p_ctx.astype(BF16), vc[:, sl])
        outs.append(o / denom)
    o_ref[...] = jnp.concatenate(outs, axis=-1).astype(BF16)


def _na(p, tab, with_ctx):
    nlat = T_LAT // NA_BLK
    nb = nlat + (1 if with_ctx else 0)
    ctx_blk = T_LAT // NA_BLK

    def kidx(b, part):
        return jnp.clip(jnp.minimum(b, nlat - 1) - 1 + part, 0, nlat - 1)

    def variant(b):
        return jnp.where(b == 0, 0, jnp.where(b == nlat - 1, 2, jnp.where(b >= nlat, 3, 1)))

    blk = (NA_BLK, NA_W)
    in_specs = [pl.BlockSpec(blk, lambda b: (b, P_NAQ))]
    in_specs += [pl.BlockSpec(blk, functools.partial(lambda b, part: (kidx(b, part), P_NAK), part=part))
                 for part in range(3)]
    in_specs += [pl.BlockSpec(blk, functools.partial(lambda b, part: (kidx(b, part), P_NAV), part=part))
                 for part in range(3)]
    in_specs += [pl.BlockSpec(blk, lambda b: (ctx_blk, P_NAK)),
                 pl.BlockSpec(blk, lambda b: (ctx_blk, P_NAV)),
                 pl.BlockSpec((1, NA_HEADS, NA_BLK, 3 * NA_BLK), lambda b: (variant(b), 0, 0, 0))]
    return pl.pallas_call(
        _na_kernel,
        grid=(nb,),
        in_specs=in_specs,
        out_specs=pl.BlockSpec(blk, lambda b: (b, 0)),
        out_shape=jax.ShapeDtypeStruct((nb * NA_BLK, NA_W), BF16),
        compiler_params=_cparams(("arbitrary",)),
        name="na_attn",
    )(p, p, p, p, p, p, p, p, p, tab)


def _rope_tables():
    t = jnp.arange(T_LAT, dtype=jnp.int32)
    half = DN_DH // 2
    pos = jnp.stack([t // GRID_W, t % GRID_W], axis=-1).astype(F32)
    inv = 1.0 / (ROPE_BASE ** (jnp.arange(0, half, 2, dtype=F32) / half))
    ang = pos[:, :, None] * inv[None, None, :]
    ang = jnp.concatenate([ang, ang], axis=-1).reshape(T_LAT, DN_DH)
    cos, sin = jnp.cos(ang), jnp.sin(ang)
    low = (np.arange(DN_DH) % half) < half // 2
    return cos, jnp.where(low[None], -sin, 0.0), jnp.where(low[None], 0.0, sin)


def _dn_prep_kernel(x_ref, prev_ref, next_ref, g_ref, cw_ref, alog_ref, dtb_ref, cos_ref, sa_ref, sb_ref,
                    qkv_ref, gb_ref, *, tm):
    b = pl.program_id(0)
    nlat = T_LAT // tm
    first = jnp.logical_or(b == 0, b == nlat)
    last = b >= nlat - 1
    is_ctx = b >= nlat
    cur = x_ref[...].astype(F32)
    prev = jnp.where(first, 0.0, prev_ref[...][14:16].astype(F32))
    nxt = jnp.where(last, 0.0, next_ref[...][0:2].astype(F32))
    ext = jnp.concatenate([prev, cur, nxt], axis=0)
    cw = cw_ref[...]
    y = ext[0:tm] * cw[0:1]
    for k in range(1, DN_CONV):
        y = y + ext[k:k + tm] * cw[k:k + 1]
    y = _silu(y)
    cos = jnp.where(is_ctx, 1.0, cos_ref[...])
    sa = jnp.where(is_ctx, 0.0, sa_ref[...])
    sb = jnp.where(is_ctx, 0.0, sb_ref[...])
    quarter = DN_DH // 4
    for s in range(2):
        for h in range(DN_HEADS):
            c0 = s * DN_W + h * DN_DH
            xh = y[:, c0:c0 + DN_DH]
            xh = xh * lax.rsqrt(jnp.sum(xh * xh, axis=-1, keepdims=True) + EPS)
            xh = (xh * cos + pltpu.roll(xh, DN_DH - quarter, axis=1) * sa + pltpu.roll(xh, quarter, axis=1) * sb)
            if s == 0:
                xh = xh * (DN_DH ** -0.5)
            qkv_ref[:, c0:c0 + DN_DH] = xh.astype(BF16)
    qkv_ref[:, 2 * DN_W:] = y[:, 2 * DN_W:].astype(BF16)
    g = g_ref[...]
    lane = lax.broadcasted_iota(jnp.int32, g.shape, 1)
    decay = -jnp.exp(alog_ref[...]) * jax.nn.softplus(g + dtb_ref[...])
    gb_ref[...] = jnp.where(lane < 2 * DN_HEADS, decay, jax.nn.sigmoid(g))


def _dn_prep(p, g, conv_w, a_log, dt_bias, rope):
    tm = 256
    nb = T_ALL // tm
    nlat = T_LAT // tm
    hb = 16
    nh = T_ALL // hb
    pad = lambda v: jnp.pad(v.reshape(1, 2 * DN_HEADS).astype(F32), ((0, 0), (0, GP - 2 * DN_HEADS)))
    rspec = pl.BlockSpec((tm, DN_DH), lambda b: (jnp.minimum(b, nlat - 1), 0))
    return pl.pallas_call(
        functools.partial(_dn_prep_kernel, tm=tm),
        grid=(nb,),
        in_specs=[pl.BlockSpec((tm, 3 * DN_W), lambda b: (b, P_DN)),
                  pl.BlockSpec((hb, 3 * DN_W), lambda b: (jnp.maximum(b * (tm // hb) - 1, 0), P_DN)),
                  pl.BlockSpec((hb, 3 * DN_W), lambda b: (jnp.minimum((b + 1) * (tm // hb), nh - 1), P_DN)),
                  pl.BlockSpec((tm, GP), lambda b: (b, 0)),
                  pl.BlockSpec((DN_CONV, 3 * DN_W), lambda b: (0, 0)),
                  pl.BlockSpec((1, GP), lambda b: (0, 0)),
                  pl.BlockSpec((1, GP), lambda b: (0, 0)),
                  rspec, rspec, rspec],
        out_specs=[pl.BlockSpec((tm, 3 * DN_W), lambda b: (b, 0)),
                   pl.BlockSpec((tm, GP), lambda b: (b, 0))],
        out_shape=[jax.ShapeDtypeStruct((T_ALL, 3 * DN_W), BF16),
                   jax.ShapeDtypeStruct((T_ALL, GP), F32)],
        compiler_params=_cparams(("arbitrary",)),
        name="dn_prep",
    )(p, p, p, g, conv_w, pad(a_log), pad(dt_bias), *rope)


def _unit_tri_inverse_minus_identity(a):
    c = a.shape[0]
    ri = lax.broadcasted_iota(jnp.int32, (c, c), 0)
    ci = lax.broadcasted_iota(jnp.int32, (c, c), 1)

    def same_block(s):
        return (ri // s) == (ci // s)

    def mm(x, y):
        return _dot(x.astype(BF16), y.astype(BF16))

    b1 = jnp.where(same_block(8), -a, 0.0)
    b2 = mm(b1, b1)
    b3 = mm(b1, b2)
    b4 = mm(b2, b2)
    n2 = b1 + b2 + b3
    n = n2 + b4 + mm(n2, b4)
    for s in (8, 16, 32):
        lo = jnp.where(jnp.logical_and(same_block(2 * s), jnp.logical_not(same_block(s))), a, 0.0)
        x = lo + mm(n, lo)
        n = n - (x + mm(x, n))
    return n


def _dn_scan_kernel(qkv_f_ref, qkv_b_ref, gb_f_ref, gb_b_ref, of_ref, ob_ref, s_scr):
    i = pl.program_id(0)

    @pl.when(i == 0)
    def _():
        s_scr[...] = jnp.zeros_like(s_scr)

    c = CHUNK
    ri = lax.broadcasted_iota(jnp.int32, (c, c), 0)
    ci = lax.broadcasted_iota(jnp.int32, (c, c), 1)
    for d in range(2):
        qkv_ref, gb_ref, o_ref = ((qkv_f_ref, gb_f_ref, of_ref), (qkv_b_ref, gb_b_ref, ob_ref))[d]
        incl = (ci <= ri) if d == 0 else (ci >= ri)
        strict = (ci < ri) if d == 0 else (ci > ri)
        last = c - 1 if d == 0 else 0
        gb = gb_ref[...]
        gc_all = jnp.dot(incl.astype(F32), gb, preferred_element_type=F32, precision=lax.Precision.HIGHEST)
        gc_all_t = gc_all.T
        for h in range(DN_HEADS):
            idx = d * DN_HEADS + h
            q = qkv_ref[:, h * DN_DH:(h + 1) * DN_DH]
            k = qkv_ref[:, DN_W + h * DN_DH:DN_W + (h + 1) * DN_DH]
            v = qkv_ref[:, 2 * DN_W + h * DN_DH:2 * DN_W + (h + 1) * DN_DH].astype(F32)
            beta = gb[:, 2 * DN_HEADS + idx:2 * DN_HEADS + idx + 1]
            gc = gc_all[:, idx:idx + 1]
            gc_row = gc_all_t[idx:idx + 1, :]
            g_last = gc_all[last:last + 1, idx:idx + 1]
            diff = gc - gc_row
            e = jnp.exp(jnp.where(incl, diff, 0.0))
            e_incl = jnp.where(incl, e, 0.0)
            e_strict = jnp.where(strict, e, 0.0)
            eg = jnp.exp(gc)
            kf = k.astype(F32)
            kb = kf * beta
            a = _dot_nt(kb.astype(BF16), k) * e_strict
            n = _unit_tri_inverse_minus_identity(a)
            nb = n.astype(BF16)
            vb = v * beta
            kbg = kb * eg
            u = vb + _dot(nb, vb.astype(BF16))
            w = kbg + _dot(nb, kbg.astype(BF16))
            k_tail = kf * jnp.exp(g_last - gc)
            qg = q.astype(F32) * eg
            qk = _dot_nt(q, k) * e_incl
            s = s_scr[idx]
            sb = s.astype(BF16)
            v_new = u - _dot(w.astype(BF16), sb)
            vnb = v_new.astype(BF16)
            o = _dot(qg.astype(BF16), sb) + _dot(qk.astype(BF16), vnb)
            s_scr[idx] = s * jnp.exp(g_last) + _dot_tn(k_tail.astype(BF16), vnb)
            o_ref[:, h * DN_DH:(h + 1) * DN_DH] = o


def _dn_scan(qkv, gb):
    n = T_ALL // CHUNK
    nlat = T_LAT // CHUNK
    nctx = T_CTX // CHUNK
    fwd = lambda i: jnp.where(i < nctx, nlat + i, i - nctx)
    bwd = lambda i: n - 1 - i
    return pl.pallas_call(
        _dn_scan_kernel,
        grid=(n,),
        in_specs=[pl.BlockSpec((CHUNK, 3 * DN_W), lambda i: (fwd(i), 0)),
                  pl.BlockSpec((CHUNK, 3 * DN_W), lambda i: (bwd(i), 0)),
                  pl.BlockSpec((CHUNK, GP), lambda i: (fwd(i), 0)),
                  pl.BlockSpec((CHUNK, GP), lambda i: (bwd(i), 0))],
        out_specs=[pl.BlockSpec((CHUNK, DN_W), lambda i: (fwd(i), 0)),
                   pl.BlockSpec((CHUNK, DN_W), lambda i: (bwd(i), 0))],
        out_shape=[jax.ShapeDtypeStruct((T_ALL, DN_W), F32),
                   jax.ShapeDtypeStruct((T_ALL, DN_W), F32)],
        scratch_shapes=[pltpu.VMEM((2 * DN_HEADS, DN_DH, DN_DH), F32)],
        compiler_params=_cparams(("arbitrary",)),
        name="dn_scan",
    )(qkv, qkv, gb, gb)


def _dft_cos_sin(n):
    k = np.arange(n)
    ang = 2.0 * np.pi * ((k[:, None] * k[None, :]) % n) / n
    return np.cos(ang), np.sin(ang)


def _fft_consts():
    n = FFT_N
    c, s = _dft_cos_sin(n)
    cs_ch = np.concatenate([c, s], axis=1)
    w1 = np.block([[c, -s], [-s, -c]])
    k2 = np.arange(n)[None, :, None]
    t2 = np.arange(n)[:, None, None]
    ang = 2.0 * np.pi * ((k2 * t2) % (n * n)) / (n * n)
    scale = 1.0 / math.sqrt(T_LAT * FN_DG)
    cc, sc = _dft_cos_sin(T_CTX)
    scale_c = 1.0 / math.sqrt(T_CTX * FN_DG)
    f32 = lambda a: jnp.asarray(a, F32)
    bf = lambda a: f32(a).astype(BF16)
    return dict(cs_ch=bf(cs_ch), w1=bf(w1), twc=f32(np.cos(ang)), tws=f32(np.sin(ang)),
                c2=bf(c * scale), s2=bf(s * scale), cc=bf(cc * scale_c), sc=bf(sc * scale_c))


def _fft1_kernel(u_ref, cs_ref, w1_ref, twc_ref, tws_ref, y_ref, *, n_t2):
    n = FFT_N
    cs = cs_ref[...]
    w1 = w1_ref[...]
    for t in range(n_t2):
        twc = twc_ref[t]
        tws = tws_ref[t]
        for g in range(FN_GROUPS):
            c0 = (t * FN_GROUPS + g) * FN_DG
            ab = _dot(u_ref[:, c0:c0 + FN_DG], cs)
            rhs = jnp.concatenate([ab[:, :FN_DG], ab[:, FN_DG:]], axis=0).astype(BF16)
            y = _dot(w1, rhs)
            yr, yi = y[:n], y[n:]
            y_ref[0:n, c0:c0 + FN_DG] = (yr * twc + yi * tws).astype(BF16)
            y_ref[n:2 * n, c0:c0 + FN_DG] = (yi * twc - yr * tws).astype(BF16)


def _fft2_kernel(y_ref, c2_ref, s2_ref, o_ref, *, n_k2):
    c2 = c2_ref[...]
    s2 = s2_ref[...]
    for j in range(n_k2):
        o_ref[:, j * FN_W:(j + 1) * FN_W] = (_dot(c2, y_ref[0, j]) + _dot(s2, y_ref[1, j])).astype(BF16)


def _fft_ctx_kernel(u_ref, cs_ref, cc_ref, sc_ref, o_in_ref, o_ref):
    del o_in_ref
    for g in range(FN_GROUPS):
        ab = _dot(u_ref[:, g * FN_DG:(g + 1) * FN_DG], cs_ref[...])
        a = ab[:, :FN_DG].astype(BF16)
        b = ab[:, FN_DG:].astype(BF16)
        o_ref[:, g * FN_DG:(g + 1) * FN_DG] = (_dot(cc_ref[...], a) - _dot(sc_ref[...], b)).astype(BF16)


def _fnet(u, fc, with_ctx):
    n = FFT_N
    row_w = n * FN_W
    n_t2 = 8
    tc = n_t2 * FN_W
    y = pl.pallas_call(
        functools.partial(_fft1_kernel, n_t2=n_t2),
        grid=(row_w // tc,),
        in_specs=[pl.BlockSpec((n, tc), lambda j: (0, j)),
                  pl.BlockSpec((n, 2 * n), lambda j: (0, 0)),
                  pl.BlockSpec((2 * n, 2 * n), lambda j: (0, 0)),
                  pl.BlockSpec((n_t2, n, 1), lambda j: (j, 0, 0)),
                  pl.BlockSpec((n_t2, n, 1), lambda j: (j, 0, 0))],
        out_specs=pl.BlockSpec((2 * n, tc), lambda j: (0, j)),
        out_shape=jax.ShapeDtypeStruct((2 * n, row_w), BF16),
        compiler_params=_cparams(("arbitrary",)),
        name="fnet_stage1",
    )(u.reshape(T_ALL // n, row_w), fc["cs_ch"], fc["w1"], fc["twc"], fc["tws"])
    n_k2 = 8
    o = pl.pallas_call(
        functools.partial(_fft2_kernel, n_k2=n_k2),
        grid=(n // n_k2,),
        in_specs=[pl.BlockSpec((2, n_k2, n, FN_W), lambda j: (0, j, 0, 0)),
                  pl.BlockSpec((n, n), lambda j: (0, 0)),
                  pl.BlockSpec((n, n), lambda j: (0, 0))],
        out_specs=pl.BlockSpec((n, n_k2 * FN_W), lambda j: (0, j)),
        out_shape=jax.ShapeDtypeStruct((T_ALL // n, row_w), BF16),
        compiler_params=_cparams(("arbitrary",)),
        name="fnet_stage2",
    )(y.reshape(2, n, n, FN_W), fc["c2"], fc["s2"])
    o = o.reshape(T_ALL, FN_W)
    if not with_ctx:
        return o
    cb = T_LAT // T_CTX
    return pl.pallas_call(
        _fft_ctx_kernel,
        grid=(1,),
        in_specs=[pl.BlockSpec((T_CTX, FN_W), lambda j: (cb, 0)),
                  pl.BlockSpec((n, 2 * n), lambda j: (0, 0)),
                  pl.BlockSpec((T_CTX, T_CTX), lambda j: (0, 0)),
                  pl.BlockSpec((T_CTX, T_CTX), lambda j: (0, 0)),
                  pl.BlockSpec(memory_space=pl.ANY)],
        out_specs=pl.BlockSpec((T_CTX, FN_W), lambda j: (cb, 0)),
        out_shape=jax.ShapeDtypeStruct((T_ALL, FN_W), BF16),
        input_output_aliases={4: 0},
        compiler_params=_cparams(("arbitrary",)),
        name="fnet_ctx",
    )(u, fc["cs_ch"], fc["cc"], fc["sc"], o)


def _merge_kernel(x_ref, mod_ref, gl_ref, ona_ref, of_ref, ob_ref, z_ref, ofn_ref, dnn_ref,
                  wna_ref, wdn_ref, wfn_ref, wout_ref, o_ref, *, tm, has_ctx):
    i = pl.program_id(0)
    g = jax.nn.sigmoid(gl_ref[...].astype(F32))
    o = of_ref[...] + ob_ref[...]
    z = z_ref[...].astype(F32)
    parts = []
    for h in range(DN_HEADS):
        sl = slice(h * DN_DH, (h + 1) * DN_DH)
        oh = o[:, sl]
        oh = oh * lax.rsqrt(jnp.mean(oh * oh, axis=-1, keepdims=True) + EPS) * dnn_ref[...]
        parts.append(oh * _silu(z[:, sl]))
    odn = jnp.concatenate(parts, axis=-1).astype(BF16)
    y = (g[:, :D] * _dot(ona_ref[...], wna_ref[...])
         + g[:, D:2 * D] * _dot(odn, wdn_ref[...])
         + g[:, 2 * D:] * _dot(ofn_ref[...], wfn_ref[...]))
    y = _dot(y.astype(BF16), wout_ref[...])
    gate = _mod_rows(mod_ref, 2, i * tm, tm, has_ctx)
    o_ref[...] = x_ref[...] + gate * y


def _merge(xs, mod, p, o_na, o_f, o_b, o_fn, dn_norm, w_na_o, w_dn_o, w_fn, w_out, has_ctx):
    rows, tm = (T_ALL, TM_MERGE_ALL) if has_ctx else (T_LAT, TM_MERGE_LAT)
    row = lambda w: pl.BlockSpec((tm, w), lambda i: (i, 0))
    full = lambda a: pl.BlockSpec(a.shape, lambda i: (0, 0))
    return pl.pallas_call(
        functools.partial(_merge_kernel, tm=tm, has_ctx=has_ctx),
        grid=(rows // tm,),
        in_specs=[row(D), pl.BlockSpec((8, 6 * D), lambda i: (0, 0)),
                  pl.BlockSpec((tm, 3 * D), lambda i: (i, P_GATE)),
                  row(NA_W), row(DN_W), row(DN_W),
                  pl.BlockSpec((tm, DN_W), lambda i: (i, P_Z)),
                  row(FN_W), pl.BlockSpec((1, DN_DH), lambda i: (0, 0)),
                  full(w_na_o), full(w_dn_o), full(w_fn), full(w_out)],
        out_specs=row(D),
        out_shape=jax.ShapeDtypeStruct((rows, D), F32),
        compiler_params=_cparams(("arbitrary",)),
        name="merge",
    )(xs, mod, p, o_na, o_f, o_b, p, o_fn, dn_norm.reshape(1, DN_DH), w_na_o, w_dn_o, w_fn, w_out)


def _mlp_kernel(x_ref, mod_ref, n_ref, w1_ref, w2_ref, nf_ref, o_ref, h_scr, acc_scr, *, tm, has_ctx, final):
    i = pl.program_id(0)
    j = pl.program_id(1)

    @pl.when(j == 0)
    def _():
        shift = _mod_rows(mod_ref, 3, i * tm, tm, has_ctx)
        scale = _mod_rows(mod_ref, 4, i * tm, tm, has_ctx)
        h_scr[...] = _modnorm(x_ref[...], n_ref[...], shift, scale).astype(BF16)
        acc_scr[...] = jnp.zeros_like(acc_scr)

    a = jnp.maximum(_dot(h_scr[...], w1_ref[...]), 0.0)
    acc_scr[...] += _dot((a * a).astype(BF16), w2_ref[...])

    @pl.when(j == pl.num_programs(1) - 1)
    def _():
        gate = _mod_rows(mod_ref, 5, i * tm, tm, has_ctx)
        xn = x_ref[...] + gate * acc_scr[...]
        if final:
            xn = xn * lax.rsqrt(jnp.mean(xn * xn, axis=-1, keepdims=True) + EPS) * nf_ref[...]
        o_ref[...] = xn


def _mlp(xs, mod, norm, w1, w2, norm_f, has_ctx, final):
    rows, tm = (T_ALL, TM_ALL) if has_ctx else (T_LAT, TM_LAT)
    th = 1024
    return pl.pallas_call(
        functools.partial(_mlp_kernel, tm=tm, has_ctx=has_ctx, final=final),
        grid=(rows // tm, HID // th),
        in_specs=[pl.BlockSpec((tm, D), lambda i, j: (i, 0)),
                  pl.BlockSpec((8, 6 * D), lambda i, j: (0, 0)),
                  pl.BlockSpec((1, D), lambda i, j: (0, 0)),
                  pl.BlockSpec((D, th), lambda i, j: (0, j)),
                  pl.BlockSpec((th, D), lambda i, j: (j, 0)),
                  pl.BlockSpec((1, D), lambda i, j: (0, 0))],
        out_specs=pl.BlockSpec((tm, D), lambda i, j: (i, 0)),
        out_shape=jax.ShapeDtypeStruct((rows, D), F32),
        scratch_shapes=[pltpu.VMEM((tm, D), BF16), pltpu.VMEM((tm, D), F32)],
        compiler_params=_cparams(("arbitrary", "arbitrary")),
        name="mlp",
    )(xs, mod, norm.reshape(1, D), w1, w2, norm_f.reshape(1, D))


def _split_in_weights(w_in):
    na_k, na_v, dn_k, dn_v, dn_ab, na_q, dn_q, dn_z, fn_u, gate = jnp.split(
        w_in, np.cumsum([NA_W, NA_W, DN_W, DN_W, 4 * DN_HEADS, NA_W, DN_W, DN_W, FN_W]).tolist(), axis=-1)
    wcat = jnp.concatenate([fn_u, gate, na_q, na_k, na_v, dn_q, dn_k, dn_v, dn_z], axis=-1).astype(BF16)
    wg = jnp.pad(dn_ab, ((0, 0), (0, GP - 4 * DN_HEADS))).astype(BF16)
    return wcat, wg


def kernel(x, c, ctx, c_ctx, w_ada, b_ada, norm1, w_in, conv_w, a_log, dt_bias, dn_norm, rpb,
           w_na_o, w_dn_o, w_fn, w_out, norm2, w_mlp1, w_mlp2, norm_f):
    xs = jnp.concatenate([x[0], ctx[0]], axis=0)
    cc = jnp.concatenate([c, c_ctx[None, :], jnp.zeros((6, D), F32)], axis=0)
    mods = _ada(cc, w_ada, b_ada)
    rope = _rope_tables()
    fc = _fft_consts()
    for l in range(DEPTH):
        has_ctx = l < DEPTH - 1
        final = l == DEPTH - 1
        wcat, wg = _split_in_weights(w_in[l])
        u, p, g = _in_proj(xs, mods[l], norm1[l], wcat, wg)
        o_na = _na(p, _na_tables(rpb[l]), has_ctx)
        qkv, gb = _dn_prep(p, g, conv_w[l], a_log[l], dt_bias[l], rope)
        o_f, o_b = _dn_scan(qkv, gb)
        o_fn = _fnet(u, fc, has_ctx)
        xs = _merge(xs, mods[l], p, o_na, o_f, o_b, o_fn, dn_norm[l],
                    w_na_o[l].astype(BF16), w_dn_o[l].astype(BF16), w_fn[l].astype(BF16), w_out[l].astype(BF16),
                    has_ctx)
        xs = _mlp(xs, mods[l], norm2[l], w_mlp1[l].astype(BF16), w_mlp2[l].astype(BF16), norm_f, has_ctx, final)
    return xs[None]
```

```python
import functools
import math

import numpy as np
import jax
import jax.numpy as jnp
from jax import lax
from jax.experimental import pallas as pl
from jax.experimental.pallas import tpu as pltpu

F32 = jnp.float32
BF16 = jnp.bfloat16

D = 1024
T_LAT = 16384
T_CTX = 256
T_ALL = T_LAT + T_CTX
DEPTH = 2
GRID_W = 64
GRID_H = T_LAT // GRID_W
NA_HEADS = 8
NA_DH = 64
NA_WIN_H = 8
NA_WIN_W = 16
NA_W = NA_HEADS * NA_DH
DN_HEADS = 4
DN_DH = 128
DN_W = DN_HEADS * DN_DH
DN_CONV = 5
CHUNK = 64
FN_GROUPS = 4
FN_DG = 128
FN_W = FN_GROUPS * FN_DG
HID = 4 * D
ROPE_BASE = 10000.0
EPS = 1e-6
NEG = -1e30

P_W = 3 * D + 3 * NA_W + 4 * DN_W + FN_W
P_GATE = 0
P_NAQ, P_NAK, P_NAV = 6, 7, 8
P_DN = 3
P_Z = 12
TN_IN = P_W // 4
GP = 128

TM_ALL = 1280
TM_LAT = 1024
TM_MERGE_ALL = 640
TM_MERGE_LAT = 512
NA_BLK = 4 * GRID_W
NA_HEAD_GROUP = 2
FFT_N = 128
DN_BLK_CHUNKS = 4
WY_GROUP_CHUNKS = 2
VMEM_LIMIT = 56 * 1024 * 1024


def _cparams(sem):
    return pltpu.CompilerParams(dimension_semantics=sem, vmem_limit_bytes=VMEM_LIMIT)


def _dot(a, b):
    return jnp.dot(a, b, preferred_element_type=F32)


def _dot_nt(a, b):
    return lax.dot_general(a, b, (((1,), (1,)), ((), ())), preferred_element_type=F32)


def _dot_tn(a, b):
    return lax.dot_general(a, b, (((0,), (0,)), ((), ())), preferred_element_type=F32)


def _silu(x):
    return x * jax.nn.sigmoid(x)


def _ada_kernel(c_ref, w_ref, b_ref, o_ref):
    s = _silu(c_ref[...])
    o_ref[0] = jnp.dot(s, w_ref[0], preferred_element_type=F32, precision=lax.Precision.HIGHEST) + b_ref[0]


def _ada(cc, w_ada, b_ada):
    tn = 1536
    return pl.pallas_call(
        _ada_kernel,
        grid=(DEPTH, 6 * D // tn),
        in_specs=[pl.BlockSpec((8, D), lambda l, j: (0, 0)),
                  pl.BlockSpec((1, D, tn), lambda l, j: (l, 0, j)),
                  pl.BlockSpec((1, 1, tn), lambda l, j: (l, 0, j))],
        out_specs=pl.BlockSpec((1, 8, tn), lambda l, j: (l, 0, j)),
        out_shape=jax.ShapeDtypeStruct((DEPTH, 8, 6 * D), F32),
        compiler_params=_cparams(("arbitrary", "arbitrary")),
        name="ada_mod",
    )(cc, w_ada, b_ada.reshape(DEPTH, 1, 6 * D))


def _mod_rows(mod_ref, k, row0, tm, has_ctx):
    lat = mod_ref[0:1, k * D:(k + 1) * D]
    if not has_ctx:
        return lat
    ctx = mod_ref[1:2, k * D:(k + 1) * D]
    rows = row0 + lax.broadcasted_iota(jnp.int32, (tm, 1), 0)
    return jnp.where(rows >= T_LAT, ctx, lat)


def _modnorm(x, g, shift, scale):
    y = x * lax.rsqrt(jnp.mean(x * x, axis=-1, keepdims=True) + EPS) * g
    return y * (1.0 + scale) + shift


def _in_proj_kernel(x_ref, mod_ref, n_ref, w_ref, wg_ref, u_ref, p_ref, g_ref, h_scr, *, tm):
    i = pl.program_id(0)
    j = pl.program_id(1)

    @pl.when(j == 0)
    def _():
        shift = _mod_rows(mod_ref, 0, i * tm, tm, True)
        scale = _mod_rows(mod_ref, 1, i * tm, tm, True)
        h = _modnorm(x_ref[...], n_ref[...], shift, scale).astype(BF16)
        h_scr[...] = h
        g_ref[...] = _dot(h, wg_ref[...])

    p = _dot(h_scr[...], w_ref[...]).astype(BF16)
    p_ref[...] = p

    @pl.when(j == pl.num_programs(1) - 1)
    def _():
        u_ref[...] = p[:, TN_IN - FN_W:]


def _in_proj(xs, mod, norm, wcat, wg):
    tm = TM_ALL
    nj = P_W // TN_IN
    return pl.pallas_call(
        functools.partial(_in_proj_kernel, tm=tm),
        grid=(T_ALL // tm, nj),
        in_specs=[pl.BlockSpec((tm, D), lambda i, j: (i, 0)),
                  pl.BlockSpec((8, 6 * D), lambda i, j: (0, 0)),
                  pl.BlockSpec((1, D), lambda i, j: (0, 0)),
                  pl.BlockSpec((D, TN_IN), lambda i, j: (0, j)),
                  pl.BlockSpec((D, GP), lambda i, j: (0, 0))],
        out_specs=[pl.BlockSpec((tm, FN_W), lambda i, j: (i, 0)),
                   pl.BlockSpec((tm, TN_IN), lambda i, j: (i, j)),
                   pl.BlockSpec((tm, GP), lambda i, j: (i, 0))],
        out_shape=[jax.ShapeDtypeStruct((T_ALL, FN_W), BF16),
                   jax.ShapeDtypeStruct((T_ALL, P_W), BF16),
                   jax.ShapeDtypeStruct((T_ALL, GP), F32)],
        scratch_shapes=[pltpu.VMEM((tm, D), BF16)],
        compiler_params=_cparams(("arbitrary", "arbitrary")),
        name="in_proj",
    )(xs, mod, norm.reshape(1, D), wcat, wg)


def _na_tables(rpb):
    qr = np.arange(4)[:, None, None, None]
    qc = np.arange(GRID_W)[None, :, None, None]
    kk = np.arange(12)[None, None, :, None]
    kc = np.arange(GRID_W)[None, None, None, :]
    valid = []
    for b in (0, 1, GRID_H // 4 - 1):
        r = 4 * b + qr
        kr = 4 * (b - 1) + kk
        rs = np.clip(r - NA_WIN_H // 2, 0, GRID_H - NA_WIN_H)
        cs = np.clip(qc - NA_WIN_W // 2, 0, GRID_W - NA_WIN_W)
        ok = (kr >= 0) & (kr < GRID_H) & (kr >= rs) & (kr < rs + NA_WIN_H) & (kc >= cs) & (kc < cs + NA_WIN_W)
        valid.append(np.broadcast_to(ok, (4, GRID_W, 12, GRID_W)).reshape(NA_BLK, 12 * GRID_W))
    valid.append(np.zeros_like(valid[0]))
    valid = np.stack(valid)
    col = np.arange(GRID_W)
    onehot = ((col[None, None, :] - col[None, :, None] + NA_WIN_W - 1)
              == np.arange(2 * NA_WIN_W - 1)[:, None, None]).astype(np.float32)
    tcol = jnp.einsum('hrd,dqk->hrqk', rpb, onehot, precision=lax.Precision.HIGHEST)
    bias = jnp.stack([jnp.stack([tcol[:, k - q + NA_WIN_H - 5] for k in range(12)], axis=2) for q in range(4)], axis=1)
    bias = bias.reshape(NA_HEADS, NA_BLK, 12 * GRID_W)
    return jnp.where(valid[:, None], bias[None], NEG).astype(BF16)


def _na_kernel(q_ref, k0_ref, k1_ref, k2_ref, v0_ref, v1_ref, v2_ref, kc_ref, vc_ref, tab_ref, o_ref):
    q = q_ref[...]
    kl = jnp.concatenate([k0_ref[...], k1_ref[...], k2_ref[...]], axis=0)
    vl = jnp.concatenate([v0_ref[...], v1_ref[...], v2_ref[...]], axis=0)
    kc = kc_ref[...]
    vc = vc_ref[...]
    outs = []
    for h0 in range(0, NA_HEADS, NA_HEAD_GROUP):
        heads = range(h0, h0 + NA_HEAD_GROUP)
        sls = [slice(h * NA_DH, (h + 1) * NA_DH) for h in heads]
        qs = [q[:, sl] * (NA_DH ** -0.5) for sl in sls]
        s_loc = [_dot_nt(qh, kl[:, sl]) + tab_ref[0, h].astype(F32) for h, qh, sl in zip(heads, qs, sls)]
        s_ctx = [_dot_nt(qh, kc[:, sl]) for qh, sl in zip(qs, sls)]
        m = [jnp.maximum(jnp.max(a, axis=-1, keepdims=True), jnp.max(b, axis=-1, keepdims=True))
             for a, b in zip(s_loc, s_ctx)]
        p_loc = [jnp.exp(a - mm) for a, mm in zip(s_loc, m)]
        p_ctx = [jnp.exp(b - mm) for b, mm in zip(s_ctx, m)]
        denom = [jnp.sum(a, axis=-1, keepdims=True) + jnp.sum(b, axis=-1, keepdims=True) for a, b in zip(p_loc, p_ctx)]
        o = [_dot(a.astype(BF16), vl[:, sl]) + _dot(b.astype(BF16), vc[:, sl]) for a, b, sl in zip(p_loc, p_ctx, sls)]
        outs += [oo / dd for oo, dd in zip(o, denom)]
    o_ref[...] = jnp.concatenate(outs, axis=-1).astype(BF16)


def _na(p, tab, with_ctx):
    nlat = T_LAT // NA_BLK
    nb = nlat + (1 if with_ctx else 0)
    ctx_blk = T_LAT // NA_BLK

    def kidx(b, part):
        return jnp.clip(jnp.minimum(b, nlat - 1) - 1 + part, 0, nlat - 1)

    def variant(b):
        return jnp.where(b == 0, 0, jnp.where(b == nlat - 1, 2, jnp.where(b >= nlat, 3, 1)))

    blk = (NA_BLK, NA_W)
    in_specs = [pl.BlockSpec(blk, lambda b: (b, P_NAQ))]
    in_specs += [pl.BlockSpec(blk, functools.partial(lambda b, part: (kidx(b, part), P_NAK), part=part))
                 for part in range(3)]
    in_specs += [pl.BlockSpec(blk, functools.partial(lambda b, part: (kidx(b, part), P_NAV), part=part))
                 for part in range(3)]
    in_specs += [pl.BlockSpec(blk, lambda b: (ctx_blk, P_NAK)),
                 pl.BlockSpec(blk, lambda b: (ctx_blk, P_NAV)),
                 pl.BlockSpec((1, NA_HEADS, NA_BLK, 3 * NA_BLK), lambda b: (variant(b), 0, 0, 0))]
    return pl.pallas_call(
        _na_kernel,
        grid=(nb,),
        in_specs=in_specs,
        out_specs=pl.BlockSpec(blk, lambda b: (b, 0)),
        out_shape=jax.ShapeDtypeStruct((nb * NA_BLK, NA_W), BF16),
        compiler_params=_cparams(("arbitrary",)),
        name="na_attn",
    )(p, p, p, p, p, p, p, p, p, tab)


def _rope_tables():
    t = jnp.arange(T_LAT, dtype=jnp.int32)
    half = DN_DH // 2
    pos = jnp.stack([t // GRID_W, t % GRID_W], axis=-1).astype(F32)
    inv = 1.0 / (ROPE_BASE ** (jnp.arange(0, half, 2, dtype=F32) / half))
    ang = pos[:, :, None] * inv[None, None, :]
    ang = jnp.concatenate([ang, ang], axis=-1).reshape(T_LAT, DN_DH)
    cos, sin = jnp.cos(ang), jnp.sin(ang)
    low = (np.arange(DN_DH) % half) < half // 2
    return cos, jnp.where(low[None], -sin, 0.0), jnp.where(low[None], 0.0, sin)


def _dn_prep_kernel(x_ref, prev_ref, next_ref, g_ref, cw_ref, alog_ref, dtb_ref, cos_ref, sa_ref, sb_ref,
                    qkv_ref, gb_ref, *, tm):
    b = pl.program_id(0)
    nlat = T_LAT // tm
    first = jnp.logical_or(b == 0, b == nlat)
    last = b >= nlat - 1
    is_ctx = b >= nlat
    cur = x_ref[...].astype(F32)
    prev = jnp.where(first, 0.0, prev_ref[...][14:16].astype(F32))
    nxt = jnp.where(last, 0.0, next_ref[...][0:2].astype(F32))
    ext = jnp.concatenate([prev, cur, nxt], axis=0)
    cw = cw_ref[...]
    y = ext[0:tm] * cw[0:1]
    for k in range(1, DN_CONV):
        y = y + ext[k:k + tm] * cw[k:k + 1]
    y = _silu(y)
    cos = jnp.where(is_ctx, 1.0, cos_ref[...])
    sa = jnp.where(is_ctx, 0.0, sa_ref[...])
    sb = jnp.where(is_ctx, 0.0, sb_ref[...])
    quarter = DN_DH // 4
    for s in range(2):
        for h in range(DN_HEADS):
            c0 = s * DN_W + h * DN_DH
            xh = y[:, c0:c0 + DN_DH]
            xh = xh * lax.rsqrt(jnp.sum(xh * xh, axis=-1, keepdims=True) + EPS)
            xh = (xh * cos + pltpu.roll(xh, DN_DH - quarter, axis=1) * sa + pltpu.roll(xh, quarter, axis=1) * sb)
            if s == 0:
                xh = xh * (DN_DH ** -0.5)
            qkv_ref[:, c0:c0 + DN_DH] = xh.astype(BF16)
    qkv_ref[:, 2 * DN_W:] = y[:, 2 * DN_W:].astype(BF16)
    g = g_ref[...]
    lane = lax.broadcasted_iota(jnp.int32, g.shape, 1)
    decay = -jnp.exp(alog_ref[...]) * jax.nn.softplus(g + dtb_ref[...])
    gb_ref[...] = jnp.where(lane < 2 * DN_HEADS, decay, jax.nn.sigmoid(g))


def _dn_prep(p, g, conv_w, a_log, dt_bias, rope):
    tm = 256
    nb = T_ALL // tm
    nlat = T_LAT // tm
    hb = 16
    nh = T_ALL // hb
    pad = lambda v: jnp.pad(v.reshape(1, 2 * DN_HEADS).astype(F32), ((0, 0), (0, GP - 2 * DN_HEADS)))
    rspec = pl.BlockSpec((tm, DN_DH), lambda b: (jnp.minimum(b, nlat - 1), 0))
    return pl.pallas_call(
        functools.partial(_dn_prep_kernel, tm=tm),
        grid=(nb,),
        in_specs=[pl.BlockSpec((tm, 3 * DN_W), lambda b: (b, P_DN)),
                  pl.BlockSpec((hb, 3 * DN_W), lambda b: (jnp.maximum(b * (tm // hb) - 1, 0), P_DN)),
                  pl.BlockSpec((hb, 3 * DN_W), lambda b: (jnp.minimum((b + 1) * (tm // hb), nh - 1), P_DN)),
                  pl.BlockSpec((tm, GP), lambda b: (b, 0)),
                  pl.BlockSpec((DN_CONV, 3 * DN_W), lambda b: (0, 0)),
                  pl.BlockSpec((1, GP), lambda b: (0, 0)),
                  pl.BlockSpec((1, GP), lambda b: (0, 0)),
                  rspec, rspec, rspec],
        out_specs=[pl.BlockSpec((tm, 3 * DN_W), lambda b: (b, 0)),
                   pl.BlockSpec((tm, GP), lambda b: (b, 0))],
        out_shape=[jax.ShapeDtypeStruct((T_ALL, 3 * DN_W), BF16),
                   jax.ShapeDtypeStruct((T_ALL, GP), F32)],
        compiler_params=_cparams(("arbitrary",)),
        name="dn_prep",
    )(p, p, p, g, conv_w, pad(a_log), pad(dt_bias), *rope)


def _unit_tri_inverses_minus_identity(mats):
    c = mats[0].shape[0]
    ri = lax.broadcasted_iota(jnp.int32, (c, c), 0)
    ci = lax.broadcasted_iota(jnp.int32, (c, c), 1)

    def same_block(s):
        return (ri // s) == (ci // s)

    def mm(xs, ys):
        return [_dot(x.astype(BF16), y.astype(BF16)) for x, y in zip(xs, ys)]

    diag = same_block(8)
    b1 = [jnp.where(diag, -a, 0.0) for a in mats]
    b2 = mm(b1, b1)
    b3 = mm(b1, b2)
    b4 = mm(b2, b2)
    n2 = [x + y + z for x, y, z in zip(b1, b2, b3)]
    n = [x + y + z for x, y, z in zip(n2, b4, mm(n2, b4))]
    for s in (8, 16, 32):
        off = jnp.logical_and(same_block(2 * s), jnp.logical_not(same_block(s)))
        lo = [jnp.where(off, a, 0.0) for a in mats]
        x = [p + q for p, q in zip(lo, mm(n, lo))]
        n = [p - (q + r) for p, q, r in zip(n, x, mm(x, n))]
    return n


def _dn_wy_kernel(qkv_ref, gb_ref, u_ref, wq_ref, kq_ref, dl_ref, *, nchunk):
    c = CHUNK
    tm = nchunk * c
    gb = gb_ref[...]
    qkv = qkv_ref[...]
    rt = lax.broadcasted_iota(jnp.int32, (tm, tm), 0)
    ct = lax.broadcasted_iota(jnp.int32, (tm, tm), 1)
    same_chunk = (rt // c) == (ct // c)
    ri = lax.broadcasted_iota(jnp.int32, (c, c), 0)
    ci = lax.broadcasted_iota(jnp.int32, (c, c), 1)
    lane = lax.broadcasted_iota(jnp.int32, (8, GP), 1)
    dl_rows, gcs = [], []
    for d in range(2):
        cum = jnp.logical_and(same_chunk, (ct <= rt) if d == 0 else (ct >= rt))
        gc_all = jnp.dot(cum.astype(F32), gb, preferred_element_type=F32, precision=lax.Precision.HIGHEST)
        gcs.append((gc_all, gc_all.T))
        dl_rows.append([jnp.exp(gc_all[j * c + (c - 1 if d == 0 else 0):j * c + (c if d == 0 else 1), :])
                        for j in range(nchunk)])
    for j0 in range(0, nchunk, WY_GROUP_CHUNKS):
        chains = [(d, j, h) for j in range(j0, j0 + WY_GROUP_CHUNKS) for d in range(2) for h in range(DN_HEADS)]
        pre = []
        for d, j, h in chains:
            gc_all, gc_all_t = gcs[d]
            incl = (ci <= ri) if d == 0 else (ci >= ri)
            strict = (ci < ri) if d == 0 else (ci > ri)
            r0 = j * c
            last = r0 + (c - 1 if d == 0 else 0)
            idx = d * DN_HEADS + h
            q = qkv[r0:r0 + c, h * DN_DH:(h + 1) * DN_DH]
            k = qkv[r0:r0 + c, DN_W + h * DN_DH:DN_W + (h + 1) * DN_DH]
            v = qkv[r0:r0 + c, 2 * DN_W + h * DN_DH:2 * DN_W + (h + 1) * DN_DH].astype(F32)
            beta = gb[r0:r0 + c, 2 * DN_HEADS + idx:2 * DN_HEADS + idx + 1]
            gc = gc_all[r0:r0 + c, idx:idx + 1]
            gc_row = gc_all_t[idx:idx + 1, r0:r0 + c]
            g_last = gc_all[last:last + 1, idx:idx + 1]
            e = jnp.exp(jnp.where(incl, gc - gc_row, 0.0))
            eg = jnp.exp(gc)
            kf = k.astype(F32)
            kb = kf * beta
            pre.append(dict(q=q, k=k, kb=kb, eg=eg, e_incl=jnp.where(incl, e, 0.0), e_strict=jnp.where(strict, e, 0.0),
                            rhs=jnp.concatenate([v * beta, kb * eg], axis=1),
                            k_tail=kf * jnp.exp(g_last - gc)))
        aq = [_dot_nt(jnp.concatenate([p["kb"].astype(BF16), p["q"]], axis=0), p["k"]) for p in pre]
        ns = _unit_tri_inverses_minus_identity([x[:c] * p["e_strict"] for x, p in zip(aq, pre)])
        uws = [p["rhs"] + _dot(n.astype(BF16), p["rhs"].astype(BF16)) for n, p in zip(ns, pre)]
        for (d, j, h), p, x, uw in zip(chains, pre, aq, uws):
            r0 = j * c
            idx = d * DN_HEADS + h
            cs = slice(idx * DN_DH, (idx + 1) * DN_DH)
            ks = slice(idx * c, (idx + 1) * c)
            u_ref[r0:r0 + c, cs] = uw[:, :DN_DH].astype(BF16)
            wq_ref[2 * r0:2 * r0 + c, cs] = uw[:, DN_DH:].astype(BF16)
            wq_ref[2 * r0 + c:2 * r0 + 2 * c, cs] = (p["q"].astype(F32) * p["eg"]).astype(BF16)
            kq_ref[3 * r0:3 * r0 + DN_DH, ks] = p["k_tail"].T.astype(BF16)
            kq_ref[3 * r0 + DN_DH:3 * r0 + DN_DH + c, ks] = (x[c:] * p["e_incl"]).astype(BF16)
    for j in range(nchunk):
        row = jnp.where(lane < DN_HEADS, dl_rows[0][j], dl_rows[1][j])
        dl_ref[8 * j:8 * j + 8, :] = row


def _dn_wy(qkv, gb, nchunk):
    tm = nchunk * CHUNK
    nb = T_ALL // tm
    nch = 2 * DN_HEADS
    return pl.pallas_call(
        functools.partial(_dn_wy_kernel, nchunk=nchunk),
        grid=(nb,),
        in_specs=[pl.BlockSpec((tm, 3 * DN_W), lambda i: (i, 0)),
                  pl.BlockSpec((tm, GP), lambda i: (i, 0))],
        out_specs=[pl.BlockSpec((tm, nch * DN_DH), lambda i: (i, 0)),
                   pl.BlockSpec((2 * tm, nch * DN_DH), lambda i: (i, 0)),
                   pl.BlockSpec((3 * tm, nch * CHUNK), lambda i: (i, 0)),
                   pl.BlockSpec((8 * nchunk, GP), lambda i: (i, 0))],
        out_shape=[jax.ShapeDtypeStruct((T_ALL, nch * DN_DH), BF16),
                   jax.ShapeDtypeStruct((2 * T_ALL, nch * DN_DH), BF16),
                   jax.ShapeDtypeStruct((3 * T_ALL, nch * CHUNK), BF16),
                   jax.ShapeDtypeStruct((8 * T_ALL // CHUNK, GP), F32)],
        compiler_params=_cparams(("arbitrary",)),
        name="dn_wy",
    )(qkv, gb)


def _dn_scan_kernel(u_f_ref, wq_f_ref, kq_f_ref, dl_f_ref, u_b_ref, wq_b_ref, kq_b_ref, dl_b_ref,
                    of_ref, ob_ref, s_scr, *, nchunk):
    i = pl.program_id(0)

    @pl.when(i == 0)
    def _():
        s_scr[...] = jnp.zeros_like(s_scr)

    c = CHUNK
    states = [s_scr[idx] for idx in range(2 * DN_HEADS)]
    zero = jnp.zeros((c, DN_DH), BF16)
    refs = ((u_f_ref, wq_f_ref, kq_f_ref, dl_f_ref, of_ref), (u_b_ref, wq_b_ref, kq_b_ref, dl_b_ref, ob_ref))
    chains = [(d, h) for d in range(2) for h in range(DN_HEADS)]
    pairs = [(d, p) for d in range(2) for p in range(DN_HEADS // 2)]
    for step in range(nchunk):
        blk = lambda d: step if d == 0 else nchunk - 1 - step
        hs = lambda h: slice(h * DN_DH, (h + 1) * DN_DH)
        r1 = [_dot(refs[d][1][2 * blk(d) * c:2 * (blk(d) + 1) * c, hs(h)], states[d * DN_HEADS + h].astype(BF16))
              for d, h in chains]
        v_new = [(refs[d][0][blk(d) * c:(blk(d) + 1) * c, hs(h)].astype(F32) - r[:c]).astype(BF16)
                 for (d, h), r in zip(chains, r1)]
        r2 = [_dot(refs[d][2][3 * blk(d) * c:3 * (blk(d) + 1) * c, p * 2 * c:(p + 1) * 2 * c],
                   jnp.concatenate([jnp.concatenate([v_new[d * DN_HEADS + 2 * p], zero], axis=1),
                                    jnp.concatenate([zero, v_new[d * DN_HEADS + 2 * p + 1]], axis=1)], axis=0))
              for d, p in pairs]
        for d, h in chains:
            idx = d * DN_HEADS + h
            r = r2[d * (DN_HEADS // 2) + h // 2]
            ts = slice((h % 2) * DN_DH, (h % 2 + 1) * DN_DH)
            dl_row = refs[d][3][8 * blk(d):8 * blk(d) + 1, :]
            states[idx] = states[idx] * dl_row[:, idx:idx + 1] + r[:DN_DH, ts]
            refs[d][4][blk(d) * c:(blk(d) + 1) * c, hs(h)] = (r1[idx][c:] + r[DN_DH:, ts]).astype(BF16)
    for idx in range(2 * DN_HEADS):
        s_scr[idx] = states[idx]


def _dn_scan(u, wq, kq, dl, nchunk):
    tm = nchunk * CHUNK
    nb = T_ALL // tm
    nlat = T_LAT // tm
    fwd = lambda i: jnp.where(i == 0, nlat, i - 1)
    bwd = lambda i: nb - 1 - i
    specs = lambda f, col: [pl.BlockSpec((tm, DN_W), lambda i: (f(i), col)),
                            pl.BlockSpec((2 * tm, DN_W), lambda i: (f(i), col)),
                            pl.BlockSpec((3 * tm, DN_HEADS * CHUNK), lambda i: (f(i), col)),
                            pl.BlockSpec((8 * nchunk, GP), lambda i: (f(i), 0))]
    return pl.pallas_call(
        functools.partial(_dn_scan_kernel, nchunk=nchunk),
        grid=(nb,),
        in_specs=specs(fwd, 0) + specs(bwd, 1),
        out_specs=[pl.BlockSpec((tm, DN_W), lambda i: (fwd(i), 0)),
                   pl.BlockSpec((tm, DN_W), lambda i: (bwd(i), 0))],
        out_shape=[jax.ShapeDtypeStruct((T_ALL, DN_W), BF16),
                   jax.ShapeDtypeStruct((T_ALL, DN_W), BF16)],
        scratch_shapes=[pltpu.VMEM((2 * DN_HEADS, DN_DH, DN_DH), F32)],
        compiler_params=_cparams(("arbitrary",)),
        name="dn_scan",
    )(u, wq, kq, dl, u, wq, kq, dl)


def _dft_cos_sin(n):
    k = np.arange(n)
    ang = 2.0 * np.pi * ((k[:, None] * k[None, :]) % n) / n
    return np.cos(ang), np.sin(ang)


def _fft_consts():
    n = FFT_N
    c, s = _dft_cos_sin(n)
    cs_ch = np.concatenate([c, s], axis=1)
    w1 = np.block([[c, -s], [-s, -c]])
    k2 = np.arange(n)[None, :, None]
    t2 = np.arange(n)[:, None, None]
    ang = 2.0 * np.pi * ((k2 * t2) % (n * n)) / (n * n)
    scale = 1.0 / math.sqrt(T_LAT * FN_DG)
    cc, sc = _dft_cos_sin(T_CTX)
    scale_c = 1.0 / math.sqrt(T_CTX * FN_DG)
    f32 = lambda a: jnp.asarray(a, F32)
    bf = lambda a: f32(a).astype(BF16)
    return dict(cs_ch=bf(cs_ch), w1=bf(w1), twc=f32(np.cos(ang)), tws=f32(np.sin(ang)),
                c2=bf(c * scale), s2=bf(s * scale), cc=bf(cc * scale_c), sc=bf(sc * scale_c))


def _fft1_kernel(u_ref, cs_ref, w1_ref, twc_ref, tws_ref, y_ref, *, n_t2):
    n = FFT_N
    cs = cs_ref[...]
    w1 = w1_ref[...]
    for t in range(n_t2):
        twc = twc_ref[t]
        tws = tws_ref[t]
        for g in range(FN_GROUPS):
            c0 = (t * FN_GROUPS + g) * FN_DG
            ab = _dot(u_ref[:, c0:c0 + FN_DG], cs)
            rhs = jnp.concatenate([ab[:, :FN_DG], ab[:, FN_DG:]], axis=0).astype(BF16)
            y = _dot(w1, rhs)
            yr, yi = y[:n], y[n:]
            y_ref[0:n, c0:c0 + FN_DG] = (yr * twc + yi * tws).astype(BF16)
            y_ref[n:2 * n, c0:c0 + FN_DG] = (yi * twc - yr * tws).astype(BF16)


def _fft2_kernel(y_ref, c2_ref, s2_ref, o_ref, *, n_k2):
    c2 = c2_ref[...]
    s2 = s2_ref[...]
    for j in range(n_k2):
        o_ref[:, j * FN_W:(j + 1) * FN_W] = (_dot(c2, y_ref[0, j]) + _dot(s2, y_ref[1, j])).astype(BF16)


def _fft_ctx_kernel(u_ref, cs_ref, cc_ref, sc_ref, o_in_ref, o_ref):
    del o_in_ref
    for g in range(FN_GROUPS):
        ab = _dot(u_ref[:, g * FN_DG:(g + 1) * FN_DG], cs_ref[...])
        a = ab[:, :FN_DG].astype(BF16)
        b = ab[:, FN_DG:].astype(BF16)
        o_ref[:, g * FN_DG:(g + 1) * FN_DG] = (_dot(cc_ref[...], a) - _dot(sc_ref[...], b)).astype(BF16)


def _fnet(u, fc, with_ctx):
    n = FFT_N
    row_w = n * FN_W
    n_t2 = 8
    tc = n_t2 * FN_W
    y = pl.pallas_call(
        functools.partial(_fft1_kernel, n_t2=n_t2),
        grid=(row_w // tc,),
        in_specs=[pl.BlockSpec((n, tc), lambda j: (0, j)),
                  pl.BlockSpec((n, 2 * n), lambda j: (0, 0)),
                  pl.BlockSpec((2 * n, 2 * n), lambda j: (0, 0)),
                  pl.BlockSpec((n_t2, n, 1), lambda j: (j, 0, 0)),
                  pl.BlockSpec((n_t2, n, 1), lambda j: (j, 0, 0))],
        out_specs=pl.BlockSpec((2 * n, tc), lambda j: (0, j)),
        out_shape=jax.ShapeDtypeStruct((2 * n, row_w), BF16),
        compiler_params=_cparams(("arbitrary",)),
        name="fnet_stage1",
    )(u.reshape(T_ALL // n, row_w), fc["cs_ch"], fc["w1"], fc["twc"], fc["tws"])
    n_k2 = 8
    o = pl.pallas_call(
        functools.partial(_fft2_kernel, n_k2=n_k2),
        grid=(n // n_k2,),
        in_specs=[pl.BlockSpec((2, n_k2, n, FN_W), lambda j: (0, j, 0, 0)),
                  pl.BlockSpec((n, n), lambda j: (0, 0)),
                  pl.BlockSpec((n, n), lambda j: (0, 0))],
        out_specs=pl.BlockSpec((n, n_k2 * FN_W), lambda j: (0, j)),
        out_shape=jax.ShapeDtypeStruct((T_ALL // n, row_w), BF16),
        compiler_params=_cparams(("arbitrary",)),
        name="fnet_stage2",
    )(y.reshape(2, n, n, FN_W), fc["c2"], fc["s2"])
    o = o.reshape(T_ALL, FN_W)
    if not with_ctx:
        return o
    cb = T_LAT // T_CTX
    return pl.pallas_call(
        _fft_ctx_kernel,
        grid=(1,),
        in_specs=[pl.BlockSpec((T_CTX, FN_W), lambda j: (cb, 0)),
                  pl.BlockSpec((n, 2 * n), lambda j: (0, 0)),
                  pl.BlockSpec((T_CTX, T_CTX), lambda j: (0, 0)),
                  pl.BlockSpec((T_CTX, T_CTX), lambda j: (0, 0)),
                  pl.BlockSpec(memory_space=pl.ANY)],
        out_specs=pl.BlockSpec((T_CTX, FN_W), lambda j: (cb, 0)),
        out_shape=jax.ShapeDtypeStruct((T_ALL, FN_W), BF16),
        input_output_aliases={4: 0},
        compiler_params=_cparams(("arbitrary",)),
        name="fnet_ctx",
    )(u, fc["cs_ch"], fc["cc"], fc["sc"], o)


def _merge_kernel(x_ref, mod_ref, gl_ref, ona_ref, of_ref, ob_ref, z_ref, ofn_ref, dnn_ref,
                  wna_ref, wdn_ref, wfn_ref, wout_ref, o_ref, *, tm, has_ctx):
    i = pl.program_id(0)
    g = jax.nn.sigmoid(gl_ref[...].astype(F32))
    o = of_ref[...].astype(F32) + ob_ref[...].astype(F32)
    z = z_ref[...].astype(F32)
    parts = []
    for h in range(DN_HEADS):
        sl = slice(h * DN_DH, (h + 1) * DN_DH)
        oh = o[:, sl]
        oh = oh * lax.rsqrt(jnp.mean(oh * oh, axis=-1, keepdims=True) + EPS) * dnn_ref[...]
        parts.append(oh * _silu(z[:, sl]))
    odn = jnp.concatenate(parts, axis=-1).astype(BF16)
    y = (g[:, :D] * _dot(ona_ref[...], wna_ref[...])
         + g[:, D:2 * D] * _dot(odn, wdn_ref[...])
         + g[:, 2 * D:] * _dot(ofn_ref[...], wfn_ref[...]))
    y = _dot(y.astype(BF16), wout_ref[...])
    gate = _mod_rows(mod_ref, 2, i * tm, tm, has_ctx)
    o_ref[...] = x_ref[...] + gate * y


def _merge(xs, mod, p, o_na, o_f, o_b, o_fn, dn_norm, w_na_o, w_dn_o, w_fn, w_out, has_ctx):
    rows, tm = (T_ALL, TM_MERGE_ALL) if has_ctx else (T_LAT, TM_MERGE_LAT)
    row = lambda w: pl.BlockSpec((tm, w), lambda i: (i, 0))
    full = lambda a: pl.BlockSpec(a.shape, lambda i: (0, 0))
    return pl.pallas_call(
        functools.partial(_merge_kernel, tm=tm, has_ctx=has_ctx),
        grid=(rows // tm,),
        in_specs=[row(D), pl.BlockSpec((8, 6 * D), lambda i: (0, 0)),
                  pl.BlockSpec((tm, 3 * D), lambda i: (i, P_GATE)),
                  row(NA_W), row(DN_W), row(DN_W),
                  pl.BlockSpec((tm, DN_W), lambda i: (i, P_Z)),
                  row(FN_W), pl.BlockSpec((1, DN_DH), lambda i: (0, 0)),
                  full(w_na_o), full(w_dn_o), full(w_fn), full(w_out)],
        out_specs=row(D),
        out_shape=jax.ShapeDtypeStruct((rows, D), F32),
        compiler_params=_cparams(("arbitrary",)),
        name="merge",
    )(xs, mod, p, o_na, o_f, o_b, p, o_fn, dn_norm.reshape(1, DN_DH), w_na_o, w_dn_o, w_fn, w_out)


def _mlp_kernel(x_ref, mod_ref, n_ref, w1_ref, w2_ref, nf_ref, o_ref, h_scr, acc_scr, *, tm, has_ctx, final):
    i = pl.program_id(0)
    j = pl.program_id(1)

    @pl.when(j == 0)
    def _():
        shift = _mod_rows(mod_ref, 3, i * tm, tm, has_ctx)
        scale = _mod_rows(mod_ref, 4, i * tm, tm, has_ctx)
        h_scr[...] = _modnorm(x_ref[...], n_ref[...], shift, scale).astype(BF16)
        acc_scr[...] = jnp.zeros_like(acc_scr)

    a = jnp.maximum(_dot(h_scr[...], w1_ref[...]), 0.0)
    acc_scr[...] += _dot((a * a).astype(BF16), w2_ref[...])

    @pl.when(j == pl.num_programs(1) - 1)
    def _():
        gate = _mod_rows(mod_ref, 5, i * tm, tm, has_ctx)
        xn = x_ref[...] + gate * acc_scr[...]
        if final:
            xn = xn * lax.rsqrt(jnp.mean(xn * xn, axis=-1, keepdims=True) + EPS) * nf_ref[...]
        o_ref[...] = xn


def _mlp(xs, mod, norm, w1, w2, norm_f, has_ctx, final):
    rows, tm = (T_ALL, TM_ALL) if has_ctx else (T_LAT, TM_LAT)
    th = 1024
    return pl.pallas_call(
        functools.partial(_mlp_kernel, tm=tm, has_ctx=has_ctx, final=final),
        grid=(rows // tm, HID // th),
        in_specs=[pl.BlockSpec((tm, D), lambda i, j: (i, 0)),
                  pl.BlockSpec((8, 6 * D), lambda i, j: (0, 0)),
                  pl.BlockSpec((1, D), lambda i, j: (0, 0)),
                  pl.BlockSpec((D, th), lambda i, j: (0, j)),
                  pl.BlockSpec((th, D), lambda i, j: (j, 0)),
                  pl.BlockSpec((1, D), lambda i, j: (0, 0))],
        out_specs=pl.BlockSpec((tm, D), lambda i, j: (i, 0)),
        out_shape=jax.ShapeDtypeStruct((rows, D), F32),
        scratch_shapes=[pltpu.VMEM((tm, D), BF16), pltpu.VMEM((tm, D), F32)],
        compiler_params=_cparams(("arbitrary", "arbitrary")),
        name="mlp",
    )(xs, mod, norm.reshape(1, D), w1, w2, norm_f.reshape(1, D))


def _split_in_weights(w_in):
    na_k, na_v, dn_k, dn_v, dn_ab, na_q, dn_q, dn_z, fn_u, gate = jnp.split(
        w_in, np.cumsum([NA_W, NA_W, DN_W, DN_W, 4 * DN_HEADS, NA_W, DN_W, DN_W, FN_W]).tolist(), axis=-1)
    wcat = jnp.concatenate([gate, na_q, na_k, na_v, dn_q, dn_k, dn_v, dn_z, fn_u], axis=-1).astype(BF16)
    wg = jnp.pad(dn_ab, ((0, 0), (0, GP - 4 * DN_HEADS))).astype(BF16)
    return wcat, wg


def kernel(x, c, ctx, c_ctx, w_ada, b_ada, norm1, w_in, conv_w, a_log, dt_bias, dn_norm, rpb,
           w_na_o, w_dn_o, w_fn, w_out, norm2, w_mlp1, w_mlp2, norm_f):
    xs = jnp.concatenate([x[0], ctx[0]], axis=0)
    cc = jnp.concatenate([c, c_ctx[None, :], jnp.zeros((6, D), F32)], axis=0)
    mods = _ada(cc, w_ada, b_ada)
    rope = _rope_tables()
    fc = _fft_consts()
    for l in range(DEPTH):
        has_ctx = l < DEPTH - 1
        final = l == DEPTH - 1
        wcat, wg = _split_in_weights(w_in[l])
        u, p, g = _in_proj(xs, mods[l], norm1[l], wcat, wg)
        o_na = _na(p, _na_tables(rpb[l]), has_ctx)
        qkv, gb = _dn_prep(p, g, conv_w[l], a_log[l], dt_bias[l], rope)
        o_f, o_b = _dn_scan(*_dn_wy(qkv, gb, DN_BLK_CHUNKS), DN_BLK_CHUNKS)
        o_fn = _fnet(u, fc, has_ctx)
        xs = _merge(xs, mods[l], p, o_na, o_f, o_b, o_fn, dn_norm[l],
                    w_na_o[l].astype(BF16), w_dn_o[l].astype(BF16), w_fn[l].astype(BF16), w_out[l].astype(BF16),
                    has_ctx)
        xs = _mlp(xs, mods[l], norm2[l], w_mlp1[l].astype(BF16), w_mlp2[l].astype(BF16), norm_f, has_ctx, final)
    return xs[None]
```

```python
import functools
import math

import numpy as np
import jax
import jax.numpy as jnp
from jax import lax
from jax.experimental import pallas as pl
from jax.experimental.pallas import tpu as pltpu

F32 = jnp.float32
BF16 = jnp.bfloat16

D = 1024
T_LAT = 16384
T_CTX = 256
T_ALL = T_LAT + T_CTX
DEPTH = 2
GRID_W = 64
GRID_H = T_LAT // GRID_W
NA_HEADS = 8
NA_DH = 64
NA_WIN_H = 8
NA_WIN_W = 16
NA_W = NA_HEADS * NA_DH
DN_HEADS = 4
DN_DH = 128
DN_W = DN_HEADS * DN_DH
DN_CONV = 5
CHUNK = 64
FN_GROUPS = 4
FN_DG = 128
FN_W = FN_GROUPS * FN_DG
HID = 4 * D
ROPE_BASE = 10000.0
EPS = 1e-6
NEG = -1e30

P_W = 3 * NA_W + 4 * DN_W + FN_W
P_NAQ, P_NAK, P_NAV = 0, 1, 2
P_DN = 1
P_Z = 6
TN_IN = P_W // 2
GP = 128

TM_ALL = 1280
TM_LAT = 1024
TM_MERGE_ALL = 640
TM_MERGE_LAT = 512
NA_BLK = 4 * GRID_W
NA_HEAD_GROUP = 2
FFT_N = 128
DN_BLK_CHUNKS = 4
WY_GROUP_CHUNKS = 2
VMEM_LIMIT = 56 * 1024 * 1024


def _cparams(sem):
    return pltpu.CompilerParams(dimension_semantics=sem, vmem_limit_bytes=VMEM_LIMIT)


def _dot(a, b):
    return jnp.dot(a, b, preferred_element_type=F32)


def _dot_nt(a, b):
    return lax.dot_general(a, b, (((1,), (1,)), ((), ())), preferred_element_type=F32)


def _dot_tn(a, b):
    return lax.dot_general(a, b, (((0,), (0,)), ((), ())), preferred_element_type=F32)


def _silu(x):
    return x * jax.nn.sigmoid(x)


def _ada_kernel(c_ref, w_ref, b_ref, o_ref):
    s = _silu(c_ref[...])
    o_ref[0] = jnp.dot(s, w_ref[0], preferred_element_type=F32, precision=lax.Precision.HIGHEST) + b_ref[0]


def _ada(cc, w_ada, b_ada):
    tn = 1536
    return pl.pallas_call(
        _ada_kernel,
        grid=(DEPTH, 6 * D // tn),
        in_specs=[pl.BlockSpec((8, D), lambda l, j: (0, 0)),
                  pl.BlockSpec((1, D, tn), lambda l, j: (l, 0, j)),
                  pl.BlockSpec((1, 1, tn), lambda l, j: (l, 0, j))],
        out_specs=pl.BlockSpec((1, 8, tn), lambda l, j: (l, 0, j)),
        out_shape=jax.ShapeDtypeStruct((DEPTH, 8, 6 * D), F32),
        compiler_params=_cparams(("arbitrary", "arbitrary")),
        name="ada_mod",
    )(cc, w_ada, b_ada.reshape(DEPTH, 1, 6 * D))


def _mod_rows(mod_ref, k, row0, tm, has_ctx):
    lat = mod_ref[0:1, k * D:(k + 1) * D]
    if not has_ctx:
        return lat
    ctx = mod_ref[1:2, k * D:(k + 1) * D]
    rows = row0 + lax.broadcasted_iota(jnp.int32, (tm, 1), 0)
    return jnp.where(rows >= T_LAT, ctx, lat)


def _modnorm(x, g, shift, scale):
    y = x * lax.rsqrt(jnp.mean(x * x, axis=-1, keepdims=True) + EPS) * g
    return y * (1.0 + scale) + shift


def _in_proj_kernel(x_ref, mod_ref, n_ref, w_ref, wg_ref, u_ref, p_ref, g_ref, h_scr, *, tm):
    i = pl.program_id(0)
    j = pl.program_id(1)

    @pl.when(j == 0)
    def _():
        shift = _mod_rows(mod_ref, 0, i * tm, tm, True)
        scale = _mod_rows(mod_ref, 1, i * tm, tm, True)
        h = _modnorm(x_ref[...], n_ref[...], shift, scale).astype(BF16)
        h_scr[...] = h
        g_ref[...] = _dot(h, wg_ref[...])

    p = _dot(h_scr[...], w_ref[...]).astype(BF16)
    p_ref[...] = p

    @pl.when(j == pl.num_programs(1) - 1)
    def _():
        u_ref[...] = p[:, TN_IN - FN_W:]


def _in_proj(xs, mod, norm, wcat, wg):
    tm = TM_ALL
    nj = P_W // TN_IN
    return pl.pallas_call(
        functools.partial(_in_proj_kernel, tm=tm),
        grid=(T_ALL // tm, nj),
        in_specs=[pl.BlockSpec((tm, D), lambda i, j: (i, 0)),
                  pl.BlockSpec((8, 6 * D), lambda i, j: (0, 0)),
                  pl.BlockSpec((1, D), lambda i, j: (0, 0)),
                  pl.BlockSpec((D, TN_IN), lambda i, j: (0, j)),
                  pl.BlockSpec((D, GP), lambda i, j: (0, 0))],
        out_specs=[pl.BlockSpec((tm, FN_W), lambda i, j: (i, 0)),
                   pl.BlockSpec((tm, TN_IN), lambda i, j: (i, j)),
                   pl.BlockSpec((tm, GP), lambda i, j: (i, 0))],
        out_shape=[jax.ShapeDtypeStruct((T_ALL, FN_W), BF16),
                   jax.ShapeDtypeStruct((T_ALL, P_W), BF16),
                   jax.ShapeDtypeStruct((T_ALL, GP), F32)],
        scratch_shapes=[pltpu.VMEM((tm, D), BF16)],
        compiler_params=_cparams(("arbitrary", "arbitrary")),
        name="in_proj",
    )(xs, mod, norm.reshape(1, D), wcat, wg)


def _na_tables(rpb):
    qr = np.arange(4)[:, None, None, None]
    qc = np.arange(GRID_W)[None, :, None, None]
    kk = np.arange(12)[None, None, :, None]
    kc = np.arange(GRID_W)[None, None, None, :]
    valid = []
    for b in (0, 1, GRID_H // 4 - 1):
        r = 4 * b + qr
        kr = 4 * (b - 1) + kk
        rs = np.clip(r - NA_WIN_H // 2, 0, GRID_H - NA_WIN_H)
        cs = np.clip(qc - NA_WIN_W // 2, 0, GRID_W - NA_WIN_W)
        ok = (kr >= 0) & (kr < GRID_H) & (kr >= rs) & (kr < rs + NA_WIN_H) & (kc >= cs) & (kc < cs + NA_WIN_W)
        valid.append(np.broadcast_to(ok, (4, GRID_W, 12, GRID_W)).reshape(NA_BLK, 12 * GRID_W))
    valid.append(np.zeros_like(valid[0]))
    valid = np.stack(valid)
    col = np.arange(GRID_W)
    onehot = ((col[None, None, :] - col[None, :, None] + NA_WIN_W - 1)
              == np.arange(2 * NA_WIN_W - 1)[:, None, None]).astype(np.float32)
    tcol = jnp.einsum('hrd,dqk->hrqk', rpb, onehot, precision=lax.Precision.HIGHEST)
    bias = jnp.stack([jnp.stack([tcol[:, k - q + NA_WIN_H - 5] for k in range(12)], axis=2) for q in range(4)], axis=1)
    bias = bias.reshape(NA_HEADS, NA_BLK, 12 * GRID_W)
    return jnp.where(valid[:, None], bias[None], NEG).astype(BF16)


def _na_kernel(q_ref, k0_ref, k1_ref, k2_ref, v0_ref, v1_ref, v2_ref, kc_ref, vc_ref, tab_ref, o_ref):
    q = q_ref[...]
    kl = jnp.concatenate([k0_ref[...], k1_ref[...], k2_ref[...]], axis=0)
    vl = jnp.concatenate([v0_ref[...], v1_ref[...], v2_ref[...]], axis=0)
    kc = kc_ref[...]
    vc = vc_ref[...]
    outs = []
    for h0 in range(0, NA_HEADS, NA_HEAD_GROUP):
        heads = range(h0, h0 + NA_HEAD_GROUP)
        sls = [slice(h * NA_DH, (h + 1) * NA_DH) for h in heads]
        qs = [q[:, sl] * (NA_DH ** -0.5) for sl in sls]
        s_loc = [_dot_nt(qh, kl[:, sl]) + tab_ref[0, h].astype(F32) for h, qh, sl in zip(heads, qs, sls)]
        s_ctx = [_dot_nt(qh, kc[:, sl]) for qh, sl in zip(qs, sls)]
        m = [jnp.maximum(jnp.max(a, axis=-1, keepdims=True), jnp.max(b, axis=-1, keepdims=True))
             for a, b in zip(s_loc, s_ctx)]
        p_loc = [jnp.exp(a - mm) for a, mm in zip(s_loc, m)]
        p_ctx = [jnp.exp(b - mm) for b, mm in zip(s_ctx, m)]
        denom = [jnp.sum(a, axis=-1, keepdims=True) + jnp.sum(b, axis=-1, keepdims=True) for a, b in zip(p_loc, p_ctx)]
        o = [_dot(a.astype(BF16), vl[:, sl]) + _dot(b.astype(BF16), vc[:, sl]) for a, b, sl in zip(p_loc, p_ctx, sls)]
        outs += [oo / dd for oo, dd in zip(o, denom)]
    o_ref[...] = jnp.concatenate(outs, axis=-1).astype(BF16)


def _na(p, tab, with_ctx):
    nlat = T_LAT // NA_BLK
    nb = nlat + (1 if with_ctx else 0)
    ctx_blk = T_LAT // NA_BLK

    def kidx(b, part):
        return jnp.clip(jnp.minimum(b, nlat - 1) - 1 + part, 0, nlat - 1)

    def variant(b):
        return jnp.where(b == 0, 0, jnp.where(b == nlat - 1, 2, jnp.where(b >= nlat, 3, 1)))

    blk = (NA_BLK, NA_W)
    in_specs = [pl.BlockSpec(blk, lambda b: (b, P_NAQ))]
    in_specs += [pl.BlockSpec(blk, functools.partial(lambda b, part: (kidx(b, part), P_NAK), part=part))
                 for part in range(3)]
    in_specs += [pl.BlockSpec(blk, functools.partial(lambda b, part: (kidx(b, part), P_NAV), part=part))
                 for part in range(3)]
    in_specs += [pl.BlockSpec(blk, lambda b: (ctx_blk, P_NAK)),
                 pl.BlockSpec(blk, lambda b: (ctx_blk, P_NAV)),
                 pl.BlockSpec((1, NA_HEADS, NA_BLK, 3 * NA_BLK), lambda b: (variant(b), 0, 0, 0))]
    return pl.pallas_call(
        _na_kernel,
        grid=(nb,),
        in_specs=in_specs,
        out_specs=pl.BlockSpec(blk, lambda b: (b, 0)),
        out_shape=jax.ShapeDtypeStruct((nb * NA_BLK, NA_W), BF16),
        compiler_params=_cparams(("arbitrary",)),
        name="na_attn",
    )(p, p, p, p, p, p, p, p, p, tab)


def _rope_tables():
    t = jnp.arange(T_LAT, dtype=jnp.int32)
    half = DN_DH // 2
    pos = jnp.stack([t // GRID_W, t % GRID_W], axis=-1).astype(F32)
    inv = 1.0 / (ROPE_BASE ** (jnp.arange(0, half, 2, dtype=F32) / half))
    ang = pos[:, :, None] * inv[None, None, :]
    ang = jnp.concatenate([ang, ang], axis=-1).reshape(T_LAT, DN_DH)
    cos, sin = jnp.cos(ang), jnp.sin(ang)
    low = (np.arange(DN_DH) % half) < half // 2
    return cos, jnp.where(low[None], -sin, 0.0), jnp.where(low[None], 0.0, sin)


def _dn_prep_kernel(x_ref, prev_ref, next_ref, g_ref, cw_ref, alog_ref, dtb_ref, cos_ref, sa_ref, sb_ref,
                    qkv_ref, gb_ref, *, tm):
    b = pl.program_id(0)
    nlat = T_LAT // tm
    first = jnp.logical_or(b == 0, b == nlat)
    last = b >= nlat - 1
    is_ctx = b >= nlat
    cur = x_ref[...].astype(F32)
    prev = jnp.where(first, 0.0, prev_ref[...][14:16].astype(F32))
    nxt = jnp.where(last, 0.0, next_ref[...][0:2].astype(F32))
    ext = jnp.concatenate([prev, cur, nxt], axis=0)
    cw = cw_ref[...]
    y = ext[0:tm] * cw[0:1]
    for k in range(1, DN_CONV):
        y = y + ext[k:k + tm] * cw[k:k + 1]
    y = _silu(y)
    cos = jnp.where(is_ctx, 1.0, cos_ref[...])
    sa = jnp.where(is_ctx, 0.0, sa_ref[...])
    sb = jnp.where(is_ctx, 0.0, sb_ref[...])
    quarter = DN_DH // 4
    for s in range(2):
        for h in range(DN_HEADS):
            c0 = s * DN_W + h * DN_DH
            xh = y[:, c0:c0 + DN_DH]
            xh = xh * lax.rsqrt(jnp.sum(xh * xh, axis=-1, keepdims=True) + EPS)
            xh = (xh * cos + pltpu.roll(xh, DN_DH - quarter, axis=1) * sa + pltpu.roll(xh, quarter, axis=1) * sb)
            if s == 0:
                xh = xh * (DN_DH ** -0.5)
            qkv_ref[:, c0:c0 + DN_DH] = xh.astype(BF16)
    qkv_ref[:, 2 * DN_W:] = y[:, 2 * DN_W:].astype(BF16)
    g = g_ref[...]
    lane = lax.broadcasted_iota(jnp.int32, g.shape, 1)
    decay = -jnp.exp(alog_ref[...]) * jax.nn.softplus(g + dtb_ref[...])
    gb_ref[...] = jnp.where(lane < 2 * DN_HEADS, decay, jax.nn.sigmoid(g))


def _dn_prep(p, g, conv_w, a_log, dt_bias, rope):
    tm = 256
    nb = T_ALL // tm
    nlat = T_LAT // tm
    hb = 16
    nh = T_ALL // hb
    pad = lambda v: jnp.pad(v.reshape(1, 2 * DN_HEADS).astype(F32), ((0, 0), (0, GP - 2 * DN_HEADS)))
    rspec = pl.BlockSpec((tm, DN_DH), lambda b: (jnp.minimum(b, nlat - 1), 0))
    return pl.pallas_call(
        functools.partial(_dn_prep_kernel, tm=tm),
        grid=(nb,),
        in_specs=[pl.BlockSpec((tm, 3 * DN_W), lambda b: (b, P_DN)),
                  pl.BlockSpec((hb, 3 * DN_W), lambda b: (jnp.maximum(b * (tm // hb) - 1, 0), P_DN)),
                  pl.BlockSpec((hb, 3 * DN_W), lambda b: (jnp.minimum((b + 1) * (tm // hb), nh - 1), P_DN)),
                  pl.BlockSpec((tm, GP), lambda b: (b, 0)),
                  pl.BlockSpec((DN_CONV, 3 * DN_W), lambda b: (0, 0)),
                  pl.BlockSpec((1, GP), lambda b: (0, 0)),
                  pl.BlockSpec((1, GP), lambda b: (0, 0)),
                  rspec, rspec, rspec],
        out_specs=[pl.BlockSpec((tm, 3 * DN_W), lambda b: (b, 0)),
                   pl.BlockSpec((tm, GP), lambda b: (b, 0))],
        out_shape=[jax.ShapeDtypeStruct((T_ALL, 3 * DN_W), BF16),
                   jax.ShapeDtypeStruct((T_ALL, GP), F32)],
        compiler_params=_cparams(("arbitrary",)),
        name="dn_prep",
    )(p, p, p, g, conv_w, pad(a_log), pad(dt_bias), *rope)


def _unit_tri_inverses_minus_identity(mats):
    c = mats[0].shape[0]
    ri = lax.broadcasted_iota(jnp.int32, (c, c), 0)
    ci = lax.broadcasted_iota(jnp.int32, (c, c), 1)

    def same_block(s):
        return (ri // s) == (ci // s)

    def mm(xs, ys):
        return [_dot(x.astype(BF16), y.astype(BF16)) for x, y in zip(xs, ys)]

    diag = same_block(8)
    b1 = [jnp.where(diag, -a, 0.0) for a in mats]
    b2 = mm(b1, b1)
    b3 = mm(b1, b2)
    b4 = mm(b2, b2)
    n2 = [x + y + z for x, y, z in zip(b1, b2, b3)]
    n = [x + y + z for x, y, z in zip(n2, b4, mm(n2, b4))]
    for s in (8, 16, 32):
        off = jnp.logical_and(same_block(2 * s), jnp.logical_not(same_block(s)))
        lo = [jnp.where(off, a, 0.0) for a in mats]
        x = [p + q for p, q in zip(lo, mm(n, lo))]
        n = [p - (q + r) for p, q, r in zip(n, x, mm(x, n))]
    return n


def _dn_wy_kernel(qkv_ref, gb_ref, u_ref, wq_ref, kq_ref, dl_ref, *, nchunk):
    c = CHUNK
    tm = nchunk * c
    gb = gb_ref[...]
    qkv = qkv_ref[...]
    rt = lax.broadcasted_iota(jnp.int32, (tm, tm), 0)
    ct = lax.broadcasted_iota(jnp.int32, (tm, tm), 1)
    same_chunk = (rt // c) == (ct // c)
    ri = lax.broadcasted_iota(jnp.int32, (c, c), 0)
    ci = lax.broadcasted_iota(jnp.int32, (c, c), 1)
    lane = lax.broadcasted_iota(jnp.int32, (8, GP), 1)
    dl_rows, gcs = [], []
    for d in range(2):
        cum = jnp.logical_and(same_chunk, (ct <= rt) if d == 0 else (ct >= rt))
        gc_all = jnp.dot(cum.astype(F32), gb, preferred_element_type=F32, precision=lax.Precision.HIGHEST)
        gcs.append((gc_all, gc_all.T))
        dl_rows.append([jnp.exp(gc_all[j * c + (c - 1 if d == 0 else 0):j * c + (c if d == 0 else 1), :])
                        for j in range(nchunk)])
    for j0 in range(0, nchunk, WY_GROUP_CHUNKS):
        chains = [(d, j, h) for j in range(j0, j0 + WY_GROUP_CHUNKS) for d in range(2) for h in range(DN_HEADS)]
        pre = []
        for d, j, h in chains:
            gc_all, gc_all_t = gcs[d]
            incl = (ci <= ri) if d == 0 else (ci >= ri)
            strict = (ci < ri) if d == 0 else (ci > ri)
            r0 = j * c
            last = r0 + (c - 1 if d == 0 else 0)
            idx = d * DN_HEADS + h
            q = qkv[r0:r0 + c, h * DN_DH:(h + 1) * DN_DH]
            k = qkv[r0:r0 + c, DN_W + h * DN_DH:DN_W + (h + 1) * DN_DH]
            v = qkv[r0:r0 + c, 2 * DN_W + h * DN_DH:2 * DN_W + (h + 1) * DN_DH].astype(F32)
            beta = gb[r0:r0 + c, 2 * DN_HEADS + idx:2 * DN_HEADS + idx + 1]
            gc = gc_all[r0:r0 + c, idx:idx + 1]
            gc_row = gc_all_t[idx:idx + 1, r0:r0 + c]
            g_last = gc_all[last:last + 1, idx:idx + 1]
            e = jnp.exp(jnp.where(incl, gc - gc_row, 0.0))
            eg = jnp.exp(gc)
            kf = k.astype(F32)
            kb = kf * beta
            pre.append(dict(q=q, k=k, beta=beta, eg=eg, e_incl=jnp.where(incl, e, 0.0),
                            e_strict=jnp.where(strict, e, 0.0),
                            rhs=jnp.concatenate([v * beta, kb * eg], axis=1),
                            k_tail=kf * jnp.exp(g_last - gc)))
        gram = {(j, h): _dot_nt(jnp.concatenate([p["k"], p["q"]], axis=0), p["k"])
                for (d, j, h), p in zip(chains, pre) if d == 0}
        aq = [gram[(j, h)] for d, j, h in chains]
        ns = _unit_tri_inverses_minus_identity([p["beta"] * x[:c] * p["e_strict"] for x, p in zip(aq, pre)])
        uws = [p["rhs"] + _dot(n.astype(BF16), p["rhs"].astype(BF16)) for n, p in zip(ns, pre)]
        for (d, j, h), p, x, uw in zip(chains, pre, aq, uws):
            r0 = j * c
            idx = d * DN_HEADS + h
            cs = slice(idx * DN_DH, (idx + 1) * DN_DH)
            ks = slice(idx * c, (idx + 1) * c)
            u_ref[r0:r0 + c, cs] = uw[:, :DN_DH].astype(BF16)
            wq_ref[2 * r0:2 * r0 + c, cs] = uw[:, DN_DH:].astype(BF16)
            wq_ref[2 * r0 + c:2 * r0 + 2 * c, cs] = (p["q"].astype(F32) * p["eg"]).astype(BF16)
            kq_ref[3 * r0:3 * r0 + DN_DH, ks] = p["k_tail"].T.astype(BF16)
            kq_ref[3 * r0 + DN_DH:3 * r0 + DN_DH + c, ks] = (x[c:] * p["e_incl"]).astype(BF16)
    for j in range(nchunk):
        row = jnp.where(lane < DN_HEADS, dl_rows[0][j], dl_rows[1][j])
        dl_ref[8 * j:8 * j + 8, :] = row


def _dn_wy(qkv, gb, nchunk):
    tm = nchunk * CHUNK
    nb = T_ALL // tm
    nch = 2 * DN_HEADS
    return pl.pallas_call(
        functools.partial(_dn_wy_kernel, nchunk=nchunk),
        grid=(nb,),
        in_specs=[pl.BlockSpec((tm, 3 * DN_W), lambda i: (i, 0)),
                  pl.BlockSpec((tm, GP), lambda i: (i, 0))],
        out_specs=[pl.BlockSpec((tm, nch * DN_DH), lambda i: (i, 0)),
                   pl.BlockSpec((2 * tm, nch * DN_DH), lambda i: (i, 0)),
                   pl.BlockSpec((3 * tm, nch * CHUNK), lambda i: (i, 0)),
                   pl.BlockSpec((8 * nchunk, GP), lambda i: (i, 0))],
        out_shape=[jax.ShapeDtypeStruct((T_ALL, nch * DN_DH), BF16),
                   jax.ShapeDtypeStruct((2 * T_ALL, nch * DN_DH), BF16),
                   jax.ShapeDtypeStruct((3 * T_ALL, nch * CHUNK), BF16),
                   jax.ShapeDtypeStruct((8 * T_ALL // CHUNK, GP), F32)],
        compiler_params=_cparams(("arbitrary",)),
        name="dn_wy",
    )(qkv, gb)


def _dn_scan_kernel(u_f_ref, wq_f_ref, kq_f_ref, dl_f_ref, u_b_ref, wq_b_ref, kq_b_ref, dl_b_ref,
                    of_ref, ob_ref, s_scr, *, nchunk):
    i = pl.program_id(0)

    @pl.when(i == 0)
    def _():
        s_scr[...] = jnp.zeros_like(s_scr)

    c = CHUNK
    states = [s_scr[idx] for idx in range(2 * DN_HEADS)]
    zero = jnp.zeros((c, DN_DH), BF16)
    refs = ((u_f_ref, wq_f_ref, kq_f_ref, dl_f_ref, of_ref), (u_b_ref, wq_b_ref, kq_b_ref, dl_b_ref, ob_ref))
    chains = [(d, h) for d in range(2) for h in range(DN_HEADS)]
    pairs = [(d, p) for d in range(2) for p in range(DN_HEADS // 2)]
    for step in range(nchunk):
        blk = lambda d: step if d == 0 else nchunk - 1 - step
        hs = lambda h: slice(h * DN_DH, (h + 1) * DN_DH)
        r1 = [_dot(refs[d][1][2 * blk(d) * c:2 * (blk(d) + 1) * c, hs(h)], states[d * DN_HEADS + h].astype(BF16))
              for d, h in chains]
        v_new = [(refs[d][0][blk(d) * c:(blk(d) + 1) * c, hs(h)].astype(F32) - r[:c]).astype(BF16)
                 for (d, h), r in zip(chains, r1)]
        r2 = [_dot(refs[d][2][3 * blk(d) * c:3 * (blk(d) + 1) * c, p * 2 * c:(p + 1) * 2 * c],
                   jnp.concatenate([jnp.concatenate([v_new[d * DN_HEADS + 2 * p], zero], axis=1),
                                    jnp.concatenate([zero, v_new[d * DN_HEADS + 2 * p + 1]], axis=1)], axis=0))
              for d, p in pairs]
        for d, h in chains:
            idx = d * DN_HEADS + h
            r = r2[d * (DN_HEADS // 2) + h // 2]
            ts = slice((h % 2) * DN_DH, (h % 2 + 1) * DN_DH)
            dl_row = refs[d][3][8 * blk(d):8 * blk(d) + 1, :]
            states[idx] = states[idx] * dl_row[:, idx:idx + 1] + r[:DN_DH, ts]
            refs[d][4][blk(d) * c:(blk(d) + 1) * c, hs(h)] = (r1[idx][c:] + r[DN_DH:, ts]).astype(BF16)
    for idx in range(2 * DN_HEADS):
        s_scr[idx] = states[idx]


def _dn_scan(u, wq, kq, dl, nchunk):
    tm = nchunk * CHUNK
    nb = T_ALL // tm
    nlat = T_LAT // tm
    fwd = lambda i: jnp.where(i == 0, nlat, i - 1)
    bwd = lambda i: nb - 1 - i
    specs = lambda f, col: [pl.BlockSpec((tm, DN_W), lambda i: (f(i), col)),
                            pl.BlockSpec((2 * tm, DN_W), lambda i: (f(i), col)),
                            pl.BlockSpec((3 * tm, DN_HEADS * CHUNK), lambda i: (f(i), col)),
                            pl.BlockSpec((8 * nchunk, GP), lambda i: (f(i), 0))]
    return pl.pallas_call(
        functools.partial(_dn_scan_kernel, nchunk=nchunk),
        grid=(nb,),
        in_specs=specs(fwd, 0) + specs(bwd, 1),
        out_specs=[pl.BlockSpec((tm, DN_W), lambda i: (fwd(i), 0)),
                   pl.BlockSpec((tm, DN_W), lambda i: (bwd(i), 0))],
        out_shape=[jax.ShapeDtypeStruct((T_ALL, DN_W), BF16),
                   jax.ShapeDtypeStruct((T_ALL, DN_W), BF16)],
        scratch_shapes=[pltpu.VMEM((2 * DN_HEADS, DN_DH, DN_DH), F32)],
        compiler_params=_cparams(("arbitrary",)),
        name="dn_scan",
    )(u, wq, kq, dl, u, wq, kq, dl)


def _dft_cos_sin(n):
    k = np.arange(n)
    ang = 2.0 * np.pi * ((k[:, None] * k[None, :]) % n) / n
    return np.cos(ang), np.sin(ang)


def _fft_consts():
    n = FFT_N
    c, s = _dft_cos_sin(n)
    cs_ch = np.concatenate([c, s], axis=1)
    w1 = np.block([[c, -s], [-s, -c]])
    k2 = np.arange(n)[None, :, None]
    t2 = np.arange(n)[:, None, None]
    ang = 2.0 * np.pi * ((k2 * t2) % (n * n)) / (n * n)
    scale = 1.0 / math.sqrt(T_LAT * FN_DG)
    cc, sc = _dft_cos_sin(T_CTX)
    scale_c = 1.0 / math.sqrt(T_CTX * FN_DG)
    f32 = lambda a: jnp.asarray(a, F32)
    bf = lambda a: f32(a).astype(BF16)
    return dict(cs_ch=bf(cs_ch), w1=bf(w1), twc=f32(np.cos(ang)), tws=f32(np.sin(ang)),
                c2=bf(c * scale), s2=bf(s * scale), cc=bf(cc * scale_c), sc=bf(sc * scale_c))


def _fft1_kernel(u_ref, cs_ref, w1_ref, twc_ref, tws_ref, y_ref, *, n_t2):
    n = FFT_N
    cs = cs_ref[...]
    w1 = w1_ref[...]
    for t in range(n_t2):
        twc = twc_ref[t]
        tws = tws_ref[t]
        for g in range(0, FN_GROUPS, 2):
            c0 = (t * FN_GROUPS + g) * FN_DG
            ab = [_dot(u_ref[:, c0 + s * FN_DG:c0 + (s + 1) * FN_DG], cs) for s in range(2)]
            rhs = jnp.concatenate([jnp.concatenate([ab[0][:, :FN_DG], ab[1][:, :FN_DG]], axis=1),
                                   jnp.concatenate([ab[0][:, FN_DG:], ab[1][:, FN_DG:]], axis=1)], axis=0)
            y = _dot(w1, rhs.astype(BF16))
            yr, yi = y[:n], y[n:]
            y_ref[0:n, c0:c0 + 2 * FN_DG] = (yr * twc + yi * tws).astype(BF16)
            y_ref[n:2 * n, c0:c0 + 2 * FN_DG] = (yi * twc - yr * tws).astype(BF16)


def _fft2_kernel(y_ref, c2_ref, s2_ref, o_ref, *, n_k2):
    c2 = c2_ref[...]
    s2 = s2_ref[...]
    for j in range(n_k2):
        o_ref[:, j * FN_W:(j + 1) * FN_W] = (_dot(c2, y_ref[0, j]) + _dot(s2, y_ref[1, j])).astype(BF16)


def _fft_ctx_kernel(u_ref, cs_ref, cc_ref, sc_ref, o_in_ref, o_ref):
    del o_in_ref
    for g in range(FN_GROUPS):
        ab = _dot(u_ref[:, g * FN_DG:(g + 1) * FN_DG], cs_ref[...])
        a = ab[:, :FN_DG].astype(BF16)
        b = ab[:, FN_DG:].astype(BF16)
        o_ref[:, g * FN_DG:(g + 1) * FN_DG] = (_dot(cc_ref[...], a) - _dot(sc_ref[...], b)).astype(BF16)


def _fnet(u, fc, with_ctx):
    n = FFT_N
    row_w = n * FN_W
    n_t2 = 8
    tc = n_t2 * FN_W
    y = pl.pallas_call(
        functools.partial(_fft1_kernel, n_t2=n_t2),
        grid=(row_w // tc,),
        in_specs=[pl.BlockSpec((n, tc), lambda j: (0, j)),
                  pl.BlockSpec((n, 2 * n), lambda j: (0, 0)),
                  pl.BlockSpec((2 * n, 2 * n), lambda j: (0, 0)),
                  pl.BlockSpec((n_t2, n, 1), lambda j: (j, 0, 0)),
                  pl.BlockSpec((n_t2, n, 1), lambda j: (j, 0, 0))],
        out_specs=pl.BlockSpec((2 * n, tc), lambda j: (0, j)),
        out_shape=jax.ShapeDtypeStruct((2 * n, row_w), BF16),
        compiler_params=_cparams(("arbitrary",)),
        name="fnet_stage1",
    )(u.reshape(T_ALL // n, row_w), fc["cs_ch"], fc["w1"], fc["twc"], fc["tws"])
    n_k2 = 8
    o = pl.pallas_call(
        functools.partial(_fft2_kernel, n_k2=n_k2),
        grid=(n // n_k2,),
        in_specs=[pl.BlockSpec((2, n_k2, n, FN_W), lambda j: (0, j, 0, 0)),
                  pl.BlockSpec((n, n), lambda j: (0, 0)),
                  pl.BlockSpec((n, n), lambda j: (0, 0))],
        out_specs=pl.BlockSpec((n, n_k2 * FN_W), lambda j: (0, j)),
        out_shape=jax.ShapeDtypeStruct((T_ALL // n, row_w), BF16),
        compiler_params=_cparams(("arbitrary",)),
        name="fnet_stage2",
    )(y.reshape(2, n, n, FN_W), fc["c2"], fc["s2"])
    o = o.reshape(T_ALL, FN_W)
    if not with_ctx:
        return o
    cb = T_LAT // T_CTX
    return pl.pallas_call(
        _fft_ctx_kernel,
        grid=(1,),
        in_specs=[pl.BlockSpec((T_CTX, FN_W), lambda j: (cb, 0)),
                  pl.BlockSpec((n, 2 * n), lambda j: (0, 0)),
                  pl.BlockSpec((T_CTX, T_CTX), lambda j: (0, 0)),
                  pl.BlockSpec((T_CTX, T_CTX), lambda j: (0, 0)),
                  pl.BlockSpec(memory_space=pl.ANY)],
        out_specs=pl.BlockSpec((T_CTX, FN_W), lambda j: (cb, 0)),
        out_shape=jax.ShapeDtypeStruct((T_ALL, FN_W), BF16),
        input_output_aliases={4: 0},
        compiler_params=_cparams(("arbitrary",)),
        name="fnet_ctx",
    )(u, fc["cs_ch"], fc["cc"], fc["sc"], o)


def _merge_kernel(x_ref, mod_ref, n1_ref, ona_ref, of_ref, ob_ref, z_ref, ofn_ref, dnn_ref,
                  wg_ref, wna_ref, wdn_ref, wfn_ref, wout_ref, o_ref, *, tm, has_ctx):
    i = pl.program_id(0)
    x = x_ref[...]
    shift = _mod_rows(mod_ref, 0, i * tm, tm, has_ctx)
    scale = _mod_rows(mod_ref, 1, i * tm, tm, has_ctx)
    h = _modnorm(x, n1_ref[...], shift, scale).astype(BF16)
    o = of_ref[...].astype(F32) + ob_ref[...].astype(F32)
    z = z_ref[...].astype(F32)
    parts = []
    for hd in range(DN_HEADS):
        sl = slice(hd * DN_DH, (hd + 1) * DN_DH)
        oh = o[:, sl]
        oh = oh * lax.rsqrt(jnp.mean(oh * oh, axis=-1, keepdims=True) + EPS) * dnn_ref[...]
        parts.append(oh * _silu(z[:, sl]))
    odn = jnp.concatenate(parts, axis=-1).astype(BF16)
    y = None
    for b, (br, w_ref) in enumerate(((ona_ref[...], wna_ref), (odn, wdn_ref), (ofn_ref[...], wfn_ref))):
        g = jax.nn.sigmoid(_dot(h, wg_ref[:, b * D:(b + 1) * D]))
        t = g * _dot(br, w_ref[...])
        y = t if y is None else y + t
    y = _dot(y.astype(BF16), wout_ref[...])
    gate = _mod_rows(mod_ref, 2, i * tm, tm, has_ctx)
    o_ref[...] = x + gate * y


def _merge(xs, mod, norm1, p, o_na, o_f, o_b, o_fn, dn_norm, w_gate, w_na_o, w_dn_o, w_fn, w_out, has_ctx):
    rows, tm = (T_ALL, TM_MERGE_ALL) if has_ctx else (T_LAT, TM_MERGE_LAT)
    row = lambda w: pl.BlockSpec((tm, w), lambda i: (i, 0))
    full = lambda a: pl.BlockSpec(a.shape, lambda i: (0, 0), pipeline_mode=pl.Buffered(1))
    return pl.pallas_call(
        functools.partial(_merge_kernel, tm=tm, has_ctx=has_ctx),
        grid=(rows // tm,),
        in_specs=[row(D), pl.BlockSpec((8, 6 * D), lambda i: (0, 0)), pl.BlockSpec((1, D), lambda i: (0, 0)),
                  row(NA_W), row(DN_W), row(DN_W),
                  pl.BlockSpec((tm, DN_W), lambda i: (i, P_Z)),
                  row(FN_W), pl.BlockSpec((1, DN_DH), lambda i: (0, 0)),
                  full(w_gate), full(w_na_o), full(w_dn_o), full(w_fn), full(w_out)],
        out_specs=row(D),
        out_shape=jax.ShapeDtypeStruct((rows, D), F32),
        compiler_params=_cparams(("arbitrary",)),
        name="merge",
    )(xs, mod, norm1.reshape(1, D), o_na, o_f, o_b, p, o_fn, dn_norm.reshape(1, DN_DH),
      w_gate, w_na_o, w_dn_o, w_fn, w_out)


def _mlp_kernel(x_ref, mod_ref, n_ref, w1_ref, w2_ref, nf_ref, o_ref, h_scr, acc_scr, *, tm, has_ctx, final):
    i = pl.program_id(0)
    j = pl.program_id(1)

    @pl.when(j == 0)
    def _():
        shift = _mod_rows(mod_ref, 3, i * tm, tm, has_ctx)
        scale = _mod_rows(mod_ref, 4, i * tm, tm, has_ctx)
        h_scr[...] = _modnorm(x_ref[...], n_ref[...], shift, scale).astype(BF16)
        acc_scr[...] = jnp.zeros_like(acc_scr)

    a = jnp.maximum(_dot(h_scr[...], w1_ref[...]), 0.0)
    acc_scr[...] += _dot((a * a).astype(BF16), w2_ref[...])

    @pl.when(j == pl.num_programs(1) - 1)
    def _():
        gate = _mod_rows(mod_ref, 5, i * tm, tm, has_ctx)
        xn = x_ref[...] + gate * acc_scr[...]
        if final:
            xn = xn * lax.rsqrt(jnp.mean(xn * xn, axis=-1, keepdims=True) + EPS) * nf_ref[...]
        o_ref[...] = xn


def _mlp(xs, mod, norm, w1, w2, norm_f, has_ctx, final):
    rows, tm = (T_ALL, TM_ALL) if has_ctx else (T_LAT, TM_LAT)
    th = 1024
    return pl.pallas_call(
        functools.partial(_mlp_kernel, tm=tm, has_ctx=has_ctx, final=final),
        grid=(rows // tm, HID // th),
        in_specs=[pl.BlockSpec((tm, D), lambda i, j: (i, 0)),
                  pl.BlockSpec((8, 6 * D), lambda i, j: (0, 0)),
                  pl.BlockSpec((1, D), lambda i, j: (0, 0)),
                  pl.BlockSpec((D, th), lambda i, j: (0, j)),
                  pl.BlockSpec((th, D), lambda i, j: (j, 0)),
                  pl.BlockSpec((1, D), lambda i, j: (0, 0))],
        out_specs=pl.BlockSpec((tm, D), lambda i, j: (i, 0)),
        out_shape=jax.ShapeDtypeStruct((rows, D), F32),
        scratch_shapes=[pltpu.VMEM((tm, D), BF16), pltpu.VMEM((tm, D), F32)],
        compiler_params=_cparams(("arbitrary", "arbitrary")),
        name="mlp",
    )(xs, mod, norm.reshape(1, D), w1, w2, norm_f.reshape(1, D))


def _split_in_weights(w_in):
    na_k, na_v, dn_k, dn_v, dn_ab, na_q, dn_q, dn_z, fn_u, gate = jnp.split(
        w_in, np.cumsum([NA_W, NA_W, DN_W, DN_W, 4 * DN_HEADS, NA_W, DN_W, DN_W, FN_W]).tolist(), axis=-1)
    wcat = jnp.concatenate([na_q, na_k, na_v, dn_q, dn_k, dn_v, dn_z, fn_u], axis=-1).astype(BF16)
    wg = jnp.pad(dn_ab, ((0, 0), (0, GP - 4 * DN_HEADS))).astype(BF16)
    return wcat, wg, gate.astype(BF16)


def kernel(x, c, ctx, c_ctx, w_ada, b_ada, norm1, w_in, conv_w, a_log, dt_bias, dn_norm, rpb,
           w_na_o, w_dn_o, w_fn, w_out, norm2, w_mlp1, w_mlp2, norm_f):
    xs = jnp.concatenate([x[0], ctx[0]], axis=0)
    cc = jnp.concatenate([c, c_ctx[None, :], jnp.zeros((6, D), F32)], axis=0)
    mods = _ada(cc, w_ada, b_ada)
    rope = _rope_tables()
    fc = _fft_consts()
    for l in range(DEPTH):
        has_ctx = l < DEPTH - 1
        final = l == DEPTH - 1
        wcat, wg, w_gate = _split_in_weights(w_in[l])
        u, p, g = _in_proj(xs, mods[l], norm1[l], wcat, wg)
        o_na = _na(p, _na_tables(rpb[l]), has_ctx)
        qkv, gb = _dn_prep(p, g, conv_w[l], a_log[l], dt_bias[l], rope)
        o_f, o_b = _dn_scan(*_dn_wy(qkv, gb, DN_BLK_CHUNKS), DN_BLK_CHUNKS)
        o_fn = _fnet(u, fc, has_ctx)
        xs = _merge(xs, mods[l], norm1[l], p, o_na, o_f, o_b, o_fn, dn_norm[l], w_gate,
                    w_na_o[l].astype(BF16), w_dn_o[l].astype(BF16), w_fn[l].astype(BF16), w_out[l].astype(BF16),
                    has_ctx)
        xs = _mlp(xs, mods[l], norm2[l], w_mlp1[l].astype(BF16), w_mlp2[l].astype(BF16), norm_f, has_ctx, final)
    return xs[None]
```

```python
import functools
import math

import numpy as np
import jax
import jax.numpy as jnp
from jax import lax
from jax.experimental import pallas as pl
from jax.experimental.pallas import tpu as pltpu

F32 = jnp.float32
BF16 = jnp.bfloat16

D = 1024
T_LAT = 16384
T_CTX = 256
T_ALL = T_LAT + T_CTX
DEPTH = 2
GRID_W = 64
GRID_H = T_LAT // GRID_W
NA_HEADS = 8
NA_DH = 64
NA_WIN_H = 8
NA_WIN_W = 16
NA_W = NA_HEADS * NA_DH
DN_HEADS = 4
DN_DH = 128
DN_W = DN_HEADS * DN_DH
DN_CONV = 5
CHUNK = 64
FN_GROUPS = 4
FN_DG = 128
FN_W = FN_GROUPS * FN_DG
HID = 4 * D
ROPE_BASE = 10000.0
EPS = 1e-6
NEG = -1e30

P_W = 3 * NA_W + 4 * DN_W + FN_W
P_NAQ, P_NAK, P_NAV = 0, 1, 2
P_DN = 1
P_Z = 6
TN_IN = P_W // 2
GP = 128

TM_ALL = 1280
TM_LAT = 1024
TM_MERGE_ALL = 640
TM_MERGE_LAT = 512
NA_BLK = 4 * GRID_W
NA_HEAD_GROUP = 2
FFT_N = 128
DN_BLK_CHUNKS = 4
WY_GROUP_CHUNKS = 4
VMEM_LIMIT = 56 * 1024 * 1024


def _cparams(sem):
    return pltpu.CompilerParams(dimension_semantics=sem, vmem_limit_bytes=VMEM_LIMIT)


def _dot(a, b):
    return jnp.dot(a, b, preferred_element_type=F32)


def _dot_nt(a, b):
    return lax.dot_general(a, b, (((1,), (1,)), ((), ())), preferred_element_type=F32)


def _dot_tn(a, b):
    return lax.dot_general(a, b, (((0,), (0,)), ((), ())), preferred_element_type=F32)


def _silu(x):
    return x * jax.nn.sigmoid(x)


def _ada_kernel(c_ref, w_ref, b_ref, o_ref):
    s = _silu(c_ref[...])
    o_ref[0] = jnp.dot(s, w_ref[0], preferred_element_type=F32, precision=lax.Precision.HIGHEST) + b_ref[0]


def _ada(cc, w_ada, b_ada):
    tn = 1536
    return pl.pallas_call(
        _ada_kernel,
        grid=(DEPTH, 6 * D // tn),
        in_specs=[pl.BlockSpec((8, D), lambda l, j: (0, 0)),
                  pl.BlockSpec((1, D, tn), lambda l, j: (l, 0, j)),
                  pl.BlockSpec((1, 1, tn), lambda l, j: (l, 0, j))],
        out_specs=pl.BlockSpec((1, 8, tn), lambda l, j: (l, 0, j)),
        out_shape=jax.ShapeDtypeStruct((DEPTH, 8, 6 * D), F32),
        compiler_params=_cparams(("arbitrary", "arbitrary")),
        name="ada_mod",
    )(cc, w_ada, b_ada.reshape(DEPTH, 1, 6 * D))


def _mod_rows(mod_ref, k, row0, tm, has_ctx):
    lat = mod_ref[0:1, k * D:(k + 1) * D]
    if not has_ctx:
        return lat
    ctx = mod_ref[1:2, k * D:(k + 1) * D]
    rows = row0 + lax.broadcasted_iota(jnp.int32, (tm, 1), 0)
    return jnp.where(rows >= T_LAT, ctx, lat)


def _modnorm(x, g, shift, scale):
    y = x * lax.rsqrt(jnp.mean(x * x, axis=-1, keepdims=True) + EPS) * g
    return y * (1.0 + scale) + shift


def _in_proj_kernel(x_ref, mod_ref, n_ref, w_ref, wg_ref, u_ref, p_ref, g_ref, h_scr, *, tm):
    i = pl.program_id(0)
    j = pl.program_id(1)

    @pl.when(j == 0)
    def _():
        shift = _mod_rows(mod_ref, 0, i * tm, tm, True)
        scale = _mod_rows(mod_ref, 1, i * tm, tm, True)
        h = _modnorm(x_ref[...], n_ref[...], shift, scale).astype(BF16)
        h_scr[...] = h
        g_ref[...] = _dot(h, wg_ref[...])

    p = _dot(h_scr[...], w_ref[...]).astype(BF16)
    p_ref[...] = p

    @pl.when(j == pl.num_programs(1) - 1)
    def _():
        u_ref[...] = p[:, TN_IN - FN_W:]


def _in_proj(xs, mod, norm, wcat, wg):
    tm = TM_ALL
    nj = P_W // TN_IN
    return pl.pallas_call(
        functools.partial(_in_proj_kernel, tm=tm),
        grid=(T_ALL // tm, nj),
        in_specs=[pl.BlockSpec((tm, D), lambda i, j: (i, 0)),
                  pl.BlockSpec((8, 6 * D), lambda i, j: (0, 0)),
                  pl.BlockSpec((1, D), lambda i, j: (0, 0)),
                  pl.BlockSpec((D, TN_IN), lambda i, j: (0, j)),
                  pl.BlockSpec((D, GP), lambda i, j: (0, 0))],
        out_specs=[pl.BlockSpec((tm, FN_W), lambda i, j: (i, 0)),
                   pl.BlockSpec((tm, TN_IN), lambda i, j: (i, j)),
                   pl.BlockSpec((tm, GP), lambda i, j: (i, 0))],
        out_shape=[jax.ShapeDtypeStruct((T_ALL, FN_W), BF16),
                   jax.ShapeDtypeStruct((T_ALL, P_W), BF16),
                   jax.ShapeDtypeStruct((T_ALL, GP), F32)],
        scratch_shapes=[pltpu.VMEM((tm, D), BF16)],
        compiler_params=_cparams(("arbitrary", "arbitrary")),
        name="in_proj",
    )(xs, mod, norm.reshape(1, D), wcat, wg)


def _na_tables(rpb):
    qr = np.arange(4)[:, None, None, None]
    qc = np.arange(GRID_W)[None, :, None, None]
    kk = np.arange(12)[None, None, :, None]
    kc = np.arange(GRID_W)[None, None, None, :]
    valid = []
    for b in (0, 1, GRID_H // 4 - 1):
        r = 4 * b + qr
        kr = 4 * (b - 1) + kk
        rs = np.clip(r - NA_WIN_H // 2, 0, GRID_H - NA_WIN_H)
        cs = np.clip(qc - NA_WIN_W // 2, 0, GRID_W - NA_WIN_W)
        ok = (kr >= 0) & (kr < GRID_H) & (kr >= rs) & (kr < rs + NA_WIN_H) & (kc >= cs) & (kc < cs + NA_WIN_W)
        valid.append(np.broadcast_to(ok, (4, GRID_W, 12, GRID_W)).reshape(NA_BLK, 12 * GRID_W))
    valid.append(np.zeros_like(valid[0]))
    valid = np.stack(valid)
    col = np.arange(GRID_W)
    onehot = ((col[None, None, :] - col[None, :, None] + NA_WIN_W - 1)
              == np.arange(2 * NA_WIN_W - 1)[:, None, None]).astype(np.float32)
    tcol = jnp.einsum('hrd,dqk->hrqk', rpb, onehot, precision=lax.Precision.HIGHEST)
    bias = jnp.stack([jnp.stack([tcol[:, k - q + NA_WIN_H - 5] for k in range(12)], axis=2) for q in range(4)], axis=1)
    bias = bias.reshape(NA_HEADS, NA_BLK, 12 * GRID_W)
    return jnp.where(valid[:, None], bias[None], NEG).astype(BF16)


def _na_kernel(q_ref, k0_ref, k1_ref, k2_ref, v0_ref, v1_ref, v2_ref, kc_ref, vc_ref, tab_ref, o_ref):
    q = q_ref[...]
    kl = jnp.concatenate([k0_ref[...], k1_ref[...], k2_ref[...]], axis=0)
    vl = jnp.concatenate([v0_ref[...], v1_ref[...], v2_ref[...]], axis=0)
    kc = kc_ref[...]
    vc = vc_ref[...]
    lane_l = lax.broadcasted_iota(jnp.int32, (vl.shape[0], 2 * NA_DH), 1)
    lane_c = lax.broadcasted_iota(jnp.int32, (vc.shape[0], 2 * NA_DH), 1)
    one = jnp.ones((), BF16)
    outs = []
    for h0 in range(0, NA_HEADS, NA_HEAD_GROUP):
        heads = range(h0, h0 + NA_HEAD_GROUP)
        sls = [slice(h * NA_DH, (h + 1) * NA_DH) for h in heads]
        pls = [slice((h // 2) * 2 * NA_DH, (h // 2 + 1) * 2 * NA_DH) for h in heads]
        own = [(lambda ln, h=h: (ln < NA_DH) if h % 2 == 0 else (ln >= NA_DH)) for h in heads]
        qs = [q[:, sl] * (NA_DH ** -0.5) for sl in sls]
        z_loc = [_dot_nt(qh, kl[:, sl]).astype(BF16) + tab_ref[0, h] for h, qh, sl in zip(heads, qs, sls)]
        z_ctx = [_dot_nt(qh, kc[:, sl]).astype(BF16) for qh, sl in zip(qs, sls)]
        m = [jnp.maximum(jnp.max(a, axis=-1, keepdims=True), jnp.max(b, axis=-1, keepdims=True))
             for a, b in zip(z_loc, z_ctx)]
        p_loc = [jnp.exp(a - mm) for a, mm in zip(z_loc, m)]
        p_ctx = [jnp.exp(b - mm) for b, mm in zip(z_ctx, m)]
        oa = [_dot(a, jnp.where(f(lane_l), vl[:, ps], one)) + _dot(b, jnp.where(f(lane_c), vc[:, ps], one))
              for a, b, ps, f in zip(p_loc, p_ctx, pls, own)]
        outs += [(x[:, :NA_DH] / x[:, NA_DH:NA_DH + 1]) if h % 2 == 0 else (x[:, NA_DH:] / x[:, 0:1])
                 for h, x in zip(heads, oa)]
    o_ref[...] = jnp.concatenate(outs, axis=-1).astype(BF16)


def _na(p, tab, with_ctx):
    nlat = T_LAT // NA_BLK
    nb = nlat + (1 if with_ctx else 0)
    ctx_blk = T_LAT // NA_BLK

    def kidx(b, part):
        return jnp.clip(jnp.minimum(b, nlat - 1) - 1 + part, 0, nlat - 1)

    def variant(b):
        return jnp.where(b == 0, 0, jnp.where(b == nlat - 1, 2, jnp.where(b >= nlat, 3, 1)))

    blk = (NA_BLK, NA_W)
    in_specs = [pl.BlockSpec(blk, lambda b: (b, P_NAQ))]
    in_specs += [pl.BlockSpec(blk, functools.partial(lambda b, part: (kidx(b, part), P_NAK), part=part))
                 for part in range(3)]
    in_specs += [pl.BlockSpec(blk, functools.partial(lambda b, part: (kidx(b, part), P_NAV), part=part))
                 for part in range(3)]
    in_specs += [pl.BlockSpec(blk, lambda b: (ctx_blk, P_NAK)),
                 pl.BlockSpec(blk, lambda b: (ctx_blk, P_NAV)),
                 pl.BlockSpec((1, NA_HEADS, NA_BLK, 3 * NA_BLK), lambda b: (variant(b), 0, 0, 0))]
    return pl.pallas_call(
        _na_kernel,
        grid=(nb,),
        in_specs=in_specs,
        out_specs=pl.BlockSpec(blk, lambda b: (b, 0)),
        out_shape=jax.ShapeDtypeStruct((nb * NA_BLK, NA_W), BF16),
        compiler_params=_cparams(("arbitrary",)),
        name="na_attn",
    )(p, p, p, p, p, p, p, p, p, tab)


def _rope_tables():
    t = jnp.arange(T_LAT, dtype=jnp.int32)
    half = DN_DH // 2
    pos = jnp.stack([t // GRID_W, t % GRID_W], axis=-1).astype(F32)
    inv = 1.0 / (ROPE_BASE ** (jnp.arange(0, half, 2, dtype=F32) / half))
    ang = pos[:, :, None] * inv[None, None, :]
    ang = jnp.concatenate([ang, ang], axis=-1).reshape(T_LAT, DN_DH)
    cos, sin = jnp.cos(ang), jnp.sin(ang)
    low = (np.arange(DN_DH) % half) < half // 2
    return cos, jnp.where(low[None], -sin, 0.0), jnp.where(low[None], 0.0, sin)


def _dn_prep_kernel(x_ref, prev_ref, next_ref, g_ref, cw_ref, alog_ref, dtb_ref, cos_ref, sa_ref, sb_ref,
                    qkv_ref, gb_ref, *, tm):
    b = pl.program_id(0)
    nlat = T_LAT // tm
    first = jnp.logical_or(b == 0, b == nlat)
    last = b >= nlat - 1
    is_ctx = b >= nlat
    cur = x_ref[...].astype(F32)
    prev = jnp.where(first, 0.0, prev_ref[...][14:16].astype(F32))
    nxt = jnp.where(last, 0.0, next_ref[...][0:2].astype(F32))
    ext = jnp.concatenate([prev, cur, nxt], axis=0)
    cw = cw_ref[...]
    y = ext[0:tm] * cw[0:1]
    for k in range(1, DN_CONV):
        y = y + ext[k:k + tm] * cw[k:k + 1]
    y = _silu(y)
    cos = jnp.where(is_ctx, 1.0, cos_ref[...])
    sa = jnp.where(is_ctx, 0.0, sa_ref[...])
    sb = jnp.where(is_ctx, 0.0, sb_ref[...])
    quarter = DN_DH // 4
    for s in range(2):
        for h in range(DN_HEADS):
            c0 = s * DN_W + h * DN_DH
            xh = y[:, c0:c0 + DN_DH]
            xh = xh * lax.rsqrt(jnp.sum(xh * xh, axis=-1, keepdims=True) + EPS)
            xh = (xh * cos + pltpu.roll(xh, DN_DH - quarter, axis=1) * sa + pltpu.roll(xh, quarter, axis=1) * sb)
            if s == 0:
                xh = xh * (DN_DH ** -0.5)
            qkv_ref[:, c0:c0 + DN_DH] = xh.astype(BF16)
    qkv_ref[:, 2 * DN_W:] = y[:, 2 * DN_W:].astype(BF16)
    g = g_ref[...]
    lane = lax.broadcasted_iota(jnp.int32, g.shape, 1)
    decay = -jnp.exp(alog_ref[...]) * jax.nn.softplus(g + dtb_ref[...])
    gb_ref[...] = jnp.where(lane < 2 * DN_HEADS, decay, jax.nn.sigmoid(g))


def _dn_prep(p, g, conv_w, a_log, dt_bias, rope):
    tm = 256
    nb = T_ALL // tm
    nlat = T_LAT // tm
    hb = 16
    nh = T_ALL // hb
    pad = lambda v: jnp.pad(v.reshape(1, 2 * DN_HEADS).astype(F32), ((0, 0), (0, GP - 2 * DN_HEADS)))
    rspec = pl.BlockSpec((tm, DN_DH), lambda b: (jnp.minimum(b, nlat - 1), 0))
    return pl.pallas_call(
        functools.partial(_dn_prep_kernel, tm=tm),
        grid=(nb,),
        in_specs=[pl.BlockSpec((tm, 3 * DN_W), lambda b: (b, P_DN)),
                  pl.BlockSpec((hb, 3 * DN_W), lambda b: (jnp.maximum(b * (tm // hb) - 1, 0), P_DN)),
                  pl.BlockSpec((hb, 3 * DN_W), lambda b: (jnp.minimum((b + 1) * (tm // hb), nh - 1), P_DN)),
                  pl.BlockSpec((tm, GP), lambda b: (b, 0)),
                  pl.BlockSpec((DN_CONV, 3 * DN_W), lambda b: (0, 0)),
                  pl.BlockSpec((1, GP), lambda b: (0, 0)),
                  pl.BlockSpec((1, GP), lambda b: (0, 0)),
                  rspec, rspec, rspec],
        out_specs=[pl.BlockSpec((tm, 3 * DN_W), lambda b: (b, 0)),
                   pl.BlockSpec((tm, GP), lambda b: (b, 0))],
        out_shape=[jax.ShapeDtypeStruct((T_ALL, 3 * DN_W), BF16),
                   jax.ShapeDtypeStruct((T_ALL, GP), F32)],
        compiler_params=_cparams(("arbitrary",)),
        name="dn_prep",
    )(p, p, p, g, conv_w, pad(a_log), pad(dt_bias), *rope)


def _pair_blockdiag(y):
    c = y.shape[0]
    lane = lax.broadcasted_iota(jnp.int32, y.shape, 1)
    zero = jnp.zeros((), y.dtype)
    return jnp.concatenate([jnp.where(lane < c, y, zero), jnp.where(lane >= c, y, zero)], axis=0)


def _unit_tri_inverses_minus_identity(mats):
    c = mats[0].shape[0]
    ri = lax.broadcasted_iota(jnp.int32, (c, 2 * c), 0)
    ci = lax.broadcasted_iota(jnp.int32, (c, 2 * c), 1) % c

    def same_block(s):
        return (ri // s) == (ci // s)

    def mm(xs, ys):
        return [_dot(x.astype(BF16), _pair_blockdiag(y.astype(BF16))) for x, y in zip(xs, ys)]

    diag = same_block(8)
    b1 = [jnp.where(diag, -a, 0.0) for a in mats]
    b2 = mm(b1, b1)
    b3 = mm(b1, b2)
    b4 = mm(b2, b2)
    n2 = [x + y + z for x, y, z in zip(b1, b2, b3)]
    n = [x + y + z for x, y, z in zip(n2, b4, mm(n2, b4))]
    for s in (8, 16, 32):
        off = jnp.logical_and(same_block(2 * s), jnp.logical_not(same_block(s)))
        lo = [jnp.where(off, a, 0.0) for a in mats]
        x = [p + q for p, q in zip(lo, mm(n, lo))]
        n = [p - (q + r) for p, q, r in zip(n, x, mm(x, n))]
    return n


def _dn_wy_kernel(qkv_ref, gb_ref, u_ref, wq_ref, kq_ref, dl_ref, *, nchunk):
    c = CHUNK
    tm = nchunk * c
    gb = gb_ref[...]
    qkv = qkv_ref[...]
    rt = lax.broadcasted_iota(jnp.int32, (tm, tm), 0)
    ct = lax.broadcasted_iota(jnp.int32, (tm, tm), 1)
    same_chunk = (rt // c) == (ct // c)
    ri = lax.broadcasted_iota(jnp.int32, (c, 2 * c), 0)
    lane2 = lax.broadcasted_iota(jnp.int32, (c, 2 * c), 1)
    ci = lane2 % c
    first = lane2 < c
    lane = lax.broadcasted_iota(jnp.int32, (8, GP), 1)
    lane_k = lax.broadcasted_iota(jnp.int32, (c, 2 * DN_DH), 1)
    zk = jnp.zeros((), BF16)
    eye_p = (ri == ci).astype(F32)
    eye2 = (lax.broadcasted_iota(jnp.int32, (DN_DH, 2 * DN_DH), 1) % DN_DH
            == lax.broadcasted_iota(jnp.int32, (DN_DH, 2 * DN_DH), 0)).astype(BF16)
    gb_t = gb.T
    dl_rows, gcs = [], []
    for d in range(2):
        cum = jnp.logical_and(same_chunk, (ct <= rt) if d == 0 else (ct >= rt))
        gc_all = jnp.dot(cum.astype(F32), gb, preferred_element_type=F32, precision=lax.Precision.HIGHEST)
        gcs.append((gc_all, gc_all.T))
        dl_rows.append([jnp.exp(gc_all[j * c + (c - 1 if d == 0 else 0):j * c + (c if d == 0 else 1), :])
                        for j in range(nchunk)])
    for j0 in range(0, nchunk, WY_GROUP_CHUNKS):
        shared = {}
        for j in range(j0, j0 + WY_GROUP_CHUNKS):
            for pr in range(DN_HEADS // 2):
                r0 = j * c
                hs = slice(2 * pr * DN_DH, (2 * pr + 2) * DN_DH)
                bd = lambda x: jnp.concatenate([jnp.where(lane_k < DN_DH, x, zk), jnp.where(lane_k >= DN_DH, x, zk)],
                                               axis=0)
                q2 = qkv[r0:r0 + c, hs]
                k2 = qkv[r0:r0 + c, DN_W + hs.start:DN_W + hs.stop]
                v2 = qkv[r0:r0 + c, 2 * DN_W + hs.start:2 * DN_W + hs.stop]
                bd_k = bd(k2)
                shared[(j, pr)] = dict(q2=q2, bd_k=bd_k, bd_v=bd(v2),
                                       gqt=_dot_nt(jnp.concatenate([k2, q2, eye2], axis=0), bd_k))
        units = [(d, j, pr) for j in range(j0, j0 + WY_GROUP_CHUNKS) for d in range(2) for pr in range(DN_HEADS // 2)]
        pre = []
        for d, j, pr in units:
            gc_all, gc_all_t = gcs[d]
            incl = (ci <= ri) if d == 0 else (ci >= ri)
            strict = (ci < ri) if d == 0 else (ci > ri)
            r0 = j * c
            last = r0 + (c - 1 if d == 0 else 0)
            i0 = d * DN_HEADS + 2 * pr
            col = lambda x, a: x[r0:r0 + c, a:a + 1]
            row = lambda x, a: jnp.concatenate([x[a:a + 1, r0:r0 + c], x[a + 1:a + 2, r0:r0 + c]], axis=1)
            beta_p = jnp.where(first, col(gb, 2 * DN_HEADS + i0), col(gb, 2 * DN_HEADS + i0 + 1))
            gc_p = jnp.where(first, col(gc_all, i0), col(gc_all, i0 + 1))
            gc_row = row(gc_all_t, i0)
            beta_row = row(gb_t, 2 * DN_HEADS + i0)
            g_last = jnp.where(first[0:1], gc_all[last:last + 1, i0:i0 + 1], gc_all[last:last + 1, i0 + 1:i0 + 2])
            e = jnp.exp(jnp.where(incl, gc_p - gc_row, 0.0))
            eg_row = jnp.exp(gc_row)
            pre.append(dict(beta_p=beta_p, e_incl=jnp.where(incl, e, 0.0), e_strict=jnp.where(strict, e, 0.0),
                            beta_row=beta_row, eg_row=eg_row, tail_row=jnp.exp(g_last - gc_row)))
        gqt = [shared[(j, pr)]["gqt"] for d, j, pr in units]
        ns = _unit_tri_inverses_minus_identity([p["beta_p"] * x[:c] * p["e_strict"] for x, p in zip(gqt, pre)])
        tmat = {u_: n + eye_p for u_, n in zip(units, ns)}
        prd = dict(zip(units, pre))
        both = [(j, pr) for j in range(j0, j0 + WY_GROUP_CHUNKS) for pr in range(DN_HEADS // 2)]
        lhs_u = [jnp.concatenate([(tmat[(d, j, pr)] * prd[(d, j, pr)]["beta_row"]).astype(BF16) for d in range(2)], axis=0)
                 for j, pr in both]
        lhs_w = [jnp.concatenate([(tmat[(d, j, pr)] * (prd[(d, j, pr)]["beta_row"] * prd[(d, j, pr)]["eg_row"])
                                   ).astype(BF16) for d in range(2)], axis=0) for j, pr in both]
        u2 = dict(zip(both, [_dot(a, shared[jp]["bd_v"]) for a, jp in zip(lhs_u, both)]))
        w2 = dict(zip(both, [_dot(a, shared[jp]["bd_k"]) for a, jp in zip(lhs_w, both)]))
        for (d, j, pr), p, x in zip(units, pre, gqt):
            r0 = j * c
            i0 = d * DN_HEADS + 2 * pr
            cs = slice(i0 * DN_DH, (i0 + 2) * DN_DH)
            ks = slice(i0 * c, (i0 + 2) * c)
            q2 = shared[(j, pr)]["q2"].astype(F32)
            gcol = gcs[d][0][r0:r0 + c]
            eg = jnp.where(lane_k < DN_DH, jnp.exp(gcol[:, i0:i0 + 1]), jnp.exp(gcol[:, i0 + 1:i0 + 2]))
            u_ref[r0:r0 + c, cs] = u2[(j, pr)][d * c:(d + 1) * c].astype(BF16)
            wq_ref[2 * r0:2 * r0 + c, cs] = w2[(j, pr)][d * c:(d + 1) * c].astype(BF16)
            wq_ref[2 * r0 + c:2 * r0 + 2 * c, cs] = (q2 * eg).astype(BF16)
            kq_ref[3 * r0:3 * r0 + DN_DH, ks] = (x[2 * c:] * p["tail_row"]).astype(BF16)
            kq_ref[3 * r0 + DN_DH:3 * r0 + DN_DH + c, ks] = (x[c:2 * c] * p["e_incl"]).astype(BF16)
    for j in range(nchunk):
        row = jnp.where(lane < DN_HEADS, dl_rows[0][j], dl_rows[1][j])
        dl_ref[8 * j:8 * j + 8, :] = row


def _dn_wy(qkv, gb, nchunk):
    tm = nchunk * CHUNK
    nb = T_ALL // tm
    nch = 2 * DN_HEADS
    return pl.pallas_call(
        functools.partial(_dn_wy_kernel, nchunk=nchunk),
        grid=(nb,),
        in_specs=[pl.BlockSpec((tm, 3 * DN_W), lambda i: (i, 0)),
                  pl.BlockSpec((tm, GP), lambda i: (i, 0))],
        out_specs=[pl.BlockSpec((tm, nch * DN_DH), lambda i: (i, 0)),
                   pl.BlockSpec((2 * tm, nch * DN_DH), lambda i: (i, 0)),
                   pl.BlockSpec((3 * tm, nch * CHUNK), lambda i: (i, 0)),
                   pl.BlockSpec((8 * nchunk, GP), lambda i: (i, 0))],
        out_shape=[jax.ShapeDtypeStruct((T_ALL, nch * DN_DH), BF16),
                   jax.ShapeDtypeStruct((2 * T_ALL, nch * DN_DH), BF16),
                   jax.ShapeDtypeStruct((3 * T_ALL, nch * CHUNK), BF16),
                   jax.ShapeDtypeStruct((8 * T_ALL // CHUNK, GP), F32)],
        compiler_params=_cparams(("arbitrary",)),
        name="dn_wy",
    )(qkv, gb)


def _dn_scan_kernel(u_f_ref, wq_f_ref, kq_f_ref, dl_f_ref, u_b_ref, wq_b_ref, kq_b_ref, dl_b_ref,
                    of_ref, ob_ref, s_scr, *, nchunk):
    i = pl.program_id(0)

    @pl.when(i == 0)
    def _():
        s_scr[...] = jnp.zeros_like(s_scr)

    c = CHUNK
    states = [s_scr[idx] for idx in range(2 * DN_HEADS)]
    zero = jnp.zeros((c, DN_DH), BF16)
    refs = ((u_f_ref, wq_f_ref, kq_f_ref, dl_f_ref, of_ref), (u_b_ref, wq_b_ref, kq_b_ref, dl_b_ref, ob_ref))
    chains = [(d, h) for d in range(2) for h in range(DN_HEADS)]
    pairs = [(d, p) for d in range(2) for p in range(DN_HEADS // 2)]
    for step in range(nchunk):
        blk = lambda d: step if d == 0 else nchunk - 1 - step
        hs = lambda h: slice(h * DN_DH, (h + 1) * DN_DH)
        r1 = [_dot(refs[d][1][2 * blk(d) * c:2 * (blk(d) + 1) * c, hs(h)], states[d * DN_HEADS + h].astype(BF16))
              for d, h in chains]
        v_new = [(refs[d][0][blk(d) * c:(blk(d) + 1) * c, hs(h)].astype(F32) - r[:c]).astype(BF16)
                 for (d, h), r in zip(chains, r1)]
        r2 = [_dot(refs[d][2][3 * blk(d) * c:3 * (blk(d) + 1) * c, p * 2 * c:(p + 1) * 2 * c],
                   jnp.concatenate([jnp.concatenate([v_new[d * DN_HEADS + 2 * p], zero], axis=1),
                                    jnp.concatenate([zero, v_new[d * DN_HEADS + 2 * p + 1]], axis=1)], axis=0))
              for d, p in pairs]
        for d, h in chains:
            idx = d * DN_HEADS + h
            r = r2[d * (DN_HEADS // 2) + h // 2]
            ts = slice((h % 2) * DN_DH, (h % 2 + 1) * DN_DH)
            dl_row = refs[d][3][8 * blk(d):8 * blk(d) + 1, :]
            states[idx] = states[idx] * dl_row[:, idx:idx + 1] + r[:DN_DH, ts]
            refs[d][4][blk(d) * c:(blk(d) + 1) * c, hs(h)] = (r1[idx][c:] + r[DN_DH:, ts]).astype(BF16)
    for idx in range(2 * DN_HEADS):
        s_scr[idx] = states[idx]


def _dn_scan(u, wq, kq, dl, nchunk):
    tm = nchunk * CHUNK
    nb = T_ALL // tm
    nlat = T_LAT // tm
    fwd = lambda i: jnp.where(i == 0, nlat, i - 1)
    bwd = lambda i: nb - 1 - i
    specs = lambda f, col: [pl.BlockSpec((tm, DN_W), lambda i: (f(i), col)),
                            pl.BlockSpec((2 * tm, DN_W), lambda i: (f(i), col)),
                            pl.BlockSpec((3 * tm, DN_HEADS * CHUNK), lambda i: (f(i), col)),
                            pl.BlockSpec((8 * nchunk, GP), lambda i: (f(i), 0))]
    return pl.pallas_call(
        functools.partial(_dn_scan_kernel, nchunk=nchunk),
        grid=(nb,),
        in_specs=specs(fwd, 0) + specs(bwd, 1),
        out_specs=[pl.BlockSpec((tm, DN_W), lambda i: (fwd(i), 0)),
                   pl.BlockSpec((tm, DN_W), lambda i: (bwd(i), 0))],
        out_shape=[jax.ShapeDtypeStruct((T_ALL, DN_W), BF16),
                   jax.ShapeDtypeStruct((T_ALL, DN_W), BF16)],
        scratch_shapes=[pltpu.VMEM((2 * DN_HEADS, DN_DH, DN_DH), F32)],
        compiler_params=_cparams(("arbitrary",)),
        name="dn_scan",
    )(u, wq, kq, dl, u, wq, kq, dl)


def _dft_cos_sin(n):
    k = np.arange(n)
    ang = 2.0 * np.pi * ((k[:, None] * k[None, :]) % n) / n
    return np.cos(ang), np.sin(ang)


def _fft_consts():
    n = FFT_N
    c, s = _dft_cos_sin(n)
    cs_ch = np.concatenate([c, s], axis=1)
    w1 = np.block([[c, -s], [-s, -c]])
    k2 = np.arange(n)[None, :, None]
    t2 = np.arange(n)[:, None, None]
    ang = 2.0 * np.pi * ((k2 * t2) % (n * n)) / (n * n)
    scale = 1.0 / math.sqrt(T_LAT * FN_DG)
    cc, sc = _dft_cos_sin(T_CTX)
    scale_c = 1.0 / math.sqrt(T_CTX * FN_DG)
    f32 = lambda a: jnp.asarray(a, F32)
    bf = lambda a: f32(a).astype(BF16)
    return dict(cs_ch=bf(cs_ch), w1=bf(w1), twc=f32(np.cos(ang)), tws=f32(np.sin(ang)),
                c2=bf(c * scale), s2=bf(s * scale), cc=bf(cc * scale_c), sc=bf(sc * scale_c))


def _fft1_kernel(u_ref, cs_ref, w1_ref, twc_ref, tws_ref, y_ref, *, n_t2):
    n = FFT_N
    cs = cs_ref[...]
    w1 = w1_ref[...]
    for t in range(n_t2):
        twc = twc_ref[t]
        tws = tws_ref[t]
        for g in range(0, FN_GROUPS, 2):
            c0 = (t * FN_GROUPS + g) * FN_DG
            ab = [_dot(u_ref[:, c0 + s * FN_DG:c0 + (s + 1) * FN_DG], cs) for s in range(2)]
            rhs = jnp.concatenate([jnp.concatenate([ab[0][:, :FN_DG], ab[1][:, :FN_DG]], axis=1),
                                   jnp.concatenate([ab[0][:, FN_DG:], ab[1][:, FN_DG:]], axis=1)], axis=0)
            y = _dot(w1, rhs.astype(BF16))
            yr, yi = y[:n], y[n:]
            y_ref[0:n, c0:c0 + 2 * FN_DG] = (yr * twc + yi * tws).astype(BF16)
            y_ref[n:2 * n, c0:c0 + 2 * FN_DG] = (yi * twc - yr * tws).astype(BF16)


def _fft2_kernel(y_ref, c2_ref, s2_ref, o_ref, *, n_k2):
    c2 = c2_ref[...]
    s2 = s2_ref[...]
    for j in range(n_k2):
        o_ref[:, j * FN_W:(j + 1) * FN_W] = (_dot(c2, y_ref[0, j]) + _dot(s2, y_ref[1, j])).astype(BF16)


def _fft_ctx_kernel(u_ref, cs_ref, cc_ref, sc_ref, o_in_ref, o_ref):
    del o_in_ref
    for g in range(FN_GROUPS):
        ab = _dot(u_ref[:, g * FN_DG:(g + 1) * FN_DG], cs_ref[...])
        a = ab[:, :FN_DG].astype(BF16)
        b = ab[:, FN_DG:].astype(BF16)
        o_ref[:, g * FN_DG:(g + 1) * FN_DG] = (_dot(cc_ref[...], a) - _dot(sc_ref[...], b)).astype(BF16)


def _fnet(u, fc, with_ctx):
    n = FFT_N
    row_w = n * FN_W
    n_t2 = 8
    tc = n_t2 * FN_W
    y = pl.pallas_call(
        functools.partial(_fft1_kernel, n_t2=n_t2),
        grid=(row_w // tc,),
        in_specs=[pl.BlockSpec((n, tc), lambda j: (0, j)),
                  pl.BlockSpec((n, 2 * n), lambda j: (0, 0)),
                  pl.BlockSpec((2 * n, 2 * n), lambda j: (0, 0)),
                  pl.BlockSpec((n_t2, n, 1), lambda j: (j, 0, 0)),
                  pl.BlockSpec((n_t2, n, 1), lambda j: (j, 0, 0))],
        out_specs=pl.BlockSpec((2 * n, tc), lambda j: (0, j)),
        out_shape=jax.ShapeDtypeStruct((2 * n, row_w), BF16),
        compiler_params=_cparams(("arbitrary",)),
        name="fnet_stage1",
    )(u.reshape(T_ALL // n, row_w), fc["cs_ch"], fc["w1"], fc["twc"], fc["tws"])
    n_k2 = 8
    o = pl.pallas_call(
        functools.partial(_fft2_kernel, n_k2=n_k2),
        grid=(n // n_k2,),
        in_specs=[pl.BlockSpec((2, n_k2, n, FN_W), lambda j: (0, j, 0, 0)),
                  pl.BlockSpec((n, n), lambda j: (0, 0)),
                  pl.BlockSpec((n, n), lambda j: (0, 0))],
        out_specs=pl.BlockSpec((n, n_k2 * FN_W), lambda j: (0, j)),
        out_shape=jax.ShapeDtypeStruct((T_ALL // n, row_w), BF16),
        compiler_params=_cparams(("arbitrary",)),
        name="fnet_stage2",
    )(y.reshape(2, n, n, FN_W), fc["c2"], fc["s2"])
    o = o.reshape(T_ALL, FN_W)
    if not with_ctx:
        return o
    cb = T_LAT // T_CTX
    return pl.pallas_call(
        _fft_ctx_kernel,
        grid=(1,),
        in_specs=[pl.BlockSpec((T_CTX, FN_W), lambda j: (cb, 0)),
                  pl.BlockSpec((n, 2 * n), lambda j: (0, 0)),
                  pl.BlockSpec((T_CTX, T_CTX), lambda j: (0, 0)),
                  pl.BlockSpec((T_CTX, T_CTX), lambda j: (0, 0)),
                  pl.BlockSpec(memory_space=pl.ANY)],
        out_specs=pl.BlockSpec((T_CTX, FN_W), lambda j: (cb, 0)),
        out_shape=jax.ShapeDtypeStruct((T_ALL, FN_W), BF16),
        input_output_aliases={4: 0},
        compiler_params=_cparams(("arbitrary",)),
        name="fnet_ctx",
    )(u, fc["cs_ch"], fc["cc"], fc["sc"], o)


def _merge_kernel(x_ref, mod_ref, n1_ref, ona_ref, of_ref, ob_ref, z_ref, ofn_ref, dnn_ref,
                  wg_ref, wna_ref, wdn_ref, wfn_ref, wout_ref, o_ref, *, tm, has_ctx):
    i = pl.program_id(0)
    x = x_ref[...]
    shift = _mod_rows(mod_ref, 0, i * tm, tm, has_ctx)
    scale = _mod_rows(mod_ref, 1, i * tm, tm, has_ctx)
    h = _modnorm(x, n1_ref[...], shift, scale).astype(BF16)
    o = of_ref[...].astype(F32) + ob_ref[...].astype(F32)
    z = z_ref[...].astype(F32)
    parts = []
    for hd in range(DN_HEADS):
        sl = slice(hd * DN_DH, (hd + 1) * DN_DH)
        oh = o[:, sl]
        oh = oh * lax.rsqrt(jnp.mean(oh * oh, axis=-1, keepdims=True) + EPS) * dnn_ref[...]
        parts.append(oh * _silu(z[:, sl]))
    odn = jnp.concatenate(parts, axis=-1).astype(BF16)
    y = None
    for b, (br, w_ref) in enumerate(((ona_ref[...], wna_ref), (odn, wdn_ref), (ofn_ref[...], wfn_ref))):
        g = jax.nn.sigmoid(_dot(h, wg_ref[:, b * D:(b + 1) * D]))
        t = g * _dot(br, w_ref[...])
        y = t if y is None else y + t
    y = _dot(y.astype(BF16), wout_ref[...])
    gate = _mod_rows(mod_ref, 2, i * tm, tm, has_ctx)
    o_ref[...] = x + gate * y


def _merge(xs, mod, norm1, p, o_na, o_f, o_b, o_fn, dn_norm, w_gate, w_na_o, w_dn_o, w_fn, w_out, has_ctx):
    rows, tm = (T_ALL, TM_MERGE_ALL) if has_ctx else (T_LAT, TM_MERGE_LAT)
    row = lambda w: pl.BlockSpec((tm, w), lambda i: (i, 0))
    full = lambda a: pl.BlockSpec(a.shape, lambda i: (0, 0), pipeline_mode=pl.Buffered(1))
    return pl.pallas_call(
        functools.partial(_merge_kernel, tm=tm, has_ctx=has_ctx),
        grid=(rows // tm,),
        in_specs=[row(D), pl.BlockSpec((8, 6 * D), lambda i: (0, 0)), pl.BlockSpec((1, D), lambda i: (0, 0)),
                  row(NA_W), row(DN_W), row(DN_W),
                  pl.BlockSpec((tm, DN_W), lambda i: (i, P_Z)),
                  row(FN_W), pl.BlockSpec((1, DN_DH), lambda i: (0, 0)),
                  full(w_gate), full(w_na_o), full(w_dn_o), full(w_fn), full(w_out)],
        out_specs=row(D),
        out_shape=jax.ShapeDtypeStruct((rows, D), F32),
        compiler_params=_cparams(("arbitrary",)),
        name="merge",
    )(xs, mod, norm1.reshape(1, D), o_na, o_f, o_b, p, o_fn, dn_norm.reshape(1, DN_DH),
      w_gate, w_na_o, w_dn_o, w_fn, w_out)


def _mlp_kernel(x_ref, mod_ref, n_ref, w1_ref, w2_ref, nf_ref, o_ref, h_scr, acc_scr, *, tm, has_ctx, final):
    i = pl.program_id(0)
    j = pl.program_id(1)

    @pl.when(j == 0)
    def _():
        shift = _mod_rows(mod_ref, 3, i * tm, tm, has_ctx)
        scale = _mod_rows(mod_ref, 4, i * tm, tm, has_ctx)
        h_scr[...] = _modnorm(x_ref[...], n_ref[...], shift, scale).astype(BF16)
        acc_scr[...] = jnp.zeros_like(acc_scr)

    a = jnp.maximum(_dot(h_scr[...], w1_ref[...]), 0.0)
    acc_scr[...] += _dot((a * a).astype(BF16), w2_ref[...])

    @pl.when(j == pl.num_programs(1) - 1)
    def _():
        gate = _mod_rows(mod_ref, 5, i * tm, tm, has_ctx)
        xn = x_ref[...] + gate * acc_scr[...]
        if final:
            xn = xn * lax.rsqrt(jnp.mean(xn * xn, axis=-1, keepdims=True) + EPS) * nf_ref[...]
        o_ref[...] = xn


def _mlp(xs, mod, norm, w1, w2, norm_f, has_ctx, final):
    rows, tm = (T_ALL, TM_ALL) if has_ctx else (T_LAT, TM_LAT)
    th = 1024
    return pl.pallas_call(
        functools.partial(_mlp_kernel, tm=tm, has_ctx=has_ctx, final=final),
        grid=(rows // tm, HID // th),
        in_specs=[pl.BlockSpec((tm, D), lambda i, j: (i, 0)),
                  pl.BlockSpec((8, 6 * D), lambda i, j: (0, 0)),
                  pl.BlockSpec((1, D), lambda i, j: (0, 0)),
                  pl.BlockSpec((D, th), lambda i, j: (0, j)),
                  pl.BlockSpec((th, D), lambda i, j: (j, 0)),
                  pl.BlockSpec((1, D), lambda i, j: (0, 0))],
        out_specs=pl.BlockSpec((tm, D), lambda i, j: (i, 0)),
        out_shape=jax.ShapeDtypeStruct((rows, D), F32),
        scratch_shapes=[pltpu.VMEM((tm, D), BF16), pltpu.VMEM((tm, D), F32)],
        compiler_params=_cparams(("arbitrary", "arbitrary")),
        name="mlp",
    )(xs, mod, norm.reshape(1, D), w1, w2, norm_f.reshape(1, D))


def _split_in_weights(w_in):
    na_k, na_v, dn_k, dn_v, dn_ab, na_q, dn_q, dn_z, fn_u, gate = jnp.split(
        w_in, np.cumsum([NA_W, NA_W, DN_W, DN_W, 4 * DN_HEADS, NA_W, DN_W, DN_W, FN_W]).tolist(), axis=-1)
    wcat = jnp.concatenate([na_q, na_k, na_v, dn_q, dn_k, dn_v, dn_z, fn_u], axis=-1).astype(BF16)
    wg = jnp.pad(dn_ab, ((0, 0), (0, GP - 4 * DN_HEADS))).astype(BF16)
    return wcat, wg, gate.astype(BF16)


def kernel(x, c, ctx, c_ctx, w_ada, b_ada, norm1, w_in, conv_w, a_log, dt_bias, dn_norm, rpb,
           w_na_o, w_dn_o, w_fn, w_out, norm2, w_mlp1, w_mlp2, norm_f):
    xs = jnp.concatenate([x[0], ctx[0]], axis=0)
    cc = jnp.concatenate([c, c_ctx[None, :], jnp.zeros((6, D), F32)], axis=0)
    mods = _ada(cc, w_ada, b_ada)
    rope = _rope_tables()
    fc = _fft_consts()
    for l in range(DEPTH):
        has_ctx = l < DEPTH - 1
        final = l == DEPTH - 1
        wcat, wg, w_gate = _split_in_weights(w_in[l])
        u, p, g = _in_proj(xs, mods[l], norm1[l], wcat, wg)
        o_na = _na(p, _na_tables(rpb[l]), has_ctx)
        qkv, gb = _dn_prep(p, g, conv_w[l], a_log[l], dt_bias[l], rope)
        o_f, o_b = _dn_scan(*_dn_wy(qkv, gb, DN_BLK_CHUNKS), DN_BLK_CHUNKS)
        o_fn = _fnet(u, fc, has_ctx)
        xs = _merge(xs, mods[l], norm1[l], p, o_na, o_f, o_b, o_fn, dn_norm[l], w_gate,
                    w_na_o[l].astype(BF16), w_dn_o[l].astype(BF16), w_fn[l].astype(BF16), w_out[l].astype(BF16),
                    has_ctx)
        xs = _mlp(xs, mods[l], norm2[l], w_mlp1[l].astype(BF16), w_mlp2[l].astype(BF16), norm_f, has_ctx, final)
    return xs[None]
```

```python
import functools
import math

import numpy as np
import jax
import jax.numpy as jnp
from jax import lax
from jax.experimental import pallas as pl
from jax.experimental.pallas import tpu as pltpu

F32 = jnp.float32
BF16 = jnp.bfloat16

D = 1024
T_LAT = 16384
T_CTX = 256
T_ALL = T_LAT + T_CTX
DEPTH = 2
GRID_W = 64
GRID_H = T_LAT // GRID_W
NA_HEADS = 8
NA_DH = 64
NA_WIN_H = 8
NA_WIN_W = 16
NA_W = NA_HEADS * NA_DH
DN_HEADS = 4
DN_DH = 128
DN_W = DN_HEADS * DN_DH
DN_CONV = 5
CHUNK = 64
FN_GROUPS = 4
FN_DG = 128
FN_W = FN_GROUPS * FN_DG
N_BRANCH = 3
HID = 4 * D
ROPE_BASE = 10000.0
EPS = 1e-6
NEG = -1e30

P_W = 3 * NA_W + 4 * DN_W + FN_W
P_NAQ, P_NAK, P_NAV = 0, 1, 2
P_DN = 1
P_Z = 6
TN_IN = P_W // 2
GP = 128

TM_ALL = 1280
TM_LAT = 1024
TM_MERGE_ALL = 640
TM_MERGE_LAT = 512
NA_BLK = 4 * GRID_W
NA_HEAD_GROUP = 2
FFT_N = 128
DN_BLK_CHUNKS = 4
WY_GROUP_CHUNKS = 4
VMEM_LIMIT = 56 * 1024 * 1024


def _cparams(sem):
    return pltpu.CompilerParams(dimension_semantics=sem, vmem_limit_bytes=VMEM_LIMIT)


def _dot(a, b):
    return jnp.dot(a, b, preferred_element_type=F32)


def _dot_nt(a, b):
    return lax.dot_general(a, b, (((1,), (1,)), ((), ())), preferred_element_type=F32)


def _dot_tn(a, b):
    return lax.dot_general(a, b, (((0,), (0,)), ((), ())), preferred_element_type=F32)


def _silu(x):
    return x * jax.nn.sigmoid(x)


def _ada_kernel(c_ref, w_ref, b_ref, o_ref):
    s = _silu(c_ref[...])
    o_ref[0] = jnp.dot(s, w_ref[0], preferred_element_type=F32, precision=lax.Precision.HIGHEST) + b_ref[0]


def _ada(cc, w_ada, b_ada):
    tn = 1536
    return pl.pallas_call(
        _ada_kernel,
        grid=(DEPTH, 6 * D // tn),
        in_specs=[pl.BlockSpec((8, D), lambda l, j: (0, 0)),
                  pl.BlockSpec((1, D, tn), lambda l, j: (l, 0, j)),
                  pl.BlockSpec((1, 1, tn), lambda l, j: (l, 0, j))],
        out_specs=pl.BlockSpec((1, 8, tn), lambda l, j: (l, 0, j)),
        out_shape=jax.ShapeDtypeStruct((DEPTH, 8, 6 * D), F32),
        compiler_params=_cparams(("arbitrary", "arbitrary")),
        name="ada_mod",
    )(cc, w_ada, b_ada.reshape(DEPTH, 1, 6 * D))


def _row_slices(tm, has_ctx):
    return [slice(0, tm)] if not has_ctx else [slice(0, tm - T_CTX), slice(tm - T_CTX, tm)]


def _mod_vecs(mod_ref, k, tail_is_ctx, has_ctx):
    lat = mod_ref[0:1, k * D:(k + 1) * D]
    if not has_ctx:
        return [lat]
    return [lat, jnp.where(tail_is_ctx, mod_ref[1:2, k * D:(k + 1) * D], lat)]


def _modnorm(x, g, shift, scale):
    y = x * lax.rsqrt(jnp.mean(x * x, axis=-1, keepdims=True) + EPS)
    return y * (g * (1.0 + scale)) + shift


def _modnorm_tile(x, g, mod_ref, k_shift, k_scale, tail_is_ctx, has_ctx):
    tm = x.shape[0]
    parts = [_modnorm(x[sl], g, sh, sc) for sl, sh, sc in zip(_row_slices(tm, has_ctx),
                                                            _mod_vecs(mod_ref, k_shift, tail_is_ctx, has_ctx),
                                                            _mod_vecs(mod_ref, k_scale, tail_is_ctx, has_ctx))]
    return parts[0] if len(parts) == 1 else jnp.concatenate(parts, axis=0)


def _gated_residual(x, y, mod_ref, k_gate, tail_is_ctx, has_ctx):
    tm = x.shape[0]
    parts = [x[sl] + gt * y[sl] for sl, gt in zip(_row_slices(tm, has_ctx), _mod_vecs(mod_ref, k_gate, tail_is_ctx, has_ctx))]
    return parts[0] if len(parts) == 1 else jnp.concatenate(parts, axis=0)


def _in_proj_kernel(x_ref, mod_ref, n_ref, w_ref, wg_ref, u_ref, p_ref, g_ref, h_scr, *, tm):
    i = pl.program_id(0)
    j = pl.program_id(1)

    @pl.when(j == 0)
    def _():
        h = _modnorm_tile(x_ref[...], n_ref[...], mod_ref, 0, 1, i == pl.num_programs(0) - 1, True).astype(BF16)
        h_scr[...] = h
        g_ref[...] = _dot(h, wg_ref[...])

    p = _dot(h_scr[...], w_ref[...]).astype(BF16)
    p_ref[...] = p

    @pl.when(j == pl.num_programs(1) - 1)
    def _():
        u_ref[...] = p[:, TN_IN - FN_W:]


def _in_proj(xs, mod, norm, wcat, wg):
    tm = TM_ALL
    nj = P_W // TN_IN
    return pl.pallas_call(
        functools.partial(_in_proj_kernel, tm=tm),
        grid=(T_ALL // tm, nj),
        in_specs=[pl.BlockSpec((tm, D), lambda i, j: (i, 0)),
                  pl.BlockSpec((8, 6 * D), lambda i, j: (0, 0)),
                  pl.BlockSpec((1, D), lambda i, j: (0, 0)),
                  pl.BlockSpec((D, TN_IN), lambda i, j: (0, j)),
                  pl.BlockSpec((D, GP), lambda i, j: (0, 0))],
        out_specs=[pl.BlockSpec((tm, FN_W), lambda i, j: (i, 0)),
                   pl.BlockSpec((tm, TN_IN), lambda i, j: (i, j)),
                   pl.BlockSpec((tm, GP), lambda i, j: (i, 0))],
        out_shape=[jax.ShapeDtypeStruct((T_ALL, FN_W), BF16),
                   jax.ShapeDtypeStruct((T_ALL, P_W), BF16),
                   jax.ShapeDtypeStruct((T_ALL, GP), F32)],
        scratch_shapes=[pltpu.VMEM((tm, D), BF16)],
        compiler_params=_cparams(("arbitrary", "arbitrary")),
        name="in_proj",
    )(xs, mod, norm.reshape(1, D), wcat, wg)


def _na_tables(rpb):
    qr = np.arange(4)[:, None, None, None]
    qc = np.arange(GRID_W)[None, :, None, None]
    kk = np.arange(12)[None, None, :, None]
    kc = np.arange(GRID_W)[None, None, None, :]
    valid = []
    for b in (0, 1, GRID_H // 4 - 1):
        r = 4 * b + qr
        kr = 4 * (b - 1) + kk
        rs = np.clip(r - NA_WIN_H // 2, 0, GRID_H - NA_WIN_H)
        cs = np.clip(qc - NA_WIN_W // 2, 0, GRID_W - NA_WIN_W)
        ok = (kr >= 0) & (kr < GRID_H) & (kr >= rs) & (kr < rs + NA_WIN_H) & (kc >= cs) & (kc < cs + NA_WIN_W)
        valid.append(np.broadcast_to(ok, (4, GRID_W, 12, GRID_W)).reshape(NA_BLK, 12 * GRID_W))
    valid.append(np.zeros_like(valid[0]))
    valid = np.stack(valid)
    col = np.arange(GRID_W)
    onehot = ((col[None, None, :] - col[None, :, None] + NA_WIN_W - 1)
              == np.arange(2 * NA_WIN_W - 1)[:, None, None]).astype(np.float32)
    tcol = jnp.einsum('hrd,dqk->hrqk', rpb, onehot, precision=lax.Precision.HIGHEST)
    bias = jnp.stack([jnp.stack([tcol[:, k - q + NA_WIN_H - 5] for k in range(12)], axis=2) for q in range(4)], axis=1)
    bias = bias.reshape(NA_HEADS, NA_BLK, 12 * GRID_W)
    return jnp.where(valid[:, None], bias[None], NEG).astype(BF16)


def _na_kernel(q_ref, k0_ref, k1_ref, k2_ref, v0_ref, v1_ref, v2_ref, kc_ref, vc_ref, tab_ref, o_ref):
    q = q_ref[...]
    kl = jnp.concatenate([k0_ref[...], k1_ref[...], k2_ref[...]], axis=0)
    vl = jnp.concatenate([v0_ref[...], v1_ref[...], v2_ref[...]], axis=0)
    kc = kc_ref[...]
    vc = vc_ref[...]
    lane_l = lax.broadcasted_iota(jnp.int32, (vl.shape[0], 2 * NA_DH), 1)
    lane_c = lax.broadcasted_iota(jnp.int32, (vc.shape[0], 2 * NA_DH), 1)
    one = jnp.ones((), BF16)
    outs = []
    for h0 in range(0, NA_HEADS, NA_HEAD_GROUP):
        heads = range(h0, h0 + NA_HEAD_GROUP)
        sls = [slice(h * NA_DH, (h + 1) * NA_DH) for h in heads]
        pls = [slice((h // 2) * 2 * NA_DH, (h // 2 + 1) * 2 * NA_DH) for h in heads]
        own = [(lambda ln, h=h: (ln < NA_DH) if h % 2 == 0 else (ln >= NA_DH)) for h in heads]
        qs = [q[:, sl] * (NA_DH ** -0.5) for sl in sls]
        z_loc = [_dot_nt(qh, kl[:, sl]).astype(BF16) + tab_ref[0, h] for h, qh, sl in zip(heads, qs, sls)]
        z_ctx = [_dot_nt(qh, kc[:, sl]).astype(BF16) for qh, sl in zip(qs, sls)]
        m = [jnp.maximum(jnp.max(a, axis=-1, keepdims=True), jnp.max(b, axis=-1, keepdims=True))
             for a, b in zip(z_loc, z_ctx)]
        p_loc = [jnp.exp(a - mm) for a, mm in zip(z_loc, m)]
        p_ctx = [jnp.exp(b - mm) for b, mm in zip(z_ctx, m)]
        oa = [_dot(a, jnp.where(f(lane_l), vl[:, ps], one)) + _dot(b, jnp.where(f(lane_c), vc[:, ps], one))
              for a, b, ps, f in zip(p_loc, p_ctx, pls, own)]
        outs += [(x[:, :NA_DH] / x[:, NA_DH:NA_DH + 1]) if h % 2 == 0 else (x[:, NA_DH:] / x[:, 0:1])
                 for h, x in zip(heads, oa)]
    o_ref[...] = jnp.concatenate(outs, axis=-1).astype(BF16)


def _na(p, tab, with_ctx):
    nlat = T_LAT // NA_BLK
    nb = nlat + (1 if with_ctx else 0)
    ctx_blk = T_LAT // NA_BLK

    def kidx(b, part):
        return jnp.clip(jnp.minimum(b, nlat - 1) - 1 + part, 0, nlat - 1)

    def variant(b):
        return jnp.where(b == 0, 0, jnp.where(b == nlat - 1, 2, jnp.where(b >= nlat, 3, 1)))

    blk = (NA_BLK, NA_W)
    in_specs = [pl.BlockSpec(blk, lambda b: (b, P_NAQ))]
    in_specs += [pl.BlockSpec(blk, functools.partial(lambda b, part: (kidx(b, part), P_NAK), part=part))
                 for part in range(3)]
    in_specs += [pl.BlockSpec(blk, functools.partial(lambda b, part: (kidx(b, part), P_NAV), part=part))
                 for part in range(3)]
    in_specs += [pl.BlockSpec(blk, lambda b: (ctx_blk, P_NAK)),
                 pl.BlockSpec(blk, lambda b: (ctx_blk, P_NAV)),
                 pl.BlockSpec((1, NA_HEADS, NA_BLK, 3 * NA_BLK), lambda b: (variant(b), 0, 0, 0))]
    return pl.pallas_call(
        _na_kernel,
        grid=(nb,),
        in_specs=in_specs,
        out_specs=pl.BlockSpec(blk, lambda b: (b, 0)),
        out_shape=jax.ShapeDtypeStruct((nb * NA_BLK, NA_W), BF16),
        compiler_params=_cparams(("arbitrary",)),
        name="na_attn",
    )(p, p, p, p, p, p, p, p, p, tab)


def _rope_tables():
    t = jnp.arange(T_LAT, dtype=jnp.int32)
    half = DN_DH // 2
    pos = jnp.stack([t // GRID_W, t % GRID_W], axis=-1).astype(F32)
    inv = 1.0 / (ROPE_BASE ** (jnp.arange(0, half, 2, dtype=F32) / half))
    ang = pos[:, :, None] * inv[None, None, :]
    ang = jnp.concatenate([ang, ang], axis=-1).reshape(T_LAT, DN_DH)
    cos, sin = jnp.cos(ang), jnp.sin(ang)
    low = (np.arange(DN_DH) % half) < half // 2
    return cos, jnp.where(low[None], -sin, 0.0), jnp.where(low[None], 0.0, sin)


def _dn_prep_kernel(x_ref, prev_ref, next_ref, g_ref, cw_ref, alog_ref, dtb_ref, cos_ref, sa_ref, sb_ref,
                    qkv_ref, gb_ref, *, tm):
    b = pl.program_id(0)
    nlat = T_LAT // tm
    first = jnp.logical_or(b == 0, b == nlat)
    last = b >= nlat - 1
    is_ctx = b >= nlat
    cur = x_ref[...].astype(F32)
    prev = jnp.where(first, 0.0, prev_ref[...][14:16].astype(F32))
    nxt = jnp.where(last, 0.0, next_ref[...][0:2].astype(F32))
    ext = jnp.concatenate([prev, cur, nxt], axis=0)
    cw = cw_ref[...]
    y = ext[0:tm] * cw[0:1]
    for k in range(1, DN_CONV):
        y = y + ext[k:k + tm] * cw[k:k + 1]
    y = _silu(y)
    cos = jnp.where(is_ctx, 1.0, cos_ref[...])
    sa = jnp.where(is_ctx, 0.0, sa_ref[...])
    sb = jnp.where(is_ctx, 0.0, sb_ref[...])
    quarter = DN_DH // 4
    for s in range(2):
        for h in range(DN_HEADS):
            c0 = s * DN_W + h * DN_DH
            xh = y[:, c0:c0 + DN_DH]
            xh = xh * lax.rsqrt(jnp.sum(xh * xh, axis=-1, keepdims=True) + EPS)
            xh = (xh * cos + pltpu.roll(xh, DN_DH - quarter, axis=1) * sa + pltpu.roll(xh, quarter, axis=1) * sb)
            if s == 0:
                xh = xh * (DN_DH ** -0.5)
            qkv_ref[:, c0:c0 + DN_DH] = xh.astype(BF16)
    qkv_ref[:, 2 * DN_W:] = y[:, 2 * DN_W:].astype(BF16)
    g = g_ref[...]
    lane = lax.broadcasted_iota(jnp.int32, g.shape, 1)
    decay = -jnp.exp(alog_ref[...]) * jax.nn.softplus(g + dtb_ref[...])
    gb_ref[...] = jnp.where(lane < 2 * DN_HEADS, decay, jax.nn.sigmoid(g))


def _dn_prep(p, g, conv_w, a_log, dt_bias, rope):
    tm = 256
    nb = T_ALL // tm
    nlat = T_LAT // tm
    hb = 16
    nh = T_ALL // hb
    pad = lambda v: jnp.pad(v.reshape(1, 2 * DN_HEADS).astype(F32), ((0, 0), (0, GP - 2 * DN_HEADS)))
    rspec = pl.BlockSpec((tm, DN_DH), lambda b: (jnp.minimum(b, nlat - 1), 0))
    return pl.pallas_call(
        functools.partial(_dn_prep_kernel, tm=tm),
        grid=(nb,),
        in_specs=[pl.BlockSpec((tm, 3 * DN_W), lambda b: (b, P_DN)),
                  pl.BlockSpec((hb, 3 * DN_W), lambda b: (jnp.maximum(b * (tm // hb) - 1, 0), P_DN)),
                  pl.BlockSpec((hb, 3 * DN_W), lambda b: (jnp.minimum((b + 1) * (tm // hb), nh - 1), P_DN)),
                  pl.BlockSpec((tm, GP), lambda b: (b, 0)),
                  pl.BlockSpec((DN_CONV, 3 * DN_W), lambda b: (0, 0)),
                  pl.BlockSpec((1, GP), lambda b: (0, 0)),
                  pl.BlockSpec((1, GP), lambda b: (0, 0)),
                  rspec, rspec, rspec],
        out_specs=[pl.BlockSpec((tm, 3 * DN_W), lambda b: (b, 0)),
                   pl.BlockSpec((tm, GP), lambda b: (b, 0))],
        out_shape=[jax.ShapeDtypeStruct((T_ALL, 3 * DN_W), BF16),
                   jax.ShapeDtypeStruct((T_ALL, GP), F32)],
        compiler_params=_cparams(("arbitrary",)),
        name="dn_prep",
    )(p, p, p, g, conv_w, pad(a_log), pad(dt_bias), *rope)


def _pair_blockdiag(y):
    c = y.shape[0]
    lane = lax.broadcasted_iota(jnp.int32, y.shape, 1)
    zero = jnp.zeros((), y.dtype)
    return jnp.concatenate([jnp.where(lane < c, y, zero), jnp.where(lane >= c, y, zero)], axis=0)


def _unit_tri_inverses_minus_identity(mats):
    c = mats[0].shape[0]
    ri = lax.broadcasted_iota(jnp.int32, (c, 2 * c), 0)
    ci = lax.broadcasted_iota(jnp.int32, (c, 2 * c), 1) % c

    def same_block(s):
        return (ri // s) == (ci // s)

    def mm(xs, ys):
        return [_dot(x.astype(BF16), _pair_blockdiag(y.astype(BF16))) for x, y in zip(xs, ys)]

    diag = same_block(8)
    b1 = [jnp.where(diag, -a, 0.0) for a in mats]
    b2 = mm(b1, b1)
    b3 = mm(b1, b2)
    b4 = mm(b2, b2)
    n2 = [x + y + z for x, y, z in zip(b1, b2, b3)]
    n = [x + y + z for x, y, z in zip(n2, b4, mm(n2, b4))]
    for s in (8, 16, 32):
        off = jnp.logical_and(same_block(2 * s), jnp.logical_not(same_block(s)))
        lo = [jnp.where(off, a, 0.0) for a in mats]
        x = [p + q for p, q in zip(lo, mm(n, lo))]
        n = [p - (q + r) for p, q, r in zip(n, x, mm(x, n))]
    return n


def _dn_wy_kernel(qkv_ref, gb_ref, u_ref, wq_ref, kq_ref, dl_ref, *, nchunk):
    c = CHUNK
    tm = nchunk * c
    gb = gb_ref[...]
    qkv = qkv_ref[...]
    rt = lax.broadcasted_iota(jnp.int32, (tm, tm), 0)
    ct = lax.broadcasted_iota(jnp.int32, (tm, tm), 1)
    same_chunk = (rt // c) == (ct // c)
    ri = lax.broadcasted_iota(jnp.int32, (c, 2 * c), 0)
    lane2 = lax.broadcasted_iota(jnp.int32, (c, 2 * c), 1)
    ci = lane2 % c
    first = lane2 < c
    lane = lax.broadcasted_iota(jnp.int32, (8, GP), 1)
    lane_k = lax.broadcasted_iota(jnp.int32, (c, 2 * DN_DH), 1)
    zk = jnp.zeros((), BF16)
    eye_p = (ri == ci).astype(F32)
    eye2 = (lax.broadcasted_iota(jnp.int32, (DN_DH, 2 * DN_DH), 1) % DN_DH
            == lax.broadcasted_iota(jnp.int32, (DN_DH, 2 * DN_DH), 0)).astype(BF16)
    gb_t = gb.T
    dl_rows, gcs = [], []
    for d in range(2):
        cum = jnp.logical_and(same_chunk, (ct <= rt) if d == 0 else (ct >= rt))
        gc_all = jnp.dot(cum.astype(F32), gb, preferred_element_type=F32, precision=lax.Precision.HIGHEST)
        gcs.append((gc_all, gc_all.T))
        dl_rows.append([jnp.exp(gc_all[j * c + (c - 1 if d == 0 else 0):j * c + (c if d == 0 else 1), :])
                        for j in range(nchunk)])
    for j0 in range(0, nchunk, WY_GROUP_CHUNKS):
        shared = {}
        for j in range(j0, j0 + WY_GROUP_CHUNKS):
            for pr in range(DN_HEADS // 2):
                r0 = j * c
                hs = slice(2 * pr * DN_DH, (2 * pr + 2) * DN_DH)
                bd = lambda x: jnp.concatenate([jnp.where(lane_k < DN_DH, x, zk), jnp.where(lane_k >= DN_DH, x, zk)],
                                               axis=0)
                q2 = qkv[r0:r0 + c, hs]
                k2 = qkv[r0:r0 + c, DN_W + hs.start:DN_W + hs.stop]
                v2 = qkv[r0:r0 + c, 2 * DN_W + hs.start:2 * DN_W + hs.stop]
                bd_k = bd(k2)
                shared[(j, pr)] = dict(q2=q2, bd_k=bd_k, bd_v=bd(v2),
                                       gqt=_dot_nt(jnp.concatenate([k2, q2, eye2], axis=0), bd_k))
        units = [(d, j, pr) for j in range(j0, j0 + WY_GROUP_CHUNKS) for d in range(2) for pr in range(DN_HEADS // 2)]
        pre = []
        for d, j, pr in units:
            gc_all, gc_all_t = gcs[d]
            incl = (ci <= ri) if d == 0 else (ci >= ri)
            strict = (ci < ri) if d == 0 else (ci > ri)
            r0 = j * c
            last = r0 + (c - 1 if d == 0 else 0)
            i0 = d * DN_HEADS + 2 * pr
            col = lambda x, a: x[r0:r0 + c, a:a + 1]
            row = lambda x, a: jnp.concatenate([x[a:a + 1, r0:r0 + c], x[a + 1:a + 2, r0:r0 + c]], axis=1)
            beta_p = jnp.where(first, col(gb, 2 * DN_HEADS + i0), col(gb, 2 * DN_HEADS + i0 + 1))
            gc_p = jnp.where(first, col(gc_all, i0), col(gc_all, i0 + 1))
            gc_row = row(gc_all_t, i0)
            beta_row = row(gb_t, 2 * DN_HEADS + i0)
            g_last = jnp.where(first[0:1], gc_all[last:last + 1, i0:i0 + 1], gc_all[last:last + 1, i0 + 1:i0 + 2])
            e = jnp.exp(jnp.where(incl, gc_p - gc_row, 0.0))
            eg_row = jnp.exp(gc_row)
            pre.append(dict(beta_p=beta_p, e_incl=jnp.where(incl, e, 0.0), e_strict=jnp.where(strict, e, 0.0),
                            beta_row=beta_row, eg_row=eg_row, tail_row=jnp.exp(g_last - gc_row)))
        gqt = [shared[(j, pr)]["gqt"] for d, j, pr in units]
        ns = _unit_tri_inverses_minus_identity([p["beta_p"] * x[:c] * p["e_strict"] for x, p in zip(gqt, pre)])
        tmat = {u_: n + eye_p for u_, n in zip(units, ns)}
        prd = dict(zip(units, pre))
        both = [(j, pr) for j in range(j0, j0 + WY_GROUP_CHUNKS) for pr in range(DN_HEADS // 2)]
        lhs_u = [jnp.concatenate([(tmat[(d, j, pr)] * prd[(d, j, pr)]["beta_row"]).astype(BF16) for d in range(2)], axis=0)
                 for j, pr in both]
        lhs_w = [jnp.concatenate([(tmat[(d, j, pr)] * (prd[(d, j, pr)]["beta_row"] * prd[(d, j, pr)]["eg_row"])
                                   ).astype(BF16) for d in range(2)], axis=0) for j, pr in both]
        u2 = dict(zip(both, [_dot(a, shared[jp]["bd_v"]) for a, jp in zip(lhs_u, both)]))
        w2 = dict(zip(both, [_dot(a, shared[jp]["bd_k"]) for a, jp in zip(lhs_w, both)]))
        for (d, j, pr), p, x in zip(units, pre, gqt):
            r0 = j * c
            i0 = d * DN_HEADS + 2 * pr
            cs = slice(i0 * DN_DH, (i0 + 2) * DN_DH)
            ks = slice(i0 * c, (i0 + 2) * c)
            q2 = shared[(j, pr)]["q2"].astype(F32)
            gcol = gcs[d][0][r0:r0 + c]
            eg = jnp.where(lane_k < DN_DH, jnp.exp(gcol[:, i0:i0 + 1]), jnp.exp(gcol[:, i0 + 1:i0 + 2]))
            u_ref[r0:r0 + c, cs] = u2[(j, pr)][d * c:(d + 1) * c].astype(BF16)
            wq_ref[2 * r0:2 * r0 + c, cs] = w2[(j, pr)][d * c:(d + 1) * c].astype(BF16)
            wq_ref[2 * r0 + c:2 * r0 + 2 * c, cs] = (q2 * eg).astype(BF16)
            kq_ref[3 * r0:3 * r0 + DN_DH, ks] = (x[2 * c:] * p["tail_row"]).astype(BF16)
            kq_ref[3 * r0 + DN_DH:3 * r0 + DN_DH + c, ks] = (x[c:2 * c] * p["e_incl"]).astype(BF16)
    for j in range(nchunk):
        row = jnp.where(lane < DN_HEADS, dl_rows[0][j], dl_rows[1][j])
        dl_ref[8 * j:8 * j + 8, :] = row


def _dn_wy(qkv, gb, nchunk):
    tm = nchunk * CHUNK
    nb = T_ALL // tm
    nch = 2 * DN_HEADS
    return pl.pallas_call(
        functools.partial(_dn_wy_kernel, nchunk=nchunk),
        grid=(nb,),
        in_specs=[pl.BlockSpec((tm, 3 * DN_W), lambda i: (i, 0)),
                  pl.BlockSpec((tm, GP), lambda i: (i, 0))],
        out_specs=[pl.BlockSpec((tm, nch * DN_DH), lambda i: (i, 0)),
                   pl.BlockSpec((2 * tm, nch * DN_DH), lambda i: (i, 0)),
                   pl.BlockSpec((3 * tm, nch * CHUNK), lambda i: (i, 0)),
                   pl.BlockSpec((8 * nchunk, GP), lambda i: (i, 0))],
        out_shape=[jax.ShapeDtypeStruct((T_ALL, nch * DN_DH), BF16),
                   jax.ShapeDtypeStruct((2 * T_ALL, nch * DN_DH), BF16),
                   jax.ShapeDtypeStruct((3 * T_ALL, nch * CHUNK), BF16),
                   jax.ShapeDtypeStruct((8 * T_ALL // CHUNK, GP), F32)],
        compiler_params=_cparams(("arbitrary",)),
        name="dn_wy",
    )(qkv, gb)


def _dn_scan_kernel(u_f_ref, wq_f_ref, kq_f_ref, dl_f_ref, u_b_ref, wq_b_ref, kq_b_ref, dl_b_ref,
                    of_ref, ob_ref, s_scr, *, nchunk):
    i = pl.program_id(0)

    @pl.when(i == 0)
    def _():
        s_scr[...] = jnp.zeros_like(s_scr)

    c = CHUNK
    states = [s_scr[idx] for idx in range(2 * DN_HEADS)]
    zero = jnp.zeros((c, DN_DH), BF16)
    refs = ((u_f_ref, wq_f_ref, kq_f_ref, dl_f_ref, of_ref), (u_b_ref, wq_b_ref, kq_b_ref, dl_b_ref, ob_ref))
    chains = [(d, h) for d in range(2) for h in range(DN_HEADS)]
    pairs = [(d, p) for d in range(2) for p in range(DN_HEADS // 2)]
    for step in range(nchunk):
        blk = lambda d: step if d == 0 else nchunk - 1 - step
        hs = lambda h: slice(h * DN_DH, (h + 1) * DN_DH)
        r1 = [_dot(refs[d][1][2 * blk(d) * c:2 * (blk(d) + 1) * c, hs(h)], states[d * DN_HEADS + h].astype(BF16))
              for d, h in chains]
        v_new = [(refs[d][0][blk(d) * c:(blk(d) + 1) * c, hs(h)].astype(F32) - r[:c]).astype(BF16)
                 for (d, h), r in zip(chains, r1)]
        r2 = [_dot(refs[d][2][3 * blk(d) * c:3 * (blk(d) + 1) * c, p * 2 * c:(p + 1) * 2 * c],
                   jnp.concatenate([jnp.concatenate([v_new[d * DN_HEADS + 2 * p], zero], axis=1),
                                    jnp.concatenate([zero, v_new[d * DN_HEADS + 2 * p + 1]], axis=1)], axis=0))
              for d, p in pairs]
        for d, h in chains:
            idx = d * DN_HEADS + h
            r = r2[d * (DN_HEADS // 2) + h // 2]
            ts = slice((h % 2) * DN_DH, (h % 2 + 1) * DN_DH)
            dl_row = refs[d][3][8 * blk(d):8 * blk(d) + 1, :]
            states[idx] = states[idx] * dl_row[:, idx:idx + 1] + r[:DN_DH, ts]
            refs[d][4][blk(d) * c:(blk(d) + 1) * c, hs(h)] = (r1[idx][c:] + r[DN_DH:, ts]).astype(BF16)
    for idx in range(2 * DN_HEADS):
        s_scr[idx] = states[idx]


def _dn_scan(u, wq, kq, dl, nchunk):
    tm = nchunk * CHUNK
    nb = T_ALL // tm
    nlat = T_LAT // tm
    fwd = lambda i: jnp.where(i == 0, nlat, i - 1)
    bwd = lambda i: nb - 1 - i
    specs = lambda f, col: [pl.BlockSpec((tm, DN_W), lambda i: (f(i), col)),
                            pl.BlockSpec((2 * tm, DN_W), lambda i: (f(i), col)),
                            pl.BlockSpec((3 * tm, DN_HEADS * CHUNK), lambda i: (f(i), col)),
                            pl.BlockSpec((8 * nchunk, GP), lambda i: (f(i), 0))]
    return pl.pallas_call(
        functools.partial(_dn_scan_kernel, nchunk=nchunk),
        grid=(nb,),
        in_specs=specs(fwd, 0) + specs(bwd, 1),
        out_specs=[pl.BlockSpec((tm, DN_W), lambda i: (fwd(i), 0)),
                   pl.BlockSpec((tm, DN_W), lambda i: (bwd(i), 0))],
        out_shape=[jax.ShapeDtypeStruct((T_ALL, DN_W), BF16),
                   jax.ShapeDtypeStruct((T_ALL, DN_W), BF16)],
        scratch_shapes=[pltpu.VMEM((2 * DN_HEADS, DN_DH, DN_DH), F32)],
        compiler_params=_cparams(("arbitrary",)),
        name="dn_scan",
    )(u, wq, kq, dl, u, wq, kq, dl)


def _dft_cos_sin(n):
    k = np.arange(n)
    ang = 2.0 * np.pi * ((k[:, None] * k[None, :]) % n) / n
    return np.cos(ang), np.sin(ang)


def _fft_consts():
    n = FFT_N
    c, s = _dft_cos_sin(n)
    cs_ch = np.concatenate([c, s], axis=1)
    w1 = np.block([[c, -s], [-s, -c]])
    k2 = np.arange(n)[None, :, None]
    t2 = np.arange(n)[:, None, None]
    ang = 2.0 * np.pi * ((k2 * t2) % (n * n)) / (n * n)
    scale = 1.0 / math.sqrt(T_LAT * FN_DG)
    cc, sc = _dft_cos_sin(T_CTX)
    scale_c = 1.0 / math.sqrt(T_CTX * FN_DG)
    f32 = lambda a: jnp.asarray(a, F32)
    bf = lambda a: f32(a).astype(BF16)
    return dict(cs_ch=bf(cs_ch), w1=bf(w1), twc=f32(np.cos(ang)), tws=f32(np.sin(ang)),
                c2=bf(c * scale), s2=bf(s * scale), cc=bf(cc * scale_c), sc=bf(sc * scale_c))


def _fft1_kernel(u_ref, cs_ref, w1_ref, twc_ref, tws_ref, y_ref, *, n_t2):
    n = FFT_N
    cs = cs_ref[...]
    w1 = w1_ref[...]
    for t in range(n_t2):
        twc = twc_ref[t]
        tws = tws_ref[t]
        for g in range(0, FN_GROUPS, 2):
            c0 = (t * FN_GROUPS + g) * FN_DG
            ab = [_dot(u_ref[:, c0 + s * FN_DG:c0 + (s + 1) * FN_DG], cs) for s in range(2)]
            rhs = jnp.concatenate([jnp.concatenate([ab[0][:, :FN_DG], ab[1][:, :FN_DG]], axis=1),
                                   jnp.concatenate([ab[0][:, FN_DG:], ab[1][:, FN_DG:]], axis=1)], axis=0)
            y = _dot(w1, rhs.astype(BF16))
            yr, yi = y[:n], y[n:]
            y_ref[0:n, c0:c0 + 2 * FN_DG] = (yr * twc + yi * tws).astype(BF16)
            y_ref[n:2 * n, c0:c0 + 2 * FN_DG] = (yi * twc - yr * tws).astype(BF16)


def _fft2_kernel(y_ref, c2_ref, s2_ref, o_ref, *, n_k2):
    c2 = c2_ref[...]
    s2 = s2_ref[...]
    for j in range(n_k2):
        o_ref[:, j * FN_W:(j + 1) * FN_W] = (_dot(c2, y_ref[0, j]) + _dot(s2, y_ref[1, j])).astype(BF16)


def _fft_ctx_kernel(u_ref, cs_ref, cc_ref, sc_ref, o_in_ref, o_ref):
    del o_in_ref
    for g in range(FN_GROUPS):
        ab = _dot(u_ref[:, g * FN_DG:(g + 1) * FN_DG], cs_ref[...])
        a = ab[:, :FN_DG].astype(BF16)
        b = ab[:, FN_DG:].astype(BF16)
        o_ref[:, g * FN_DG:(g + 1) * FN_DG] = (_dot(cc_ref[...], a) - _dot(sc_ref[...], b)).astype(BF16)


def _fnet(u, fc, with_ctx):
    n = FFT_N
    row_w = n * FN_W
    n_t2 = 8
    tc = n_t2 * FN_W
    y = pl.pallas_call(
        functools.partial(_fft1_kernel, n_t2=n_t2),
        grid=(row_w // tc,),
        in_specs=[pl.BlockSpec((n, tc), lambda j: (0, j)),
                  pl.BlockSpec((n, 2 * n), lambda j: (0, 0)),
                  pl.BlockSpec((2 * n, 2 * n), lambda j: (0, 0)),
                  pl.BlockSpec((n_t2, n, 1), lambda j: (j, 0, 0)),
                  pl.BlockSpec((n_t2, n, 1), lambda j: (j, 0, 0))],
        out_specs=pl.BlockSpec((2 * n, tc), lambda j: (0, j)),
        out_shape=jax.ShapeDtypeStruct((2 * n, row_w), BF16),
        compiler_params=_cparams(("arbitrary",)),
        name="fnet_stage1",
    )(u.reshape(T_ALL // n, row_w), fc["cs_ch"], fc["w1"], fc["twc"], fc["tws"])
    n_k2 = 8
    o = pl.pallas_call(
        functools.partial(_fft2_kernel, n_k2=n_k2),
        grid=(n // n_k2,),
        in_specs=[pl.BlockSpec((2, n_k2, n, FN_W), lambda j: (0, j, 0, 0)),
                  pl.BlockSpec((n, n), lambda j: (0, 0)),
                  pl.BlockSpec((n, n), lambda j: (0, 0))],
        out_specs=pl.BlockSpec((n, n_k2 * FN_W), lambda j: (0, j)),
        out_shape=jax.ShapeDtypeStruct((T_ALL // n, row_w), BF16),
        compiler_params=_cparams(("arbitrary",)),
        name="fnet_stage2",
    )(y.reshape(2, n, n, FN_W), fc["c2"], fc["s2"])
    o = o.reshape(T_ALL, FN_W)
    if not with_ctx:
        return o
    cb = T_LAT // T_CTX
    return pl.pallas_call(
        _fft_ctx_kernel,
        grid=(1,),
        in_specs=[pl.BlockSpec((T_CTX, FN_W), lambda j: (cb, 0)),
                  pl.BlockSpec((n, 2 * n), lambda j: (0, 0)),
                  pl.BlockSpec((T_CTX, T_CTX), lambda j: (0, 0)),
                  pl.BlockSpec((T_CTX, T_CTX), lambda j: (0, 0)),
                  pl.BlockSpec(memory_space=pl.ANY)],
        out_specs=pl.BlockSpec((T_CTX, FN_W), lambda j: (cb, 0)),
        out_shape=jax.ShapeDtypeStruct((T_ALL, FN_W), BF16),
        input_output_aliases={4: 0},
        compiler_params=_cparams(("arbitrary",)),
        name="fnet_ctx",
    )(u, fc["cs_ch"], fc["cc"], fc["sc"], o)


def _merge_kernel(x_ref, mod_ref, n1_ref, ona_ref, of_ref, ob_ref, z_ref, ofn_ref, dnn_ref,
                  wg_ref, wna_ref, wdn_ref, wfn_ref, wout_ref, o_ref, *, tm, has_ctx):
    i = pl.program_id(0)
    x = x_ref[...]
    tail_is_ctx = i == pl.num_programs(0) - 1
    h = _modnorm_tile(x, n1_ref[...], mod_ref, 0, 1, tail_is_ctx, has_ctx).astype(BF16)
    o = of_ref[...].astype(F32) + ob_ref[...].astype(F32)
    z = z_ref[...].astype(F32)
    parts = []
    for hd in range(DN_HEADS):
        sl = slice(hd * DN_DH, (hd + 1) * DN_DH)
        oh = o[:, sl]
        oh = oh * lax.rsqrt(jnp.mean(oh * oh, axis=-1, keepdims=True) + EPS) * dnn_ref[...]
        parts.append(oh * _silu(z[:, sl]))
    odn = jnp.concatenate(parts, axis=-1).astype(BF16)
    branches = ((ona_ref[...], wna_ref), (odn, wdn_ref), (ofn_ref[...], wfn_ref))
    logits = [_dot(h, wg_ref[:, b * D:(b + 1) * D]) for b in range(N_BRANCH)]
    proj = [_dot(br, w_ref[...]) for br, w_ref in branches]
    y = jax.nn.sigmoid(logits[0]) * proj[0]
    for b in range(1, N_BRANCH):
        y = y + jax.nn.sigmoid(logits[b]) * proj[b]
    y = _dot(y.astype(BF16), wout_ref[...])
    o_ref[...] = _gated_residual(x, y, mod_ref, 2, tail_is_ctx, has_ctx)


def _merge(xs, mod, norm1, p, o_na, o_f, o_b, o_fn, dn_norm, w_gate, w_na_o, w_dn_o, w_fn, w_out, has_ctx):
    rows, tm = (T_ALL, TM_MERGE_ALL) if has_ctx else (T_LAT, TM_MERGE_LAT)
    row = lambda w: pl.BlockSpec((tm, w), lambda i: (i, 0))
    full = lambda a: pl.BlockSpec(a.shape, lambda i: (0, 0), pipeline_mode=pl.Buffered(1))
    return pl.pallas_call(
        functools.partial(_merge_kernel, tm=tm, has_ctx=has_ctx),
        grid=(rows // tm,),
        in_specs=[row(D), pl.BlockSpec((8, 6 * D), lambda i: (0, 0)), pl.BlockSpec((1, D), lambda i: (0, 0)),
                  row(NA_W), row(DN_W), row(DN_W),
                  pl.BlockSpec((tm, DN_W), lambda i: (i, P_Z)),
                  row(FN_W), pl.BlockSpec((1, DN_DH), lambda i: (0, 0)),
                  full(w_gate), full(w_na_o), full(w_dn_o), full(w_fn), full(w_out)],
        out_specs=row(D),
        out_shape=jax.ShapeDtypeStruct((rows, D), F32),
        compiler_params=_cparams(("arbitrary",)),
        name="merge",
    )(xs, mod, norm1.reshape(1, D), o_na, o_f, o_b, p, o_fn, dn_norm.reshape(1, DN_DH),
      w_gate, w_na_o, w_dn_o, w_fn, w_out)


def _mlp_kernel(x_ref, mod_ref, n_ref, w1_ref, w2_ref, nf_ref, o_ref, h_scr, acc_scr, *, tm, has_ctx, final):
    i = pl.program_id(0)
    j = pl.program_id(1)
    tail_is_ctx = i == pl.num_programs(0) - 1

    @pl.when(j == 0)
    def _():
        h_scr[...] = _modnorm_tile(x_ref[...], n_ref[...], mod_ref, 3, 4, tail_is_ctx, has_ctx).astype(BF16)
        acc_scr[...] = jnp.zeros_like(acc_scr)

    a = jnp.maximum(_dot(h_scr[...], w1_ref[...]), 0.0)
    acc_scr[...] += _dot((a * a).astype(BF16), w2_ref[...])

    @pl.when(j == pl.num_programs(1) - 1)
    def _():
        xn = _gated_residual(x_ref[...], acc_scr[...], mod_ref, 5, tail_is_ctx, has_ctx)
        if final:
            xn = xn * lax.rsqrt(jnp.mean(xn * xn, axis=-1, keepdims=True) + EPS) * nf_ref[...]
        o_ref[...] = xn


def _mlp(xs, mod, norm, w1, w2, norm_f, has_ctx, final):
    rows, tm = (T_ALL, TM_ALL) if has_ctx else (T_LAT, TM_LAT)
    th = 1024
    return pl.pallas_call(
        functools.partial(_mlp_kernel, tm=tm, has_ctx=has_ctx, final=final),
        grid=(rows // tm, HID // th),
        in_specs=[pl.BlockSpec((tm, D), lambda i, j: (i, 0)),
                  pl.BlockSpec((8, 6 * D), lambda i, j: (0, 0)),
                  pl.BlockSpec((1, D), lambda i, j: (0, 0)),
                  pl.BlockSpec((D, th), lambda i, j: (0, j)),
                  pl.BlockSpec((th, D), lambda i, j: (j, 0)),
                  pl.BlockSpec((1, D), lambda i, j: (0, 0))],
        out_specs=pl.BlockSpec((tm, D), lambda i, j: (i, 0)),
        out_shape=jax.ShapeDtypeStruct((rows, D), F32),
        scratch_shapes=[pltpu.VMEM((tm, D), BF16), pltpu.VMEM((tm, D), F32)],
        compiler_params=_cparams(("arbitrary", "arbitrary")),
        name="mlp",
    )(xs, mod, norm.reshape(1, D), w1, w2, norm_f.reshape(1, D))


def _split_in_weights(w_in):
    na_k, na_v, dn_k, dn_v, dn_ab, na_q, dn_q, dn_z, fn_u, gate = jnp.split(
        w_in, np.cumsum([NA_W, NA_W, DN_W, DN_W, 4 * DN_HEADS, NA_W, DN_W, DN_W, FN_W]).tolist(), axis=-1)
    wcat = jnp.concatenate([na_q, na_k, na_v, dn_q, dn_k, dn_v, dn_z, fn_u], axis=-1).astype(BF16)
    wg = jnp.pad(dn_ab, ((0, 0), (0, GP - 4 * DN_HEADS))).astype(BF16)
    return wcat, wg, gate.astype(BF16)


def kernel(x, c, ctx, c_ctx, w_ada, b_ada, norm1, w_in, conv_w, a_log, dt_bias, dn_norm, rpb,
           w_na_o, w_dn_o, w_fn, w_out, norm2, w_mlp1, w_mlp2, norm_f):
    xs = jnp.concatenate([x[0], ctx[0]], axis=0)
    cc = jnp.concatenate([c, c_ctx[None, :], jnp.zeros((6, D), F32)], axis=0)
    mods = _ada(cc, w_ada, b_ada)
    rope = _rope_tables()
    fc = _fft_consts()
    for l in range(DEPTH):
        has_ctx = l < DEPTH - 1
        final = l == DEPTH - 1
        wcat, wg, w_gate = _split_in_weights(w_in[l])
        u, p, g = _in_proj(xs, mods[l], norm1[l], wcat, wg)
        o_na = _na(p, _na_tables(rpb[l]), has_ctx)
        qkv, gb = _dn_prep(p, g, conv_w[l], a_log[l], dt_bias[l], rope)
        o_f, o_b = _dn_scan(*_dn_wy(qkv, gb, DN_BLK_CHUNKS), DN_BLK_CHUNKS)
        o_fn = _fnet(u, fc, has_ctx)
        xs = _merge(xs, mods[l], norm1[l], p, o_na, o_f, o_b, o_fn, dn_norm[l], w_gate,
                    w_na_o[l].astype(BF16), w_dn_o[l].astype(BF16), w_fn[l].astype(BF16), w_out[l].astype(BF16),
                    has_ctx)
        xs = _mlp(xs, mods[l], norm2[l], w_mlp1[l].astype(BF16), w_mlp2[l].astype(BF16), norm_f, has_ctx, final)
    return xs[None]
```

```python
import functools
import math

import numpy as np
import jax
import jax.numpy as jnp
from jax import lax
from jax.experimental import pallas as pl
from jax.experimental.pallas import tpu as pltpu

F32 = jnp.float32
BF16 = jnp.bfloat16

D = 1024
T_LAT = 16384
T_CTX = 256
T_ALL = T_LAT + T_CTX
DEPTH = 2
GRID_W = 64
GRID_H = T_LAT // GRID_W
NA_HEADS = 8
NA_DH = 64
NA_WIN_H = 8
NA_WIN_W = 16
NA_W = NA_HEADS * NA_DH
DN_HEADS = 4
DN_DH = 128
DN_W = DN_HEADS * DN_DH
DN_CONV = 5
CHUNK = 64
FN_GROUPS = 4
FN_DG = 128
FN_W = FN_GROUPS * FN_DG
N_BRANCH = 3
HID = 4 * D
ROPE_BASE = 10000.0
EPS = 1e-6
NEG = -1e30

P_W = 3 * NA_W + 4 * DN_W + FN_W
P_NAQ, P_NAK, P_NAV = 0, 1, 2
P_DN = 1
P_Z = 6
TN_IN = P_W // 2
GP = 128

TM_ALL = 1280
TM_LAT = 1024
TM_MERGE_ALL = 640
TM_MERGE_LAT = 512
NA_BLK = 4 * GRID_W
NA_HEAD_GROUP = 2
FFT_N = 128
DN_BLK_CHUNKS = 4
WY_GROUP_CHUNKS = 4
VMEM_LIMIT = 56 * 1024 * 1024


def _cparams(sem):
    return pltpu.CompilerParams(dimension_semantics=sem, vmem_limit_bytes=VMEM_LIMIT)


def _dot(a, b):
    return jnp.dot(a, b, preferred_element_type=F32)


def _dot_nt(a, b):
    return lax.dot_general(a, b, (((1,), (1,)), ((), ())), preferred_element_type=F32)


def _dot_tn(a, b):
    return lax.dot_general(a, b, (((0,), (0,)), ((), ())), preferred_element_type=F32)


def _silu(x):
    return x * jax.nn.sigmoid(x)


def _ada_kernel(c_ref, w_ref, b_ref, o_ref):
    s = _silu(c_ref[...])
    o_ref[0] = jnp.dot(s, w_ref[0], preferred_element_type=F32, precision=lax.Precision.HIGHEST) + b_ref[0]


def _ada(cc, w_ada, b_ada):
    tn = 1536
    return pl.pallas_call(
        _ada_kernel,
        grid=(DEPTH, 6 * D // tn),
        in_specs=[pl.BlockSpec((8, D), lambda l, j: (0, 0)),
                  pl.BlockSpec((1, D, tn), lambda l, j: (l, 0, j)),
                  pl.BlockSpec((1, 1, tn), lambda l, j: (l, 0, j))],
        out_specs=pl.BlockSpec((1, 8, tn), lambda l, j: (l, 0, j)),
        out_shape=jax.ShapeDtypeStruct((DEPTH, 8, 6 * D), F32),
        compiler_params=_cparams(("arbitrary", "arbitrary")),
        name="ada_mod",
    )(cc, w_ada, b_ada.reshape(DEPTH, 1, 6 * D))


def _row_slices(tm, has_ctx):
    return [slice(0, tm)] if not has_ctx else [slice(0, tm - T_CTX), slice(tm - T_CTX, tm)]


def _mod_vecs(mod_ref, k, tail_is_ctx, has_ctx):
    lat = mod_ref[0:1, k * D:(k + 1) * D]
    if not has_ctx:
        return [lat]
    return [lat, jnp.where(tail_is_ctx, mod_ref[1:2, k * D:(k + 1) * D], lat)]


def _modnorm(x, g, shift, scale):
    y = x * lax.rsqrt(jnp.mean(x * x, axis=-1, keepdims=True) + EPS)
    return y * (g * (1.0 + scale)) + shift


def _modnorm_tile(x, g, mod_ref, k_shift, k_scale, tail_is_ctx, has_ctx):
    tm = x.shape[0]
    parts = [_modnorm(x[sl], g, sh, sc) for sl, sh, sc in zip(_row_slices(tm, has_ctx),
                                                            _mod_vecs(mod_ref, k_shift, tail_is_ctx, has_ctx),
                                                            _mod_vecs(mod_ref, k_scale, tail_is_ctx, has_ctx))]
    return parts[0] if len(parts) == 1 else jnp.concatenate(parts, axis=0)


def _gated_residual(x, y, mod_ref, k_gate, tail_is_ctx, has_ctx):
    tm = x.shape[0]
    parts = [x[sl] + gt * y[sl] for sl, gt in zip(_row_slices(tm, has_ctx), _mod_vecs(mod_ref, k_gate, tail_is_ctx, has_ctx))]
    return parts[0] if len(parts) == 1 else jnp.concatenate(parts, axis=0)


def _token_tile(x_ref, ctx_ref, tail_scr, is_last):
    if ctx_ref is None:
        return x_ref[...]
    top = x_ref.shape[0] - T_CTX

    @pl.when(jnp.logical_not(is_last))
    def _():
        tail_scr[...] = x_ref[top:, :]

    @pl.when(is_last)
    def _():
        tail_scr[...] = ctx_ref[...]

    return jnp.concatenate([x_ref[0:top, :], tail_scr[...]], axis=0)


def _in_proj_kernel(*refs, tm, split):
    if split:
        x_ref, ctx_ref, mod_ref, n_ref, w_ref, wg_ref, u_ref, p_ref, g_ref, h_scr, tail_scr = refs
    else:
        x_ref, mod_ref, n_ref, w_ref, wg_ref, u_ref, p_ref, g_ref, h_scr = refs
        ctx_ref = tail_scr = None
    i = pl.program_id(0)
    j = pl.program_id(1)

    @pl.when(j == 0)
    def _():
        is_last = i == pl.num_programs(0) - 1
        x = _token_tile(x_ref, ctx_ref, tail_scr, is_last)
        h = _modnorm_tile(x, n_ref[...], mod_ref, 0, 1, is_last, True).astype(BF16)
        h_scr[...] = h
        g_ref[...] = _dot(h, wg_ref[...])

    p = _dot(h_scr[...], w_ref[...]).astype(BF16)
    p_ref[...] = p

    @pl.when(j == pl.num_programs(1) - 1)
    def _():
        u_ref[...] = p[:, TN_IN - FN_W:]


def _in_proj(x, ctx, mod, norm, wcat, wg):
    tm = TM_ALL
    nj = P_W // TN_IN
    split = ctx is not None
    ctx_specs = [pl.BlockSpec((T_CTX, D), lambda i, j: (0, 0))] if split else []
    return pl.pallas_call(
        functools.partial(_in_proj_kernel, tm=tm, split=split),
        grid=(T_ALL // tm, nj),
        in_specs=[pl.BlockSpec((tm, D), lambda i, j: (i, 0))] + ctx_specs + [
                  pl.BlockSpec((8, 6 * D), lambda i, j: (0, 0)),
                  pl.BlockSpec((1, D), lambda i, j: (0, 0)),
                  pl.BlockSpec((D, TN_IN), lambda i, j: (0, j)),
                  pl.BlockSpec((D, GP), lambda i, j: (0, 0))],
        out_specs=[pl.BlockSpec((tm, FN_W), lambda i, j: (i, 0)),
                   pl.BlockSpec((tm, TN_IN), lambda i, j: (i, j)),
                   pl.BlockSpec((tm, GP), lambda i, j: (i, 0))],
        out_shape=[jax.ShapeDtypeStruct((T_ALL, FN_W), BF16),
                   jax.ShapeDtypeStruct((T_ALL, P_W), BF16),
                   jax.ShapeDtypeStruct((T_ALL, GP), F32)],
        scratch_shapes=[pltpu.VMEM((tm, D), BF16)] + ([pltpu.VMEM((T_CTX, D), F32)] if split else []),
        compiler_params=_cparams(("arbitrary", "arbitrary")),
        name="in_proj",
    )(*([x, ctx] if split else [x]), mod, norm.reshape(1, D), wcat, wg)


def _na_tables(rpb):
    qr = np.arange(4)[:, None, None, None]
    qc = np.arange(GRID_W)[None, :, None, None]
    kk = np.arange(12)[None, None, :, None]
    kc = np.arange(GRID_W)[None, None, None, :]
    valid = []
    for b in (0, 1, GRID_H // 4 - 1):
        r = 4 * b + qr
        kr = 4 * (b - 1) + kk
        rs = np.clip(r - NA_WIN_H // 2, 0, GRID_H - NA_WIN_H)
        cs = np.clip(qc - NA_WIN_W // 2, 0, GRID_W - NA_WIN_W)
        ok = (kr >= 0) & (kr < GRID_H) & (kr >= rs) & (kr < rs + NA_WIN_H) & (kc >= cs) & (kc < cs + NA_WIN_W)
        valid.append(np.broadcast_to(ok, (4, GRID_W, 12, GRID_W)).reshape(NA_BLK, 12 * GRID_W))
    valid.append(np.zeros_like(valid[0]))
    valid = np.stack(valid)
    col = np.arange(GRID_W)
    onehot = ((col[None, None, :] - col[None, :, None] + NA_WIN_W - 1)
              == np.arange(2 * NA_WIN_W - 1)[:, None, None]).astype(np.float32)
    tcol = jnp.einsum('hrd,dqk->hrqk', rpb, onehot, precision=lax.Precision.HIGHEST)
    bias = jnp.stack([jnp.stack([tcol[:, k - q + NA_WIN_H - 5] for k in range(12)], axis=2) for q in range(4)], axis=1)
    bias = bias.reshape(NA_HEADS, NA_BLK, 12 * GRID_W)
    return jnp.where(valid[:, None], bias[None], NEG).astype(BF16)


def _na_kernel(q_ref, k0_ref, k1_ref, k2_ref, v0_ref, v1_ref, v2_ref, kc_ref, vc_ref, tab_ref, o_ref):
    q = q_ref[...]
    kl = jnp.concatenate([k0_ref[...], k1_ref[...], k2_ref[...]], axis=0)
    vl = jnp.concatenate([v0_ref[...], v1_ref[...], v2_ref[...]], axis=0)
    kc = kc_ref[...]
    vc = vc_ref[...]
    lane_l = lax.broadcasted_iota(jnp.int32, (vl.shape[0], 2 * NA_DH), 1)
    lane_c = lax.broadcasted_iota(jnp.int32, (vc.shape[0], 2 * NA_DH), 1)
    one = jnp.ones((), BF16)
    outs = []
    for h0 in range(0, NA_HEADS, NA_HEAD_GROUP):
        heads = range(h0, h0 + NA_HEAD_GROUP)
        sls = [slice(h * NA_DH, (h + 1) * NA_DH) for h in heads]
        pls = [slice((h // 2) * 2 * NA_DH, (h // 2 + 1) * 2 * NA_DH) for h in heads]
        own = [(lambda ln, h=h: (ln < NA_DH) if h % 2 == 0 else (ln >= NA_DH)) for h in heads]
        qs = [q[:, sl] * (NA_DH ** -0.5) for sl in sls]
        z_loc = [_dot_nt(qh, kl[:, sl]).astype(BF16) + tab_ref[0, h] for h, qh, sl in zip(heads, qs, sls)]
        z_ctx = [_dot_nt(qh, kc[:, sl]).astype(BF16) for qh, sl in zip(qs, sls)]
        m = [jnp.maximum(jnp.max(a, axis=-1, keepdims=True), jnp.max(b, axis=-1, keepdims=True))
             for a, b in zip(z_loc, z_ctx)]
        p_loc = [jnp.exp(a - mm) for a, mm in zip(z_loc, m)]
        p_ctx = [jnp.exp(b - mm) for b, mm in zip(z_ctx, m)]
        oa = [_dot(a, jnp.where(f(lane_l), vl[:, ps], one)) + _dot(b, jnp.where(f(lane_c), vc[:, ps], one))
              for a, b, ps, f in zip(p_loc, p_ctx, pls, own)]
        outs += [(x[:, :NA_DH] / x[:, NA_DH:NA_DH + 1]) if h % 2 == 0 else (x[:, NA_DH:] / x[:, 0:1])
                 for h, x in zip(heads, oa)]
    o_ref[...] = jnp.concatenate(outs, axis=-1).astype(BF16)


def _na(p, tab, with_ctx):
    nlat = T_LAT // NA_BLK
    nb = nlat + (1 if with_ctx else 0)
    ctx_blk = T_LAT // NA_BLK

    def kidx(b, part):
        return jnp.clip(jnp.minimum(b, nlat - 1) - 1 + part, 0, nlat - 1)

    def variant(b):
        return jnp.where(b == 0, 0, jnp.where(b == nlat - 1, 2, jnp.where(b >= nlat, 3, 1)))

    blk = (NA_BLK, NA_W)
    in_specs = [pl.BlockSpec(blk, lambda b: (b, P_NAQ))]
    in_specs += [pl.BlockSpec(blk, functools.partial(lambda b, part: (kidx(b, part), P_NAK), part=part))
                 for part in range(3)]
    in_specs += [pl.BlockSpec(blk, functools.partial(lambda b, part: (kidx(b, part), P_NAV), part=part))
                 for part in range(3)]
    in_specs += [pl.BlockSpec(blk, lambda b: (ctx_blk, P_NAK)),
                 pl.BlockSpec(blk, lambda b: (ctx_blk, P_NAV)),
                 pl.BlockSpec((1, NA_HEADS, NA_BLK, 3 * NA_BLK), lambda b: (variant(b), 0, 0, 0))]
    return pl.pallas_call(
        _na_kernel,
        grid=(nb,),
        in_specs=in_specs,
        out_specs=pl.BlockSpec(blk, lambda b: (b, 0)),
        out_shape=jax.ShapeDtypeStruct((nb * NA_BLK, NA_W), BF16),
        compiler_params=_cparams(("arbitrary",)),
        name="na_attn",
    )(p, p, p, p, p, p, p, p, p, tab)


def _rope_tables():
    t = jnp.arange(T_LAT, dtype=jnp.int32)
    half = DN_DH // 2
    pos = jnp.stack([t // GRID_W, t % GRID_W], axis=-1).astype(F32)
    inv = 1.0 / (ROPE_BASE ** (jnp.arange(0, half, 2, dtype=F32) / half))
    ang = pos[:, :, None] * inv[None, None, :]
    ang = jnp.concatenate([ang, ang], axis=-1).reshape(T_LAT, DN_DH)
    cos, sin = jnp.cos(ang), jnp.sin(ang)
    low = (np.arange(DN_DH) % half) < half // 2
    return cos, jnp.where(low[None], -sin, 0.0), jnp.where(low[None], 0.0, sin)


def _dn_prep_kernel(x_ref, prev_ref, next_ref, g_ref, cw_ref, alog_ref, dtb_ref, cos_ref, sa_ref, sb_ref,
                    qkv_ref, gb_ref, *, tm):
    b = pl.program_id(0)
    nlat = T_LAT // tm
    first = jnp.logical_or(b == 0, b == nlat)
    last = b >= nlat - 1
    is_ctx = b >= nlat
    cur = x_ref[...].astype(F32)
    prev = jnp.where(first, 0.0, prev_ref[...][14:16].astype(F32))
    nxt = jnp.where(last, 0.0, next_ref[...][0:2].astype(F32))
    ext = jnp.concatenate([prev, cur, nxt], axis=0)
    cw = cw_ref[...]
    y = ext[0:tm] * cw[0:1]
    for k in range(1, DN_CONV):
        y = y + ext[k:k + tm] * cw[k:k + 1]
    y = _silu(y)
    cos = jnp.where(is_ctx, 1.0, cos_ref[...])
    sa = jnp.where(is_ctx, 0.0, sa_ref[...])
    sb = jnp.where(is_ctx, 0.0, sb_ref[...])
    quarter = DN_DH // 4
    for s in range(2):
        for h in range(DN_HEADS):
            c0 = s * DN_W + h * DN_DH
            xh = y[:, c0:c0 + DN_DH]
            xh = xh * lax.rsqrt(jnp.sum(xh * xh, axis=-1, keepdims=True) + EPS)
            xh = (xh * cos + pltpu.roll(xh, DN_DH - quarter, axis=1) * sa + pltpu.roll(xh, quarter, axis=1) * sb)
            if s == 0:
                xh = xh * (DN_DH ** -0.5)
            qkv_ref[:, c0:c0 + DN_DH] = xh.astype(BF16)
    qkv_ref[:, 2 * DN_W:] = y[:, 2 * DN_W:].astype(BF16)
    g = g_ref[...]
    lane = lax.broadcasted_iota(jnp.int32, g.shape, 1)
    decay = -jnp.exp(alog_ref[...]) * jax.nn.softplus(g + dtb_ref[...])
    gb_ref[...] = jnp.where(lane < 2 * DN_HEADS, decay, jax.nn.sigmoid(g))


def _dn_prep(p, g, conv_w, a_log, dt_bias, rope):
    tm = 256
    nb = T_ALL // tm
    nlat = T_LAT // tm
    hb = 16
    nh = T_ALL // hb
    pad = lambda v: jnp.pad(v.reshape(1, 2 * DN_HEADS).astype(F32), ((0, 0), (0, GP - 2 * DN_HEADS)))
    rspec = pl.BlockSpec((tm, DN_DH), lambda b: (jnp.minimum(b, nlat - 1), 0))
    return pl.pallas_call(
        functools.partial(_dn_prep_kernel, tm=tm),
        grid=(nb,),
        in_specs=[pl.BlockSpec((tm, 3 * DN_W), lambda b: (b, P_DN)),
                  pl.BlockSpec((hb, 3 * DN_W), lambda b: (jnp.maximum(b * (tm // hb) - 1, 0), P_DN)),
                  pl.BlockSpec((hb, 3 * DN_W), lambda b: (jnp.minimum((b + 1) * (tm // hb), nh - 1), P_DN)),
                  pl.BlockSpec((tm, GP), lambda b: (b, 0)),
                  pl.BlockSpec((DN_CONV, 3 * DN_W), lambda b: (0, 0)),
                  pl.BlockSpec((1, GP), lambda b: (0, 0)),
                  pl.BlockSpec((1, GP), lambda b: (0, 0)),
                  rspec, rspec, rspec],
        out_specs=[pl.BlockSpec((tm, 3 * DN_W), lambda b: (b, 0)),
                   pl.BlockSpec((tm, GP), lambda b: (b, 0))],
        out_shape=[jax.ShapeDtypeStruct((T_ALL, 3 * DN_W), BF16),
                   jax.ShapeDtypeStruct((T_ALL, GP), F32)],
        compiler_params=_cparams(("arbitrary",)),
        name="dn_prep",
    )(p, p, p, g, conv_w, pad(a_log), pad(dt_bias), *rope)


def _pair_blockdiag(y):
    c = y.shape[0]
    lane = lax.broadcasted_iota(jnp.int32, y.shape, 1)
    zero = jnp.zeros((), y.dtype)
    return jnp.concatenate([jnp.where(lane < c, y, zero), jnp.where(lane >= c, y, zero)], axis=0)


def _unit_tri_inverses_minus_identity(mats):
    c = mats[0].shape[0]
    ri = lax.broadcasted_iota(jnp.int32, (c, 2 * c), 0)
    ci = lax.broadcasted_iota(jnp.int32, (c, 2 * c), 1) % c

    def same_block(s):
        return (ri // s) == (ci // s)

    def mm(xs, ys):
        return [_dot(x.astype(BF16), _pair_blockdiag(y.astype(BF16))) for x, y in zip(xs, ys)]

    diag = same_block(8)
    b1 = [jnp.where(diag, -a, 0.0) for a in mats]
    b2 = mm(b1, b1)
    b3 = mm(b1, b2)
    b4 = mm(b2, b2)
    n2 = [x + y + z for x, y, z in zip(b1, b2, b3)]
    n = [x + y + z for x, y, z in zip(n2, b4, mm(n2, b4))]
    for s in (8, 16, 32):
        off = jnp.logical_and(same_block(2 * s), jnp.logical_not(same_block(s)))
        lo = [jnp.where(off, a, 0.0) for a in mats]
        x = [p + q for p, q in zip(lo, mm(n, lo))]
        n = [p - (q + r) for p, q, r in zip(n, x, mm(x, n))]
    return n


def _dn_wy_kernel(qkv_ref, gb_ref, u_ref, wq_ref, kq_ref, dl_ref, *, nchunk):
    c = CHUNK
    tm = nchunk * c
    gb = gb_ref[...]
    qkv = qkv_ref[...]
    rt = lax.broadcasted_iota(jnp.int32, (tm, tm), 0)
    ct = lax.broadcasted_iota(jnp.int32, (tm, tm), 1)
    same_chunk = (rt // c) == (ct // c)
    ri = lax.broadcasted_iota(jnp.int32, (c, 2 * c), 0)
    lane2 = lax.broadcasted_iota(jnp.int32, (c, 2 * c), 1)
    ci = lane2 % c
    first = lane2 < c
    lane = lax.broadcasted_iota(jnp.int32, (8, GP), 1)
    lane_k = lax.broadcasted_iota(jnp.int32, (c, 2 * DN_DH), 1)
    zk = jnp.zeros((), BF16)
    eye_p = (ri == ci).astype(F32)
    eye2 = (lax.broadcasted_iota(jnp.int32, (DN_DH, 2 * DN_DH), 1) % DN_DH
            == lax.broadcasted_iota(jnp.int32, (DN_DH, 2 * DN_DH), 0)).astype(BF16)
    gb_t = gb.T
    dl_rows, gcs = [], []
    for d in range(2):
        cum = jnp.logical_and(same_chunk, (ct <= rt) if d == 0 else (ct >= rt))
        gc_all = jnp.dot(cum.astype(F32), gb, preferred_element_type=F32, precision=lax.Precision.HIGHEST)
        gcs.append((gc_all, gc_all.T))
        dl_rows.append([jnp.exp(gc_all[j * c + (c - 1 if d == 0 else 0):j * c + (c if d == 0 else 1), :])
                        for j in range(nchunk)])
    for j0 in range(0, nchunk, WY_GROUP_CHUNKS):
        shared = {}
        for j in range(j0, j0 + WY_GROUP_CHUNKS):
            for pr in range(DN_HEADS // 2):
                r0 = j * c
                hs = slice(2 * pr * DN_DH, (2 * pr + 2) * DN_DH)
                bd = lambda x: jnp.concatenate([jnp.where(lane_k < DN_DH, x, zk), jnp.where(lane_k >= DN_DH, x, zk)],
                                               axis=0)
                q2 = qkv[r0:r0 + c, hs]
                k2 = qkv[r0:r0 + c, DN_W + hs.start:DN_W + hs.stop]
                v2 = qkv[r0:r0 + c, 2 * DN_W + hs.start:2 * DN_W + hs.stop]
                bd_k = bd(k2)
                shared[(j, pr)] = dict(q2=q2, bd_k=bd_k, bd_v=bd(v2),
                                       gqt=_dot_nt(jnp.concatenate([k2, q2, eye2], axis=0), bd_k))
        units = [(d, j, pr) for j in range(j0, j0 + WY_GROUP_CHUNKS) for d in range(2) for pr in range(DN_HEADS // 2)]
        pre = []
        for d, j, pr in units:
            gc_all, gc_all_t = gcs[d]
            incl = (ci <= ri) if d == 0 else (ci >= ri)
            strict = (ci < ri) if d == 0 else (ci > ri)
            r0 = j * c
            last = r0 + (c - 1 if d == 0 else 0)
            i0 = d * DN_HEADS + 2 * pr
            col = lambda x, a: x[r0:r0 + c, a:a + 1]
            row = lambda x, a: jnp.concatenate([x[a:a + 1, r0:r0 + c], x[a + 1:a + 2, r0:r0 + c]], axis=1)
            beta_p = jnp.where(first, col(gb, 2 * DN_HEADS + i0), col(gb, 2 * DN_HEADS + i0 + 1))
            gc_p = jnp.where(first, col(gc_all, i0), col(gc_all, i0 + 1))
            gc_row = row(gc_all_t, i0)
            beta_row = row(gb_t, 2 * DN_HEADS + i0)
            g_last = jnp.where(first[0:1], gc_all[last:last + 1, i0:i0 + 1], gc_all[last:last + 1, i0 + 1:i0 + 2])
            e = jnp.exp(jnp.where(incl, gc_p - gc_row, 0.0))
            eg_row = jnp.exp(gc_row)
            pre.append(dict(beta_p=beta_p, e_incl=jnp.where(incl, e, 0.0), e_strict=jnp.where(strict, e, 0.0),
                            beta_row=beta_row, eg_row=eg_row, tail_row=jnp.exp(g_last - gc_row)))
        gqt = [shared[(j, pr)]["gqt"] for d, j, pr in units]
        ns = _unit_tri_inverses_minus_identity([p["beta_p"] * x[:c] * p["e_strict"] for x, p in zip(gqt, pre)])
        tmat = {u_: n + eye_p for u_, n in zip(units, ns)}
        prd = dict(zip(units, pre))
        both = [(j, pr) for j in range(j0, j0 + WY_GROUP_CHUNKS) for pr in range(DN_HEADS // 2)]
        lhs_u = [jnp.concatenate([(tmat[(d, j, pr)] * prd[(d, j, pr)]["beta_row"]).astype(BF16) for d in range(2)], axis=0)
                 for j, pr in both]
        lhs_w = [jnp.concatenate([(tmat[(d, j, pr)] * (prd[(d, j, pr)]["beta_row"] * prd[(d, j, pr)]["eg_row"])
                                   ).astype(BF16) for d in range(2)], axis=0) for j, pr in both]
        u2 = dict(zip(both, [_dot(a, shared[jp]["bd_v"]) for a, jp in zip(lhs_u, both)]))
        w2 = dict(zip(both, [_dot(a, shared[jp]["bd_k"]) for a, jp in zip(lhs_w, both)]))
        for (d, j, pr), p, x in zip(units, pre, gqt):
            r0 = j * c
            i0 = d * DN_HEADS + 2 * pr
            cs = slice(i0 * DN_DH, (i0 + 2) * DN_DH)
            ks = slice(i0 * c, (i0 + 2) * c)
            q2 = shared[(j, pr)]["q2"].astype(F32)
            gcol = gcs[d][0][r0:r0 + c]
            eg = jnp.where(lane_k < DN_DH, jnp.exp(gcol[:, i0:i0 + 1]), jnp.exp(gcol[:, i0 + 1:i0 + 2]))
            u_ref[r0:r0 + c, cs] = u2[(j, pr)][d * c:(d + 1) * c].astype(BF16)
            wq_ref[2 * r0:2 * r0 + c, cs] = w2[(j, pr)][d * c:(d + 1) * c].astype(BF16)
            wq_ref[2 * r0 + c:2 * r0 + 2 * c, cs] = (q2 * eg).astype(BF16)
            kq_ref[3 * r0:3 * r0 + DN_DH, ks] = (x[2 * c:] * p["tail_row"]).astype(BF16)
            kq_ref[3 * r0 + DN_DH:3 * r0 + DN_DH + c, ks] = (x[c:2 * c] * p["e_incl"]).astype(BF16)
    for j in range(nchunk):
        row = jnp.where(lane < DN_HEADS, dl_rows[0][j], dl_rows[1][j])
        dl_ref[8 * j:8 * j + 8, :] = row


def _dn_wy(qkv, gb, nchunk):
    tm = nchunk * CHUNK
    nb = T_ALL // tm
    nch = 2 * DN_HEADS
    return pl.pallas_call(
        functools.partial(_dn_wy_kernel, nchunk=nchunk),
        grid=(nb,),
        in_specs=[pl.BlockSpec((tm, 3 * DN_W), lambda i: (i, 0)),
                  pl.BlockSpec((tm, GP), lambda i: (i, 0))],
        out_specs=[pl.BlockSpec((tm, nch * DN_DH), lambda i: (i, 0)),
                   pl.BlockSpec((2 * tm, nch * DN_DH), lambda i: (i, 0)),
                   pl.BlockSpec((3 * tm, nch * CHUNK), lambda i: (i, 0)),
                   pl.BlockSpec((8 * nchunk, GP), lambda i: (i, 0))],
        out_shape=[jax.ShapeDtypeStruct((T_ALL, nch * DN_DH), BF16),
                   jax.ShapeDtypeStruct((2 * T_ALL, nch * DN_DH), BF16),
                   jax.ShapeDtypeStruct((3 * T_ALL, nch * CHUNK), BF16),
                   jax.ShapeDtypeStruct((8 * T_ALL // CHUNK, GP), F32)],
        compiler_params=_cparams(("arbitrary",)),
        name="dn_wy",
    )(qkv, gb)


def _dn_scan_kernel(u_f_ref, wq_f_ref, kq_f_ref, dl_f_ref, u_b_ref, wq_b_ref, kq_b_ref, dl_b_ref,
                    of_ref, ob_ref, s_scr, *, nchunk):
    i = pl.program_id(0)

    @pl.when(i == 0)
    def _():
        s_scr[...] = jnp.zeros_like(s_scr)

    c = CHUNK
    states = [s_scr[idx] for idx in range(2 * DN_HEADS)]
    zero = jnp.zeros((c, DN_DH), BF16)
    refs = ((u_f_ref, wq_f_ref, kq_f_ref, dl_f_ref, of_ref), (u_b_ref, wq_b_ref, kq_b_ref, dl_b_ref, ob_ref))
    chains = [(d, h) for d in range(2) for h in range(DN_HEADS)]
    pairs = [(d, p) for d in range(2) for p in range(DN_HEADS // 2)]
    for step in range(nchunk):
        blk = lambda d: step if d == 0 else nchunk - 1 - step
        hs = lambda h: slice(h * DN_DH, (h + 1) * DN_DH)
        r1 = [_dot(refs[d][1][2 * blk(d) * c:2 * (blk(d) + 1) * c, hs(h)], states[d * DN_HEADS + h].astype(BF16))
              for d, h in chains]
        v_new = [(refs[d][0][blk(d) * c:(blk(d) + 1) * c, hs(h)].astype(F32) - r[:c]).astype(BF16)
                 for (d, h), r in zip(chains, r1)]
        r2 = [_dot(refs[d][2][3 * blk(d) * c:3 * (blk(d) + 1) * c, p * 2 * c:(p + 1) * 2 * c],
                   jnp.concatenate([jnp.concatenate([v_new[d * DN_HEADS + 2 * p], zero], axis=1),
                                    jnp.concatenate([zero, v_new[d * DN_HEADS + 2 * p + 1]], axis=1)], axis=0))
              for d, p in pairs]
        for d, h in chains:
            idx = d * DN_HEADS + h
            r = r2[d * (DN_HEADS // 2) + h // 2]
            ts = slice((h % 2) * DN_DH, (h % 2 + 1) * DN_DH)
            dl_row = refs[d][3][8 * blk(d):8 * blk(d) + 1, :]
            states[idx] = states[idx] * dl_row[:, idx:idx + 1] + r[:DN_DH, ts]
            refs[d][4][blk(d) * c:(blk(d) + 1) * c, hs(h)] = (r1[idx][c:] + r[DN_DH:, ts]).astype(BF16)
    for idx in range(2 * DN_HEADS):
        s_scr[idx] = states[idx]


def _dn_scan(u, wq, kq, dl, nchunk):
    tm = nchunk * CHUNK
    nb = T_ALL // tm
    nlat = T_LAT // tm
    fwd = lambda i: jnp.where(i == 0, nlat, i - 1)
    bwd = lambda i: nb - 1 - i
    specs = lambda f, col: [pl.BlockSpec((tm, DN_W), lambda i: (f(i), col)),
                            pl.BlockSpec((2 * tm, DN_W), lambda i: (f(i), col)),
                            pl.BlockSpec((3 * tm, DN_HEADS * CHUNK), lambda i: (f(i), col)),
                            pl.BlockSpec((8 * nchunk, GP), lambda i: (f(i), 0))]
    return pl.pallas_call(
        functools.partial(_dn_scan_kernel, nchunk=nchunk),
        grid=(nb,),
        in_specs=specs(fwd, 0) + specs(bwd, 1),
        out_specs=[pl.BlockSpec((tm, DN_W), lambda i: (fwd(i), 0)),
                   pl.BlockSpec((tm, DN_W), lambda i: (bwd(i), 0))],
        out_shape=[jax.ShapeDtypeStruct((T_ALL, DN_W), BF16),
                   jax.ShapeDtypeStruct((T_ALL, DN_W), BF16)],
        scratch_shapes=[pltpu.VMEM((2 * DN_HEADS, DN_DH, DN_DH), F32)],
        compiler_params=_cparams(("arbitrary",)),
        name="dn_scan",
    )(u, wq, kq, dl, u, wq, kq, dl)


def _dft_cos_sin(n):
    k = np.arange(n)
    ang = 2.0 * np.pi * ((k[:, None] * k[None, :]) % n) / n
    return np.cos(ang), np.sin(ang)


def _fft_consts():
    n = FFT_N
    c, s = _dft_cos_sin(n)
    cs_ch = np.concatenate([c, s], axis=1)
    w1 = np.block([[c, -s], [-s, -c]])
    k2 = np.arange(n)[None, :, None]
    t2 = np.arange(n)[:, None, None]
    ang = 2.0 * np.pi * ((k2 * t2) % (n * n)) / (n * n)
    scale = 1.0 / math.sqrt(T_LAT * FN_DG)
    cc, sc = _dft_cos_sin(T_CTX)
    scale_c = 1.0 / math.sqrt(T_CTX * FN_DG)
    f32 = lambda a: jnp.asarray(a, F32)
    bf = lambda a: f32(a).astype(BF16)
    return dict(cs_ch=bf(cs_ch), w1=bf(w1), twc=f32(np.cos(ang)), tws=f32(np.sin(ang)),
                c2=bf(c * scale), s2=bf(s * scale), cc=bf(cc * scale_c), sc=bf(sc * scale_c))


def _fft1_kernel(u_ref, cs_ref, w1_ref, twc_ref, tws_ref, y_ref, *, n_t2):
    n = FFT_N
    cs = cs_ref[...]
    w1 = w1_ref[...]
    for t in range(n_t2):
        twc = twc_ref[t]
        tws = tws_ref[t]
        for g in range(0, FN_GROUPS, 2):
            c0 = (t * FN_GROUPS + g) * FN_DG
            ab = [_dot(u_ref[:, c0 + s * FN_DG:c0 + (s + 1) * FN_DG], cs) for s in range(2)]
            rhs = jnp.concatenate([jnp.concatenate([ab[0][:, :FN_DG], ab[1][:, :FN_DG]], axis=1),
                                   jnp.concatenate([ab[0][:, FN_DG:], ab[1][:, FN_DG:]], axis=1)], axis=0)
            y = _dot(w1, rhs.astype(BF16))
            yr, yi = y[:n], y[n:]
            y_ref[0:n, c0:c0 + 2 * FN_DG] = (yr * twc + yi * tws).astype(BF16)
            y_ref[n:2 * n, c0:c0 + 2 * FN_DG] = (yi * twc - yr * tws).astype(BF16)


def _fft2_kernel(y_ref, c2_ref, s2_ref, o_ref, *, n_k2):
    c2 = c2_ref[...]
    s2 = s2_ref[...]
    for j in range(n_k2):
        o_ref[:, j * FN_W:(j + 1) * FN_W] = (_dot(c2, y_ref[0, j]) + _dot(s2, y_ref[1, j])).astype(BF16)


def _fft_ctx_kernel(u_ref, cs_ref, cc_ref, sc_ref, o_in_ref, o_ref):
    del o_in_ref
    for g in range(FN_GROUPS):
        ab = _dot(u_ref[:, g * FN_DG:(g + 1) * FN_DG], cs_ref[...])
        a = ab[:, :FN_DG].astype(BF16)
        b = ab[:, FN_DG:].astype(BF16)
        o_ref[:, g * FN_DG:(g + 1) * FN_DG] = (_dot(cc_ref[...], a) - _dot(sc_ref[...], b)).astype(BF16)


def _fnet(u, fc, with_ctx):
    n = FFT_N
    row_w = n * FN_W
    n_t2 = 8
    tc = n_t2 * FN_W
    y = pl.pallas_call(
        functools.partial(_fft1_kernel, n_t2=n_t2),
        grid=(row_w // tc,),
        in_specs=[pl.BlockSpec((n, tc), lambda j: (0, j)),
                  pl.BlockSpec((n, 2 * n), lambda j: (0, 0)),
                  pl.BlockSpec((2 * n, 2 * n), lambda j: (0, 0)),
                  pl.BlockSpec((n_t2, n, 1), lambda j: (j, 0, 0)),
                  pl.BlockSpec((n_t2, n, 1), lambda j: (j, 0, 0))],
        out_specs=pl.BlockSpec((2 * n, tc), lambda j: (0, j)),
        out_shape=jax.ShapeDtypeStruct((2 * n, row_w), BF16),
        compiler_params=_cparams(("arbitrary",)),
        name="fnet_stage1",
    )(u.reshape(T_ALL // n, row_w), fc["cs_ch"], fc["w1"], fc["twc"], fc["tws"])
    n_k2 = 8
    o = pl.pallas_call(
        functools.partial(_fft2_kernel, n_k2=n_k2),
        grid=(n // n_k2,),
        in_specs=[pl.BlockSpec((2, n_k2, n, FN_W), lambda j: (0, j, 0, 0)),
                  pl.BlockSpec((n, n), lambda j: (0, 0)),
                  pl.BlockSpec((n, n), lambda j: (0, 0))],
        out_specs=pl.BlockSpec((n, n_k2 * FN_W), lambda j: (0, j)),
        out_shape=jax.ShapeDtypeStruct(((T_ALL if with_ctx else T_LAT) // n, row_w), BF16),
        compiler_params=_cparams(("arbitrary",)),
        name="fnet_stage2",
    )(y.reshape(2, n, n, FN_W), fc["c2"], fc["s2"])
    if not with_ctx:
        return o.reshape(T_LAT, FN_W)
    o = o.reshape(T_ALL, FN_W)
    cb = T_LAT // T_CTX
    return pl.pallas_call(
        _fft_ctx_kernel,
        grid=(1,),
        in_specs=[pl.BlockSpec((T_CTX, FN_W), lambda j: (cb, 0)),
                  pl.BlockSpec((n, 2 * n), lambda j: (0, 0)),
                  pl.BlockSpec((T_CTX, T_CTX), lambda j: (0, 0)),
                  pl.BlockSpec((T_CTX, T_CTX), lambda j: (0, 0)),
                  pl.BlockSpec(memory_space=pl.ANY)],
        out_specs=pl.BlockSpec((T_CTX, FN_W), lambda j: (cb, 0)),
        out_shape=jax.ShapeDtypeStruct((T_ALL, FN_W), BF16),
        input_output_aliases={4: 0},
        compiler_params=_cparams(("arbitrary",)),
        name="fnet_ctx",
    )(u, fc["cs_ch"], fc["cc"], fc["sc"], o)


def _merge_kernel(*refs, tm, has_ctx, split):
    if split:
        (x_ref, ctx_ref, mod_ref, n1_ref, ona_ref, of_ref, ob_ref, z_ref, ofn_ref, dnn_ref,
         wg_ref, wna_ref, wdn_ref, wfn_ref, wout_ref, o_ref, tail_scr) = refs
    else:
        (x_ref, mod_ref, n1_ref, ona_ref, of_ref, ob_ref, z_ref, ofn_ref, dnn_ref,
         wg_ref, wna_ref, wdn_ref, wfn_ref, wout_ref, o_ref) = refs
        ctx_ref = tail_scr = None
    i = pl.program_id(0)
    tail_is_ctx = i == pl.num_programs(0) - 1
    x = _token_tile(x_ref, ctx_ref, tail_scr, tail_is_ctx)
    h = _modnorm_tile(x, n1_ref[...], mod_ref, 0, 1, tail_is_ctx, has_ctx).astype(BF16)
    o = of_ref[...].astype(F32) + ob_ref[...].astype(F32)
    z = z_ref[...].astype(F32)
    parts = []
    for hd in range(DN_HEADS):
        sl = slice(hd * DN_DH, (hd + 1) * DN_DH)
        oh = o[:, sl]
        oh = oh * lax.rsqrt(jnp.mean(oh * oh, axis=-1, keepdims=True) + EPS) * dnn_ref[...]
        parts.append(oh * _silu(z[:, sl]))
    odn = jnp.concatenate(parts, axis=-1).astype(BF16)
    branches = ((ona_ref[...], wna_ref), (odn, wdn_ref), (ofn_ref[...], wfn_ref))
    logits = [_dot(h, wg_ref[:, b * D:(b + 1) * D]) for b in range(N_BRANCH)]
    proj = [_dot(br, w_ref[...]) for br, w_ref in branches]
    y = jax.nn.sigmoid(logits[0]) * proj[0]
    for b in range(1, N_BRANCH):
        y = y + jax.nn.sigmoid(logits[b]) * proj[b]
    y = _dot(y.astype(BF16), wout_ref[...])
    o_ref[...] = _gated_residual(x, y, mod_ref, 2, tail_is_ctx, has_ctx)


def _merge(x, ctx, mod, norm1, p, o_na, o_f, o_b, o_fn, dn_norm, w_gate, w_na_o, w_dn_o, w_fn, w_out, has_ctx):
    rows, tm = (T_ALL, TM_MERGE_ALL) if has_ctx else (T_LAT, TM_MERGE_LAT)
    split = ctx is not None
    row = lambda w: pl.BlockSpec((tm, w), lambda i: (i, 0))
    full = lambda a: pl.BlockSpec(a.shape, lambda i: (0, 0), pipeline_mode=pl.Buffered(1))
    ctx_specs = [pl.BlockSpec((T_CTX, D), lambda i: (0, 0))] if split else []
    return pl.pallas_call(
        functools.partial(_merge_kernel, tm=tm, has_ctx=has_ctx, split=split),
        grid=(rows // tm,),
        in_specs=[row(D)] + ctx_specs + [
                  pl.BlockSpec((8, 6 * D), lambda i: (0, 0)), pl.BlockSpec((1, D), lambda i: (0, 0)),
                  row(NA_W), row(DN_W), row(DN_W),
                  pl.BlockSpec((tm, DN_W), lambda i: (i, P_Z)),
                  row(FN_W), pl.BlockSpec((1, DN_DH), lambda i: (0, 0)),
                  full(w_gate), full(w_na_o), full(w_dn_o), full(w_fn), full(w_out)],
        out_specs=row(D),
        out_shape=jax.ShapeDtypeStruct((rows, D), F32),
        scratch_shapes=[pltpu.VMEM((T_CTX, D), F32)] if split else [],
        compiler_params=_cparams(("arbitrary",)),
        name="merge",
    )(*([x, ctx] if split else [x]), mod, norm1.reshape(1, D), o_na, o_f, o_b, p, o_fn, dn_norm.reshape(1, DN_DH),
      w_gate, w_na_o, w_dn_o, w_fn, w_out)


def _mlp_kernel(x_ref, xnext_ref, mod_ref, n_ref, w1_ref, w2_ref, nf_ref, o_ref, h_scr, acc_scr, *,
                tm, has_ctx, final):
    i = pl.program_id(0)
    j = pl.program_id(1)
    nt = pl.num_programs(0)
    nj = pl.num_programs(1)
    slot = i % 2

    @pl.when(jnp.logical_and(i == 0, j == 0))
    def _():
        h_scr[0] = _modnorm_tile(x_ref[...], n_ref[...], mod_ref, 3, 4, nt == 1, has_ctx).astype(BF16)

    def hidden_step(first):
        a = jnp.maximum(_dot(h_scr[slot], w1_ref[...]), 0.0)
        upd = _dot((a * a).astype(BF16), w2_ref[...])
        if first:
            acc_scr[...] = upd
        else:
            acc_scr[...] += upd

    @pl.when(j == 0)
    def _():
        hidden_step(True)

    @pl.when(jnp.logical_and(j > 0, j < nj - 1))
    def _():
        hidden_step(False)

    @pl.when(j == nj - 1)
    def _():
        h_scr[1 - slot] = _modnorm_tile(xnext_ref[...], n_ref[...], mod_ref, 3, 4, i + 1 == nt - 1, has_ctx).astype(BF16)
        hidden_step(False)
        xn = _gated_residual(x_ref[...], acc_scr[...], mod_ref, 5, i == nt - 1, has_ctx)
        if final:
            xn = xn * lax.rsqrt(jnp.mean(xn * xn, axis=-1, keepdims=True) + EPS) * nf_ref[...]
        o_ref[...] = xn


def _mlp(xs, mod, norm, w1, w2, norm_f, has_ctx, final):
    rows, tm = (T_ALL, TM_ALL) if has_ctx else (T_LAT, TM_LAT)
    th = 1024
    return pl.pallas_call(
        functools.partial(_mlp_kernel, tm=tm, has_ctx=has_ctx, final=final),
        grid=(rows // tm, HID // th),
        in_specs=[pl.BlockSpec((tm, D), lambda i, j: (i, 0)),
                  pl.BlockSpec((tm, D), lambda i, j: (jnp.minimum(i + 1, rows // tm - 1), 0)),
                  pl.BlockSpec((8, 6 * D), lambda i, j: (0, 0)),
                  pl.BlockSpec((1, D), lambda i, j: (0, 0)),
                  pl.BlockSpec((D, th), lambda i, j: (0, j)),
                  pl.BlockSpec((th, D), lambda i, j: (j, 0)),
                  pl.BlockSpec((1, D), lambda i, j: (0, 0))],
        out_specs=pl.BlockSpec((tm, D), lambda i, j: (i, 0)),
        out_shape=jax.ShapeDtypeStruct((rows, D), F32),
        scratch_shapes=[pltpu.VMEM((2, tm, D), BF16), pltpu.VMEM((tm, D), F32)],
        compiler_params=_cparams(("arbitrary", "arbitrary")),
        name="mlp",
    )(xs, xs, mod, norm.reshape(1, D), w1, w2, norm_f.reshape(1, D))


def _split_in_weights(w_in):
    na_k, na_v, dn_k, dn_v, dn_ab, na_q, dn_q, dn_z, fn_u, gate = jnp.split(
        w_in, np.cumsum([NA_W, NA_W, DN_W, DN_W, 4 * DN_HEADS, NA_W, DN_W, DN_W, FN_W]).tolist(), axis=-1)
    wcat = jnp.concatenate([na_q, na_k, na_v, dn_q, dn_k, dn_v, dn_z, fn_u], axis=-1).astype(BF16)
    wg = jnp.pad(dn_ab, ((0, 0), (0, GP - 4 * DN_HEADS))).astype(BF16)
    return wcat, wg, gate.astype(BF16)


def kernel(x, c, ctx, c_ctx, w_ada, b_ada, norm1, w_in, conv_w, a_log, dt_bias, dn_norm, rpb,
           w_na_o, w_dn_o, w_fn, w_out, norm2, w_mlp1, w_mlp2, norm_f):
    xs, xc = x[0], ctx[0]
    cc = jnp.concatenate([c, c_ctx[None, :], jnp.zeros((6, D), F32)], axis=0)
    mods = _ada(cc, w_ada, b_ada)
    rope = _rope_tables()
    fc = _fft_consts()
    for l in range(DEPTH):
        has_ctx = l < DEPTH - 1
        final = l == DEPTH - 1
        wcat, wg, w_gate = _split_in_weights(w_in[l])
        u, p, g = _in_proj(xs, xc, mods[l], norm1[l], wcat, wg)
        o_na = _na(p, _na_tables(rpb[l]), has_ctx)
        qkv, gb = _dn_prep(p, g, conv_w[l], a_log[l], dt_bias[l], rope)
        o_f, o_b = _dn_scan(*_dn_wy(qkv, gb, DN_BLK_CHUNKS), DN_BLK_CHUNKS)
        o_fn = _fnet(u, fc, has_ctx)
        xs, xc = _merge(xs, xc, mods[l], norm1[l], p, o_na, o_f, o_b, o_fn, dn_norm[l], w_gate,
                    w_na_o[l].astype(BF16), w_dn_o[l].astype(BF16), w_fn[l].astype(BF16), w_out[l].astype(BF16),
                    has_ctx), None
        xs = _mlp(xs, mods[l], norm2[l], w_mlp1[l].astype(BF16), w_mlp2[l].astype(BF16), norm_f, has_ctx, final)
    return xs[None]
```

```python
import functools
import math

import numpy as np
import jax
import jax.numpy as jnp
from jax import lax
from jax.experimental import pallas as pl
from jax.experimental.pallas import tpu as pltpu

F32 = jnp.float32
BF16 = jnp.bfloat16

D = 1024
T_LAT = 16384
T_CTX = 256
T_ALL = T_LAT + T_CTX
DEPTH = 2
GRID_W = 64
GRID_H = T_LAT // GRID_W
NA_HEADS = 8
NA_DH = 64
NA_WIN_H = 8
NA_WIN_W = 16
NA_W = NA_HEADS * NA_DH
DN_HEADS = 4
DN_DH = 128
DN_W = DN_HEADS * DN_DH
DN_CONV = 5
CHUNK = 64
FN_GROUPS = 4
FN_DG = 128
FN_W = FN_GROUPS * FN_DG
N_BRANCH = 3
HID = 4 * D
ROPE_BASE = 10000.0
EPS = 1e-6
NEG = -1e30

P_W = 3 * NA_W + 4 * DN_W + FN_W
P_NAQ, P_NAK, P_NAV = 0, 1, 2
P_DN = 1
P_Z = 6
TN_IN = P_W // 2
GP = 128

TM_ALL = 1280
TM_LAT = 1024
TM_MERGE_ALL = 640
TM_MERGE_LAT = 512
NA_BLK = 4 * GRID_W
NA_HEAD_GROUP = 2
NA_VARIANTS = 4
FFT_N = 128
DN_BLK_CHUNKS = 4
WY_GROUP_CHUNKS = 4
VMEM_LIMIT = 56 * 1024 * 1024


def _cparams(sem):
    return pltpu.CompilerParams(dimension_semantics=sem, vmem_limit_bytes=VMEM_LIMIT)


def _dot(a, b):
    return jnp.dot(a, b, preferred_element_type=F32)


def _dot_nt(a, b):
    return lax.dot_general(a, b, (((1,), (1,)), ((), ())), preferred_element_type=F32)


def _dot_tn(a, b):
    return lax.dot_general(a, b, (((0,), (0,)), ((), ())), preferred_element_type=F32)


def _silu(x):
    return x * jax.nn.sigmoid(x)


def _ada_kernel(c_ref, w_ref, b_ref, o_ref):
    s = _silu(c_ref[...])
    s_hi = s.astype(BF16)
    s_lo = (s - s_hi.astype(F32)).astype(BF16)
    w = w_ref[0].astype(BF16)
    o_ref[0] = _dot(s_hi, w) + _dot(s_lo, w) + b_ref[0]


def _ada(cc, w_ada, b_ada):
    tn = 1536
    return pl.pallas_call(
        _ada_kernel,
        grid=(DEPTH, 6 * D // tn),
        in_specs=[pl.BlockSpec((8, D), lambda l, j: (0, 0)),
                  pl.BlockSpec((1, D, tn), lambda l, j: (l, 0, j)),
                  pl.BlockSpec((1, 1, tn), lambda l, j: (l, 0, j))],
        out_specs=pl.BlockSpec((1, 8, tn), lambda l, j: (l, 0, j)),
        out_shape=jax.ShapeDtypeStruct((DEPTH, 8, 6 * D), F32),
        compiler_params=_cparams(("arbitrary", "arbitrary")),
        name="ada_mod",
    )(cc, w_ada, b_ada.reshape(DEPTH, 1, 6 * D))


def _row_slices(tm, has_ctx):
    return [slice(0, tm)] if not has_ctx else [slice(0, tm - T_CTX), slice(tm - T_CTX, tm)]


def _mod_vecs(mod_ref, k, tail_is_ctx, has_ctx):
    lat = mod_ref[0:1, k * D:(k + 1) * D]
    if not has_ctx:
        return [lat]
    return [lat, jnp.where(tail_is_ctx, mod_ref[1:2, k * D:(k + 1) * D], lat)]


def _modnorm(x, g, shift, scale):
    y = x * lax.rsqrt(jnp.mean(x * x, axis=-1, keepdims=True) + EPS)
    return y * (g * (1.0 + scale)) + shift


def _modnorm_tile(x, g, mod_ref, k_shift, k_scale, tail_is_ctx, has_ctx):
    tm = x.shape[0]
    parts = [_modnorm(x[sl], g, sh, sc) for sl, sh, sc in zip(_row_slices(tm, has_ctx),
                                                            _mod_vecs(mod_ref, k_shift, tail_is_ctx, has_ctx),
                                                            _mod_vecs(mod_ref, k_scale, tail_is_ctx, has_ctx))]
    return parts[0] if len(parts) == 1 else jnp.concatenate(parts, axis=0)


def _gated_residual(x, y, mod_ref, k_gate, tail_is_ctx, has_ctx):
    tm = x.shape[0]
    parts = [x[sl] + gt * y[sl] for sl, gt in zip(_row_slices(tm, has_ctx), _mod_vecs(mod_ref, k_gate, tail_is_ctx, has_ctx))]
    return parts[0] if len(parts) == 1 else jnp.concatenate(parts, axis=0)


def _token_tile(x_ref, ctx_ref, tail_scr, is_last):
    if ctx_ref is None:
        return x_ref[...]
    top = x_ref.shape[0] - T_CTX

    @pl.when(jnp.logical_not(is_last))
    def _():
        tail_scr[...] = x_ref[top:, :]

    @pl.when(is_last)
    def _():
        tail_scr[...] = ctx_ref[...]

    return jnp.concatenate([x_ref[0:top, :], tail_scr[...]], axis=0)


def _in_proj_kernel(*refs, tm, split):
    if split:
        x_ref, ctx_ref, mod_ref, n_ref, w_ref, wg_ref, u_ref, p_ref, g_ref, h_scr, tail_scr = refs
    else:
        x_ref, mod_ref, n_ref, w_ref, wg_ref, u_ref, p_ref, g_ref, h_scr = refs
        ctx_ref = tail_scr = None
    i = pl.program_id(0)
    j = pl.program_id(1)

    @pl.when(j == 0)
    def _():
        is_last = i == pl.num_programs(0) - 1
        x = _token_tile(x_ref, ctx_ref, tail_scr, is_last)
        h = _modnorm_tile(x, n_ref[...], mod_ref, 0, 1, is_last, True).astype(BF16)
        h_scr[...] = h
        g_ref[...] = _dot(h, wg_ref[...])

    p = _dot(h_scr[...], w_ref[...]).astype(BF16)
    p_ref[...] = p

    @pl.when(j == pl.num_programs(1) - 1)
    def _():
        u_ref[...] = p[:, TN_IN - FN_W:]


def _in_proj(x, ctx, mod, norm, wcat, wg):
    tm = TM_ALL
    nj = P_W // TN_IN
    split = ctx is not None
    ctx_specs = [pl.BlockSpec((T_CTX, D), lambda i, j: (0, 0))] if split else []
    return pl.pallas_call(
        functools.partial(_in_proj_kernel, tm=tm, split=split),
        grid=(T_ALL // tm, nj),
        in_specs=[pl.BlockSpec((tm, D), lambda i, j: (i, 0))] + ctx_specs + [
                  pl.BlockSpec((8, 6 * D), lambda i, j: (0, 0)),
                  pl.BlockSpec((1, D), lambda i, j: (0, 0)),
                  pl.BlockSpec((D, TN_IN), lambda i, j: (0, j)),
                  pl.BlockSpec((D, GP), lambda i, j: (0, 0))],
        out_specs=[pl.BlockSpec((tm, FN_W), lambda i, j: (i, 0)),
                   pl.BlockSpec((tm, TN_IN), lambda i, j: (i, j)),
                   pl.BlockSpec((tm, GP), lambda i, j: (i, 0))],
        out_shape=[jax.ShapeDtypeStruct((T_ALL, FN_W), BF16),
                   jax.ShapeDtypeStruct((T_ALL, P_W), BF16),
                   jax.ShapeDtypeStruct((T_ALL, GP), F32)],
        scratch_shapes=[pltpu.VMEM((tm, D), BF16)] + ([pltpu.VMEM((T_CTX, D), F32)] if split else []),
        compiler_params=_cparams(("arbitrary", "arbitrary")),
        name="in_proj",
    )(*([x, ctx] if split else [x]), mod, norm.reshape(1, D), wcat, wg)


def _na_tables(rpb):
    qr = np.arange(4)[:, None, None, None]
    qc = np.arange(GRID_W)[None, :, None, None]
    kk = np.arange(12)[None, None, :, None]
    kc = np.arange(GRID_W)[None, None, None, :]
    valid = []
    for b in (0, 1, GRID_H // 4 - 1):
        r = 4 * b + qr
        kr = 4 * (b - 1) + kk
        rs = np.clip(r - NA_WIN_H // 2, 0, GRID_H - NA_WIN_H)
        cs = np.clip(qc - NA_WIN_W // 2, 0, GRID_W - NA_WIN_W)
        ok = (kr >= 0) & (kr < GRID_H) & (kr >= rs) & (kr < rs + NA_WIN_H) & (kc >= cs) & (kc < cs + NA_WIN_W)
        valid.append(np.broadcast_to(ok, (4, GRID_W, 12, GRID_W)).reshape(NA_BLK, 12 * GRID_W))
    valid.append(np.zeros_like(valid[0]))
    valid = np.stack(valid)
    col = np.arange(GRID_W)
    onehot = ((col[None, None, :] - col[None, :, None] + NA_WIN_W - 1)
              == np.arange(2 * NA_WIN_W - 1)[:, None, None]).astype(np.float32)
    tcol = jnp.einsum('lhrd,dqk->lhrqk', rpb, onehot, precision=lax.Precision.HIGHEST)
    bias = jnp.stack([jnp.stack([tcol[:, :, k - q + NA_WIN_H - 5] for k in range(12)], axis=3) for q in range(4)], axis=2)
    bias = bias.reshape(DEPTH, 1, NA_HEADS, NA_BLK, 12 * GRID_W)
    tab = jnp.where(valid[None, :, None], bias, NEG).astype(BF16)
    return tab.reshape(DEPTH * len(valid), NA_HEADS, NA_BLK, 12 * GRID_W)


def _na_kernel(q_ref, k0_ref, k1_ref, k2_ref, v0_ref, v1_ref, v2_ref, kc_ref, vc_ref, tab_ref, o_ref):
    q = q_ref[...]
    kl = jnp.concatenate([k0_ref[...], k1_ref[...], k2_ref[...]], axis=0)
    vl = jnp.concatenate([v0_ref[...], v1_ref[...], v2_ref[...]], axis=0)
    kc = kc_ref[...]
    vc = vc_ref[...]
    lane_l = lax.broadcasted_iota(jnp.int32, (vl.shape[0], 2 * NA_DH), 1)
    lane_c = lax.broadcasted_iota(jnp.int32, (vc.shape[0], 2 * NA_DH), 1)
    one = jnp.ones((), BF16)
    outs = []
    for h0 in range(0, NA_HEADS, NA_HEAD_GROUP):
        heads = range(h0, h0 + NA_HEAD_GROUP)
        sls = [slice(h * NA_DH, (h + 1) * NA_DH) for h in heads]
        pls = [slice((h // 2) * 2 * NA_DH, (h // 2 + 1) * 2 * NA_DH) for h in heads]
        own = [(lambda ln, h=h: (ln < NA_DH) if h % 2 == 0 else (ln >= NA_DH)) for h in heads]
        qs = [q[:, sl] * (NA_DH ** -0.5) for sl in sls]
        z_loc = [_dot_nt(qh, kl[:, sl]).astype(BF16) + tab_ref[0, h] for h, qh, sl in zip(heads, qs, sls)]
        z_ctx = [_dot_nt(qh, kc[:, sl]).astype(BF16) for qh, sl in zip(qs, sls)]
        m = [jnp.maximum(jnp.max(a, axis=-1, keepdims=True), jnp.max(b, axis=-1, keepdims=True))
             for a, b in zip(z_loc, z_ctx)]
        p_loc = [jnp.exp(a - mm) for a, mm in zip(z_loc, m)]
        p_ctx = [jnp.exp(b - mm) for b, mm in zip(z_ctx, m)]
        oa = [_dot(a, jnp.where(f(lane_l), vl[:, ps], one)) + _dot(b, jnp.where(f(lane_c), vc[:, ps], one))
              for a, b, ps, f in zip(p_loc, p_ctx, pls, own)]
        outs += [(x[:, :NA_DH] / x[:, NA_DH:NA_DH + 1]) if h % 2 == 0 else (x[:, NA_DH:] / x[:, 0:1])
                 for h, x in zip(heads, oa)]
    o_ref[...] = jnp.concatenate(outs, axis=-1).astype(BF16)


def _na(p, tab, layer, with_ctx):
    nlat = T_LAT // NA_BLK
    nb = nlat + (1 if with_ctx else 0)
    ctx_blk = T_LAT // NA_BLK

    def kidx(b, part):
        return jnp.clip(jnp.minimum(b, nlat - 1) - 1 + part, 0, nlat - 1)

    def variant(b):
        return layer * NA_VARIANTS + jnp.where(b == 0, 0, jnp.where(b == nlat - 1, 2, jnp.where(b >= nlat, 3, 1)))

    blk = (NA_BLK, NA_W)
    in_specs = [pl.BlockSpec(blk, lambda b: (b, P_NAQ))]
    in_specs += [pl.BlockSpec(blk, functools.partial(lambda b, part: (kidx(b, part), P_NAK), part=part))
                 for part in range(3)]
    in_specs += [pl.BlockSpec(blk, functools.partial(lambda b, part: (kidx(b, part), P_NAV), part=part))
                 for part in range(3)]
    in_specs += [pl.BlockSpec(blk, lambda b: (ctx_blk, P_NAK)),
                 pl.BlockSpec(blk, lambda b: (ctx_blk, P_NAV)),
                 pl.BlockSpec((1, NA_HEADS, NA_BLK, 3 * NA_BLK), lambda b: (variant(b), 0, 0, 0))]
    return pl.pallas_call(
        _na_kernel,
        grid=(nb,),
        in_specs=in_specs,
        out_specs=pl.BlockSpec(blk, lambda b: (b, 0)),
        out_shape=jax.ShapeDtypeStruct((nb * NA_BLK, NA_W), BF16),
        compiler_params=_cparams(("arbitrary",)),
        name="na_attn",
    )(p, p, p, p, p, p, p, p, p, tab)


def _rope_tables():
    t = jnp.arange(T_LAT, dtype=jnp.int32)
    half = DN_DH // 2
    pos = jnp.stack([t // GRID_W, t % GRID_W], axis=-1).astype(F32)
    inv = 1.0 / (ROPE_BASE ** (jnp.arange(0, half, 2, dtype=F32) / half))
    ang = pos[:, :, None] * inv[None, None, :]
    ang = jnp.concatenate([ang, ang], axis=-1).reshape(T_LAT, DN_DH)
    cos, sin = jnp.cos(ang), jnp.sin(ang)
    low = (np.arange(DN_DH) % half) < half // 2
    return cos, jnp.where(low[None], -sin, 0.0), jnp.where(low[None], 0.0, sin)


def _dn_prep_kernel(x_ref, prev_ref, next_ref, g_ref, cw_ref, alog_ref, dtb_ref, cos_ref, sa_ref, sb_ref,
                    qkv_ref, gb_ref, *, tm):
    b = pl.program_id(0)
    nlat = T_LAT // tm
    first = jnp.logical_or(b == 0, b == nlat)
    last = b >= nlat - 1
    is_ctx = b >= nlat
    cur = x_ref[...].astype(F32)
    prev = jnp.where(first, 0.0, prev_ref[...][14:16].astype(F32))
    nxt = jnp.where(last, 0.0, next_ref[...][0:2].astype(F32))
    ext = jnp.concatenate([prev, cur, nxt], axis=0)
    cw = cw_ref[...]
    y = ext[0:tm] * cw[0:1]
    for k in range(1, DN_CONV):
        y = y + ext[k:k + tm] * cw[k:k + 1]
    y = _silu(y)
    cos = jnp.where(is_ctx, 1.0, cos_ref[...])
    sa = jnp.where(is_ctx, 0.0, sa_ref[...])
    sb = jnp.where(is_ctx, 0.0, sb_ref[...])
    quarter = DN_DH // 4
    for s in range(2):
        for h in range(DN_HEADS):
            c0 = s * DN_W + h * DN_DH
            xh = y[:, c0:c0 + DN_DH]
            xh = xh * lax.rsqrt(jnp.sum(xh * xh, axis=-1, keepdims=True) + EPS)
            xh = (xh * cos + pltpu.roll(xh, DN_DH - quarter, axis=1) * sa + pltpu.roll(xh, quarter, axis=1) * sb)
            if s == 0:
                xh = xh * (DN_DH ** -0.5)
            qkv_ref[:, c0:c0 + DN_DH] = xh.astype(BF16)
    qkv_ref[:, 2 * DN_W:] = y[:, 2 * DN_W:].astype(BF16)
    g = g_ref[...]
    lane = lax.broadcasted_iota(jnp.int32, g.shape, 1)
    decay = -jnp.exp(alog_ref[...]) * jax.nn.softplus(g + dtb_ref[...])
    gb_ref[...] = jnp.where(lane < 2 * DN_HEADS, decay, jax.nn.sigmoid(g))


def _dn_prep(p, g, conv_w, a_log, dt_bias, rope):
    tm = 256
    nb = T_ALL // tm
    nlat = T_LAT // tm
    hb = 16
    nh = T_ALL // hb
    pad = lambda v: jnp.pad(v.reshape(1, 2 * DN_HEADS).astype(F32), ((0, 0), (0, GP - 2 * DN_HEADS)))
    rspec = pl.BlockSpec((tm, DN_DH), lambda b: (jnp.minimum(b, nlat - 1), 0))
    return pl.pallas_call(
        functools.partial(_dn_prep_kernel, tm=tm),
        grid=(nb,),
        in_specs=[pl.BlockSpec((tm, 3 * DN_W), lambda b: (b, P_DN)),
                  pl.BlockSpec((hb, 3 * DN_W), lambda b: (jnp.maximum(b * (tm // hb) - 1, 0), P_DN)),
                  pl.BlockSpec((hb, 3 * DN_W), lambda b: (jnp.minimum((b + 1) * (tm // hb), nh - 1), P_DN)),
                  pl.BlockSpec((tm, GP), lambda b: (b, 0)),
                  pl.BlockSpec((DN_CONV, 3 * DN_W), lambda b: (0, 0)),
                  pl.BlockSpec((1, GP), lambda b: (0, 0)),
                  pl.BlockSpec((1, GP), lambda b: (0, 0)),
                  rspec, rspec, rspec],
        out_specs=[pl.BlockSpec((tm, 3 * DN_W), lambda b: (b, 0)),
                   pl.BlockSpec((tm, GP), lambda b: (b, 0))],
        out_shape=[jax.ShapeDtypeStruct((T_ALL, 3 * DN_W), BF16),
                   jax.ShapeDtypeStruct((T_ALL, GP), F32)],
        compiler_params=_cparams(("arbitrary",)),
        name="dn_prep",
    )(p, p, p, g, conv_w, pad(a_log), pad(dt_bias), *rope)


def _pair_blockdiag(y):
    c = y.shape[0]
    lane = lax.broadcasted_iota(jnp.int32, y.shape, 1)
    zero = jnp.zeros((), y.dtype)
    return jnp.concatenate([jnp.where(lane < c, y, zero), jnp.where(lane >= c, y, zero)], axis=0)


def _unit_tri_inverses_minus_identity(mats):
    c = mats[0].shape[0]
    ri = lax.broadcasted_iota(jnp.int32, (c, 2 * c), 0)
    ci = lax.broadcasted_iota(jnp.int32, (c, 2 * c), 1) % c

    def same_block(s):
        return (ri // s) == (ci // s)

    def mm(xs, ys):
        return [_dot(x.astype(BF16), _pair_blockdiag(y.astype(BF16))) for x, y in zip(xs, ys)]

    diag = same_block(8)
    b1 = [jnp.where(diag, -a, 0.0) for a in mats]
    b2 = mm(b1, b1)
    b3 = mm(b1, b2)
    b4 = mm(b2, b2)
    n2 = [x + y + z for x, y, z in zip(b1, b2, b3)]
    n = [x + y + z for x, y, z in zip(n2, b4, mm(n2, b4))]
    for s in (8, 16, 32):
        off = jnp.logical_and(same_block(2 * s), jnp.logical_not(same_block(s)))
        lo = [jnp.where(off, a, 0.0) for a in mats]
        x = [p + q for p, q in zip(lo, mm(n, lo))]
        n = [p - (q + r) for p, q, r in zip(n, x, mm(x, n))]
    return n


def _dn_wy_kernel(qkv_ref, gb_ref, u_ref, wq_ref, kq_ref, dl_ref, *, nchunk):
    c = CHUNK
    tm = nchunk * c
    gb = gb_ref[...]
    qkv = qkv_ref[...]
    rt = lax.broadcasted_iota(jnp.int32, (tm, tm), 0)
    ct = lax.broadcasted_iota(jnp.int32, (tm, tm), 1)
    same_chunk = (rt // c) == (ct // c)
    ri = lax.broadcasted_iota(jnp.int32, (c, 2 * c), 0)
    lane2 = lax.broadcasted_iota(jnp.int32, (c, 2 * c), 1)
    ci = lane2 % c
    first = lane2 < c
    lane = lax.broadcasted_iota(jnp.int32, (8, GP), 1)
    lane_k = lax.broadcasted_iota(jnp.int32, (c, 2 * DN_DH), 1)
    zk = jnp.zeros((), BF16)
    eye_p = (ri == ci).astype(F32)
    eye2 = (lax.broadcasted_iota(jnp.int32, (DN_DH, 2 * DN_DH), 1) % DN_DH
            == lax.broadcasted_iota(jnp.int32, (DN_DH, 2 * DN_DH), 0)).astype(BF16)
    gb_t = gb.T
    dl_rows, gcs = [], []
    for d in range(2):
        cum = jnp.logical_and(same_chunk, (ct <= rt) if d == 0 else (ct >= rt))
        gc_all = jnp.dot(cum.astype(F32), gb, preferred_element_type=F32, precision=lax.Precision.HIGHEST)
        gcs.append((gc_all, gc_all.T))
        dl_rows.append([jnp.exp(gc_all[j * c + (c - 1 if d == 0 else 0):j * c + (c if d == 0 else 1), :])
                        for j in range(nchunk)])
    for j0 in range(0, nchunk, WY_GROUP_CHUNKS):
        shared = {}
        for j in range(j0, j0 + WY_GROUP_CHUNKS):
            for pr in range(DN_HEADS // 2):
                r0 = j * c
                hs = slice(2 * pr * DN_DH, (2 * pr + 2) * DN_DH)
                bd = lambda x: jnp.concatenate([jnp.where(lane_k < DN_DH, x, zk), jnp.where(lane_k >= DN_DH, x, zk)],
                                               axis=0)
                q2 = qkv[r0:r0 + c, hs]
                k2 = qkv[r0:r0 + c, DN_W + hs.start:DN_W + hs.stop]
                v2 = qkv[r0:r0 + c, 2 * DN_W + hs.start:2 * DN_W + hs.stop]
                bd_k = bd(k2)
                shared[(j, pr)] = dict(q2=q2, bd_k=bd_k, bd_v=bd(v2),
                                       gqt=_dot_nt(jnp.concatenate([k2, q2, eye2], axis=0), bd_k))
        units = [(d, j, pr) for j in range(j0, j0 + WY_GROUP_CHUNKS) for d in range(2) for pr in range(DN_HEADS // 2)]
        pre = []
        for d, j, pr in units:
            gc_all, gc_all_t = gcs[d]
            incl = (ci <= ri) if d == 0 else (ci >= ri)
            strict = (ci < ri) if d == 0 else (ci > ri)
            r0 = j * c
            last = r0 + (c - 1 if d == 0 else 0)
            i0 = d * DN_HEADS + 2 * pr
            col = lambda x, a: x[r0:r0 + c, a:a + 1]
            row = lambda x, a: jnp.concatenate([x[a:a + 1, r0:r0 + c], x[a + 1:a + 2, r0:r0 + c]], axis=1)
            beta_p = jnp.where(first, col(gb, 2 * DN_HEADS + i0), col(gb, 2 * DN_HEADS + i0 + 1))
            gc_p = jnp.where(first, col(gc_all, i0), col(gc_all, i0 + 1))
            gc_row = row(gc_all_t, i0)
            beta_row = row(gb_t, 2 * DN_HEADS + i0)
            g_last = jnp.where(first[0:1], gc_all[last:last + 1, i0:i0 + 1], gc_all[last:last + 1, i0 + 1:i0 + 2])
            e = jnp.exp(jnp.where(incl, gc_p - gc_row, 0.0))
            eg_row = jnp.exp(gc_row)
            pre.append(dict(beta_p=beta_p, e_incl=jnp.where(incl, e, 0.0), e_strict=jnp.where(strict, e, 0.0),
                            beta_row=beta_row, eg_row=eg_row, tail_row=jnp.exp(g_last - gc_row)))
        gqt = [shared[(j, pr)]["gqt"] for d, j, pr in units]
        ns = _unit_tri_inverses_minus_identity([p["beta_p"] * x[:c] * p["e_strict"] for x, p in zip(gqt, pre)])
        tmat = {u_: n + eye_p for u_, n in zip(units, ns)}
        prd = dict(zip(units, pre))
        both = [(j, pr) for j in range(j0, j0 + WY_GROUP_CHUNKS) for pr in range(DN_HEADS // 2)]
        lhs_u = [jnp.concatenate([(tmat[(d, j, pr)] * prd[(d, j, pr)]["beta_row"]).astype(BF16) for d in range(2)], axis=0)
                 for j, pr in both]
        lhs_w = [jnp.concatenate([(tmat[(d, j, pr)] * (prd[(d, j, pr)]["beta_row"] * prd[(d, j, pr)]["eg_row"])
                                   ).astype(BF16) for d in range(2)], axis=0) for j, pr in both]
        u2 = dict(zip(both, [_dot(a, shared[jp]["bd_v"]) for a, jp in zip(lhs_u, both)]))
        w2 = dict(zip(both, [_dot(a, shared[jp]["bd_k"]) for a, jp in zip(lhs_w, both)]))
        for (d, j, pr), p, x in zip(units, pre, gqt):
            r0 = j * c
            i0 = d * DN_HEADS + 2 * pr
            cs = slice(i0 * DN_DH, (i0 + 2) * DN_DH)
            ks = slice(i0 * c, (i0 + 2) * c)
            q2 = shared[(j, pr)]["q2"].astype(F32)
            gcol = gcs[d][0][r0:r0 + c]
            eg = jnp.where(lane_k < DN_DH, jnp.exp(gcol[:, i0:i0 + 1]), jnp.exp(gcol[:, i0 + 1:i0 + 2]))
            u_ref[r0:r0 + c, cs] = u2[(j, pr)][d * c:(d + 1) * c].astype(BF16)
            wq_ref[2 * r0:2 * r0 + c, cs] = w2[(j, pr)][d * c:(d + 1) * c].astype(BF16)
            wq_ref[2 * r0 + c:2 * r0 + 2 * c, cs] = (q2 * eg).astype(BF16)
            kq_ref[3 * r0:3 * r0 + DN_DH, ks] = (x[2 * c:] * p["tail_row"]).astype(BF16)
            kq_ref[3 * r0 + DN_DH:3 * r0 + DN_DH + c, ks] = (x[c:2 * c] * p["e_incl"]).astype(BF16)
    for j in range(nchunk):
        row = jnp.where(lane < DN_HEADS, dl_rows[0][j], dl_rows[1][j])
        dl_ref[8 * j:8 * j + 8, :] = row


def _dn_wy(qkv, gb, nchunk):
    tm = nchunk * CHUNK
    nb = T_ALL // tm
    nch = 2 * DN_HEADS
    return pl.pallas_call(
        functools.partial(_dn_wy_kernel, nchunk=nchunk),
        grid=(nb,),
        in_specs=[pl.BlockSpec((tm, 3 * DN_W), lambda i: (i, 0)),
                  pl.BlockSpec((tm, GP), lambda i: (i, 0))],
        out_specs=[pl.BlockSpec((tm, nch * DN_DH), lambda i: (i, 0)),
                   pl.BlockSpec((2 * tm, nch * DN_DH), lambda i: (i, 0)),
                   pl.BlockSpec((3 * tm, nch * CHUNK), lambda i: (i, 0)),
                   pl.BlockSpec((8 * nchunk, GP), lambda i: (i, 0))],
        out_shape=[jax.ShapeDtypeStruct((T_ALL, nch * DN_DH), BF16),
                   jax.ShapeDtypeStruct((2 * T_ALL, nch * DN_DH), BF16),
                   jax.ShapeDtypeStruct((3 * T_ALL, nch * CHUNK), BF16),
                   jax.ShapeDtypeStruct((8 * T_ALL // CHUNK, GP), F32)],
        compiler_params=_cparams(("arbitrary",)),
        name="dn_wy",
    )(qkv, gb)


def _dn_scan_kernel(u_f_ref, wq_f_ref, kq_f_ref, dl_f_ref, u_b_ref, wq_b_ref, kq_b_ref, dl_b_ref,
                    of_ref, ob_ref, s_scr, *, nchunk):
    i = pl.program_id(0)

    @pl.when(i == 0)
    def _():
        s_scr[...] = jnp.zeros_like(s_scr)

    c = CHUNK
    states = [s_scr[idx] for idx in range(2 * DN_HEADS)]
    zero = jnp.zeros((c, DN_DH), BF16)
    refs = ((u_f_ref, wq_f_ref, kq_f_ref, dl_f_ref, of_ref), (u_b_ref, wq_b_ref, kq_b_ref, dl_b_ref, ob_ref))
    chains = [(d, h) for d in range(2) for h in range(DN_HEADS)]
    pairs = [(d, p) for d in range(2) for p in range(DN_HEADS // 2)]
    for step in range(nchunk):
        blk = lambda d: step if d == 0 else nchunk - 1 - step
        hs = lambda h: slice(h * DN_DH, (h + 1) * DN_DH)
        r1 = [_dot(refs[d][1][2 * blk(d) * c:2 * (blk(d) + 1) * c, hs(h)], states[d * DN_HEADS + h].astype(BF16))
              for d, h in chains]
        v_new = [(refs[d][0][blk(d) * c:(blk(d) + 1) * c, hs(h)].astype(F32) - r[:c]).astype(BF16)
                 for (d, h), r in zip(chains, r1)]
        r2 = [_dot(refs[d][2][3 * blk(d) * c:3 * (blk(d) + 1) * c, p * 2 * c:(p + 1) * 2 * c],
                   jnp.concatenate([jnp.concatenate([v_new[d * DN_HEADS + 2 * p], zero], axis=1),
                                    jnp.concatenate([zero, v_new[d * DN_HEADS + 2 * p + 1]], axis=1)], axis=0))
              for d, p in pairs]
        for d, h in chains:
            idx = d * DN_HEADS + h
            r = r2[d * (DN_HEADS // 2) + h // 2]
            ts = slice((h % 2) * DN_DH, (h % 2 + 1) * DN_DH)
            dl_row = refs[d][3][8 * blk(d):8 * blk(d) + 1, :]
            states[idx] = states[idx] * dl_row[:, idx:idx + 1] + r[:DN_DH, ts]
            refs[d][4][blk(d) * c:(blk(d) + 1) * c, hs(h)] = (r1[idx][c:] + r[DN_DH:, ts]).astype(BF16)
    for idx in range(2 * DN_HEADS):
        s_scr[idx] = states[idx]


def _dn_scan(u, wq, kq, dl, nchunk):
    tm = nchunk * CHUNK
    nb = T_ALL // tm
    nlat = T_LAT // tm
    fwd = lambda i: jnp.where(i == 0, nlat, i - 1)
    bwd = lambda i: nb - 1 - i
    specs = lambda f, col: [pl.BlockSpec((tm, DN_W), lambda i: (f(i), col)),
                            pl.BlockSpec((2 * tm, DN_W), lambda i: (f(i), col)),
                            pl.BlockSpec((3 * tm, DN_HEADS * CHUNK), lambda i: (f(i), col)),
                            pl.BlockSpec((8 * nchunk, GP), lambda i: (f(i), 0))]
    return pl.pallas_call(
        functools.partial(_dn_scan_kernel, nchunk=nchunk),
        grid=(nb,),
        in_specs=specs(fwd, 0) + specs(bwd, 1),
        out_specs=[pl.BlockSpec((tm, DN_W), lambda i: (fwd(i), 0)),
                   pl.BlockSpec((tm, DN_W), lambda i: (bwd(i), 0))],
        out_shape=[jax.ShapeDtypeStruct((T_ALL, DN_W), BF16),
                   jax.ShapeDtypeStruct((T_ALL, DN_W), BF16)],
        scratch_shapes=[pltpu.VMEM((2 * DN_HEADS, DN_DH, DN_DH), F32)],
        compiler_params=_cparams(("arbitrary",)),
        name="dn_scan",
    )(u, wq, kq, dl, u, wq, kq, dl)


def _dft_cos_sin(n):
    k = np.arange(n)
    ang = 2.0 * np.pi * ((k[:, None] * k[None, :]) % n) / n
    return np.cos(ang), np.sin(ang)


def _fft_consts():
    n = FFT_N
    c, s = _dft_cos_sin(n)
    cs_ch = np.concatenate([c, s], axis=1)
    w1 = np.block([[c, -s], [-s, -c]])
    k2 = np.arange(n)[None, :, None]
    t2 = np.arange(n)[:, None, None]
    ang = 2.0 * np.pi * ((k2 * t2) % (n * n)) / (n * n)
    scale = 1.0 / math.sqrt(T_LAT * FN_DG)
    cc, sc = _dft_cos_sin(T_CTX)
    scale_c = 1.0 / math.sqrt(T_CTX * FN_DG)
    f32 = lambda a: jnp.asarray(a, F32)
    bf = lambda a: f32(a).astype(BF16)
    return dict(cs_ch=bf(cs_ch), w1=bf(w1), twc=f32(np.cos(ang)), tws=f32(np.sin(ang)),
                c2=bf(c * scale), s2=bf(s * scale), cc=bf(cc * scale_c), sc=bf(sc * scale_c))


def _fft1_kernel(u_ref, cs_ref, w1_ref, twc_ref, tws_ref, y_ref, *, n_t2):
    n = FFT_N
    cs = cs_ref[...]
    w1 = w1_ref[...]
    for t in range(n_t2):
        twc = twc_ref[t]
        tws = tws_ref[t]
        for g in range(0, FN_GROUPS, 2):
            c0 = (t * FN_GROUPS + g) * FN_DG
            ab = [_dot(u_ref[:, c0 + s * FN_DG:c0 + (s + 1) * FN_DG], cs) for s in range(2)]
            rhs = jnp.concatenate([jnp.concatenate([ab[0][:, :FN_DG], ab[1][:, :FN_DG]], axis=1),
                                   jnp.concatenate([ab[0][:, FN_DG:], ab[1][:, FN_DG:]], axis=1)], axis=0)
            y = _dot(w1, rhs.astype(BF16))
            yr, yi = y[:n], y[n:]
            y_ref[0:n, c0:c0 + 2 * FN_DG] = (yr * twc + yi * tws).astype(BF16)
            y_ref[n:2 * n, c0:c0 + 2 * FN_DG] = (yi * twc - yr * tws).astype(BF16)


def _fft2_kernel(y_ref, c2_ref, s2_ref, o_ref, *, n_k2):
    c2 = c2_ref[...]
    s2 = s2_ref[...]
    for j in range(n_k2):
        o_ref[:, j * FN_W:(j + 1) * FN_W] = (_dot(c2, y_ref[0, j]) + _dot(s2, y_ref[1, j])).astype(BF16)


def _fft_ctx_kernel(u_ref, cs_ref, cc_ref, sc_ref, o_in_ref, o_ref):
    del o_in_ref
    for g in range(FN_GROUPS):
        ab = _dot(u_ref[:, g * FN_DG:(g + 1) * FN_DG], cs_ref[...])
        a = ab[:, :FN_DG].astype(BF16)
        b = ab[:, FN_DG:].astype(BF16)
        o_ref[:, g * FN_DG:(g + 1) * FN_DG] = (_dot(cc_ref[...], a) - _dot(sc_ref[...], b)).astype(BF16)


def _fnet(u, fc, with_ctx):
    n = FFT_N
    row_w = n * FN_W
    n_t2 = 8
    tc = n_t2 * FN_W
    y = pl.pallas_call(
        functools.partial(_fft1_kernel, n_t2=n_t2),
        grid=(row_w // tc,),
        in_specs=[pl.BlockSpec((n, tc), lambda j: (0, j)),
                  pl.BlockSpec((n, 2 * n), lambda j: (0, 0)),
                  pl.BlockSpec((2 * n, 2 * n), lambda j: (0, 0)),
                  pl.BlockSpec((n_t2, n, 1), lambda j: (j, 0, 0)),
                  pl.BlockSpec((n_t2, n, 1), lambda j: (j, 0, 0))],
        out_specs=pl.BlockSpec((2 * n, tc), lambda j: (0, j)),
        out_shape=jax.ShapeDtypeStruct((2 * n, row_w), BF16),
        compiler_params=_cparams(("arbitrary",)),
        name="fnet_stage1",
    )(u.reshape(T_ALL // n, row_w), fc["cs_ch"], fc["w1"], fc["twc"], fc["tws"])
    n_k2 = 8
    o = pl.pallas_call(
        functools.partial(_fft2_kernel, n_k2=n_k2),
        grid=(n // n_k2,),
        in_specs=[pl.BlockSpec((2, n_k2, n, FN_W), lambda j: (0, j, 0, 0)),
                  pl.BlockSpec((n, n), lambda j: (0, 0)),
                  pl.BlockSpec((n, n), lambda j: (0, 0))],
        out_specs=pl.BlockSpec((n, n_k2 * FN_W), lambda j: (0, j)),
        out_shape=jax.ShapeDtypeStruct(((T_ALL if with_ctx else T_LAT) // n, row_w), BF16),
        compiler_params=_cparams(("arbitrary",)),
        name="fnet_stage2",
    )(y.reshape(2, n, n, FN_W), fc["c2"], fc["s2"])
    if not with_ctx:
        return o.reshape(T_LAT, FN_W)
    o = o.reshape(T_ALL, FN_W)
    cb = T_LAT // T_CTX
    return pl.pallas_call(
        _fft_ctx_kernel,
        grid=(1,),
        in_specs=[pl.BlockSpec((T_CTX, FN_W), lambda j: (cb, 0)),
                  pl.BlockSpec((n, 2 * n), lambda j: (0, 0)),
                  pl.BlockSpec((T_CTX, T_CTX), lambda j: (0, 0)),
                  pl.BlockSpec((T_CTX, T_CTX), lambda j: (0, 0)),
                  pl.BlockSpec(memory_space=pl.ANY)],
        out_specs=pl.BlockSpec((T_CTX, FN_W), lambda j: (cb, 0)),
        out_shape=jax.ShapeDtypeStruct((T_ALL, FN_W), BF16),
        input_output_aliases={4: 0},
        compiler_params=_cparams(("arbitrary",)),
        name="fnet_ctx",
    )(u, fc["cs_ch"], fc["cc"], fc["sc"], o)


def _merge_kernel(*refs, tm, has_ctx, split):
    if split:
        (x_ref, ctx_ref, mod_ref, n1_ref, ona_ref, of_ref, ob_ref, z_ref, ofn_ref, dnn_ref,
         wg_ref, wna_ref, wdn_ref, wfn_ref, wout_ref, o_ref, tail_scr) = refs
    else:
        (x_ref, mod_ref, n1_ref, ona_ref, of_ref, ob_ref, z_ref, ofn_ref, dnn_ref,
         wg_ref, wna_ref, wdn_ref, wfn_ref, wout_ref, o_ref) = refs
        ctx_ref = tail_scr = None
    i = pl.program_id(0)
    tail_is_ctx = i == pl.num_programs(0) - 1
    x = _token_tile(x_ref, ctx_ref, tail_scr, tail_is_ctx)
    h = _modnorm_tile(x, n1_ref[...], mod_ref, 0, 1, tail_is_ctx, has_ctx).astype(BF16)
    o = of_ref[...].astype(F32) + ob_ref[...].astype(F32)
    z = z_ref[...].astype(F32)
    parts = []
    for hd in range(DN_HEADS):
        sl = slice(hd * DN_DH, (hd + 1) * DN_DH)
        oh = o[:, sl]
        oh = oh * lax.rsqrt(jnp.mean(oh * oh, axis=-1, keepdims=True) + EPS) * dnn_ref[...]
        parts.append(oh * _silu(z[:, sl]))
    odn = jnp.concatenate(parts, axis=-1).astype(BF16)
    branches = ((ona_ref[...], wna_ref), (odn, wdn_ref), (ofn_ref[...], wfn_ref))
    logits = [_dot(h, wg_ref[:, b * D:(b + 1) * D]) for b in range(N_BRANCH)]
    proj = [_dot(br, w_ref[...]) for br, w_ref in branches]
    y = jax.nn.sigmoid(logits[0]) * proj[0]
    for b in range(1, N_BRANCH):
        y = y + jax.nn.sigmoid(logits[b]) * proj[b]
    y = _dot(y.astype(BF16), wout_ref[...])
    o_ref[...] = _gated_residual(x, y, mod_ref, 2, tail_is_ctx, has_ctx)


def _merge(x, ctx, mod, norm1, p, o_na, o_f, o_b, o_fn, dn_norm, w_gate, w_na_o, w_dn_o, w_fn, w_out, has_ctx):
    rows, tm = (T_ALL, TM_MERGE_ALL) if has_ctx else (T_LAT, TM_MERGE_LAT)
    split = ctx is not None
    row = lambda w: pl.BlockSpec((tm, w), lambda i: (i, 0))
    full = lambda a: pl.BlockSpec(a.shape, lambda i: (0, 0), pipeline_mode=pl.Buffered(1))
    ctx_specs = [pl.BlockSpec((T_CTX, D), lambda i: (0, 0))] if split else []
    return pl.pallas_call(
        functools.partial(_merge_kernel, tm=tm, has_ctx=has_ctx, split=split),
        grid=(rows // tm,),
        in_specs=[row(D)] + ctx_specs + [
                  pl.BlockSpec((8, 6 * D), lambda i: (0, 0)), pl.BlockSpec((1, D), lambda i: (0, 0)),
                  row(NA_W), row(DN_W), row(DN_W),
                  pl.BlockSpec((tm, DN_W), lambda i: (i, P_Z)),
                  row(FN_W), pl.BlockSpec((1, DN_DH), lambda i: (0, 0)),
                  full(w_gate), full(w_na_o), full(w_dn_o), full(w_fn), full(w_out)],
        out_specs=row(D),
        out_shape=jax.ShapeDtypeStruct((rows, D), F32),
        scratch_shapes=[pltpu.VMEM((T_CTX, D), F32)] if split else [],
        compiler_params=_cparams(("arbitrary",)),
        name="merge",
    )(*([x, ctx] if split else [x]), mod, norm1.reshape(1, D), o_na, o_f, o_b, p, o_fn, dn_norm.reshape(1, DN_DH),
      w_gate, w_na_o, w_dn_o, w_fn, w_out)


def _mlp_kernel(x_ref, xnext_ref, mod_ref, n_ref, w1_ref, w2_ref, nf_ref, o_ref, h_scr, acc_scr, *,
                tm, has_ctx, final):
    i = pl.program_id(0)
    j = pl.program_id(1)
    nt = pl.num_programs(0)
    nj = pl.num_programs(1)
    slot = i % 2

    @pl.when(jnp.logical_and(i == 0, j == 0))
    def _():
        h_scr[0] = _modnorm_tile(x_ref[...], n_ref[...], mod_ref, 3, 4, nt == 1, has_ctx).astype(BF16)

    def hidden_step(first):
        a = jnp.maximum(_dot(h_scr[slot], w1_ref[...]), 0.0)
        upd = _dot((a * a).astype(BF16), w2_ref[...])
        if first:
            acc_scr[...] = upd
        else:
            acc_scr[...] += upd

    @pl.when(j == 0)
    def _():
        hidden_step(True)

    @pl.when(jnp.logical_and(j > 0, j < nj - 1))
    def _():
        hidden_step(False)

    @pl.when(j == nj - 1)
    def _():
        h_scr[1 - slot] = _modnorm_tile(xnext_ref[...], n_ref[...], mod_ref, 3, 4, i + 1 == nt - 1, has_ctx).astype(BF16)
        hidden_step(False)
        xn = _gated_residual(x_ref[...], acc_scr[...], mod_ref, 5, i == nt - 1, has_ctx)
        if final:
            xn = xn * lax.rsqrt(jnp.mean(xn * xn, axis=-1, keepdims=True) + EPS) * nf_ref[...]
        o_ref[...] = xn


def _mlp(xs, mod, norm, w1, w2, norm_f, has_ctx, final):
    rows, tm = (T_ALL, TM_ALL) if has_ctx else (T_LAT, TM_LAT)
    th = 1024
    return pl.pallas_call(
        functools.partial(_mlp_kernel, tm=tm, has_ctx=has_ctx, final=final),
        grid=(rows // tm, HID // th),
        in_specs=[pl.BlockSpec((tm, D), lambda i, j: (i, 0)),
                  pl.BlockSpec((tm, D), lambda i, j: (jnp.minimum(i + 1, rows // tm - 1), 0)),
                  pl.BlockSpec((8, 6 * D), lambda i, j: (0, 0)),
                  pl.BlockSpec((1, D), lambda i, j: (0, 0)),
                  pl.BlockSpec((D, th), lambda i, j: (0, j)),
                  pl.BlockSpec((th, D), lambda i, j: (j, 0)),
                  pl.BlockSpec((1, D), lambda i, j: (0, 0))],
        out_specs=pl.BlockSpec((tm, D), lambda i, j: (i, 0)),
        out_shape=jax.ShapeDtypeStruct((rows, D), F32),
        scratch_shapes=[pltpu.VMEM((2, tm, D), BF16), pltpu.VMEM((tm, D), F32)],
        compiler_params=_cparams(("arbitrary", "arbitrary")),
        name="mlp",
    )(xs, xs, mod, norm.reshape(1, D), w1, w2, norm_f.reshape(1, D))


def _split_in_weights(w_in):
    offs = np.cumsum([0, NA_W, NA_W, DN_W, DN_W, 4 * DN_HEADS, NA_W, DN_W, DN_W, FN_W, N_BRANCH * D]).tolist()
    na_k, na_v, dn_k, dn_v, dn_ab, na_q, dn_q, dn_z, fn_u, gate = [w_in[:, a:b] for a, b in zip(offs[:-1], offs[1:])]
    wcat = jnp.concatenate([na_q, na_k, na_v, dn_q, dn_k, dn_v, dn_z, fn_u], axis=-1).astype(BF16)
    wg = jnp.pad(dn_ab, ((0, 0), (0, GP - 4 * DN_HEADS))).astype(BF16)
    return wcat, wg, gate.astype(BF16)


def kernel(x, c, ctx, c_ctx, w_ada, b_ada, norm1, w_in, conv_w, a_log, dt_bias, dn_norm, rpb,
           w_na_o, w_dn_o, w_fn, w_out, norm2, w_mlp1, w_mlp2, norm_f):
    xs, xc = x[0], ctx[0]
    cc = jnp.concatenate([c, c_ctx[None, :], jnp.zeros((6, D), F32)], axis=0)
    mods = _ada(cc, w_ada, b_ada)
    rope = _rope_tables()
    fc = _fft_consts()
    na_tab = _na_tables(rpb)
    for l in range(DEPTH):
        has_ctx = l < DEPTH - 1
        final = l == DEPTH - 1
        wcat, wg, w_gate = _split_in_weights(w_in[l])
        u, p, g = _in_proj(xs, xc, mods[l], norm1[l], wcat, wg)
        o_na = _na(p, na_tab, l, has_ctx)
        qkv, gb = _dn_prep(p, g, conv_w[l], a_log[l], dt_bias[l], rope)
        o_f, o_b = _dn_scan(*_dn_wy(qkv, gb, DN_BLK_CHUNKS), DN_BLK_CHUNKS)
        o_fn = _fnet(u, fc, has_ctx)
        xs, xc = _merge(xs, xc, mods[l], norm1[l], p, o_na, o_f, o_b, o_fn, dn_norm[l], w_gate,
                    w_na_o[l].astype(BF16), w_dn_o[l].astype(BF16), w_fn[l].astype(BF16), w_out[l].astype(BF16),
                    has_ctx), None
        xs = _mlp(xs, mods[l], norm2[l], w_mlp1[l].astype(BF16), w_mlp2[l].astype(BF16), norm_f, has_ctx, final)
    return xs[None]
```

```python
import functools
import math

import numpy as np
import jax
import jax.numpy as jnp
from jax import lax
from jax.experimental import pallas as pl
from jax.experimental.pallas import tpu as pltpu

F32 = jnp.float32
BF16 = jnp.bfloat16

D = 1024
T_LAT = 16384
T_CTX = 256
T_ALL = T_LAT + T_CTX
DEPTH = 2
GRID_W = 64
GRID_H = T_LAT // GRID_W
NA_HEADS = 8
NA_DH = 64
NA_WIN_H = 8
NA_WIN_W = 16
NA_W = NA_HEADS * NA_DH
DN_HEADS = 4
DN_DH = 128
DN_W = DN_HEADS * DN_DH
DN_CONV = 5
CHUNK = 64
FN_GROUPS = 4
FN_DG = 128
FN_W = FN_GROUPS * FN_DG
N_BRANCH = 3
HID = 4 * D
ROPE_BASE = 10000.0
EPS = 1e-6
NEG = -1e30

P_W = 3 * NA_W + 4 * DN_W + FN_W
P_NAQ, P_NAK, P_NAV = 0, 1, 2
P_DN = 1
P_Z = 6
TN_IN = P_W // 2
GP = 128

TM_ALL = 1280
TM_LAT = 1024
TM_MERGE_ALL = 640
TM_MERGE_LAT = 512
NA_BLK = 4 * GRID_W
NA_HEAD_GROUP = 2
NA_VARIANTS = 4
FFT_N = 128
DN_BLK_CHUNKS = 4
WY_GROUP_CHUNKS = 4
VMEM_LIMIT = 56 * 1024 * 1024


def _cparams(sem):
    return pltpu.CompilerParams(dimension_semantics=sem, vmem_limit_bytes=VMEM_LIMIT)


def _dot(a, b):
    return jnp.dot(a, b, preferred_element_type=F32)


def _dot_nt(a, b):
    return lax.dot_general(a, b, (((1,), (1,)), ((), ())), preferred_element_type=F32)


def _dot_tn(a, b):
    return lax.dot_general(a, b, (((0,), (0,)), ((), ())), preferred_element_type=F32)


def _silu(x):
    return x * jax.nn.sigmoid(x)


def _ada_kernel(c_ref, w_ref, b_ref, o_ref):
    s = _silu(c_ref[...])
    s_hi = s.astype(BF16)
    s_lo = (s - s_hi.astype(F32)).astype(BF16)
    w = w_ref[0].astype(BF16)
    o_ref[0] = _dot(s_hi, w) + _dot(s_lo, w) + b_ref[0]


def _ada(cc, w_ada, b_ada):
    tn = 1536
    return pl.pallas_call(
        _ada_kernel,
        grid=(DEPTH, 6 * D // tn),
        in_specs=[pl.BlockSpec((8, D), lambda l, j: (0, 0)),
                  pl.BlockSpec((1, D, tn), lambda l, j: (l, 0, j)),
                  pl.BlockSpec((1, 1, tn), lambda l, j: (l, 0, j))],
        out_specs=pl.BlockSpec((1, 8, tn), lambda l, j: (l, 0, j)),
        out_shape=jax.ShapeDtypeStruct((DEPTH, 8, 6 * D), F32),
        compiler_params=_cparams(("arbitrary", "arbitrary")),
        name="ada_mod",
    )(cc, w_ada, b_ada.reshape(DEPTH, 1, 6 * D))


def _row_slices(tm, has_ctx):
    return [slice(0, tm)] if not has_ctx else [slice(0, tm - T_CTX), slice(tm - T_CTX, tm)]


def _mod_vecs(mod_ref, k, tail_is_ctx, has_ctx):
    lat = mod_ref[0:1, k * D:(k + 1) * D]
    if not has_ctx:
        return [lat]
    return [lat, jnp.where(tail_is_ctx, mod_ref[1:2, k * D:(k + 1) * D], lat)]


def _modnorm(x, g, shift, scale):
    y = x * lax.rsqrt(jnp.mean(x * x, axis=-1, keepdims=True) + EPS)
    return y * (g * (1.0 + scale)) + shift


def _modnorm_tile(x, g, mod_ref, k_shift, k_scale, tail_is_ctx, has_ctx):
    tm = x.shape[0]
    parts = [_modnorm(x[sl], g, sh, sc) for sl, sh, sc in zip(_row_slices(tm, has_ctx),
                                                            _mod_vecs(mod_ref, k_shift, tail_is_ctx, has_ctx),
                                                            _mod_vecs(mod_ref, k_scale, tail_is_ctx, has_ctx))]
    return parts[0] if len(parts) == 1 else jnp.concatenate(parts, axis=0)


def _gated_residual(x, y, mod_ref, k_gate, tail_is_ctx, has_ctx):
    tm = x.shape[0]
    parts = [x[sl] + gt * y[sl] for sl, gt in zip(_row_slices(tm, has_ctx), _mod_vecs(mod_ref, k_gate, tail_is_ctx, has_ctx))]
    return parts[0] if len(parts) == 1 else jnp.concatenate(parts, axis=0)


def _token_tile(x_ref, ctx_ref, tail_scr, is_last):
    if ctx_ref is None:
        return x_ref[...]
    top = x_ref.shape[0] - T_CTX

    @pl.when(jnp.logical_not(is_last))
    def _():
        tail_scr[...] = x_ref[top:, :]

    @pl.when(is_last)
    def _():
        tail_scr[...] = ctx_ref[...]

    return jnp.concatenate([x_ref[0:top, :], tail_scr[...]], axis=0)


def _in_proj_kernel(*refs, tm, split):
    if split:
        x_ref, ctx_ref, mod_ref, n_ref, w_ref, wg_ref, u_ref, p_ref, g_ref, h_scr, tail_scr = refs
    else:
        x_ref, mod_ref, n_ref, w_ref, wg_ref, u_ref, p_ref, g_ref, h_scr = refs
        ctx_ref = tail_scr = None
    i = pl.program_id(0)
    j = pl.program_id(1)

    @pl.when(j == 0)
    def _():
        is_last = i == pl.num_programs(0) - 1
        x = _token_tile(x_ref, ctx_ref, tail_scr, is_last)
        h = _modnorm_tile(x, n_ref[...], mod_ref, 0, 1, is_last, True).astype(BF16)
        h_scr[...] = h
        g_ref[...] = _dot(h, wg_ref[...])

    p = _dot(h_scr[...], w_ref[...]).astype(BF16)
    p_ref[...] = p

    @pl.when(j == pl.num_programs(1) - 1)
    def _():
        u_ref[...] = p[:, TN_IN - FN_W:]


def _in_proj(x, ctx, mod, norm, wcat, wg, layer):
    tm = TM_ALL
    nj = P_W // TN_IN
    split = ctx is not None
    ctx_specs = [pl.BlockSpec((T_CTX, D), lambda i, j: (0, 0))] if split else []
    return pl.pallas_call(
        functools.partial(_in_proj_kernel, tm=tm, split=split),
        grid=(T_ALL // tm, nj),
        in_specs=[pl.BlockSpec((tm, D), lambda i, j: (i, 0))] + ctx_specs + [
                  pl.BlockSpec((8, 6 * D), lambda i, j: (0, 0)),
                  pl.BlockSpec((1, D), lambda i, j: (0, 0)),
                  pl.BlockSpec((None, D, TN_IN), lambda i, j: (layer, 0, j)),
                  pl.BlockSpec((None, D, GP), lambda i, j: (layer, 0, 0))],
        out_specs=[pl.BlockSpec((tm, FN_W), lambda i, j: (i, 0)),
                   pl.BlockSpec((tm, TN_IN), lambda i, j: (i, j)),
                   pl.BlockSpec((tm, GP), lambda i, j: (i, 0))],
        out_shape=[jax.ShapeDtypeStruct((T_ALL, FN_W), BF16),
                   jax.ShapeDtypeStruct((T_ALL, P_W), BF16),
                   jax.ShapeDtypeStruct((T_ALL, GP), F32)],
        scratch_shapes=[pltpu.VMEM((tm, D), BF16)] + ([pltpu.VMEM((T_CTX, D), F32)] if split else []),
        compiler_params=_cparams(("arbitrary", "arbitrary")),
        name="in_proj",
    )(*([x, ctx] if split else [x]), mod, norm.reshape(1, D), wcat, wg)


def _na_tables(rpb):
    qr = np.arange(4)[:, None, None, None]
    qc = np.arange(GRID_W)[None, :, None, None]
    kk = np.arange(12)[None, None, :, None]
    kc = np.arange(GRID_W)[None, None, None, :]
    valid = []
    for b in (0, 1, GRID_H // 4 - 1):
        r = 4 * b + qr
        kr = 4 * (b - 1) + kk
        rs = np.clip(r - NA_WIN_H // 2, 0, GRID_H - NA_WIN_H)
        cs = np.clip(qc - NA_WIN_W // 2, 0, GRID_W - NA_WIN_W)
        ok = (kr >= 0) & (kr < GRID_H) & (kr >= rs) & (kr < rs + NA_WIN_H) & (kc >= cs) & (kc < cs + NA_WIN_W)
        valid.append(np.broadcast_to(ok, (4, GRID_W, 12, GRID_W)).reshape(NA_BLK, 12 * GRID_W))
    valid.append(np.zeros_like(valid[0]))
    valid = np.stack(valid)
    col = np.arange(GRID_W)
    onehot = ((col[None, None, :] - col[None, :, None] + NA_WIN_W - 1)
              == np.arange(2 * NA_WIN_W - 1)[:, None, None]).astype(np.float32)
    tcol = jnp.einsum('lhrd,dqk->lhrqk', rpb, onehot, precision=lax.Precision.HIGHEST)
    bias = jnp.stack([jnp.stack([tcol[:, :, k - q + NA_WIN_H - 5] for k in range(12)], axis=3) for q in range(4)], axis=2)
    bias = bias.reshape(DEPTH, 1, NA_HEADS, NA_BLK, 12 * GRID_W)
    tab = jnp.where(valid[None, :, None], bias, NEG).astype(BF16)
    return tab.reshape(DEPTH * len(valid), NA_HEADS, NA_BLK, 12 * GRID_W)


def _na_kernel(q_ref, k0_ref, k1_ref, k2_ref, v0_ref, v1_ref, v2_ref, kc_ref, vc_ref, tab_ref, o_ref):
    q = q_ref[...]
    kl = jnp.concatenate([k0_ref[...], k1_ref[...], k2_ref[...]], axis=0)
    vl = jnp.concatenate([v0_ref[...], v1_ref[...], v2_ref[...]], axis=0)
    kc = kc_ref[...]
    vc = vc_ref[...]
    lane_l = lax.broadcasted_iota(jnp.int32, (vl.shape[0], 2 * NA_DH), 1)
    lane_c = lax.broadcasted_iota(jnp.int32, (vc.shape[0], 2 * NA_DH), 1)
    one = jnp.ones((), BF16)
    outs = []
    for h0 in range(0, NA_HEADS, NA_HEAD_GROUP):
        heads = range(h0, h0 + NA_HEAD_GROUP)
        sls = [slice(h * NA_DH, (h + 1) * NA_DH) for h in heads]
        pls = [slice((h // 2) * 2 * NA_DH, (h // 2 + 1) * 2 * NA_DH) for h in heads]
        own = [(lambda ln, h=h: (ln < NA_DH) if h % 2 == 0 else (ln >= NA_DH)) for h in heads]
        qs = [q[:, sl] * (NA_DH ** -0.5) for sl in sls]
        z_loc = [_dot_nt(qh, kl[:, sl]).astype(BF16) + tab_ref[0, h] for h, qh, sl in zip(heads, qs, sls)]
        z_ctx = [_dot_nt(qh, kc[:, sl]).astype(BF16) for qh, sl in zip(qs, sls)]
        m = [jnp.maximum(jnp.max(a, axis=-1, keepdims=True), jnp.max(b, axis=-1, keepdims=True))
             for a, b in zip(z_loc, z_ctx)]
        p_loc = [jnp.exp(a - mm) for a, mm in zip(z_loc, m)]
        p_ctx = [jnp.exp(b - mm) for b, mm in zip(z_ctx, m)]
        oa = [_dot(a, jnp.where(f(lane_l), vl[:, ps], one)) + _dot(b, jnp.where(f(lane_c), vc[:, ps], one))
              for a, b, ps, f in zip(p_loc, p_ctx, pls, own)]
        outs += [(x[:, :NA_DH] / x[:, NA_DH:NA_DH + 1]) if h % 2 == 0 else (x[:, NA_DH:] / x[:, 0:1])
                 for h, x in zip(heads, oa)]
    o_ref[...] = jnp.concatenate(outs, axis=-1).astype(BF16)


def _na(p, tab, layer, with_ctx):
    nlat = T_LAT // NA_BLK
    nb = nlat + (1 if with_ctx else 0)
    ctx_blk = T_LAT // NA_BLK

    def kidx(b, part):
        return jnp.clip(jnp.minimum(b, nlat - 1) - 1 + part, 0, nlat - 1)

    def variant(b):
        return layer * NA_VARIANTS + jnp.where(b == 0, 0, jnp.where(b == nlat - 1, 2, jnp.where(b >= nlat, 3, 1)))

    blk = (NA_BLK, NA_W)
    in_specs = [pl.BlockSpec(blk, lambda b: (b, P_NAQ))]
    in_specs += [pl.BlockSpec(blk, functools.partial(lambda b, part: (kidx(b, part), P_NAK), part=part))
                 for part in range(3)]
    in_specs += [pl.BlockSpec(blk, functools.partial(lambda b, part: (kidx(b, part), P_NAV), part=part))
                 for part in range(3)]
    in_specs += [pl.BlockSpec(blk, lambda b: (ctx_blk, P_NAK)),
                 pl.BlockSpec(blk, lambda b: (ctx_blk, P_NAV)),
                 pl.BlockSpec((1, NA_HEADS, NA_BLK, 3 * NA_BLK), lambda b: (variant(b), 0, 0, 0))]
    return pl.pallas_call(
        _na_kernel,
        grid=(nb,),
        in_specs=in_specs,
        out_specs=pl.BlockSpec(blk, lambda b: (b, 0)),
        out_shape=jax.ShapeDtypeStruct((nb * NA_BLK, NA_W), BF16),
        compiler_params=_cparams(("arbitrary",)),
        name="na_attn",
    )(p, p, p, p, p, p, p, p, p, tab)


def _rope_tables():
    t = jnp.arange(T_LAT, dtype=jnp.int32)
    half = DN_DH // 2
    pos = jnp.stack([t // GRID_W, t % GRID_W], axis=-1).astype(F32)
    inv = 1.0 / (ROPE_BASE ** (jnp.arange(0, half, 2, dtype=F32) / half))
    ang = pos[:, :, None] * inv[None, None, :]
    ang = jnp.concatenate([ang, ang], axis=-1).reshape(T_LAT, DN_DH)
    cos, sin = jnp.cos(ang), jnp.sin(ang)
    low = (np.arange(DN_DH) % half) < half // 2
    return cos, jnp.where(low[None], -sin, 0.0), jnp.where(low[None], 0.0, sin)


def _dn_prep_kernel(x_ref, prev_ref, next_ref, g_ref, cw_ref, alog_ref, dtb_ref, cos_ref, sa_ref, sb_ref,
                    qkv_ref, gb_ref, *, tm):
    b = pl.program_id(0)
    nlat = T_LAT // tm
    first = jnp.logical_or(b == 0, b == nlat)
    last = b >= nlat - 1
    is_ctx = b >= nlat
    cur = x_ref[...].astype(F32)
    prev = jnp.where(first, 0.0, prev_ref[...][14:16].astype(F32))
    nxt = jnp.where(last, 0.0, next_ref[...][0:2].astype(F32))
    ext = jnp.concatenate([prev, cur, nxt], axis=0)
    cw = cw_ref[...]
    y = ext[0:tm] * cw[0:1]
    for k in range(1, DN_CONV):
        y = y + ext[k:k + tm] * cw[k:k + 1]
    y = _silu(y)
    cos = jnp.where(is_ctx, 1.0, cos_ref[...])
    sa = jnp.where(is_ctx, 0.0, sa_ref[...])
    sb = jnp.where(is_ctx, 0.0, sb_ref[...])
    quarter = DN_DH // 4
    for s in range(2):
        for h in range(DN_HEADS):
            c0 = s * DN_W + h * DN_DH
            xh = y[:, c0:c0 + DN_DH]
            xh = xh * lax.rsqrt(jnp.sum(xh * xh, axis=-1, keepdims=True) + EPS)
            xh = (xh * cos + pltpu.roll(xh, DN_DH - quarter, axis=1) * sa + pltpu.roll(xh, quarter, axis=1) * sb)
            if s == 0:
                xh = xh * (DN_DH ** -0.5)
            qkv_ref[:, c0:c0 + DN_DH] = xh.astype(BF16)
    qkv_ref[:, 2 * DN_W:] = y[:, 2 * DN_W:].astype(BF16)
    g = g_ref[...]
    lane = lax.broadcasted_iota(jnp.int32, g.shape, 1)
    decay = -jnp.exp(alog_ref[...]) * jax.nn.softplus(g + dtb_ref[...])
    gb_ref[...] = jnp.where(lane < 2 * DN_HEADS, decay, jax.nn.sigmoid(g))


def _dn_prep(p, g, conv_w, a_log, dt_bias, rope):
    tm = 256
    nb = T_ALL // tm
    nlat = T_LAT // tm
    hb = 16
    nh = T_ALL // hb
    pad = lambda v: jnp.pad(v.reshape(1, 2 * DN_HEADS).astype(F32), ((0, 0), (0, GP - 2 * DN_HEADS)))
    rspec = pl.BlockSpec((tm, DN_DH), lambda b: (jnp.minimum(b, nlat - 1), 0))
    return pl.pallas_call(
        functools.partial(_dn_prep_kernel, tm=tm),
        grid=(nb,),
        in_specs=[pl.BlockSpec((tm, 3 * DN_W), lambda b: (b, P_DN)),
                  pl.BlockSpec((hb, 3 * DN_W), lambda b: (jnp.maximum(b * (tm // hb) - 1, 0), P_DN)),
                  pl.BlockSpec((hb, 3 * DN_W), lambda b: (jnp.minimum((b + 1) * (tm // hb), nh - 1), P_DN)),
                  pl.BlockSpec((tm, GP), lambda b: (b, 0)),
                  pl.BlockSpec((DN_CONV, 3 * DN_W), lambda b: (0, 0)),
                  pl.BlockSpec((1, GP), lambda b: (0, 0)),
                  pl.BlockSpec((1, GP), lambda b: (0, 0)),
                  rspec, rspec, rspec],
        out_specs=[pl.BlockSpec((tm, 3 * DN_W), lambda b: (b, 0)),
                   pl.BlockSpec((tm, GP), lambda b: (b, 0))],
        out_shape=[jax.ShapeDtypeStruct((T_ALL, 3 * DN_W), BF16),
                   jax.ShapeDtypeStruct((T_ALL, GP), F32)],
        compiler_params=_cparams(("arbitrary",)),
        name="dn_prep",
    )(p, p, p, g, conv_w, pad(a_log), pad(dt_bias), *rope)


def _pair_blockdiag(y):
    c = y.shape[0]
    lane = lax.broadcasted_iota(jnp.int32, y.shape, 1)
    zero = jnp.zeros((), y.dtype)
    return jnp.concatenate([jnp.where(lane < c, y, zero), jnp.where(lane >= c, y, zero)], axis=0)


def _unit_tri_inverses_minus_identity(mats):
    c = mats[0].shape[0]
    ri = lax.broadcasted_iota(jnp.int32, (c, 2 * c), 0)
    ci = lax.broadcasted_iota(jnp.int32, (c, 2 * c), 1) % c

    def same_block(s):
        return (ri // s) == (ci // s)

    def mm(xs, ys):
        return [_dot(x.astype(BF16), _pair_blockdiag(y.astype(BF16))) for x, y in zip(xs, ys)]

    diag = same_block(8)
    b1 = [jnp.where(diag, -a, 0.0) for a in mats]
    b2 = mm(b1, b1)
    b3 = mm(b1, b2)
    b4 = mm(b2, b2)
    n2 = [x + y + z for x, y, z in zip(b1, b2, b3)]
    n = [x + y + z for x, y, z in zip(n2, b4, mm(n2, b4))]
    for s in (8, 16, 32):
        off = jnp.logical_and(same_block(2 * s), jnp.logical_not(same_block(s)))
        lo = [jnp.where(off, a, 0.0) for a in mats]
        x = [p + q for p, q in zip(lo, mm(n, lo))]
        n = [p - (q + r) for p, q, r in zip(n, x, mm(x, n))]
    return n


def _dn_wy_kernel(qkv_ref, gb_ref, u_ref, wq_ref, kq_ref, dl_ref, *, nchunk):
    c = CHUNK
    tm = nchunk * c
    gb = gb_ref[...]
    qkv = qkv_ref[...]
    rt = lax.broadcasted_iota(jnp.int32, (tm, tm), 0)
    ct = lax.broadcasted_iota(jnp.int32, (tm, tm), 1)
    same_chunk = (rt // c) == (ct // c)
    ri = lax.broadcasted_iota(jnp.int32, (c, 2 * c), 0)
    lane2 = lax.broadcasted_iota(jnp.int32, (c, 2 * c), 1)
    ci = lane2 % c
    first = lane2 < c
    lane = lax.broadcasted_iota(jnp.int32, (8, GP), 1)
    lane_k = lax.broadcasted_iota(jnp.int32, (c, 2 * DN_DH), 1)
    zk = jnp.zeros((), BF16)
    eye_p = (ri == ci).astype(F32)
    eye2 = (lax.broadcasted_iota(jnp.int32, (DN_DH, 2 * DN_DH), 1) % DN_DH
            == lax.broadcasted_iota(jnp.int32, (DN_DH, 2 * DN_DH), 0)).astype(BF16)
    gb_t = gb.T
    dl_rows, gcs = [], []
    for d in range(2):
        cum = jnp.logical_and(same_chunk, (ct <= rt) if d == 0 else (ct >= rt))
        gc_all = jnp.dot(cum.astype(F32), gb, preferred_element_type=F32, precision=lax.Precision.HIGHEST)
        gcs.append((gc_all, gc_all.T))
        dl_rows.append([jnp.exp(gc_all[j * c + (c - 1 if d == 0 else 0):j * c + (c if d == 0 else 1), :])
                        for j in range(nchunk)])
    for j0 in range(0, nchunk, WY_GROUP_CHUNKS):
        shared = {}
        for j in range(j0, j0 + WY_GROUP_CHUNKS):
            for pr in range(DN_HEADS // 2):
                r0 = j * c
                hs = slice(2 * pr * DN_DH, (2 * pr + 2) * DN_DH)
                bd = lambda x: jnp.concatenate([jnp.where(lane_k < DN_DH, x, zk), jnp.where(lane_k >= DN_DH, x, zk)],
                                               axis=0)
                q2 = qkv[r0:r0 + c, hs]
                k2 = qkv[r0:r0 + c, DN_W + hs.start:DN_W + hs.stop]
                v2 = qkv[r0:r0 + c, 2 * DN_W + hs.start:2 * DN_W + hs.stop]
                bd_k = bd(k2)
                shared[(j, pr)] = dict(q2=q2, bd_k=bd_k, bd_v=bd(v2),
                                       gqt=_dot_nt(jnp.concatenate([k2, q2, eye2], axis=0), bd_k))
        units = [(d, j, pr) for j in range(j0, j0 + WY_GROUP_CHUNKS) for d in range(2) for pr in range(DN_HEADS // 2)]
        pre = []
        for d, j, pr in units:
            gc_all, gc_all_t = gcs[d]
            incl = (ci <= ri) if d == 0 else (ci >= ri)
            strict = (ci < ri) if d == 0 else (ci > ri)
            r0 = j * c
            last = r0 + (c - 1 if d == 0 else 0)
            i0 = d * DN_HEADS + 2 * pr
            col = lambda x, a: x[r0:r0 + c, a:a + 1]
            row = lambda x, a: jnp.concatenate([x[a:a + 1, r0:r0 + c], x[a + 1:a + 2, r0:r0 + c]], axis=1)
            beta_p = jnp.where(first, col(gb, 2 * DN_HEADS + i0), col(gb, 2 * DN_HEADS + i0 + 1))
            gc_p = jnp.where(first, col(gc_all, i0), col(gc_all, i0 + 1))
            gc_row = row(gc_all_t, i0)
            beta_row = row(gb_t, 2 * DN_HEADS + i0)
            g_last = jnp.where(first[0:1], gc_all[last:last + 1, i0:i0 + 1], gc_all[last:last + 1, i0 + 1:i0 + 2])
            e = jnp.exp(jnp.where(incl, gc_p - gc_row, 0.0))
            eg_row = jnp.exp(gc_row)
            pre.append(dict(beta_p=beta_p, e_incl=jnp.where(incl, e, 0.0), e_strict=jnp.where(strict, e, 0.0),
                            beta_row=beta_row, eg_row=eg_row, tail_row=jnp.exp(g_last - gc_row)))
        gqt = [shared[(j, pr)]["gqt"] for d, j, pr in units]
        ns = _unit_tri_inverses_minus_identity([p["beta_p"] * x[:c] * p["e_strict"] for x, p in zip(gqt, pre)])
        tmat = {u_: n + eye_p for u_, n in zip(units, ns)}
        prd = dict(zip(units, pre))
        both = [(j, pr) for j in range(j0, j0 + WY_GROUP_CHUNKS) for pr in range(DN_HEADS // 2)]
        lhs_u = [jnp.concatenate([(tmat[(d, j, pr)] * prd[(d, j, pr)]["beta_row"]).astype(BF16) for d in range(2)], axis=0)
                 for j, pr in both]
        lhs_w = [jnp.concatenate([(tmat[(d, j, pr)] * (prd[(d, j, pr)]["beta_row"] * prd[(d, j, pr)]["eg_row"])
                                   ).astype(BF16) for d in range(2)], axis=0) for j, pr in both]
        u2 = dict(zip(both, [_dot(a, shared[jp]["bd_v"]) for a, jp in zip(lhs_u, both)]))
        w2 = dict(zip(both, [_dot(a, shared[jp]["bd_k"]) for a, jp in zip(lhs_w, both)]))
        for (d, j, pr), p, x in zip(units, pre, gqt):
            r0 = j * c
            i0 = d * DN_HEADS + 2 * pr
            cs = slice(i0 * DN_DH, (i0 + 2) * DN_DH)
            ks = slice(i0 * c, (i0 + 2) * c)
            q2 = shared[(j, pr)]["q2"].astype(F32)
            gcol = gcs[d][0][r0:r0 + c]
            eg = jnp.where(lane_k < DN_DH, jnp.exp(gcol[:, i0:i0 + 1]), jnp.exp(gcol[:, i0 + 1:i0 + 2]))
            u_ref[r0:r0 + c, cs] = u2[(j, pr)][d * c:(d + 1) * c].astype(BF16)
            wq_ref[2 * r0:2 * r0 + c, cs] = w2[(j, pr)][d * c:(d + 1) * c].astype(BF16)
            wq_ref[2 * r0 + c:2 * r0 + 2 * c, cs] = (q2 * eg).astype(BF16)
            kq_ref[3 * r0:3 * r0 + DN_DH, ks] = (x[2 * c:] * p["tail_row"]).astype(BF16)
            kq_ref[3 * r0 + DN_DH:3 * r0 + DN_DH + c, ks] = (x[c:2 * c] * p["e_incl"]).astype(BF16)
    for j in range(nchunk):
        row = jnp.where(lane < DN_HEADS, dl_rows[0][j], dl_rows[1][j])
        dl_ref[8 * j:8 * j + 8, :] = row


def _dn_wy(qkv, gb, nchunk):
    tm = nchunk * CHUNK
    nb = T_ALL // tm
    nch = 2 * DN_HEADS
    return pl.pallas_call(
        functools.partial(_dn_wy_kernel, nchunk=nchunk),
        grid=(nb,),
        in_specs=[pl.BlockSpec((tm, 3 * DN_W), lambda i: (i, 0)),
                  pl.BlockSpec((tm, GP), lambda i: (i, 0))],
        out_specs=[pl.BlockSpec((tm, nch * DN_DH), lambda i: (i, 0)),
                   pl.BlockSpec((2 * tm, nch * DN_DH), lambda i: (i, 0)),
                   pl.BlockSpec((3 * tm, nch * CHUNK), lambda i: (i, 0)),
                   pl.BlockSpec((8 * nchunk, GP), lambda i: (i, 0))],
        out_shape=[jax.ShapeDtypeStruct((T_ALL, nch * DN_DH), BF16),
                   jax.ShapeDtypeStruct((2 * T_ALL, nch * DN_DH), BF16),
                   jax.ShapeDtypeStruct((3 * T_ALL, nch * CHUNK), BF16),
                   jax.ShapeDtypeStruct((8 * T_ALL // CHUNK, GP), F32)],
        compiler_params=_cparams(("arbitrary",)),
        name="dn_wy",
    )(qkv, gb)


def _dn_scan_kernel(u_f_ref, wq_f_ref, kq_f_ref, dl_f_ref, u_b_ref, wq_b_ref, kq_b_ref, dl_b_ref,
                    of_ref, ob_ref, s_scr, *, nchunk):
    i = pl.program_id(0)

    @pl.when(i == 0)
    def _():
        s_scr[...] = jnp.zeros_like(s_scr)

    c = CHUNK
    states = [s_scr[idx] for idx in range(2 * DN_HEADS)]
    zero = jnp.zeros((c, DN_DH), BF16)
    refs = ((u_f_ref, wq_f_ref, kq_f_ref, dl_f_ref, of_ref), (u_b_ref, wq_b_ref, kq_b_ref, dl_b_ref, ob_ref))
    chains = [(d, h) for d in range(2) for h in range(DN_HEADS)]
    pairs = [(d, p) for d in range(2) for p in range(DN_HEADS // 2)]
    for step in range(nchunk):
        blk = lambda d: step if d == 0 else nchunk - 1 - step
        hs = lambda h: slice(h * DN_DH, (h + 1) * DN_DH)
        r1 = [_dot(refs[d][1][2 * blk(d) * c:2 * (blk(d) + 1) * c, hs(h)], states[d * DN_HEADS + h].astype(BF16))
              for d, h in chains]
        v_new = [(refs[d][0][blk(d) * c:(blk(d) + 1) * c, hs(h)].astype(F32) - r[:c]).astype(BF16)
                 for (d, h), r in zip(chains, r1)]
        r2 = [_dot(refs[d][2][3 * blk(d) * c:3 * (blk(d) + 1) * c, p * 2 * c:(p + 1) * 2 * c],
                   jnp.concatenate([jnp.concatenate([v_new[d * DN_HEADS + 2 * p], zero], axis=1),
                                    jnp.concatenate([zero, v_new[d * DN_HEADS + 2 * p + 1]], axis=1)], axis=0))
              for d, p in pairs]
        for d, h in chains:
            idx = d * DN_HEADS + h
            r = r2[d * (DN_HEADS // 2) + h // 2]
            ts = slice((h % 2) * DN_DH, (h % 2 + 1) * DN_DH)
            dl_row = refs[d][3][8 * blk(d):8 * blk(d) + 1, :]
            states[idx] = states[idx] * dl_row[:, idx:idx + 1] + r[:DN_DH, ts]
            refs[d][4][blk(d) * c:(blk(d) + 1) * c, hs(h)] = (r1[idx][c:] + r[DN_DH:, ts]).astype(BF16)
    for idx in range(2 * DN_HEADS):
        s_scr[idx] = states[idx]


def _dn_scan(u, wq, kq, dl, nchunk):
    tm = nchunk * CHUNK
    nb = T_ALL // tm
    nlat = T_LAT // tm
    fwd = lambda i: jnp.where(i == 0, nlat, i - 1)
    bwd = lambda i: nb - 1 - i
    specs = lambda f, col: [pl.BlockSpec((tm, DN_W), lambda i: (f(i), col)),
                            pl.BlockSpec((2 * tm, DN_W), lambda i: (f(i), col)),
                            pl.BlockSpec((3 * tm, DN_HEADS * CHUNK), lambda i: (f(i), col)),
                            pl.BlockSpec((8 * nchunk, GP), lambda i: (f(i), 0))]
    return pl.pallas_call(
        functools.partial(_dn_scan_kernel, nchunk=nchunk),
        grid=(nb,),
        in_specs=specs(fwd, 0) + specs(bwd, 1),
        out_specs=[pl.BlockSpec((tm, DN_W), lambda i: (fwd(i), 0)),
                   pl.BlockSpec((tm, DN_W), lambda i: (bwd(i), 0))],
        out_shape=[jax.ShapeDtypeStruct((T_ALL, DN_W), BF16),
                   jax.ShapeDtypeStruct((T_ALL, DN_W), BF16)],
        scratch_shapes=[pltpu.VMEM((2 * DN_HEADS, DN_DH, DN_DH), F32)],
        compiler_params=_cparams(("arbitrary",)),
        name="dn_scan",
    )(u, wq, kq, dl, u, wq, kq, dl)


def _dft_cos_sin(n):
    k = np.arange(n)
    ang = 2.0 * np.pi * ((k[:, None] * k[None, :]) % n) / n
    return np.cos(ang), np.sin(ang)


def _fft_consts():
    n = FFT_N
    c, s = _dft_cos_sin(n)
    cs_ch = np.concatenate([c, s], axis=1)
    w1 = np.block([[c, -s], [-s, -c]])
    k2 = np.arange(n)[None, :, None]
    t2 = np.arange(n)[:, None, None]
    ang = 2.0 * np.pi * ((k2 * t2) % (n * n)) / (n * n)
    scale = 1.0 / math.sqrt(T_LAT * FN_DG)
    cc, sc = _dft_cos_sin(T_CTX)
    scale_c = 1.0 / math.sqrt(T_CTX * FN_DG)
    f32 = lambda a: jnp.asarray(a, F32)
    bf = lambda a: f32(a).astype(BF16)
    return dict(cs_ch=bf(cs_ch), w1=bf(w1), twc=f32(np.cos(ang)), tws=f32(np.sin(ang)),
                c2=bf(c * scale), s2=bf(s * scale), cc=bf(cc * scale_c), sc=bf(sc * scale_c))


def _fft1_kernel(u_ref, cs_ref, w1_ref, twc_ref, tws_ref, y_ref, *, n_t2):
    n = FFT_N
    cs = cs_ref[...]
    w1 = w1_ref[...]
    for t in range(n_t2):
        twc = twc_ref[t]
        tws = tws_ref[t]
        for g in range(0, FN_GROUPS, 2):
            c0 = (t * FN_GROUPS + g) * FN_DG
            ab = [_dot(u_ref[:, c0 + s * FN_DG:c0 + (s + 1) * FN_DG], cs) for s in range(2)]
            rhs = jnp.concatenate([jnp.concatenate([ab[0][:, :FN_DG], ab[1][:, :FN_DG]], axis=1),
                                   jnp.concatenate([ab[0][:, FN_DG:], ab[1][:, FN_DG:]], axis=1)], axis=0)
            y = _dot(w1, rhs.astype(BF16))
            yr, yi = y[:n], y[n:]
            y_ref[0:n, c0:c0 + 2 * FN_DG] = (yr * twc + yi * tws).astype(BF16)
            y_ref[n:2 * n, c0:c0 + 2 * FN_DG] = (yi * twc - yr * tws).astype(BF16)


def _fft2_kernel(y_ref, c2_ref, s2_ref, o_ref, *, n_k2):
    c2 = c2_ref[...]
    s2 = s2_ref[...]
    for j in range(n_k2):
        o_ref[:, j * FN_W:(j + 1) * FN_W] = (_dot(c2, y_ref[0, j]) + _dot(s2, y_ref[1, j])).astype(BF16)


def _fft_ctx_kernel(u_ref, cs_ref, cc_ref, sc_ref, o_in_ref, o_ref):
    del o_in_ref
    for g in range(FN_GROUPS):
        ab = _dot(u_ref[:, g * FN_DG:(g + 1) * FN_DG], cs_ref[...])
        a = ab[:, :FN_DG].astype(BF16)
        b = ab[:, FN_DG:].astype(BF16)
        o_ref[:, g * FN_DG:(g + 1) * FN_DG] = (_dot(cc_ref[...], a) - _dot(sc_ref[...], b)).astype(BF16)


def _fnet(u, fc, with_ctx):
    n = FFT_N
    row_w = n * FN_W
    n_t2 = 8
    tc = n_t2 * FN_W
    y = pl.pallas_call(
        functools.partial(_fft1_kernel, n_t2=n_t2),
        grid=(row_w // tc,),
        in_specs=[pl.BlockSpec((n, tc), lambda j: (0, j)),
                  pl.BlockSpec((n, 2 * n), lambda j: (0, 0)),
                  pl.BlockSpec((2 * n, 2 * n), lambda j: (0, 0)),
                  pl.BlockSpec((n_t2, n, 1), lambda j: (j, 0, 0)),
                  pl.BlockSpec((n_t2, n, 1), lambda j: (j, 0, 0))],
        out_specs=pl.BlockSpec((2 * n, tc), lambda j: (0, j)),
        out_shape=jax.ShapeDtypeStruct((2 * n, row_w), BF16),
        compiler_params=_cparams(("arbitrary",)),
        name="fnet_stage1",
    )(u.reshape(T_ALL // n, row_w), fc["cs_ch"], fc["w1"], fc["twc"], fc["tws"])
    n_k2 = 8
    o = pl.pallas_call(
        functools.partial(_fft2_kernel, n_k2=n_k2),
        grid=(n // n_k2,),
        in_specs=[pl.BlockSpec((2, n_k2, n, FN_W), lambda j: (0, j, 0, 0)),
                  pl.BlockSpec((n, n), lambda j: (0, 0)),
                  pl.BlockSpec((n, n), lambda j: (0, 0))],
        out_specs=pl.BlockSpec((n, n_k2 * FN_W), lambda j: (0, j)),
        out_shape=jax.ShapeDtypeStruct(((T_ALL if with_ctx else T_LAT) // n, row_w), BF16),
        compiler_params=_cparams(("arbitrary",)),
        name="fnet_stage2",
    )(y.reshape(2, n, n, FN_W), fc["c2"], fc["s2"])
    if not with_ctx:
        return o.reshape(T_LAT, FN_W)
    o = o.reshape(T_ALL, FN_W)
    cb = T_LAT // T_CTX
    return pl.pallas_call(
        _fft_ctx_kernel,
        grid=(1,),
        in_specs=[pl.BlockSpec((T_CTX, FN_W), lambda j: (cb, 0)),
                  pl.BlockSpec((n, 2 * n), lambda j: (0, 0)),
                  pl.BlockSpec((T_CTX, T_CTX), lambda j: (0, 0)),
                  pl.BlockSpec((T_CTX, T_CTX), lambda j: (0, 0)),
                  pl.BlockSpec(memory_space=pl.ANY)],
        out_specs=pl.BlockSpec((T_CTX, FN_W), lambda j: (cb, 0)),
        out_shape=jax.ShapeDtypeStruct((T_ALL, FN_W), BF16),
        input_output_aliases={4: 0},
        compiler_params=_cparams(("arbitrary",)),
        name="fnet_ctx",
    )(u, fc["cs_ch"], fc["cc"], fc["sc"], o)


def _merge_kernel(*refs, tm, has_ctx, split):
    if split:
        (x_ref, ctx_ref, mod_ref, n1_ref, ona_ref, of_ref, ob_ref, z_ref, ofn_ref, dnn_ref,
         wg_ref, wna_ref, wdn_ref, wfn_ref, wout_ref, o_ref, tail_scr) = refs
    else:
        (x_ref, mod_ref, n1_ref, ona_ref, of_ref, ob_ref, z_ref, ofn_ref, dnn_ref,
         wg_ref, wna_ref, wdn_ref, wfn_ref, wout_ref, o_ref) = refs
        ctx_ref = tail_scr = None
    i = pl.program_id(0)
    tail_is_ctx = i == pl.num_programs(0) - 1
    x = _token_tile(x_ref, ctx_ref, tail_scr, tail_is_ctx)
    h = _modnorm_tile(x, n1_ref[...], mod_ref, 0, 1, tail_is_ctx, has_ctx).astype(BF16)
    o = of_ref[...].astype(F32) + ob_ref[...].astype(F32)
    z = z_ref[...].astype(F32)
    parts = []
    for hd in range(DN_HEADS):
        sl = slice(hd * DN_DH, (hd + 1) * DN_DH)
        oh = o[:, sl]
        oh = oh * lax.rsqrt(jnp.mean(oh * oh, axis=-1, keepdims=True) + EPS) * dnn_ref[...]
        parts.append(oh * _silu(z[:, sl]))
    odn = jnp.concatenate(parts, axis=-1).astype(BF16)
    branches = ((ona_ref[...], wna_ref), (odn, wdn_ref), (ofn_ref[...], wfn_ref))
    logits = [_dot(h, wg_ref[:, b * D:(b + 1) * D]) for b in range(N_BRANCH)]
    proj = [_dot(br, w_ref[...]) for br, w_ref in branches]
    y = jax.nn.sigmoid(logits[0]) * proj[0]
    for b in range(1, N_BRANCH):
        y = y + jax.nn.sigmoid(logits[b]) * proj[b]
    y = _dot(y.astype(BF16), wout_ref[...])
    o_ref[...] = _gated_residual(x, y, mod_ref, 2, tail_is_ctx, has_ctx)


def _merge(x, ctx, mod, norm1, p, o_na, o_f, o_b, o_fn, dn_norm, w_gate, layer, w_na_o, w_dn_o, w_fn, w_out, has_ctx):
    rows, tm = (T_ALL, TM_MERGE_ALL) if has_ctx else (T_LAT, TM_MERGE_LAT)
    split = ctx is not None
    row = lambda w: pl.BlockSpec((tm, w), lambda i: (i, 0))
    full = lambda a: pl.BlockSpec(a.shape, lambda i: (0, 0), pipeline_mode=pl.Buffered(1))
    ctx_specs = [pl.BlockSpec((T_CTX, D), lambda i: (0, 0))] if split else []
    return pl.pallas_call(
        functools.partial(_merge_kernel, tm=tm, has_ctx=has_ctx, split=split),
        grid=(rows // tm,),
        in_specs=[row(D)] + ctx_specs + [
                  pl.BlockSpec((8, 6 * D), lambda i: (0, 0)), pl.BlockSpec((1, D), lambda i: (0, 0)),
                  row(NA_W), row(DN_W), row(DN_W),
                  pl.BlockSpec((tm, DN_W), lambda i: (i, P_Z)),
                  row(FN_W), pl.BlockSpec((1, DN_DH), lambda i: (0, 0)),
                  pl.BlockSpec((None,) + w_gate.shape[1:], lambda i: (layer, 0, 0), pipeline_mode=pl.Buffered(1)),
                  full(w_na_o), full(w_dn_o), full(w_fn), full(w_out)],
        out_specs=row(D),
        out_shape=jax.ShapeDtypeStruct((rows, D), F32),
        scratch_shapes=[pltpu.VMEM((T_CTX, D), F32)] if split else [],
        compiler_params=_cparams(("arbitrary",)),
        name="merge",
    )(*([x, ctx] if split else [x]), mod, norm1.reshape(1, D), o_na, o_f, o_b, p, o_fn, dn_norm.reshape(1, DN_DH),
      w_gate, w_na_o, w_dn_o, w_fn, w_out)


def _mlp_kernel(x_ref, xnext_ref, mod_ref, n_ref, w1_ref, w2_ref, nf_ref, o_ref, h_scr, acc_scr, *,
                tm, has_ctx, final):
    i = pl.program_id(0)
    j = pl.program_id(1)
    nt = pl.num_programs(0)
    nj = pl.num_programs(1)
    slot = i % 2

    @pl.when(jnp.logical_and(i == 0, j == 0))
    def _():
        h_scr[0] = _modnorm_tile(x_ref[...], n_ref[...], mod_ref, 3, 4, nt == 1, has_ctx).astype(BF16)

    def hidden_step(first):
        a = jnp.maximum(_dot(h_scr[slot], w1_ref[...]), 0.0)
        upd = _dot((a * a).astype(BF16), w2_ref[...])
        if first:
            acc_scr[...] = upd
        else:
            acc_scr[...] += upd

    @pl.when(j == 0)
    def _():
        hidden_step(True)

    @pl.when(jnp.logical_and(j > 0, j < nj - 1))
    def _():
        hidden_step(False)

    @pl.when(j == nj - 1)
    def _():
        h_scr[1 - slot] = _modnorm_tile(xnext_ref[...], n_ref[...], mod_ref, 3, 4, i + 1 == nt - 1, has_ctx).astype(BF16)
        hidden_step(False)
        xn = _gated_residual(x_ref[...], acc_scr[...], mod_ref, 5, i == nt - 1, has_ctx)
        if final:
            xn = xn * lax.rsqrt(jnp.mean(xn * xn, axis=-1, keepdims=True) + EPS) * nf_ref[...]
        o_ref[...] = xn


def _mlp(xs, mod, norm, w1, w2, norm_f, has_ctx, final):
    rows, tm = (T_ALL, TM_ALL) if has_ctx else (T_LAT, TM_LAT)
    th = 1024
    return pl.pallas_call(
        functools.partial(_mlp_kernel, tm=tm, has_ctx=has_ctx, final=final),
        grid=(rows // tm, HID // th),
        in_specs=[pl.BlockSpec((tm, D), lambda i, j: (i, 0)),
                  pl.BlockSpec((tm, D), lambda i, j: (jnp.minimum(i + 1, rows // tm - 1), 0)),
                  pl.BlockSpec((8, 6 * D), lambda i, j: (0, 0)),
                  pl.BlockSpec((1, D), lambda i, j: (0, 0)),
                  pl.BlockSpec((D, th), lambda i, j: (0, j)),
                  pl.BlockSpec((th, D), lambda i, j: (j, 0)),
                  pl.BlockSpec((1, D), lambda i, j: (0, 0))],
        out_specs=pl.BlockSpec((tm, D), lambda i, j: (i, 0)),
        out_shape=jax.ShapeDtypeStruct((rows, D), F32),
        scratch_shapes=[pltpu.VMEM((2, tm, D), BF16), pltpu.VMEM((tm, D), F32)],
        compiler_params=_cparams(("arbitrary", "arbitrary")),
        name="mlp",
    )(xs, xs, mod, norm.reshape(1, D), w1, w2, norm_f.reshape(1, D))


IN_W = 2 * NA_W + 2 * DN_W + 4 * DN_HEADS + NA_W + 2 * DN_W + FN_W + N_BRANCH * D


def _w_prep_kernel(w_ref, wcat_ref, wgate_ref, wg_ref):
    w = w_ref[0]
    offs = np.cumsum([0, NA_W, NA_W, DN_W, DN_W, 4 * DN_HEADS, NA_W, DN_W, DN_W, FN_W, N_BRANCH * D]).tolist()
    na_k, na_v, dn_k, dn_v, dn_ab, na_q, dn_q, dn_z, fn_u, gate = [w[:, a:b] for a, b in zip(offs[:-1], offs[1:])]
    wcat_ref[0] = jnp.concatenate([na_q, na_k, na_v, dn_q, dn_k, dn_v, dn_z, fn_u], axis=-1).astype(BF16)
    wgate_ref[0] = gate.astype(BF16)
    wg_ref[0] = jnp.concatenate([dn_ab, jnp.zeros((w.shape[0], GP - 4 * DN_HEADS), F32)], axis=-1).astype(BF16)


def _split_in_weights(w_in):
    tr = 256
    return pl.pallas_call(
        _w_prep_kernel,
        grid=(DEPTH, D // tr),
        in_specs=[pl.BlockSpec((1, tr, IN_W), lambda l, r: (l, r, 0))],
        out_specs=[pl.BlockSpec((1, tr, P_W), lambda l, r: (l, r, 0)),
                   pl.BlockSpec((1, tr, N_BRANCH * D), lambda l, r: (l, r, 0)),
                   pl.BlockSpec((1, tr, GP), lambda l, r: (l, r, 0))],
        out_shape=[jax.ShapeDtypeStruct((DEPTH, D, P_W), BF16),
                   jax.ShapeDtypeStruct((DEPTH, D, N_BRANCH * D), BF16),
                   jax.ShapeDtypeStruct((DEPTH, D, GP), BF16)],
        compiler_params=_cparams(("arbitrary", "arbitrary")),
        name="w_prep",
    )(w_in)


def kernel(x, c, ctx, c_ctx, w_ada, b_ada, norm1, w_in, conv_w, a_log, dt_bias, dn_norm, rpb,
           w_na_o, w_dn_o, w_fn, w_out, norm2, w_mlp1, w_mlp2, norm_f):
    xs, xc = x[0], ctx[0]
    cc = jnp.concatenate([c, c_ctx[None, :], jnp.zeros((6, D), F32)], axis=0)
    mods = _ada(cc, w_ada, b_ada)
    rope = _rope_tables()
    fc = _fft_consts()
    na_tab = _na_tables(rpb)
    wcat, w_gate, wg = _split_in_weights(w_in)
    for l in range(DEPTH):
        has_ctx = l < DEPTH - 1
        final = l == DEPTH - 1
        u, p, g = _in_proj(xs, xc, mods[l], norm1[l], wcat, wg, l)
        o_na = _na(p, na_tab, l, has_ctx)
        qkv, gb = _dn_prep(p, g, conv_w[l], a_log[l], dt_bias[l], rope)
        o_f, o_b = _dn_scan(*_dn_wy(qkv, gb, DN_BLK_CHUNKS), DN_BLK_CHUNKS)
        o_fn = _fnet(u, fc, has_ctx)
        xs, xc = _merge(xs, xc, mods[l], norm1[l], p, o_na, o_f, o_b, o_fn, dn_norm[l], w_gate, l,
                    w_na_o[l].astype(BF16), w_dn_o[l].astype(BF16), w_fn[l].astype(BF16), w_out[l].astype(BF16),
                    has_ctx), None
        xs = _mlp(xs, mods[l], norm2[l], w_mlp1[l].astype(BF16), w_mlp2[l].astype(BF16), norm_f, has_ctx, final)
    return xs[None]
```

```python
import functools
import math

import numpy as np
import jax
import jax.numpy as jnp
from jax import lax
from jax.experimental import pallas as pl
from jax.experimental.pallas import tpu as pltpu

F32 = jnp.float32
BF16 = jnp.bfloat16

D = 1024
T_LAT = 16384
T_CTX = 256
T_ALL = T_LAT + T_CTX
DEPTH = 2
GRID_W = 64
GRID_H = T_LAT // GRID_W
NA_HEADS = 8
NA_DH = 64
NA_WIN_H = 8
NA_WIN_W = 16
NA_W = NA_HEADS * NA_DH
DN_HEADS = 4
DN_DH = 128
DN_W = DN_HEADS * DN_DH
DN_CONV = 5
CHUNK = 64
FN_GROUPS = 4
FN_DG = 128
FN_W = FN_GROUPS * FN_DG
N_BRANCH = 3
HID = 4 * D
ROPE_BASE = 10000.0
EPS = 1e-6
NEG = -1e30

P_W = 3 * NA_W + 4 * DN_W + FN_W
P_NAQ, P_NAK, P_NAV = 0, 1, 2
P_DN = 1
P_Z = 6
TN_IN = P_W // 2
GP = 128

TM_ALL = 1280
TM_LAT = 1024
TM_MERGE_ALL = 640
TM_MERGE_LAT = 512
NA_BLK = 4 * GRID_W
NA_HEAD_GROUP = 2
NA_VARIANTS = 4
FFT_N = 128
DN_BLK_CHUNKS = 4
WY_GROUP_CHUNKS = 4
VMEM_LIMIT = 56 * 1024 * 1024


def _cparams(sem):
    return pltpu.CompilerParams(dimension_semantics=sem, vmem_limit_bytes=VMEM_LIMIT)


def _dot(a, b):
    return jnp.dot(a, b, preferred_element_type=F32)


def _dot_nt(a, b):
    return lax.dot_general(a, b, (((1,), (1,)), ((), ())), preferred_element_type=F32)


def _dot_tn(a, b):
    return lax.dot_general(a, b, (((0,), (0,)), ((), ())), preferred_element_type=F32)


def _silu(x):
    return x * jax.nn.sigmoid(x)


def _ada_kernel(c_ref, w_ref, b_ref, o_ref):
    s = _silu(c_ref[...])
    s_hi = s.astype(BF16)
    s_lo = (s - s_hi.astype(F32)).astype(BF16)
    w = w_ref[0].astype(BF16)
    o_ref[0] = _dot(s_hi, w) + _dot(s_lo, w) + b_ref[0]


def _ada(cc, w_ada, b_ada):
    tn = 1536
    return pl.pallas_call(
        _ada_kernel,
        grid=(DEPTH, 6 * D // tn),
        in_specs=[pl.BlockSpec((8, D), lambda l, j: (0, 0)),
                  pl.BlockSpec((1, D, tn), lambda l, j: (l, 0, j)),
                  pl.BlockSpec((1, 1, tn), lambda l, j: (l, 0, j))],
        out_specs=pl.BlockSpec((1, 8, tn), lambda l, j: (l, 0, j)),
        out_shape=jax.ShapeDtypeStruct((DEPTH, 8, 6 * D), F32),
        compiler_params=_cparams(("arbitrary", "arbitrary")),
        name="ada_mod",
    )(cc, w_ada, b_ada.reshape(DEPTH, 1, 6 * D))


def _row_slices(tm, has_ctx):
    return [slice(0, tm)] if not has_ctx else [slice(0, tm - T_CTX), slice(tm - T_CTX, tm)]


def _mod_vecs(mod_ref, k, tail_is_ctx, has_ctx):
    lat = mod_ref[0:1, k * D:(k + 1) * D]
    if not has_ctx:
        return [lat]
    return [lat, jnp.where(tail_is_ctx, mod_ref[1:2, k * D:(k + 1) * D], lat)]


def _modnorm(x, g, shift, scale):
    y = x * lax.rsqrt(jnp.mean(x * x, axis=-1, keepdims=True) + EPS)
    return y * (g * (1.0 + scale)) + shift


def _modnorm_tile(x, g, mod_ref, k_shift, k_scale, tail_is_ctx, has_ctx):
    tm = x.shape[0]
    parts = [_modnorm(x[sl], g, sh, sc) for sl, sh, sc in zip(_row_slices(tm, has_ctx),
                                                            _mod_vecs(mod_ref, k_shift, tail_is_ctx, has_ctx),
                                                            _mod_vecs(mod_ref, k_scale, tail_is_ctx, has_ctx))]
    return parts[0] if len(parts) == 1 else jnp.concatenate(parts, axis=0)


def _gated_residual(x, y, mod_ref, k_gate, tail_is_ctx, has_ctx):
    tm = x.shape[0]
    parts = [x[sl] + gt * y[sl] for sl, gt in zip(_row_slices(tm, has_ctx), _mod_vecs(mod_ref, k_gate, tail_is_ctx, has_ctx))]
    return parts[0] if len(parts) == 1 else jnp.concatenate(parts, axis=0)


def _token_tile(x_ref, ctx_ref, tail_scr, is_last):
    if ctx_ref is None:
        return x_ref[...]
    top = x_ref.shape[0] - T_CTX

    @pl.when(jnp.logical_not(is_last))
    def _():
        tail_scr[...] = x_ref[top:, :]

    @pl.when(is_last)
    def _():
        tail_scr[...] = ctx_ref[...]

    return jnp.concatenate([x_ref[0:top, :], tail_scr[...]], axis=0)


def _in_proj_kernel(*refs, tm, split):
    if split:
        x_ref, ctx_ref, mod_ref, n_ref, w_ref, wg_ref, u_ref, p_ref, g_ref, h_scr, tail_scr = refs
    else:
        x_ref, mod_ref, n_ref, w_ref, wg_ref, u_ref, p_ref, g_ref, h_scr = refs
        ctx_ref = tail_scr = None
    i = pl.program_id(0)
    j = pl.program_id(1)

    @pl.when(j == 0)
    def _():
        is_last = i == pl.num_programs(0) - 1
        x = _token_tile(x_ref, ctx_ref, tail_scr, is_last)
        h = _modnorm_tile(x, n_ref[...], mod_ref, 0, 1, is_last, True).astype(BF16)
        h_scr[...] = h
        g_ref[...] = _dot(h, wg_ref[...])

    p = _dot(h_scr[...], w_ref[...]).astype(BF16)
    p_ref[...] = p

    @pl.when(j == pl.num_programs(1) - 1)
    def _():
        u_ref[...] = p[:, TN_IN - FN_W:]


def _in_proj(x, ctx, mod, norm, wcat, wg, layer):
    tm = TM_ALL
    nj = P_W // TN_IN
    split = ctx is not None
    ctx_specs = [pl.BlockSpec((T_CTX, D), lambda i, j: (0, 0))] if split else []
    return pl.pallas_call(
        functools.partial(_in_proj_kernel, tm=tm, split=split),
        grid=(T_ALL // tm, nj),
        in_specs=[pl.BlockSpec((tm, D), lambda i, j: (i, 0))] + ctx_specs + [
                  pl.BlockSpec((8, 6 * D), lambda i, j: (0, 0)),
                  pl.BlockSpec((1, D), lambda i, j: (0, 0)),
                  pl.BlockSpec((None, D, TN_IN), lambda i, j: (layer, 0, j)),
                  pl.BlockSpec((None, D, GP), lambda i, j: (layer, 0, 0))],
        out_specs=[pl.BlockSpec((tm, FN_W), lambda i, j: (i, 0)),
                   pl.BlockSpec((tm, TN_IN), lambda i, j: (i, j)),
                   pl.BlockSpec((tm, GP), lambda i, j: (i, 0))],
        out_shape=[jax.ShapeDtypeStruct((T_ALL, FN_W), BF16),
                   jax.ShapeDtypeStruct((T_ALL, P_W), BF16),
                   jax.ShapeDtypeStruct((T_ALL, GP), F32)],
        scratch_shapes=[pltpu.VMEM((tm, D), BF16)] + ([pltpu.VMEM((T_CTX, D), F32)] if split else []),
        compiler_params=_cparams(("arbitrary", "arbitrary")),
        name="in_proj",
    )(*([x, ctx] if split else [x]), mod, norm.reshape(1, D), wcat, wg)


def _na_tables(rpb):
    qr = np.arange(4)[:, None, None, None]
    qc = np.arange(GRID_W)[None, :, None, None]
    kk = np.arange(12)[None, None, :, None]
    kc = np.arange(GRID_W)[None, None, None, :]
    valid = []
    for b in (0, 1, GRID_H // 4 - 1):
        r = 4 * b + qr
        kr = 4 * (b - 1) + kk
        rs = np.clip(r - NA_WIN_H // 2, 0, GRID_H - NA_WIN_H)
        cs = np.clip(qc - NA_WIN_W // 2, 0, GRID_W - NA_WIN_W)
        ok = (kr >= 0) & (kr < GRID_H) & (kr >= rs) & (kr < rs + NA_WIN_H) & (kc >= cs) & (kc < cs + NA_WIN_W)
        valid.append(np.broadcast_to(ok, (4, GRID_W, 12, GRID_W)).reshape(NA_BLK, 12 * GRID_W))
    valid.append(np.zeros_like(valid[0]))
    valid = np.stack(valid)
    col = np.arange(GRID_W)
    onehot = ((col[None, None, :] - col[None, :, None] + NA_WIN_W - 1)
              == np.arange(2 * NA_WIN_W - 1)[:, None, None]).astype(np.float32)
    tcol = jnp.einsum('lhrd,dqk->lhrqk', rpb, onehot, precision=lax.Precision.HIGHEST)
    bias = jnp.stack([jnp.stack([tcol[:, :, k - q + NA_WIN_H - 5] for k in range(12)], axis=3) for q in range(4)], axis=2)
    bias = bias.reshape(DEPTH, 1, NA_HEADS, NA_BLK, 12 * GRID_W)
    tab = jnp.where(valid[None, :, None], bias, NEG).astype(BF16)
    return tab.reshape(DEPTH * len(valid), NA_HEADS, NA_BLK, 12 * GRID_W)


def _na_kernel(q_ref, k0_ref, k1_ref, k2_ref, v0_ref, v1_ref, v2_ref, kc_ref, vc_ref, tab_ref, o_ref):
    q = q_ref[...]
    kl = jnp.concatenate([k0_ref[...], k1_ref[...], k2_ref[...]], axis=0)
    vl = jnp.concatenate([v0_ref[...], v1_ref[...], v2_ref[...]], axis=0)
    kc = kc_ref[...]
    vc = vc_ref[...]
    lane_l = lax.broadcasted_iota(jnp.int32, (vl.shape[0], 2 * NA_DH), 1)
    lane_c = lax.broadcasted_iota(jnp.int32, (vc.shape[0], 2 * NA_DH), 1)
    one = jnp.ones((), BF16)
    outs = []
    for h0 in range(0, NA_HEADS, NA_HEAD_GROUP):
        heads = range(h0, h0 + NA_HEAD_GROUP)
        sls = [slice(h * NA_DH, (h + 1) * NA_DH) for h in heads]
        pls = [slice((h // 2) * 2 * NA_DH, (h // 2 + 1) * 2 * NA_DH) for h in heads]
        own = [(lambda ln, h=h: (ln < NA_DH) if h % 2 == 0 else (ln >= NA_DH)) for h in heads]
        qs = [q[:, sl] * (NA_DH ** -0.5) for sl in sls]
        z_loc = [_dot_nt(qh, kl[:, sl]).astype(BF16) + tab_ref[0, h] for h, qh, sl in zip(heads, qs, sls)]
        z_ctx = [_dot_nt(qh, kc[:, sl]).astype(BF16) for qh, sl in zip(qs, sls)]
        m = [jnp.maximum(jnp.max(a, axis=-1, keepdims=True), jnp.max(b, axis=-1, keepdims=True))
             for a, b in zip(z_loc, z_ctx)]
        p_loc = [jnp.exp(a - mm) for a, mm in zip(z_loc, m)]
        p_ctx = [jnp.exp(b - mm) for b, mm in zip(z_ctx, m)]
        oa = [_dot(a, jnp.where(f(lane_l), vl[:, ps], one)) + _dot(b, jnp.where(f(lane_c), vc[:, ps], one))
              for a, b, ps, f in zip(p_loc, p_ctx, pls, own)]
        outs += [(x[:, :NA_DH] / x[:, NA_DH:NA_DH + 1]) if h % 2 == 0 else (x[:, NA_DH:] / x[:, 0:1])
                 for h, x in zip(heads, oa)]
    o_ref[...] = jnp.concatenate(outs, axis=-1).astype(BF16)


def _na(p, tab, layer, with_ctx):
    nlat = T_LAT // NA_BLK
    nb = nlat + (1 if with_ctx else 0)
    ctx_blk = T_LAT // NA_BLK

    def kidx(b, part):
        return jnp.clip(jnp.minimum(b, nlat - 1) - 1 + part, 0, nlat - 1)

    def variant(b):
        return layer * NA_VARIANTS + jnp.where(b == 0, 0, jnp.where(b == nlat - 1, 2, jnp.where(b >= nlat, 3, 1)))

    blk = (NA_BLK, NA_W)
    in_specs = [pl.BlockSpec(blk, lambda b: (b, P_NAQ))]
    in_specs += [pl.BlockSpec(blk, functools.partial(lambda b, part: (kidx(b, part), P_NAK), part=part))
                 for part in range(3)]
    in_specs += [pl.BlockSpec(blk, functools.partial(lambda b, part: (kidx(b, part), P_NAV), part=part))
                 for part in range(3)]
    in_specs += [pl.BlockSpec(blk, lambda b: (ctx_blk, P_NAK)),
                 pl.BlockSpec(blk, lambda b: (ctx_blk, P_NAV)),
                 pl.BlockSpec((1, NA_HEADS, NA_BLK, 3 * NA_BLK), lambda b: (variant(b), 0, 0, 0))]
    return pl.pallas_call(
        _na_kernel,
        grid=(nb,),
        in_specs=in_specs,
        out_specs=pl.BlockSpec(blk, lambda b: (b, 0)),
        out_shape=jax.ShapeDtypeStruct((nb * NA_BLK, NA_W), BF16),
        compiler_params=_cparams(("arbitrary",)),
        name="na_attn",
    )(p, p, p, p, p, p, p, p, p, tab)


def _rope_tables():
    t = jnp.arange(T_LAT, dtype=jnp.int32)
    half = DN_DH // 2
    pos = jnp.stack([t // GRID_W, t % GRID_W], axis=-1).astype(F32)
    inv = 1.0 / (ROPE_BASE ** (jnp.arange(0, half, 2, dtype=F32) / half))
    ang = pos[:, :, None] * inv[None, None, :]
    ang = jnp.concatenate([ang, ang], axis=-1).reshape(T_LAT, DN_DH)
    cos, sin = jnp.cos(ang), jnp.sin(ang)
    low = (np.arange(DN_DH) % half) < half // 2
    return cos, jnp.where(low[None], -sin, 0.0), jnp.where(low[None], 0.0, sin)


def _dn_prep_kernel(x_ref, prev_ref, next_ref, g_ref, cw_ref, alog_ref, dtb_ref, cos_ref, sa_ref, sb_ref,
                    qkv_ref, gb_ref, *, tm):
    b = pl.program_id(0)
    nlat = T_LAT // tm
    first = jnp.logical_or(b == 0, b == nlat)
    last = b >= nlat - 1
    is_ctx = b >= nlat
    cur = x_ref[...].astype(F32)
    prev = jnp.where(first, 0.0, prev_ref[...][14:16].astype(F32))
    nxt = jnp.where(last, 0.0, next_ref[...][0:2].astype(F32))
    ext = jnp.concatenate([prev, cur, nxt], axis=0)
    cw = cw_ref[...]
    y = ext[0:tm] * cw[0:1]
    for k in range(1, DN_CONV):
        y = y + ext[k:k + tm] * cw[k:k + 1]
    y = _silu(y)
    cos = jnp.where(is_ctx, 1.0, cos_ref[...])
    sa = jnp.where(is_ctx, 0.0, sa_ref[...])
    sb = jnp.where(is_ctx, 0.0, sb_ref[...])
    quarter = DN_DH // 4
    for s in range(2):
        for h in range(DN_HEADS):
            c0 = s * DN_W + h * DN_DH
            xh = y[:, c0:c0 + DN_DH]
            xh = xh * lax.rsqrt(jnp.sum(xh * xh, axis=-1, keepdims=True) + EPS)
            xh = (xh * cos + pltpu.roll(xh, DN_DH - quarter, axis=1) * sa + pltpu.roll(xh, quarter, axis=1) * sb)
            if s == 0:
                xh = xh * (DN_DH ** -0.5)
            qkv_ref[:, c0:c0 + DN_DH] = xh.astype(BF16)
    qkv_ref[:, 2 * DN_W:] = y[:, 2 * DN_W:].astype(BF16)
    g = g_ref[...]
    lane = lax.broadcasted_iota(jnp.int32, g.shape, 1)
    decay = -jnp.exp(alog_ref[...]) * jax.nn.softplus(g + dtb_ref[...])
    gb_ref[...] = jnp.where(lane < 2 * DN_HEADS, decay, jax.nn.sigmoid(g))


def _dn_prep(p, g, conv_w, a_log, dt_bias, rope):
    tm = 256
    nb = T_ALL // tm
    nlat = T_LAT // tm
    hb = 16
    nh = T_ALL // hb
    pad = lambda v: jnp.pad(v.reshape(1, 2 * DN_HEADS).astype(F32), ((0, 0), (0, GP - 2 * DN_HEADS)))
    rspec = pl.BlockSpec((tm, DN_DH), lambda b: (jnp.minimum(b, nlat - 1), 0))
    return pl.pallas_call(
        functools.partial(_dn_prep_kernel, tm=tm),
        grid=(nb,),
        in_specs=[pl.BlockSpec((tm, 3 * DN_W), lambda b: (b, P_DN)),
                  pl.BlockSpec((hb, 3 * DN_W), lambda b: (jnp.maximum(b * (tm // hb) - 1, 0), P_DN)),
                  pl.BlockSpec((hb, 3 * DN_W), lambda b: (jnp.minimum((b + 1) * (tm // hb), nh - 1), P_DN)),
                  pl.BlockSpec((tm, GP), lambda b: (b, 0)),
                  pl.BlockSpec((DN_CONV, 3 * DN_W), lambda b: (0, 0)),
                  pl.BlockSpec((1, GP), lambda b: (0, 0)),
                  pl.BlockSpec((1, GP), lambda b: (0, 0)),
                  rspec, rspec, rspec],
        out_specs=[pl.BlockSpec((tm, 3 * DN_W), lambda b: (b, 0)),
                   pl.BlockSpec((tm, GP), lambda b: (b, 0))],
        out_shape=[jax.ShapeDtypeStruct((T_ALL, 3 * DN_W), BF16),
                   jax.ShapeDtypeStruct((T_ALL, GP), F32)],
        compiler_params=_cparams(("arbitrary",)),
        name="dn_prep",
    )(p, p, p, g, conv_w, pad(a_log), pad(dt_bias), *rope)


def _pair_blockdiag(y):
    c = y.shape[0]
    lane = lax.broadcasted_iota(jnp.int32, y.shape, 1)
    zero = jnp.zeros((), y.dtype)
    return jnp.concatenate([jnp.where(lane < c, y, zero), jnp.where(lane >= c, y, zero)], axis=0)


def _unit_tri_inverses_minus_identity(mats):
    c = mats[0].shape[0]
    ri = lax.broadcasted_iota(jnp.int32, (c, 2 * c), 0)
    ci = lax.broadcasted_iota(jnp.int32, (c, 2 * c), 1) % c

    def same_block(s):
        return (ri // s) == (ci // s)

    def mm(xs, ys):
        return [_dot(x.astype(BF16), _pair_blockdiag(y.astype(BF16))) for x, y in zip(xs, ys)]

    diag = same_block(8)
    b1 = [jnp.where(diag, -a, 0.0) for a in mats]
    b2 = mm(b1, b1)
    b3 = mm(b1, b2)
    b4 = mm(b2, b2)
    n2 = [x + y + z for x, y, z in zip(b1, b2, b3)]
    n = [x + y + z for x, y, z in zip(n2, b4, mm(n2, b4))]
    for s in (8, 16, 32):
        off = jnp.logical_and(same_block(2 * s), jnp.logical_not(same_block(s)))
        lo = [jnp.where(off, a, 0.0) for a in mats]
        x = [p + q for p, q in zip(lo, mm(n, lo))]
        n = [p - (q + r) for p, q, r in zip(n, x, mm(x, n))]
    return n


def _dn_wy_kernel(qkv_ref, gb_ref, u_ref, wq_ref, kq_ref, dl_ref, *, nchunk):
    c = CHUNK
    tm = nchunk * c
    gb = gb_ref[...]
    qkv = qkv_ref[...]
    rt = lax.broadcasted_iota(jnp.int32, (tm, tm), 0)
    ct = lax.broadcasted_iota(jnp.int32, (tm, tm), 1)
    same_chunk = (rt // c) == (ct // c)
    ri = lax.broadcasted_iota(jnp.int32, (c, 2 * c), 0)
    lane2 = lax.broadcasted_iota(jnp.int32, (c, 2 * c), 1)
    ci = lane2 % c
    first = lane2 < c
    lane = lax.broadcasted_iota(jnp.int32, (8, GP), 1)
    lane_k = lax.broadcasted_iota(jnp.int32, (c, 2 * DN_DH), 1)
    zk = jnp.zeros((), BF16)
    eye_p = (ri == ci).astype(F32)
    eye2 = (lax.broadcasted_iota(jnp.int32, (DN_DH, 2 * DN_DH), 1) % DN_DH
            == lax.broadcasted_iota(jnp.int32, (DN_DH, 2 * DN_DH), 0)).astype(BF16)
    gb_t = gb.T
    dl_rows, gcs = [], []
    for d in range(2):
        cum = jnp.logical_and(same_chunk, (ct <= rt) if d == 0 else (ct >= rt))
        gc_all = jnp.dot(cum.astype(F32), gb, preferred_element_type=F32, precision=lax.Precision.HIGHEST)
        gcs.append((gc_all, gc_all.T))
        dl_rows.append([jnp.exp(gc_all[j * c + (c - 1 if d == 0 else 0):j * c + (c if d == 0 else 1), :])
                        for j in range(nchunk)])
    for j0 in range(0, nchunk, WY_GROUP_CHUNKS):
        shared = {}
        for j in range(j0, j0 + WY_GROUP_CHUNKS):
            for pr in range(DN_HEADS // 2):
                r0 = j * c
                hs = slice(2 * pr * DN_DH, (2 * pr + 2) * DN_DH)
                bd = lambda x: jnp.concatenate([jnp.where(lane_k < DN_DH, x, zk), jnp.where(lane_k >= DN_DH, x, zk)],
                                               axis=0)
                q2 = qkv[r0:r0 + c, hs]
                k2 = qkv[r0:r0 + c, DN_W + hs.start:DN_W + hs.stop]
                v2 = qkv[r0:r0 + c, 2 * DN_W + hs.start:2 * DN_W + hs.stop]
                bd_k = bd(k2)
                shared[(j, pr)] = dict(q2=q2, bd_k=bd_k, bd_v=bd(v2),
                                       gqt=_dot_nt(jnp.concatenate([k2, q2, eye2], axis=0), bd_k))
        units = [(d, j, pr) for j in range(j0, j0 + WY_GROUP_CHUNKS) for d in range(2) for pr in range(DN_HEADS // 2)]
        pre = []
        for d, j, pr in units:
            gc_all, gc_all_t = gcs[d]
            incl = (ci <= ri) if d == 0 else (ci >= ri)
            strict = (ci < ri) if d == 0 else (ci > ri)
            r0 = j * c
            last = r0 + (c - 1 if d == 0 else 0)
            i0 = d * DN_HEADS + 2 * pr
            col = lambda x, a: x[r0:r0 + c, a:a + 1]
            row = lambda x, a: jnp.concatenate([x[a:a + 1, r0:r0 + c], x[a + 1:a + 2, r0:r0 + c]], axis=1)
            beta_p = jnp.where(first, col(gb, 2 * DN_HEADS + i0), col(gb, 2 * DN_HEADS + i0 + 1))
            gc_p = jnp.where(first, col(gc_all, i0), col(gc_all, i0 + 1))
            gc_row = row(gc_all_t, i0)
            beta_row = row(gb_t, 2 * DN_HEADS + i0)
            g_last = jnp.where(first[0:1], gc_all[last:last + 1, i0:i0 + 1], gc_all[last:last + 1, i0 + 1:i0 + 2])
            e = jnp.exp(jnp.where(incl, gc_p - gc_row, 0.0))
            eg_row = jnp.exp(gc_row)
            pre.append(dict(beta_p=beta_p, e_incl=jnp.where(incl, e, 0.0), e_strict=jnp.where(strict, e, 0.0),
                            beta_row=beta_row, eg_row=eg_row, tail_row=jnp.exp(g_last - gc_row)))
        gqt = [shared[(j, pr)]["gqt"] for d, j, pr in units]
        ns = _unit_tri_inverses_minus_identity([p["beta_p"] * x[:c] * p["e_strict"] for x, p in zip(gqt, pre)])
        tmat = {u_: n + eye_p for u_, n in zip(units, ns)}
        prd = dict(zip(units, pre))
        both = [(j, pr) for j in range(j0, j0 + WY_GROUP_CHUNKS) for pr in range(DN_HEADS // 2)]
        lhs_u = [jnp.concatenate([(tmat[(d, j, pr)] * prd[(d, j, pr)]["beta_row"]).astype(BF16) for d in range(2)], axis=0)
                 for j, pr in both]
        lhs_w = [jnp.concatenate([(tmat[(d, j, pr)] * (prd[(d, j, pr)]["beta_row"] * prd[(d, j, pr)]["eg_row"])
                                   ).astype(BF16) for d in range(2)], axis=0) for j, pr in both]
        u2 = dict(zip(both, [_dot(a, shared[jp]["bd_v"]) for a, jp in zip(lhs_u, both)]))
        w2 = dict(zip(both, [_dot(a, shared[jp]["bd_k"]) for a, jp in zip(lhs_w, both)]))
        for (d, j, pr), p, x in zip(units, pre, gqt):
            r0 = j * c
            i0 = d * DN_HEADS + 2 * pr
            cs = slice(i0 * DN_DH, (i0 + 2) * DN_DH)
            ks = slice(i0 * c, (i0 + 2) * c)
            q2 = shared[(j, pr)]["q2"].astype(F32)
            gcol = gcs[d][0][r0:r0 + c]
            eg = jnp.where(lane_k < DN_DH, jnp.exp(gcol[:, i0:i0 + 1]), jnp.exp(gcol[:, i0 + 1:i0 + 2]))
            u_ref[r0:r0 + c, cs] = u2[(j, pr)][d * c:(d + 1) * c].astype(BF16)
            wq_ref[2 * r0:2 * r0 + c, cs] = w2[(j, pr)][d * c:(d + 1) * c].astype(BF16)
            wq_ref[2 * r0 + c:2 * r0 + 2 * c, cs] = (q2 * eg).astype(BF16)
            kq_ref[3 * r0:3 * r0 + DN_DH, ks] = (x[2 * c:] * p["tail_row"]).astype(BF16)
            kq_ref[3 * r0 + DN_DH:3 * r0 + DN_DH + c, ks] = (x[c:2 * c] * p["e_incl"]).astype(BF16)
    for j in range(nchunk):
        row = jnp.where(lane < DN_HEADS, dl_rows[0][j], dl_rows[1][j])
        dl_ref[8 * j:8 * j + 8, :] = row


def _dn_wy(qkv, gb, nchunk):
    tm = nchunk * CHUNK
    nb = T_ALL // tm
    nch = 2 * DN_HEADS
    return pl.pallas_call(
        functools.partial(_dn_wy_kernel, nchunk=nchunk),
        grid=(nb,),
        in_specs=[pl.BlockSpec((tm, 3 * DN_W), lambda i: (i, 0)),
                  pl.BlockSpec((tm, GP), lambda i: (i, 0))],
        out_specs=[pl.BlockSpec((tm, nch * DN_DH), lambda i: (i, 0)),
                   pl.BlockSpec((2 * tm, nch * DN_DH), lambda i: (i, 0)),
                   pl.BlockSpec((3 * tm, nch * CHUNK), lambda i: (i, 0)),
                   pl.BlockSpec((8 * nchunk, GP), lambda i: (i, 0))],
        out_shape=[jax.ShapeDtypeStruct((T_ALL, nch * DN_DH), BF16),
                   jax.ShapeDtypeStruct((2 * T_ALL, nch * DN_DH), BF16),
                   jax.ShapeDtypeStruct((3 * T_ALL, nch * CHUNK), BF16),
                   jax.ShapeDtypeStruct((8 * T_ALL // CHUNK, GP), F32)],
        compiler_params=_cparams(("arbitrary",)),
        name="dn_wy",
    )(qkv, gb)


def _dn_scan_kernel(u_f_ref, wq_f_ref, kq_f_ref, dl_f_ref, u_b_ref, wq_b_ref, kq_b_ref, dl_b_ref,
                    of_ref, ob_ref, s_scr, *, nchunk):
    i = pl.program_id(0)

    @pl.when(i == 0)
    def _():
        s_scr[...] = jnp.zeros_like(s_scr)

    c = CHUNK
    states = [s_scr[idx] for idx in range(2 * DN_HEADS)]
    zero = jnp.zeros((c, DN_DH), BF16)
    refs = ((u_f_ref, wq_f_ref, kq_f_ref, dl_f_ref, of_ref), (u_b_ref, wq_b_ref, kq_b_ref, dl_b_ref, ob_ref))
    chains = [(d, h) for d in range(2) for h in range(DN_HEADS)]
    pairs = [(d, p) for d in range(2) for p in range(DN_HEADS // 2)]
    for step in range(nchunk):
        blk = lambda d: step if d == 0 else nchunk - 1 - step
        hs = lambda h: slice(h * DN_DH, (h + 1) * DN_DH)
        r1 = [_dot(refs[d][1][2 * blk(d) * c:2 * (blk(d) + 1) * c, hs(h)], states[d * DN_HEADS + h].astype(BF16))
              for d, h in chains]
        v_new = [(refs[d][0][blk(d) * c:(blk(d) + 1) * c, hs(h)].astype(F32) - r[:c]).astype(BF16)
                 for (d, h), r in zip(chains, r1)]
        r2 = [_dot(refs[d][2][3 * blk(d) * c:3 * (blk(d) + 1) * c, p * 2 * c:(p + 1) * 2 * c],
                   jnp.concatenate([jnp.concatenate([v_new[d * DN_HEADS + 2 * p], zero], axis=1),
                                    jnp.concatenate([zero, v_new[d * DN_HEADS + 2 * p + 1]], axis=1)], axis=0))
              for d, p in pairs]
        for d, h in chains:
            idx = d * DN_HEADS + h
            r = r2[d * (DN_HEADS // 2) + h // 2]
            ts = slice((h % 2) * DN_DH, (h % 2 + 1) * DN_DH)
            dl_row = refs[d][3][8 * blk(d):8 * blk(d) + 1, :]
            states[idx] = states[idx] * dl_row[:, idx:idx + 1] + r[:DN_DH, ts]
            refs[d][4][blk(d) * c:(blk(d) + 1) * c, hs(h)] = (r1[idx][c:] + r[DN_DH:, ts]).astype(BF16)
    for idx in range(2 * DN_HEADS):
        s_scr[idx] = states[idx]


def _dn_scan(u, wq, kq, dl, nchunk):
    tm = nchunk * CHUNK
    nb = T_ALL // tm
    nlat = T_LAT // tm
    fwd = lambda i: jnp.where(i == 0, nlat, i - 1)
    bwd = lambda i: nb - 1 - i
    specs = lambda f, col: [pl.BlockSpec((tm, DN_W), lambda i: (f(i), col)),
                            pl.BlockSpec((2 * tm, DN_W), lambda i: (f(i), col)),
                            pl.BlockSpec((3 * tm, DN_HEADS * CHUNK), lambda i: (f(i), col)),
                            pl.BlockSpec((8 * nchunk, GP), lambda i: (f(i), 0))]
    return pl.pallas_call(
        functools.partial(_dn_scan_kernel, nchunk=nchunk),
        grid=(nb,),
        in_specs=specs(fwd, 0) + specs(bwd, 1),
        out_specs=[pl.BlockSpec((tm, DN_W), lambda i: (fwd(i), 0)),
                   pl.BlockSpec((tm, DN_W), lambda i: (bwd(i), 0))],
        out_shape=[jax.ShapeDtypeStruct((T_ALL, DN_W), BF16),
                   jax.ShapeDtypeStruct((T_ALL, DN_W), BF16)],
        scratch_shapes=[pltpu.VMEM((2 * DN_HEADS, DN_DH, DN_DH), F32)],
        compiler_params=_cparams(("arbitrary",)),
        name="dn_scan",
    )(u, wq, kq, dl, u, wq, kq, dl)


def _dft_cos_sin(n):
    k = np.arange(n)
    ang = 2.0 * np.pi * ((k[:, None] * k[None, :]) % n) / n
    return np.cos(ang), np.sin(ang)


def _fft_consts():
    n = FFT_N
    c, s = _dft_cos_sin(n)
    cs_ch = np.concatenate([c, s], axis=1)
    w1 = np.block([[c, -s], [-s, -c]])
    k2 = np.arange(n)[None, :, None]
    t2 = np.arange(n)[:, None, None]
    ang = 2.0 * np.pi * ((k2 * t2) % (n * n)) / (n * n)
    scale = 1.0 / math.sqrt(T_LAT * FN_DG)
    cc, sc = _dft_cos_sin(T_CTX)
    scale_c = 1.0 / math.sqrt(T_CTX * FN_DG)
    f32 = lambda a: jnp.asarray(a, F32)
    bf = lambda a: f32(a).astype(BF16)
    return dict(cs_ch=bf(cs_ch), w1=bf(w1), twc=f32(np.cos(ang)), tws=f32(np.sin(ang)),
                c2=bf(c * scale), s2=bf(s * scale), cc=bf(cc * scale_c), sc=bf(sc * scale_c))


def _fft1_kernel(u_ref, cs_ref, w1_ref, twc_ref, tws_ref, y_ref, *, n_t2):
    n = FFT_N
    cs = cs_ref[...]
    w1 = w1_ref[...]
    for t in range(n_t2):
        twc = twc_ref[t]
        tws = tws_ref[t]
        for g in range(0, FN_GROUPS, 2):
            c0 = (t * FN_GROUPS + g) * FN_DG
            ab = [_dot(u_ref[:, c0 + s * FN_DG:c0 + (s + 1) * FN_DG], cs) for s in range(2)]
            rhs = jnp.concatenate([jnp.concatenate([ab[0][:, :FN_DG], ab[1][:, :FN_DG]], axis=1),
                                   jnp.concatenate([ab[0][:, FN_DG:], ab[1][:, FN_DG:]], axis=1)], axis=0)
            y = _dot(w1, rhs.astype(BF16))
            yr, yi = y[:n], y[n:]
            y_ref[0:n, c0:c0 + 2 * FN_DG] = (yr * twc + yi * tws).astype(BF16)
            y_ref[n:2 * n, c0:c0 + 2 * FN_DG] = (yi * twc - yr * tws).astype(BF16)


def _fft2_kernel(y_ref, c2_ref, s2_ref, o_ref, *, n_k2):
    c2 = c2_ref[...]
    s2 = s2_ref[...]
    for j in range(n_k2):
        o_ref[:, j * FN_W:(j + 1) * FN_W] = (_dot(c2, y_ref[0, j]) + _dot(s2, y_ref[1, j])).astype(BF16)


def _fft_ctx_kernel(u_ref, cs_ref, cc_ref, sc_ref, o_in_ref, o_ref):
    del o_in_ref
    for g in range(FN_GROUPS):
        ab = _dot(u_ref[:, g * FN_DG:(g + 1) * FN_DG], cs_ref[...])
        a = ab[:, :FN_DG].astype(BF16)
        b = ab[:, FN_DG:].astype(BF16)
        o_ref[:, g * FN_DG:(g + 1) * FN_DG] = (_dot(cc_ref[...], a) - _dot(sc_ref[...], b)).astype(BF16)


def _fnet(u, fc, with_ctx):
    n = FFT_N
    row_w = n * FN_W
    n_t2 = 8
    tc = n_t2 * FN_W
    y = pl.pallas_call(
        functools.partial(_fft1_kernel, n_t2=n_t2),
        grid=(row_w // tc,),
        in_specs=[pl.BlockSpec((n, tc), lambda j: (0, j)),
                  pl.BlockSpec((n, 2 * n), lambda j: (0, 0)),
                  pl.BlockSpec((2 * n, 2 * n), lambda j: (0, 0)),
                  pl.BlockSpec((n_t2, n, 1), lambda j: (j, 0, 0)),
                  pl.BlockSpec((n_t2, n, 1), lambda j: (j, 0, 0))],
        out_specs=pl.BlockSpec((2 * n, tc), lambda j: (0, j)),
        out_shape=jax.ShapeDtypeStruct((2 * n, row_w), BF16),
        compiler_params=_cparams(("arbitrary",)),
        name="fnet_stage1",
    )(u.reshape(T_ALL // n, row_w), fc["cs_ch"], fc["w1"], fc["twc"], fc["tws"])
    n_k2 = 8
    o = pl.pallas_call(
        functools.partial(_fft2_kernel, n_k2=n_k2),
        grid=(n // n_k2,),
        in_specs=[pl.BlockSpec((2, n_k2, n, FN_W), lambda j: (0, j, 0, 0)),
                  pl.BlockSpec((n, n), lambda j: (0, 0)),
                  pl.BlockSpec((n, n), lambda j: (0, 0))],
        out_specs=pl.BlockSpec((n, n_k2 * FN_W), lambda j: (0, j)),
        out_shape=jax.ShapeDtypeStruct(((T_ALL if with_ctx else T_LAT) // n, row_w), BF16),
        compiler_params=_cparams(("arbitrary",)),
        name="fnet_stage2",
    )(y.reshape(2, n, n, FN_W), fc["c2"], fc["s2"])
    if not with_ctx:
        return o.reshape(T_LAT, FN_W)
    o = o.reshape(T_ALL, FN_W)
    cb = T_LAT // T_CTX
    return pl.pallas_call(
        _fft_ctx_kernel,
        grid=(1,),
        in_specs=[pl.BlockSpec((T_CTX, FN_W), lambda j: (cb, 0)),
                  pl.BlockSpec((n, 2 * n), lambda j: (0, 0)),
                  pl.BlockSpec((T_CTX, T_CTX), lambda j: (0, 0)),
                  pl.BlockSpec((T_CTX, T_CTX), lambda j: (0, 0)),
                  pl.BlockSpec(memory_space=pl.ANY)],
        out_specs=pl.BlockSpec((T_CTX, FN_W), lambda j: (cb, 0)),
        out_shape=jax.ShapeDtypeStruct((T_ALL, FN_W), BF16),
        input_output_aliases={4: 0},
        compiler_params=_cparams(("arbitrary",)),
        name="fnet_ctx",
    )(u, fc["cs_ch"], fc["cc"], fc["sc"], o)


def _merge_kernel(*refs, tm, has_ctx, split):
    if split:
        (x_ref, ctx_ref, mod_ref, n1_ref, ona_ref, of_ref, ob_ref, z_ref, ofn_ref, dnn_ref,
         wg_ref, wna_ref, wdn_ref, wfn_ref, wout_ref, o_ref, tail_scr) = refs
    else:
        (x_ref, mod_ref, n1_ref, ona_ref, of_ref, ob_ref, z_ref, ofn_ref, dnn_ref,
         wg_ref, wna_ref, wdn_ref, wfn_ref, wout_ref, o_ref) = refs
        ctx_ref = tail_scr = None
    i = pl.program_id(0)
    tail_is_ctx = i == pl.num_programs(0) - 1
    x = _token_tile(x_ref, ctx_ref, tail_scr, tail_is_ctx)
    h = _modnorm_tile(x, n1_ref[...], mod_ref, 0, 1, tail_is_ctx, has_ctx).astype(BF16)
    o = of_ref[...].astype(F32) + ob_ref[...].astype(F32)
    z = z_ref[...].astype(F32)
    parts = []
    for hd in range(DN_HEADS):
        sl = slice(hd * DN_DH, (hd + 1) * DN_DH)
        oh = o[:, sl]
        oh = oh * lax.rsqrt(jnp.mean(oh * oh, axis=-1, keepdims=True) + EPS) * dnn_ref[...]
        parts.append(oh * _silu(z[:, sl]))
    odn = jnp.concatenate(parts, axis=-1).astype(BF16)
    branches = ((ona_ref[...], wna_ref), (odn, wdn_ref), (ofn_ref[...], wfn_ref))
    logits = [_dot(h, wg_ref[:, b * D:(b + 1) * D]) for b in range(N_BRANCH)]
    proj = [_dot(br, w_ref[...]) for br, w_ref in branches]
    y = jax.nn.sigmoid(logits[0]) * proj[0]
    for b in range(1, N_BRANCH):
        y = y + jax.nn.sigmoid(logits[b]) * proj[b]
    y = _dot(y.astype(BF16), wout_ref[...])
    o_ref[...] = _gated_residual(x, y, mod_ref, 2, tail_is_ctx, has_ctx)


def _merge(x, ctx, mod, norm1, p, o_na, o_f, o_b, o_fn, dn_norm, w_gate, layer, w_na_o, w_dn_o, w_fn, w_out, has_ctx):
    rows, tm = (T_ALL, TM_MERGE_ALL) if has_ctx else (T_LAT, TM_MERGE_LAT)
    split = ctx is not None
    row = lambda w: pl.BlockSpec((tm, w), lambda i: (i, 0))
    full = lambda a: pl.BlockSpec(a.shape, lambda i: (0, 0), pipeline_mode=pl.Buffered(1))
    ctx_specs = [pl.BlockSpec((T_CTX, D), lambda i: (0, 0))] if split else []
    return pl.pallas_call(
        functools.partial(_merge_kernel, tm=tm, has_ctx=has_ctx, split=split),
        grid=(rows // tm,),
        in_specs=[row(D)] + ctx_specs + [
                  pl.BlockSpec((8, 6 * D), lambda i: (0, 0)), pl.BlockSpec((1, D), lambda i: (0, 0)),
                  row(NA_W), row(DN_W), row(DN_W),
                  pl.BlockSpec((tm, DN_W), lambda i: (i, P_Z)),
                  row(FN_W), pl.BlockSpec((1, DN_DH), lambda i: (0, 0)),
                  pl.BlockSpec((None,) + w_gate.shape[1:], lambda i: (layer, 0, 0), pipeline_mode=pl.Buffered(1)),
                  full(w_na_o), full(w_dn_o), full(w_fn), full(w_out)],
        out_specs=row(D),
        out_shape=jax.ShapeDtypeStruct((rows, D), F32),
        scratch_shapes=[pltpu.VMEM((T_CTX, D), F32)] if split else [],
        compiler_params=_cparams(("arbitrary",)),
        name="merge",
    )(*([x, ctx] if split else [x]), mod, norm1.reshape(1, D), o_na, o_f, o_b, p, o_fn, dn_norm.reshape(1, DN_DH),
      w_gate, w_na_o, w_dn_o, w_fn, w_out)


def _mlp_kernel(x_ref, xnext_ref, mod_ref, n_ref, w1_ref, w2_ref, nf_ref, o_ref, h_scr, acc_scr, *,
                tm, has_ctx, final):
    i = pl.program_id(0)
    j = pl.program_id(1)
    nt = pl.num_programs(0)
    nj = pl.num_programs(1)
    slot = i % 2

    @pl.when(jnp.logical_and(i == 0, j == 0))
    def _():
        h_scr[0] = _modnorm_tile(x_ref[...], n_ref[...], mod_ref, 3, 4, nt == 1, has_ctx).astype(BF16)

    def hidden_step(first):
        a = jnp.maximum(_dot(h_scr[slot], w1_ref[...]), 0.0)
        upd = _dot((a * a).astype(BF16), w2_ref[...])
        if first:
            acc_scr[...] = upd
        else:
            acc_scr[...] += upd

    @pl.when(j == 0)
    def _():
        hidden_step(True)

    @pl.when(jnp.logical_and(j > 0, j < nj - 1))
    def _():
        hidden_step(False)

    @pl.when(j == nj - 1)
    def _():
        h_scr[1 - slot] = _modnorm_tile(xnext_ref[...], n_ref[...], mod_ref, 3, 4, i + 1 == nt - 1, has_ctx).astype(BF16)
        hidden_step(False)
        xn = _gated_residual(x_ref[...], acc_scr[...], mod_ref, 5, i == nt - 1, has_ctx)
        if final:
            xn = xn * lax.rsqrt(jnp.mean(xn * xn, axis=-1, keepdims=True) + EPS) * nf_ref[...]
        o_ref[...] = xn


def _mlp(xs, mod, norm, w1, w2, norm_f, has_ctx, final):
    rows, tm = (T_ALL, TM_ALL) if has_ctx else (T_LAT, TM_LAT)
    th = 1024
    return pl.pallas_call(
        functools.partial(_mlp_kernel, tm=tm, has_ctx=has_ctx, final=final),
        grid=(rows // tm, HID // th),
        in_specs=[pl.BlockSpec((tm, D), lambda i, j: (i, 0)),
                  pl.BlockSpec((tm, D), lambda i, j: (jnp.minimum(i + 1, rows // tm - 1), 0)),
                  pl.BlockSpec((8, 6 * D), lambda i, j: (0, 0)),
                  pl.BlockSpec((1, D), lambda i, j: (0, 0)),
                  pl.BlockSpec((D, th), lambda i, j: (0, j)),
                  pl.BlockSpec((th, D), lambda i, j: (j, 0)),
                  pl.BlockSpec((1, D), lambda i, j: (0, 0))],
        out_specs=pl.BlockSpec((tm, D), lambda i, j: (i, 0)),
        out_shape=jax.ShapeDtypeStruct((rows, D), F32),
        scratch_shapes=[pltpu.VMEM((2, tm, D), BF16), pltpu.VMEM((tm, D), F32)],
        compiler_params=_cparams(("arbitrary", "arbitrary")),
        name="mlp",
    )(xs, xs, mod, norm.reshape(1, D), w1, w2, norm_f.reshape(1, D))


IN_W = 2 * NA_W + 2 * DN_W + 4 * DN_HEADS + NA_W + 2 * DN_W + FN_W + N_BRANCH * D


W_PREP_BLK = 512
_W_SRC = [2 * NA_W + 2 * DN_W + 4 * DN_HEADS, 0, NA_W, 3 * NA_W + 2 * DN_W + 4 * DN_HEADS, 2 * NA_W, 2 * NA_W + DN_W,
          3 * NA_W + 3 * DN_W + 4 * DN_HEADS, 3 * NA_W + 4 * DN_W + 4 * DN_HEADS] + [
          IN_W - N_BRANCH * D + i * W_PREP_BLK for i in range(N_BRANCH * D // W_PREP_BLK)]
_W_AB = 2 * NA_W + 2 * DN_W


def _w_prep_kernel(w_ref, ab_ref, wcat_ref, wgate_ref, wg_ref):
    c = pl.program_id(1)
    ncat = P_W // W_PREP_BLK
    blk = w_ref[0].T.astype(BF16)

    @pl.when(c < ncat)
    def _():
        wcat_ref[...] = blk

    @pl.when(c >= ncat)
    def _():
        wgate_ref[...] = blk

    @pl.when(c == 0)
    def _():
        ab = jnp.concatenate([ab_ref[0], jnp.zeros((GP - ab_ref.shape[1], D), F32)], axis=0)
        wg_ref[...] = ab.T.astype(BF16)


def _split_in_weights(w_in):
    ncat = P_W // W_PREP_BLK
    nblk = len(_W_SRC)

    def src(c):
        off = 0
        for k, o in enumerate(_W_SRC):
            off = off + jnp.where(c == k, o // 16, 0)
        return off * 16

    return pl.pallas_call(
        _w_prep_kernel,
        grid=(DEPTH, nblk),
        in_specs=[pl.BlockSpec((pl.Element(1), pl.Element(W_PREP_BLK), pl.Element(D)), lambda l, c: (l, src(c), 0)),
                  pl.BlockSpec((pl.Element(1), pl.Element(4 * DN_HEADS), pl.Element(D)), lambda l, c: (l, _W_AB, 0))],
        out_specs=[pl.BlockSpec((None, D, W_PREP_BLK), lambda l, c: (l, 0, jnp.minimum(c, ncat - 1))),
                   pl.BlockSpec((None, D, W_PREP_BLK), lambda l, c: (l, 0, jnp.maximum(c - ncat, 0))),
                   pl.BlockSpec((None, D, GP), lambda l, c: (l, 0, 0))],
        out_shape=[jax.ShapeDtypeStruct((DEPTH, D, P_W), BF16),
                   jax.ShapeDtypeStruct((DEPTH, D, N_BRANCH * D), BF16),
                   jax.ShapeDtypeStruct((DEPTH, D, GP), BF16)],
        compiler_params=_cparams(("arbitrary", "arbitrary")),
        name="w_prep",
    )(jnp.swapaxes(w_in, 1, 2), jnp.swapaxes(w_in, 1, 2))


def kernel(x, c, ctx, c_ctx, w_ada, b_ada, norm1, w_in, conv_w, a_log, dt_bias, dn_norm, rpb,
           w_na_o, w_dn_o, w_fn, w_out, norm2, w_mlp1, w_mlp2, norm_f):
    xs, xc = x[0], ctx[0]
    cc = jnp.concatenate([c, c_ctx[None, :], jnp.zeros((6, D), F32)], axis=0)
    mods = _ada(cc, w_ada, b_ada)
    rope = _rope_tables()
    fc = _fft_consts()
    na_tab = _na_tables(rpb)
    wcat, w_gate, wg = _split_in_weights(w_in)
    for l in range(DEPTH):
        has_ctx = l < DEPTH - 1
        final = l == DEPTH - 1
        u, p, g = _in_proj(xs, xc, mods[l], norm1[l], wcat, wg, l)
        o_na = _na(p, na_tab, l, has_ctx)
        qkv, gb = _dn_prep(p, g, conv_w[l], a_log[l], dt_bias[l], rope)
        o_f, o_b = _dn_scan(*_dn_wy(qkv, gb, DN_BLK_CHUNKS), DN_BLK_CHUNKS)
        o_fn = _fnet(u, fc, has_ctx)
        xs, xc = _merge(xs, xc, mods[l], norm1[l], p, o_na, o_f, o_b, o_fn, dn_norm[l], w_gate, l,
                    w_na_o[l].astype(BF16), w_dn_o[l].astype(BF16), w_fn[l].astype(BF16), w_out[l].astype(BF16),
                    has_ctx), None
        xs = _mlp(xs, mods[l], norm2[l], w_mlp1[l].astype(BF16), w_mlp2[l].astype(BF16), norm_f, has_ctx, final)
    return xs[None]
```

```python
import functools
import math

import numpy as np
import jax
import jax.numpy as jnp
from jax import lax
from jax.experimental import pallas as pl
from jax.experimental.pallas import tpu as pltpu

F32 = jnp.float32
BF16 = jnp.bfloat16

D = 1024
T_LAT = 16384
T_CTX = 256
T_ALL = T_LAT + T_CTX
DEPTH = 2
GRID_W = 64
GRID_H = T_LAT // GRID_W
NA_HEADS = 8
NA_DH = 64
NA_WIN_H = 8
NA_WIN_W = 16
NA_W = NA_HEADS * NA_DH
DN_HEADS = 4
DN_DH = 128
DN_W = DN_HEADS * DN_DH
DN_CONV = 5
CHUNK = 64
FN_GROUPS = 4
FN_DG = 128
FN_W = FN_GROUPS * FN_DG
N_BRANCH = 3
HID = 4 * D
ROPE_BASE = 10000.0
EPS = 1e-6
NEG = -1e30

P_W = 3 * NA_W + 4 * DN_W + FN_W
P_NAQ, P_NAK, P_NAV = 0, 1, 2
P_DN = 1
P_Z = 6
TN_IN = P_W // 2
GP = 128

TM_ALL = 1280
TM_LAT = 1024
TM_MERGE_ALL = 640
TM_MERGE_LAT = 512
NA_BLK = 4 * GRID_W
NA_HEAD_GROUP = 2
NA_VARIANTS = 4
FFT_N = 128
DN_BLK_CHUNKS = 4
WY_GROUP_CHUNKS = 4
VMEM_LIMIT = 56 * 1024 * 1024


def _cparams(sem):
    return pltpu.CompilerParams(dimension_semantics=sem, vmem_limit_bytes=VMEM_LIMIT)


def _dot(a, b):
    return jnp.dot(a, b, preferred_element_type=F32)


def _dot_nt(a, b):
    return lax.dot_general(a, b, (((1,), (1,)), ((), ())), preferred_element_type=F32)


def _dot_tn(a, b):
    return lax.dot_general(a, b, (((0,), (0,)), ((), ())), preferred_element_type=F32)


def _silu(x):
    return x * jax.nn.sigmoid(x)


def _ada_kernel(c_ref, w_ref, b_ref, o_ref):
    s = _silu(c_ref[...])
    s_hi = s.astype(BF16)
    s_lo = (s - s_hi.astype(F32)).astype(BF16)
    w = w_ref[0].astype(BF16)
    o_ref[0] = _dot(s_hi, w) + _dot(s_lo, w) + b_ref[0]


def _ada(cc, w_ada, b_ada):
    tn = 1536
    return pl.pallas_call(
        _ada_kernel,
        grid=(DEPTH, 6 * D // tn),
        in_specs=[pl.BlockSpec((8, D), lambda l, j: (0, 0)),
                  pl.BlockSpec((1, D, tn), lambda l, j: (l, 0, j)),
                  pl.BlockSpec((1, 1, tn), lambda l, j: (l, 0, j))],
        out_specs=pl.BlockSpec((1, 8, tn), lambda l, j: (l, 0, j)),
        out_shape=jax.ShapeDtypeStruct((DEPTH, 8, 6 * D), F32),
        compiler_params=_cparams(("arbitrary", "arbitrary")),
        name="ada_mod",
    )(cc, w_ada, b_ada.reshape(DEPTH, 1, 6 * D))


def _row_slices(tm, has_ctx):
    return [slice(0, tm)] if not has_ctx else [slice(0, tm - T_CTX), slice(tm - T_CTX, tm)]


def _mod_vecs(mod_ref, k, tail_is_ctx, has_ctx):
    lat = mod_ref[0:1, k * D:(k + 1) * D]
    if not has_ctx:
        return [lat]
    return [lat, jnp.where(tail_is_ctx, mod_ref[1:2, k * D:(k + 1) * D], lat)]


def _modnorm(x, g, shift, scale):
    y = x * lax.rsqrt(jnp.mean(x * x, axis=-1, keepdims=True) + EPS)
    return y * (g * (1.0 + scale)) + shift


def _modnorm_tile(x, g, mod_ref, k_shift, k_scale, tail_is_ctx, has_ctx):
    tm = x.shape[0]
    parts = [_modnorm(x[sl], g, sh, sc) for sl, sh, sc in zip(_row_slices(tm, has_ctx),
                                                            _mod_vecs(mod_ref, k_shift, tail_is_ctx, has_ctx),
                                                            _mod_vecs(mod_ref, k_scale, tail_is_ctx, has_ctx))]
    return parts[0] if len(parts) == 1 else jnp.concatenate(parts, axis=0)


def _gated_residual(x, y, mod_ref, k_gate, tail_is_ctx, has_ctx):
    tm = x.shape[0]
    parts = [x[sl] + gt * y[sl] for sl, gt in zip(_row_slices(tm, has_ctx), _mod_vecs(mod_ref, k_gate, tail_is_ctx, has_ctx))]
    return parts[0] if len(parts) == 1 else jnp.concatenate(parts, axis=0)


def _token_tile(x_ref, ctx_ref, tail_scr, is_last):
    if ctx_ref is None:
        return x_ref[...]
    top = x_ref.shape[0] - T_CTX

    @pl.when(jnp.logical_not(is_last))
    def _():
        tail_scr[...] = x_ref[top:, :]

    @pl.when(is_last)
    def _():
        tail_scr[...] = ctx_ref[...]

    return jnp.concatenate([x_ref[0:top, :], tail_scr[...]], axis=0)


def _in_proj_kernel(*refs, tm, split):
    if split:
        x_ref, ctx_ref, mod_ref, n_ref, w_ref, wg_ref, u_ref, p_ref, g_ref, h_scr, tail_scr = refs
    else:
        x_ref, mod_ref, n_ref, w_ref, wg_ref, u_ref, p_ref, g_ref, h_scr = refs
        ctx_ref = tail_scr = None
    i = pl.program_id(0)
    j = pl.program_id(1)

    @pl.when(j == 0)
    def _():
        is_last = i == pl.num_programs(0) - 1
        x = _token_tile(x_ref, ctx_ref, tail_scr, is_last)
        h = _modnorm_tile(x, n_ref[...], mod_ref, 0, 1, is_last, True).astype(BF16)
        h_scr[...] = h
        g_ref[...] = _dot(h, wg_ref[...])

    p = _dot(h_scr[...], w_ref[...]).astype(BF16)
    p_ref[...] = p

    @pl.when(j == pl.num_programs(1) - 1)
    def _():
        u_ref[...] = p[:, TN_IN - FN_W:]


def _in_proj(x, ctx, mod, norm, wcat, wg, layer):
    tm = TM_ALL
    nj = P_W // TN_IN
    split = ctx is not None
    ctx_specs = [pl.BlockSpec((T_CTX, D), lambda i, j: (0, 0))] if split else []
    return pl.pallas_call(
        functools.partial(_in_proj_kernel, tm=tm, split=split),
        grid=(T_ALL // tm, nj),
        in_specs=[pl.BlockSpec((tm, D), lambda i, j: (i, 0))] + ctx_specs + [
                  pl.BlockSpec((8, 6 * D), lambda i, j: (0, 0)),
                  pl.BlockSpec((1, D), lambda i, j: (0, 0)),
                  pl.BlockSpec((None, D, TN_IN), lambda i, j: (layer, 0, j)),
                  pl.BlockSpec((None, D, GP), lambda i, j: (layer, 0, 0))],
        out_specs=[pl.BlockSpec((tm, FN_W), lambda i, j: (i, 0)),
                   pl.BlockSpec((tm, TN_IN), lambda i, j: (i, j)),
                   pl.BlockSpec((tm, GP), lambda i, j: (i, 0))],
        out_shape=[jax.ShapeDtypeStruct((T_ALL, FN_W), BF16),
                   jax.ShapeDtypeStruct((T_ALL, P_W), BF16),
                   jax.ShapeDtypeStruct((T_ALL, GP), F32)],
        scratch_shapes=[pltpu.VMEM((tm, D), BF16)] + ([pltpu.VMEM((T_CTX, D), F32)] if split else []),
        compiler_params=_cparams(("arbitrary", "arbitrary")),
        name="in_proj",
    )(*([x, ctx] if split else [x]), mod, norm.reshape(1, D), wcat, wg)


def _na_tables(rpb):
    qr = np.arange(4)[:, None, None, None]
    qc = np.arange(GRID_W)[None, :, None, None]
    kk = np.arange(12)[None, None, :, None]
    kc = np.arange(GRID_W)[None, None, None, :]
    valid = []
    for b in (0, 1, GRID_H // 4 - 1):
        r = 4 * b + qr
        kr = 4 * (b - 1) + kk
        rs = np.clip(r - NA_WIN_H // 2, 0, GRID_H - NA_WIN_H)
        cs = np.clip(qc - NA_WIN_W // 2, 0, GRID_W - NA_WIN_W)
        ok = (kr >= 0) & (kr < GRID_H) & (kr >= rs) & (kr < rs + NA_WIN_H) & (kc >= cs) & (kc < cs + NA_WIN_W)
        valid.append(np.broadcast_to(ok, (4, GRID_W, 12, GRID_W)).reshape(NA_BLK, 12 * GRID_W))
    valid.append(np.zeros_like(valid[0]))
    valid = np.stack(valid)
    col = np.arange(GRID_W)
    onehot = ((col[None, None, :] - col[None, :, None] + NA_WIN_W - 1)
              == np.arange(2 * NA_WIN_W - 1)[:, None, None]).astype(np.float32)
    tcol = jnp.einsum('lhrd,dqk->lhrqk', rpb, onehot, precision=lax.Precision.HIGHEST)
    bias = jnp.stack([jnp.stack([tcol[:, :, k - q + NA_WIN_H - 5] for k in range(12)], axis=3) for q in range(4)], axis=2)
    bias = bias.reshape(DEPTH, 1, NA_HEADS, NA_BLK, 12 * GRID_W)
    tab = jnp.where(valid[None, :, None], bias, NEG).astype(BF16)
    return tab.reshape(DEPTH * len(valid), NA_HEADS, NA_BLK, 12 * GRID_W)


def _na_kernel(q_ref, k0_ref, k1_ref, k2_ref, v0_ref, v1_ref, v2_ref, kc_ref, vc_ref, tab_ref, o_ref):
    q = q_ref[...]
    kl = jnp.concatenate([k0_ref[...], k1_ref[...], k2_ref[...]], axis=0)
    vl = jnp.concatenate([v0_ref[...], v1_ref[...], v2_ref[...]], axis=0)
    kc = kc_ref[...]
    vc = vc_ref[...]
    lane_l = lax.broadcasted_iota(jnp.int32, (vl.shape[0], 2 * NA_DH), 1)
    lane_c = lax.broadcasted_iota(jnp.int32, (vc.shape[0], 2 * NA_DH), 1)
    one = jnp.ones((), BF16)
    outs = []
    for h0 in range(0, NA_HEADS, NA_HEAD_GROUP):
        heads = range(h0, h0 + NA_HEAD_GROUP)
        sls = [slice(h * NA_DH, (h + 1) * NA_DH) for h in heads]
        pls = [slice((h // 2) * 2 * NA_DH, (h // 2 + 1) * 2 * NA_DH) for h in heads]
        own = [(lambda ln, h=h: (ln < NA_DH) if h % 2 == 0 else (ln >= NA_DH)) for h in heads]
        qs = [q[:, sl] * (NA_DH ** -0.5) for sl in sls]
        z_loc = [_dot_nt(qh, kl[:, sl]).astype(BF16) + tab_ref[0, h] for h, qh, sl in zip(heads, qs, sls)]
        z_ctx = [_dot_nt(qh, kc[:, sl]).astype(BF16) for qh, sl in zip(qs, sls)]
        m = [jnp.maximum(jnp.max(a, axis=-1, keepdims=True), jnp.max(b, axis=-1, keepdims=True))
             for a, b in zip(z_loc, z_ctx)]
        p_loc = [jnp.exp(a - mm) for a, mm in zip(z_loc, m)]
        p_ctx = [jnp.exp(b - mm) for b, mm in zip(z_ctx, m)]
        oa = [_dot(a, jnp.where(f(lane_l), vl[:, ps], one)) + _dot(b, jnp.where(f(lane_c), vc[:, ps], one))
              for a, b, ps, f in zip(p_loc, p_ctx, pls, own)]
        outs += [(x[:, :NA_DH] / x[:, NA_DH:NA_DH + 1]) if h % 2 == 0 else (x[:, NA_DH:] / x[:, 0:1])
                 for h, x in zip(heads, oa)]
    o_ref[...] = jnp.concatenate(outs, axis=-1).astype(BF16)


def _na(p, tab, layer, with_ctx):
    nlat = T_LAT // NA_BLK
    nb = nlat + (1 if with_ctx else 0)
    ctx_blk = T_LAT // NA_BLK

    def kidx(b, part):
        return jnp.clip(jnp.minimum(b, nlat - 1) - 1 + part, 0, nlat - 1)

    def variant(b):
        return layer * NA_VARIANTS + jnp.where(b == 0, 0, jnp.where(b == nlat - 1, 2, jnp.where(b >= nlat, 3, 1)))

    blk = (NA_BLK, NA_W)
    in_specs = [pl.BlockSpec(blk, lambda b: (b, P_NAQ))]
    in_specs += [pl.BlockSpec(blk, functools.partial(lambda b, part: (kidx(b, part), P_NAK), part=part))
                 for part in range(3)]
    in_specs += [pl.BlockSpec(blk, functools.partial(lambda b, part: (kidx(b, part), P_NAV), part=part))
                 for part in range(3)]
    in_specs += [pl.BlockSpec(blk, lambda b: (ctx_blk, P_NAK)),
                 pl.BlockSpec(blk, lambda b: (ctx_blk, P_NAV)),
                 pl.BlockSpec((1, NA_HEADS, NA_BLK, 3 * NA_BLK), lambda b: (variant(b), 0, 0, 0))]
    return pl.pallas_call(
        _na_kernel,
        grid=(nb,),
        in_specs=in_specs,
        out_specs=pl.BlockSpec(blk, lambda b: (b, 0)),
        out_shape=jax.ShapeDtypeStruct((nb * NA_BLK, NA_W), BF16),
        compiler_params=_cparams(("arbitrary",)),
        name="na_attn",
    )(p, p, p, p, p, p, p, p, p, tab)


def _rope_tables():
    half = DN_DH // 2
    inv = 1.0 / (ROPE_BASE ** (np.arange(0, half, 2, dtype=np.float64) / half))
    inv = np.concatenate([inv, inv])
    low = np.arange(half) < half // 2

    def tabs(n):
        ang = np.arange(n, dtype=np.float64)[:, None] * inv[None, :]
        return np.stack([np.cos(ang), np.where(low, -np.sin(ang), 0.0), np.where(low, 0.0, np.sin(ang))])

    rows = tabs(GRID_H).reshape(3, GRID_H // 4, 4, half)
    return jnp.asarray(rows, F32), jnp.asarray(tabs(GRID_W), F32)


def _dn_prep_kernel(x_ref, prev_ref, next_ref, g_ref, cw_ref, alog_ref, dtb_ref, rt_ref, ct_ref,
                    qkv_ref, gb_ref, *, tm):
    b = pl.program_id(0)
    nlat = T_LAT // tm
    first = jnp.logical_or(b == 0, b == nlat)
    last = b >= nlat - 1
    is_ctx = b >= nlat
    cur = x_ref[...].astype(F32)
    prev = jnp.where(first, 0.0, prev_ref[...][14:16].astype(F32))
    nxt = jnp.where(last, 0.0, next_ref[...][0:2].astype(F32))
    ext = jnp.concatenate([prev, cur, nxt], axis=0)
    cw = cw_ref[...]
    y = ext[0:tm] * cw[0:1]
    for k in range(1, DN_CONV):
        y = y + ext[k:k + tm] * cw[k:k + 1]
    y = _silu(y)
    def rope_factor(k):
        col = ct_ref[k]
        return jnp.concatenate([jnp.concatenate([jnp.broadcast_to(rt_ref[k, r:r + 1, :], col.shape), col], axis=1)
                                for r in range(tm // GRID_W)], axis=0)

    cos = jnp.where(is_ctx, 1.0, rope_factor(0))
    sa = jnp.where(is_ctx, 0.0, rope_factor(1))
    sb = jnp.where(is_ctx, 0.0, rope_factor(2))
    quarter = DN_DH // 4
    for s in range(2):
        for h in range(DN_HEADS):
            c0 = s * DN_W + h * DN_DH
            xh = y[:, c0:c0 + DN_DH]
            xh = xh * lax.rsqrt(jnp.sum(xh * xh, axis=-1, keepdims=True) + EPS)
            xh = (xh * cos + pltpu.roll(xh, DN_DH - quarter, axis=1) * sa + pltpu.roll(xh, quarter, axis=1) * sb)
            if s == 0:
                xh = xh * (DN_DH ** -0.5)
            qkv_ref[:, c0:c0 + DN_DH] = xh.astype(BF16)
    qkv_ref[:, 2 * DN_W:] = y[:, 2 * DN_W:].astype(BF16)
    g = g_ref[...]
    lane = lax.broadcasted_iota(jnp.int32, g.shape, 1)
    decay = -jnp.exp(alog_ref[...]) * jax.nn.softplus(g + dtb_ref[...])
    gb_ref[...] = jnp.where(lane < 2 * DN_HEADS, decay, jax.nn.sigmoid(g))


def _dn_prep(p, g, conv_w, a_log, dt_bias, rope):
    tm = 256
    nb = T_ALL // tm
    nlat = T_LAT // tm
    hb = 16
    nh = T_ALL // hb
    pad = lambda v: jnp.pad(v.reshape(1, 2 * DN_HEADS).astype(F32), ((0, 0), (0, GP - 2 * DN_HEADS)))
    half = DN_DH // 2
    rt_spec = pl.BlockSpec((3, None, tm // GRID_W, half), lambda b: (0, jnp.minimum(b, nlat - 1), 0, 0))
    ct_spec = pl.BlockSpec((3, GRID_W, half), lambda b: (0, 0, 0))
    return pl.pallas_call(
        functools.partial(_dn_prep_kernel, tm=tm),
        grid=(nb,),
        in_specs=[pl.BlockSpec((tm, 3 * DN_W), lambda b: (b, P_DN)),
                  pl.BlockSpec((hb, 3 * DN_W), lambda b: (jnp.maximum(b * (tm // hb) - 1, 0), P_DN)),
                  pl.BlockSpec((hb, 3 * DN_W), lambda b: (jnp.minimum((b + 1) * (tm // hb), nh - 1), P_DN)),
                  pl.BlockSpec((tm, GP), lambda b: (b, 0)),
                  pl.BlockSpec((DN_CONV, 3 * DN_W), lambda b: (0, 0)),
                  pl.BlockSpec((1, GP), lambda b: (0, 0)),
                  pl.BlockSpec((1, GP), lambda b: (0, 0)),
                  rt_spec, ct_spec],
        out_specs=[pl.BlockSpec((tm, 3 * DN_W), lambda b: (b, 0)),
                   pl.BlockSpec((tm, GP), lambda b: (b, 0))],
        out_shape=[jax.ShapeDtypeStruct((T_ALL, 3 * DN_W), BF16),
                   jax.ShapeDtypeStruct((T_ALL, GP), F32)],
        compiler_params=_cparams(("arbitrary",)),
        name="dn_prep",
    )(p, p, p, g, conv_w, pad(a_log), pad(dt_bias), *rope)


def _pair_blockdiag(y):
    c = y.shape[0]
    lane = lax.broadcasted_iota(jnp.int32, y.shape, 1)
    zero = jnp.zeros((), y.dtype)
    return jnp.concatenate([jnp.where(lane < c, y, zero), jnp.where(lane >= c, y, zero)], axis=0)


def _unit_tri_inverses_minus_identity(mats):
    c = mats[0].shape[0]
    ri = lax.broadcasted_iota(jnp.int32, (c, 2 * c), 0)
    ci = lax.broadcasted_iota(jnp.int32, (c, 2 * c), 1) % c

    def same_block(s):
        return (ri // s) == (ci // s)

    def mm(xs, ys):
        return [_dot(x.astype(BF16), _pair_blockdiag(y.astype(BF16))) for x, y in zip(xs, ys)]

    diag = same_block(8)
    b1 = [jnp.where(diag, -a, 0.0) for a in mats]
    b2 = mm(b1, b1)
    b3 = mm(b1, b2)
    b4 = mm(b2, b2)
    n2 = [x + y + z for x, y, z in zip(b1, b2, b3)]
    n = [x + y + z for x, y, z in zip(n2, b4, mm(n2, b4))]
    for s in (8, 16, 32):
        off = jnp.logical_and(same_block(2 * s), jnp.logical_not(same_block(s)))
        lo = [jnp.where(off, a, 0.0) for a in mats]
        x = [p + q for p, q in zip(lo, mm(n, lo))]
        n = [p - (q + r) for p, q, r in zip(n, x, mm(x, n))]
    return n


def _dn_wy_kernel(qkv_ref, gb_ref, u_ref, wq_ref, kq_ref, dl_ref, *, nchunk):
    c = CHUNK
    tm = nchunk * c
    gb = gb_ref[...]
    qkv = qkv_ref[...]
    rt = lax.broadcasted_iota(jnp.int32, (tm, tm), 0)
    ct = lax.broadcasted_iota(jnp.int32, (tm, tm), 1)
    same_chunk = (rt // c) == (ct // c)
    ri = lax.broadcasted_iota(jnp.int32, (c, 2 * c), 0)
    lane2 = lax.broadcasted_iota(jnp.int32, (c, 2 * c), 1)
    ci = lane2 % c
    first = lane2 < c
    lane = lax.broadcasted_iota(jnp.int32, (8, GP), 1)
    lane_k = lax.broadcasted_iota(jnp.int32, (c, 2 * DN_DH), 1)
    zk = jnp.zeros((), BF16)
    eye_p = (ri == ci).astype(F32)
    eye2 = (lax.broadcasted_iota(jnp.int32, (DN_DH, 2 * DN_DH), 1) % DN_DH
            == lax.broadcasted_iota(jnp.int32, (DN_DH, 2 * DN_DH), 0)).astype(BF16)
    gb_t = gb.T
    dl_rows, gcs = [], []
    for d in range(2):
        cum = jnp.logical_and(same_chunk, (ct <= rt) if d == 0 else (ct >= rt))
        gc_all = jnp.dot(cum.astype(F32), gb, preferred_element_type=F32, precision=lax.Precision.HIGHEST)
        gcs.append((gc_all, gc_all.T))
        dl_rows.append([jnp.exp(gc_all[j * c + (c - 1 if d == 0 else 0):j * c + (c if d == 0 else 1), :])
                        for j in range(nchunk)])
    for j0 in range(0, nchunk, WY_GROUP_CHUNKS):
        shared = {}
        for j in range(j0, j0 + WY_GROUP_CHUNKS):
            for pr in range(DN_HEADS // 2):
                r0 = j * c
                hs = slice(2 * pr * DN_DH, (2 * pr + 2) * DN_DH)
                bd = lambda x: jnp.concatenate([jnp.where(lane_k < DN_DH, x, zk), jnp.where(lane_k >= DN_DH, x, zk)],
                                               axis=0)
                q2 = qkv[r0:r0 + c, hs]
                k2 = qkv[r0:r0 + c, DN_W + hs.start:DN_W + hs.stop]
                v2 = qkv[r0:r0 + c, 2 * DN_W + hs.start:2 * DN_W + hs.stop]
                bd_k = bd(k2)
                shared[(j, pr)] = dict(q2=q2, bd_k=bd_k, bd_v=bd(v2),
                                       gqt=_dot_nt(jnp.concatenate([k2, q2, eye2], axis=0), bd_k))
        units = [(d, j, pr) for j in range(j0, j0 + WY_GROUP_CHUNKS) for d in range(2) for pr in range(DN_HEADS // 2)]
        pre = []
        for d, j, pr in units:
            gc_all, gc_all_t = gcs[d]
            incl = (ci <= ri) if d == 0 else (ci >= ri)
            strict = (ci < ri) if d == 0 else (ci > ri)
            r0 = j * c
            last = r0 + (c - 1 if d == 0 else 0)
            i0 = d * DN_HEADS + 2 * pr
            col = lambda x, a: x[r0:r0 + c, a:a + 1]
            row = lambda x, a: jnp.concatenate([x[a:a + 1, r0:r0 + c], x[a + 1:a + 2, r0:r0 + c]], axis=1)
            beta_p = jnp.where(first, col(gb, 2 * DN_HEADS + i0), col(gb, 2 * DN_HEADS + i0 + 1))
            gc_p = jnp.where(first, col(gc_all, i0), col(gc_all, i0 + 1))
            gc_row = row(gc_all_t, i0)
            beta_row = row(gb_t, 2 * DN_HEADS + i0)
            g_last = jnp.where(first[0:1], gc_all[last:last + 1, i0:i0 + 1], gc_all[last:last + 1, i0 + 1:i0 + 2])
            e = jnp.exp(jnp.where(incl, gc_p - gc_row, 0.0))
            eg_row = jnp.exp(gc_row)
            pre.append(dict(beta_p=beta_p, e_incl=jnp.where(incl, e, 0.0), e_strict=jnp.where(strict, e, 0.0),
                            beta_row=beta_row, eg_row=eg_row, tail_row=jnp.exp(g_last - gc_row)))
        gqt = [shared[(j, pr)]["gqt"] for d, j, pr in units]
        ns = _unit_tri_inverses_minus_identity([p["beta_p"] * x[:c] * p["e_strict"] for x, p in zip(gqt, pre)])
        tmat = {u_: n + eye_p for u_, n in zip(units, ns)}
        prd = dict(zip(units, pre))
        both = [(j, pr) for j in range(j0, j0 + WY_GROUP_CHUNKS) for pr in range(DN_HEADS // 2)]
        lhs_u = [jnp.concatenate([(tmat[(d, j, pr)] * prd[(d, j, pr)]["beta_row"]).astype(BF16) for d in range(2)], axis=0)
                 for j, pr in both]
        lhs_w = [jnp.concatenate([(tmat[(d, j, pr)] * (prd[(d, j, pr)]["beta_row"] * prd[(d, j, pr)]["eg_row"])
                                   ).astype(BF16) for d in range(2)], axis=0) for j, pr in both]
        u2 = dict(zip(both, [_dot(a, shared[jp]["bd_v"]) for a, jp in zip(lhs_u, both)]))
        w2 = dict(zip(both, [_dot(a, shared[jp]["bd_k"]) for a, jp in zip(lhs_w, both)]))
        for (d, j, pr), p, x in zip(units, pre, gqt):
            r0 = j * c
            i0 = d * DN_HEADS + 2 * pr
            cs = slice(i0 * DN_DH, (i0 + 2) * DN_DH)
            ks = slice(i0 * c, (i0 + 2) * c)
            q2 = shared[(j, pr)]["q2"].astype(F32)
            gcol = gcs[d][0][r0:r0 + c]
            eg = jnp.where(lane_k < DN_DH, jnp.exp(gcol[:, i0:i0 + 1]), jnp.exp(gcol[:, i0 + 1:i0 + 2]))
            u_ref[r0:r0 + c, cs] = u2[(j, pr)][d * c:(d + 1) * c].astype(BF16)
            wq_ref[2 * r0:2 * r0 + c, cs] = w2[(j, pr)][d * c:(d + 1) * c].astype(BF16)
            wq_ref[2 * r0 + c:2 * r0 + 2 * c, cs] = (q2 * eg).astype(BF16)
            kq_ref[3 * r0:3 * r0 + DN_DH, ks] = (x[2 * c:] * p["tail_row"]).astype(BF16)
            kq_ref[3 * r0 + DN_DH:3 * r0 + DN_DH + c, ks] = (x[c:2 * c] * p["e_incl"]).astype(BF16)
    for j in range(nchunk):
        row = jnp.where(lane < DN_HEADS, dl_rows[0][j], dl_rows[1][j])
        dl_ref[8 * j:8 * j + 8, :] = row


def _dn_wy(qkv, gb, nchunk):
    tm = nchunk * CHUNK
    nb = T_ALL // tm
    nch = 2 * DN_HEADS
    return pl.pallas_call(
        functools.partial(_dn_wy_kernel, nchunk=nchunk),
        grid=(nb,),
        in_specs=[pl.BlockSpec((tm, 3 * DN_W), lambda i: (i, 0)),
                  pl.BlockSpec((tm, GP), lambda i: (i, 0))],
        out_specs=[pl.BlockSpec((tm, nch * DN_DH), lambda i: (i, 0)),
                   pl.BlockSpec((2 * tm, nch * DN_DH), lambda i: (i, 0)),
                   pl.BlockSpec((3 * tm, nch * CHUNK), lambda i: (i, 0)),
                   pl.BlockSpec((8 * nchunk, GP), lambda i: (i, 0))],
        out_shape=[jax.ShapeDtypeStruct((T_ALL, nch * DN_DH), BF16),
                   jax.ShapeDtypeStruct((2 * T_ALL, nch * DN_DH), BF16),
                   jax.ShapeDtypeStruct((3 * T_ALL, nch * CHUNK), BF16),
                   jax.ShapeDtypeStruct((8 * T_ALL // CHUNK, GP), F32)],
        compiler_params=_cparams(("arbitrary",)),
        name="dn_wy",
    )(qkv, gb)


def _dn_scan_kernel(u_f_ref, wq_f_ref, kq_f_ref, dl_f_ref, u_b_ref, wq_b_ref, kq_b_ref, dl_b_ref,
                    of_ref, ob_ref, s_scr, *, nchunk):
    i = pl.program_id(0)

    @pl.when(i == 0)
    def _():
        s_scr[...] = jnp.zeros_like(s_scr)

    c = CHUNK
    states = [s_scr[idx] for idx in range(2 * DN_HEADS)]
    zero = jnp.zeros((c, DN_DH), BF16)
    refs = ((u_f_ref, wq_f_ref, kq_f_ref, dl_f_ref, of_ref), (u_b_ref, wq_b_ref, kq_b_ref, dl_b_ref, ob_ref))
    chains = [(d, h) for d in range(2) for h in range(DN_HEADS)]
    pairs = [(d, p) for d in range(2) for p in range(DN_HEADS // 2)]
    for step in range(nchunk):
        blk = lambda d: step if d == 0 else nchunk - 1 - step
        hs = lambda h: slice(h * DN_DH, (h + 1) * DN_DH)
        r1 = [_dot(refs[d][1][2 * blk(d) * c:2 * (blk(d) + 1) * c, hs(h)], states[d * DN_HEADS + h].astype(BF16))
              for d, h in chains]
        v_new = [(refs[d][0][blk(d) * c:(blk(d) + 1) * c, hs(h)].astype(F32) - r[:c]).astype(BF16)
                 for (d, h), r in zip(chains, r1)]
        r2 = [_dot(refs[d][2][3 * blk(d) * c:3 * (blk(d) + 1) * c, p * 2 * c:(p + 1) * 2 * c],
                   jnp.concatenate([jnp.concatenate([v_new[d * DN_HEADS + 2 * p], zero], axis=1),
                                    jnp.concatenate([zero, v_new[d * DN_HEADS + 2 * p + 1]], axis=1)], axis=0))
              for d, p in pairs]
        for d, h in chains:
            idx = d * DN_HEADS + h
            r = r2[d * (DN_HEADS // 2) + h // 2]
            ts = slice((h % 2) * DN_DH, (h % 2 + 1) * DN_DH)
            dl_row = refs[d][3][8 * blk(d):8 * blk(d) + 1, :]
            states[idx] = states[idx] * dl_row[:, idx:idx + 1] + r[:DN_DH, ts]
            refs[d][4][blk(d) * c:(blk(d) + 1) * c, hs(h)] = (r1[idx][c:] + r[DN_DH:, ts]).astype(BF16)
    for idx in range(2 * DN_HEADS):
        s_scr[idx] = states[idx]


def _dn_scan(u, wq, kq, dl, nchunk):
    tm = nchunk * CHUNK
    nb = T_ALL // tm
    nlat = T_LAT // tm
    fwd = lambda i: jnp.where(i == 0, nlat, i - 1)
    bwd = lambda i: nb - 1 - i
    specs = lambda f, col: [pl.BlockSpec((tm, DN_W), lambda i: (f(i), col)),
                            pl.BlockSpec((2 * tm, DN_W), lambda i: (f(i), col)),
                            pl.BlockSpec((3 * tm, DN_HEADS * CHUNK), lambda i: (f(i), col)),
                            pl.BlockSpec((8 * nchunk, GP), lambda i: (f(i), 0))]
    return pl.pallas_call(
        functools.partial(_dn_scan_kernel, nchunk=nchunk),
        grid=(nb,),
        in_specs=specs(fwd, 0) + specs(bwd, 1),
        out_specs=[pl.BlockSpec((tm, DN_W), lambda i: (fwd(i), 0)),
                   pl.BlockSpec((tm, DN_W), lambda i: (bwd(i), 0))],
        out_shape=[jax.ShapeDtypeStruct((T_ALL, DN_W), BF16),
                   jax.ShapeDtypeStruct((T_ALL, DN_W), BF16)],
        scratch_shapes=[pltpu.VMEM((2 * DN_HEADS, DN_DH, DN_DH), F32)],
        compiler_params=_cparams(("arbitrary",)),
        name="dn_scan",
    )(u, wq, kq, dl, u, wq, kq, dl)


def _dft_cos_sin(n):
    k = np.arange(n)
    ang = 2.0 * np.pi * ((k[:, None] * k[None, :]) % n) / n
    return np.cos(ang), np.sin(ang)


def _fft_consts():
    n = FFT_N
    c, s = _dft_cos_sin(n)
    cs_ch = np.concatenate([c, s], axis=1)
    w1 = np.block([[c, -s], [-s, -c]])
    k2 = np.arange(n)[None, :, None]
    t2 = np.arange(n)[:, None, None]
    ang = 2.0 * np.pi * ((k2 * t2) % (n * n)) / (n * n)
    scale = 1.0 / math.sqrt(T_LAT * FN_DG)
    cc, sc = _dft_cos_sin(T_CTX)
    scale_c = 1.0 / math.sqrt(T_CTX * FN_DG)
    f32 = lambda a: jnp.asarray(a, F32)
    bf = lambda a: f32(a).astype(BF16)
    return dict(cs_ch=bf(cs_ch), w1=bf(w1), twc=f32(np.cos(ang)), tws=f32(np.sin(ang)),
                c2=bf(c * scale), s2=bf(s * scale), cc=bf(cc * scale_c), sc=bf(sc * scale_c))


def _fft1_kernel(u_ref, cs_ref, w1_ref, twc_ref, tws_ref, y_ref, *, n_t2):
    n = FFT_N
    cs = cs_ref[...]
    w1 = w1_ref[...]
    for t in range(n_t2):
        twc = twc_ref[t]
        tws = tws_ref[t]
        for g in range(0, FN_GROUPS, 2):
            c0 = (t * FN_GROUPS + g) * FN_DG
            ab = [_dot(u_ref[:, c0 + s * FN_DG:c0 + (s + 1) * FN_DG], cs) for s in range(2)]
            rhs = jnp.concatenate([jnp.concatenate([ab[0][:, :FN_DG], ab[1][:, :FN_DG]], axis=1),
                                   jnp.concatenate([ab[0][:, FN_DG:], ab[1][:, FN_DG:]], axis=1)], axis=0)
            y = _dot(w1, rhs.astype(BF16))
            yr, yi = y[:n], y[n:]
            y_ref[0:n, c0:c0 + 2 * FN_DG] = (yr * twc + yi * tws).astype(BF16)
            y_ref[n:2 * n, c0:c0 + 2 * FN_DG] = (yi * twc - yr * tws).astype(BF16)


def _fft2_kernel(y_ref, c2_ref, s2_ref, o_ref, *, n_k2):
    c2 = c2_ref[...]
    s2 = s2_ref[...]
    for j in range(n_k2):
        o_ref[:, j * FN_W:(j + 1) * FN_W] = (_dot(c2, y_ref[0, j]) + _dot(s2, y_ref[1, j])).astype(BF16)


def _fft_ctx_kernel(u_ref, cs_ref, cc_ref, sc_ref, o_in_ref, o_ref):
    del o_in_ref
    for g in range(FN_GROUPS):
        ab = _dot(u_ref[:, g * FN_DG:(g + 1) * FN_DG], cs_ref[...])
        a = ab[:, :FN_DG].astype(BF16)
        b = ab[:, FN_DG:].astype(BF16)
        o_ref[:, g * FN_DG:(g + 1) * FN_DG] = (_dot(cc_ref[...], a) - _dot(sc_ref[...], b)).astype(BF16)


def _fnet(u, fc, with_ctx):
    n = FFT_N
    row_w = n * FN_W
    n_t2 = 8
    tc = n_t2 * FN_W
    y = pl.pallas_call(
        functools.partial(_fft1_kernel, n_t2=n_t2),
        grid=(row_w // tc,),
        in_specs=[pl.BlockSpec((n, tc), lambda j: (0, j)),
                  pl.BlockSpec((n, 2 * n), lambda j: (0, 0)),
                  pl.BlockSpec((2 * n, 2 * n), lambda j: (0, 0)),
                  pl.BlockSpec((n_t2, n, 1), lambda j: (j, 0, 0)),
                  pl.BlockSpec((n_t2, n, 1), lambda j: (j, 0, 0))],
        out_specs=pl.BlockSpec((2 * n, tc), lambda j: (0, j)),
        out_shape=jax.ShapeDtypeStruct((2 * n, row_w), BF16),
        compiler_params=_cparams(("arbitrary",)),
        name="fnet_stage1",
    )(u.reshape(T_ALL // n, row_w), fc["cs_ch"], fc["w1"], fc["twc"], fc["tws"])
    n_k2 = 8
    o = pl.pallas_call(
        functools.partial(_fft2_kernel, n_k2=n_k2),
        grid=(n // n_k2,),
        in_specs=[pl.BlockSpec((2, n_k2, n, FN_W), lambda j: (0, j, 0, 0)),
                  pl.BlockSpec((n, n), lambda j: (0, 0)),
                  pl.BlockSpec((n, n), lambda j: (0, 0))],
        out_specs=pl.BlockSpec((n, n_k2 * FN_W), lambda j: (0, j)),
        out_shape=jax.ShapeDtypeStruct(((T_ALL if with_ctx else T_LAT) // n, row_w), BF16),
        compiler_params=_cparams(("arbitrary",)),
        name="fnet_stage2",
    )(y.reshape(2, n, n, FN_W), fc["c2"], fc["s2"])
    if not with_ctx:
        return o.reshape(T_LAT, FN_W)
    o = o.reshape(T_ALL, FN_W)
    cb = T_LAT // T_CTX
    return pl.pallas_call(
        _fft_ctx_kernel,
        grid=(1,),
        in_specs=[pl.BlockSpec((T_CTX, FN_W), lambda j: (cb, 0)),
                  pl.BlockSpec((n, 2 * n), lambda j: (0, 0)),
                  pl.BlockSpec((T_CTX, T_CTX), lambda j: (0, 0)),
                  pl.BlockSpec((T_CTX, T_CTX), lambda j: (0, 0)),
                  pl.BlockSpec(memory_space=pl.ANY)],
        out_specs=pl.BlockSpec((T_CTX, FN_W), lambda j: (cb, 0)),
        out_shape=jax.ShapeDtypeStruct((T_ALL, FN_W), BF16),
        input_output_aliases={4: 0},
        compiler_params=_cparams(("arbitrary",)),
        name="fnet_ctx",
    )(u, fc["cs_ch"], fc["cc"], fc["sc"], o)


def _merge_kernel(*refs, tm, has_ctx, split):
    if split:
        (x_ref, ctx_ref, mod_ref, n1_ref, ona_ref, of_ref, ob_ref, z_ref, ofn_ref, dnn_ref,
         wg_ref, wna_ref, wdn_ref, wfn_ref, wout_ref, o_ref, tail_scr) = refs
    else:
        (x_ref, mod_ref, n1_ref, ona_ref, of_ref, ob_ref, z_ref, ofn_ref, dnn_ref,
         wg_ref, wna_ref, wdn_ref, wfn_ref, wout_ref, o_ref) = refs
        ctx_ref = tail_scr = None
    i = pl.program_id(0)
    tail_is_ctx = i == pl.num_programs(0) - 1
    x = _token_tile(x_ref, ctx_ref, tail_scr, tail_is_ctx)
    h = _modnorm_tile(x, n1_ref[...], mod_ref, 0, 1, tail_is_ctx, has_ctx).astype(BF16)
    o = of_ref[...].astype(F32) + ob_ref[...].astype(F32)
    z = z_ref[...].astype(F32)
    parts = []
    for hd in range(DN_HEADS):
        sl = slice(hd * DN_DH, (hd + 1) * DN_DH)
        oh = o[:, sl]
        oh = oh * lax.rsqrt(jnp.mean(oh * oh, axis=-1, keepdims=True) + EPS) * dnn_ref[...]
        parts.append(oh * _silu(z[:, sl]))
    odn = jnp.concatenate(parts, axis=-1).astype(BF16)
    branches = ((ona_ref[...], wna_ref), (odn, wdn_ref), (ofn_ref[...], wfn_ref))
    logits = [_dot(h, wg_ref[:, b * D:(b + 1) * D]) for b in range(N_BRANCH)]
    proj = [_dot(br, w_ref[...]) for br, w_ref in branches]
    y = jax.nn.sigmoid(logits[0]) * proj[0]
    for b in range(1, N_BRANCH):
        y = y + jax.nn.sigmoid(logits[b]) * proj[b]
    y = _dot(y.astype(BF16), wout_ref[...])
    o_ref[...] = _gated_residual(x, y, mod_ref, 2, tail_is_ctx, has_ctx)


def _merge(x, ctx, mod, norm1, p, o_na, o_f, o_b, o_fn, dn_norm, w_gate, layer, w_na_o, w_dn_o, w_fn, w_out, has_ctx):
    rows, tm = (T_ALL, TM_MERGE_ALL) if has_ctx else (T_LAT, TM_MERGE_LAT)
    split = ctx is not None
    row = lambda w: pl.BlockSpec((tm, w), lambda i: (i, 0))
    full = lambda a: pl.BlockSpec(a.shape, lambda i: (0, 0), pipeline_mode=pl.Buffered(1))
    ctx_specs = [pl.BlockSpec((T_CTX, D), lambda i: (0, 0))] if split else []
    return pl.pallas_call(
        functools.partial(_merge_kernel, tm=tm, has_ctx=has_ctx, split=split),
        grid=(rows // tm,),
        in_specs=[row(D)] + ctx_specs + [
                  pl.BlockSpec((8, 6 * D), lambda i: (0, 0)), pl.BlockSpec((1, D), lambda i: (0, 0)),
                  row(NA_W), row(DN_W), row(DN_W),
                  pl.BlockSpec((tm, DN_W), lambda i: (i, P_Z)),
                  row(FN_W), pl.BlockSpec((1, DN_DH), lambda i: (0, 0)),
                  pl.BlockSpec((None,) + w_gate.shape[1:], lambda i: (layer, 0, 0), pipeline_mode=pl.Buffered(1)),
                  full(w_na_o), full(w_dn_o), full(w_fn), full(w_out)],
        out_specs=row(D),
        out_shape=jax.ShapeDtypeStruct((rows, D), F32),
        scratch_shapes=[pltpu.VMEM((T_CTX, D), F32)] if split else [],
        compiler_params=_cparams(("arbitrary",)),
        name="merge",
    )(*([x, ctx] if split else [x]), mod, norm1.reshape(1, D), o_na, o_f, o_b, p, o_fn, dn_norm.reshape(1, DN_DH),
      w_gate, w_na_o, w_dn_o, w_fn, w_out)


def _mlp_kernel(x_ref, xnext_ref, mod_ref, n_ref, w1_ref, w2_ref, nf_ref, o_ref, h_scr, acc_scr, *,
                tm, has_ctx, final):
    i = pl.program_id(0)
    j = pl.program_id(1)
    nt = pl.num_programs(0)
    nj = pl.num_programs(1)
    slot = i % 2

    @pl.when(jnp.logical_and(i == 0, j == 0))
    def _():
        h_scr[0] = _modnorm_tile(x_ref[...], n_ref[...], mod_ref, 3, 4, nt == 1, has_ctx).astype(BF16)

    def hidden_step(first):
        a = jnp.maximum(_dot(h_scr[slot], w1_ref[...]), 0.0)
        upd = _dot((a * a).astype(BF16), w2_ref[...])
        if first:
            acc_scr[...] = upd
        else:
            acc_scr[...] += upd

    @pl.when(j == 0)
    def _():
        hidden_step(True)

    @pl.when(jnp.logical_and(j > 0, j < nj - 1))
    def _():
        hidden_step(False)

    @pl.when(j == nj - 1)
    def _():
        h_scr[1 - slot] = _modnorm_tile(xnext_ref[...], n_ref[...], mod_ref, 3, 4, i + 1 == nt - 1, has_ctx).astype(BF16)
        hidden_step(False)
        xn = _gated_residual(x_ref[...], acc_scr[...], mod_ref, 5, i == nt - 1, has_ctx)
        if final:
            xn = xn * lax.rsqrt(jnp.mean(xn * xn, axis=-1, keepdims=True) + EPS) * nf_ref[...]
        o_ref[...] = xn


def _mlp(xs, mod, norm, w1, w2, norm_f, has_ctx, final):
    rows, tm = (T_ALL, TM_ALL) if has_ctx else (T_LAT, TM_LAT)
    th = 1024
    return pl.pallas_call(
        functools.partial(_mlp_kernel, tm=tm, has_ctx=has_ctx, final=final),
        grid=(rows // tm, HID // th),
        in_specs=[pl.BlockSpec((tm, D), lambda i, j: (i, 0)),
                  pl.BlockSpec((tm, D), lambda i, j: (jnp.minimum(i + 1, rows // tm - 1), 0)),
                  pl.BlockSpec((8, 6 * D), lambda i, j: (0, 0)),
                  pl.BlockSpec((1, D), lambda i, j: (0, 0)),
                  pl.BlockSpec((D, th), lambda i, j: (0, j)),
                  pl.BlockSpec((th, D), lambda i, j: (j, 0)),
                  pl.BlockSpec((1, D), lambda i, j: (0, 0))],
        out_specs=pl.BlockSpec((tm, D), lambda i, j: (i, 0)),
        out_shape=jax.ShapeDtypeStruct((rows, D), F32),
        scratch_shapes=[pltpu.VMEM((2, tm, D), BF16), pltpu.VMEM((tm, D), F32)],
        compiler_params=_cparams(("arbitrary", "arbitrary")),
        name="mlp",
    )(xs, xs, mod, norm.reshape(1, D), w1, w2, norm_f.reshape(1, D))


IN_W = 2 * NA_W + 2 * DN_W + 4 * DN_HEADS + NA_W + 2 * DN_W + FN_W + N_BRANCH * D


W_PREP_BLK = 512
_W_SRC = [2 * NA_W + 2 * DN_W + 4 * DN_HEADS, 0, NA_W, 3 * NA_W + 2 * DN_W + 4 * DN_HEADS, 2 * NA_W, 2 * NA_W + DN_W,
          3 * NA_W + 3 * DN_W + 4 * DN_HEADS, 3 * NA_W + 4 * DN_W + 4 * DN_HEADS] + [
          IN_W - N_BRANCH * D + i * W_PREP_BLK for i in range(N_BRANCH * D // W_PREP_BLK)]
_W_AB = 2 * NA_W + 2 * DN_W


def _w_prep_kernel(w_ref, ab_ref, wcat_ref, wgate_ref, wg_ref):
    c = pl.program_id(1)
    ncat = P_W // W_PREP_BLK
    blk = w_ref[0].T.astype(BF16)

    @pl.when(c < ncat)
    def _():
        wcat_ref[...] = blk

    @pl.when(c >= ncat)
    def _():
        wgate_ref[...] = blk

    @pl.when(c == 0)
    def _():
        ab = jnp.concatenate([ab_ref[0], jnp.zeros((GP - ab_ref.shape[1], D), F32)], axis=0)
        wg_ref[...] = ab.T.astype(BF16)


def _split_in_weights(w_in):
    ncat = P_W // W_PREP_BLK
    nblk = len(_W_SRC)

    def src(c):
        off = 0
        for k, o in enumerate(_W_SRC):
            off = off + jnp.where(c == k, o // 16, 0)
        return off * 16

    return pl.pallas_call(
        _w_prep_kernel,
        grid=(DEPTH, nblk),
        in_specs=[pl.BlockSpec((pl.Element(1), pl.Element(W_PREP_BLK), pl.Element(D)), lambda l, c: (l, src(c), 0)),
                  pl.BlockSpec((pl.Element(1), pl.Element(4 * DN_HEADS), pl.Element(D)), lambda l, c: (l, _W_AB, 0))],
        out_specs=[pl.BlockSpec((None, D, W_PREP_BLK), lambda l, c: (l, 0, jnp.minimum(c, ncat - 1))),
                   pl.BlockSpec((None, D, W_PREP_BLK), lambda l, c: (l, 0, jnp.maximum(c - ncat, 0))),
                   pl.BlockSpec((None, D, GP), lambda l, c: (l, 0, 0))],
        out_shape=[jax.ShapeDtypeStruct((DEPTH, D, P_W), BF16),
                   jax.ShapeDtypeStruct((DEPTH, D, N_BRANCH * D), BF16),
                   jax.ShapeDtypeStruct((DEPTH, D, GP), BF16)],
        compiler_params=_cparams(("arbitrary", "arbitrary")),
        name="w_prep",
    )(jnp.swapaxes(w_in, 1, 2), jnp.swapaxes(w_in, 1, 2))


def kernel(x, c, ctx, c_ctx, w_ada, b_ada, norm1, w_in, conv_w, a_log, dt_bias, dn_norm, rpb,
           w_na_o, w_dn_o, w_fn, w_out, norm2, w_mlp1, w_mlp2, norm_f):
    xs, xc = x[0], ctx[0]
    cc = jnp.concatenate([c, c_ctx[None, :], jnp.zeros((6, D), F32)], axis=0)
    mods = _ada(cc, w_ada, b_ada)
    rope = _rope_tables()
    fc = _fft_consts()
    na_tab = _na_tables(rpb)
    wcat, w_gate, wg = _split_in_weights(w_in)
    for l in range(DEPTH):
        has_ctx = l < DEPTH - 1
        final = l == DEPTH - 1
        u, p, g = _in_proj(xs, xc, mods[l], norm1[l], wcat, wg, l)
        o_na = _na(p, na_tab, l, has_ctx)
        qkv, gb = _dn_prep(p, g, conv_w[l], a_log[l], dt_bias[l], rope)
        o_f, o_b = _dn_scan(*_dn_wy(qkv, gb, DN_BLK_CHUNKS), DN_BLK_CHUNKS)
        o_fn = _fnet(u, fc, has_ctx)
        xs, xc = _merge(xs, xc, mods[l], norm1[l], p, o_na, o_f, o_b, o_fn, dn_norm[l], w_gate, l,
                    w_na_o[l].astype(BF16), w_dn_o[l].astype(BF16), w_fn[l].astype(BF16), w_out[l].astype(BF16),
                    has_ctx), None
        xs = _mlp(xs, mods[l], norm2[l], w_mlp1[l].astype(BF16), w_mlp2[l].astype(BF16), norm_f, has_ctx, final)
    return xs[None]
```

```python
import functools
import math

import numpy as np
import jax
import jax.numpy as jnp
from jax import lax
from jax.experimental import pallas as pl
from jax.experimental.pallas import tpu as pltpu

F32 = jnp.float32
BF16 = jnp.bfloat16

D = 1024
T_LAT = 16384
T_CTX = 256
T_ALL = T_LAT + T_CTX
DEPTH = 2
GRID_W = 64
GRID_H = T_LAT // GRID_W
NA_HEADS = 8
NA_DH = 64
NA_WIN_H = 8
NA_WIN_W = 16
NA_W = NA_HEADS * NA_DH
DN_HEADS = 4
DN_DH = 128
DN_W = DN_HEADS * DN_DH
DN_CONV = 5
CHUNK = 64
FN_GROUPS = 4
FN_DG = 128
FN_W = FN_GROUPS * FN_DG
N_BRANCH = 3
HID = 4 * D
ROPE_BASE = 10000.0
EPS = 1e-6
NEG = -1e30

P_W = 3 * NA_W + 4 * DN_W + FN_W
P_NAQ, P_NAK, P_NAV = 0, 1, 2
P_DN = 1
P_Z = 6
TN_IN = P_W // 2
GP = 128

TM_ALL = 1280
TM_LAT = 1024
TM_MERGE_ALL = 640
TM_MERGE_LAT = 512
NA_BLK = 4 * GRID_W
NA_HEAD_GROUP = 2
NA_VARIANTS = 4
FFT_N = 128
DN_BLK_CHUNKS = 4
WY_GROUP_CHUNKS = 4
VMEM_LIMIT = 56 * 1024 * 1024


def _cparams(sem):
    return pltpu.CompilerParams(dimension_semantics=sem, vmem_limit_bytes=VMEM_LIMIT)


def _dot(a, b):
    return jnp.dot(a, b, preferred_element_type=F32)


def _dot_nt(a, b):
    return lax.dot_general(a, b, (((1,), (1,)), ((), ())), preferred_element_type=F32)


def _dot_tn(a, b):
    return lax.dot_general(a, b, (((0,), (0,)), ((), ())), preferred_element_type=F32)


def _silu(x):
    return x * jax.nn.sigmoid(x)


def _ada_kernel(c_ref, w_ref, b_ref, o_ref):
    s = _silu(c_ref[...])
    s_hi = s.astype(BF16)
    s_lo = (s - s_hi.astype(F32)).astype(BF16)
    w = w_ref[0].astype(BF16)
    o_ref[0] = _dot(s_hi, w) + _dot(s_lo, w) + b_ref[0]


def _ada(cc, w_ada, b_ada):
    tn = 1536
    return pl.pallas_call(
        _ada_kernel,
        grid=(DEPTH, 6 * D // tn),
        in_specs=[pl.BlockSpec((8, D), lambda l, j: (0, 0)),
                  pl.BlockSpec((1, D, tn), lambda l, j: (l, 0, j)),
                  pl.BlockSpec((1, 1, tn), lambda l, j: (l, 0, j))],
        out_specs=pl.BlockSpec((1, 8, tn), lambda l, j: (l, 0, j)),
        out_shape=jax.ShapeDtypeStruct((DEPTH, 8, 6 * D), F32),
        compiler_params=_cparams(("arbitrary", "arbitrary")),
        name="ada_mod",
    )(cc, w_ada, b_ada.reshape(DEPTH, 1, 6 * D))


def _row_slices(tm, has_ctx):
    return [slice(0, tm)] if not has_ctx else [slice(0, tm - T_CTX), slice(tm - T_CTX, tm)]


def _mod_vecs(mod_ref, k, tail_is_ctx, has_ctx):
    lat = mod_ref[0:1, k * D:(k + 1) * D]
    if not has_ctx:
        return [lat]
    return [lat, jnp.where(tail_is_ctx, mod_ref[1:2, k * D:(k + 1) * D], lat)]


def _modnorm(x, g, shift, scale):
    y = x * lax.rsqrt(jnp.mean(x * x, axis=-1, keepdims=True) + EPS)
    return y * (g * (1.0 + scale)) + shift


def _modnorm_tile(x, g, mod_ref, k_shift, k_scale, tail_is_ctx, has_ctx):
    tm = x.shape[0]
    parts = [_modnorm(x[sl], g, sh, sc) for sl, sh, sc in zip(_row_slices(tm, has_ctx),
                                                            _mod_vecs(mod_ref, k_shift, tail_is_ctx, has_ctx),
                                                            _mod_vecs(mod_ref, k_scale, tail_is_ctx, has_ctx))]
    return parts[0] if len(parts) == 1 else jnp.concatenate(parts, axis=0)


def _gated_residual(x, y, mod_ref, k_gate, tail_is_ctx, has_ctx):
    tm = x.shape[0]
    parts = [x[sl] + gt * y[sl] for sl, gt in zip(_row_slices(tm, has_ctx), _mod_vecs(mod_ref, k_gate, tail_is_ctx, has_ctx))]
    return parts[0] if len(parts) == 1 else jnp.concatenate(parts, axis=0)


def _token_tile(x_ref, ctx_ref, tail_scr, is_last):
    if ctx_ref is None:
        return x_ref[...]
    top = x_ref.shape[0] - T_CTX

    @pl.when(jnp.logical_not(is_last))
    def _():
        tail_scr[...] = x_ref[top:, :]

    @pl.when(is_last)
    def _():
        tail_scr[...] = ctx_ref[...]

    return jnp.concatenate([x_ref[0:top, :], tail_scr[...]], axis=0)


def _in_proj_kernel(*refs, tm, split):
    if split:
        x_ref, ctx_ref, mod_ref, n_ref, w_ref, wg_ref, u_ref, p_ref, g_ref, h_scr, tail_scr = refs
    else:
        x_ref, mod_ref, n_ref, w_ref, wg_ref, u_ref, p_ref, g_ref, h_scr = refs
        ctx_ref = tail_scr = None
    i = pl.program_id(0)
    j = pl.program_id(1)

    @pl.when(j == 0)
    def _():
        is_last = i == pl.num_programs(0) - 1
        x = _token_tile(x_ref, ctx_ref, tail_scr, is_last)
        h = _modnorm_tile(x, n_ref[...], mod_ref, 0, 1, is_last, True).astype(BF16)
        h_scr[...] = h
        g_ref[...] = _dot(h, wg_ref[...])

    p = _dot(h_scr[...], w_ref[...]).astype(BF16)
    p_ref[...] = p

    @pl.when(j == pl.num_programs(1) - 1)
    def _():
        u_ref[...] = p[:, TN_IN - FN_W:]


def _in_proj(x, ctx, mod, norm, wcat, wg, layer):
    tm = TM_ALL
    nj = P_W // TN_IN
    split = ctx is not None
    ctx_specs = [pl.BlockSpec((T_CTX, D), lambda i, j: (0, 0))] if split else []
    return pl.pallas_call(
        functools.partial(_in_proj_kernel, tm=tm, split=split),
        grid=(T_ALL // tm, nj),
        in_specs=[pl.BlockSpec((tm, D), lambda i, j: (i, 0))] + ctx_specs + [
                  pl.BlockSpec((8, 6 * D), lambda i, j: (0, 0)),
                  pl.BlockSpec((1, D), lambda i, j: (0, 0)),
                  pl.BlockSpec((None, D, TN_IN), lambda i, j: (layer, 0, j)),
                  pl.BlockSpec((None, D, GP), lambda i, j: (layer, 0, 0))],
        out_specs=[pl.BlockSpec((tm, FN_W), lambda i, j: (i, 0)),
                   pl.BlockSpec((tm, TN_IN), lambda i, j: (i, j)),
                   pl.BlockSpec((tm, GP), lambda i, j: (i, 0))],
        out_shape=[jax.ShapeDtypeStruct((T_ALL, FN_W), BF16),
                   jax.ShapeDtypeStruct((T_ALL, P_W), BF16),
                   jax.ShapeDtypeStruct((T_ALL, GP), F32)],
        scratch_shapes=[pltpu.VMEM((tm, D), BF16)] + ([pltpu.VMEM((T_CTX, D), F32)] if split else []),
        compiler_params=_cparams(("arbitrary", "arbitrary")),
        name="in_proj",
    )(*([x, ctx] if split else [x]), mod, norm.reshape(1, D), wcat, wg)


def _na_tables(rpb):
    qr = np.arange(4)[:, None, None, None]
    qc = np.arange(GRID_W)[None, :, None, None]
    kk = np.arange(12)[None, None, :, None]
    kc = np.arange(GRID_W)[None, None, None, :]
    valid = []
    for b in (0, 1, GRID_H // 4 - 1):
        r = 4 * b + qr
        kr = 4 * (b - 1) + kk
        rs = np.clip(r - NA_WIN_H // 2, 0, GRID_H - NA_WIN_H)
        cs = np.clip(qc - NA_WIN_W // 2, 0, GRID_W - NA_WIN_W)
        ok = (kr >= 0) & (kr < GRID_H) & (kr >= rs) & (kr < rs + NA_WIN_H) & (kc >= cs) & (kc < cs + NA_WIN_W)
        valid.append(np.broadcast_to(ok, (4, GRID_W, 12, GRID_W)).reshape(NA_BLK, 12 * GRID_W))
    valid.append(np.zeros_like(valid[0]))
    valid = np.stack(valid)
    col = np.arange(GRID_W)
    onehot = ((col[None, None, :] - col[None, :, None] + NA_WIN_W - 1)
              == np.arange(2 * NA_WIN_W - 1)[:, None, None]).astype(np.float32)
    tcol = jnp.einsum('lhrd,dqk->lhrqk', rpb, onehot, precision=lax.Precision.HIGHEST)
    tcol = tcol.reshape(DEPTH * NA_HEADS, 2 * NA_WIN_H - 1, GRID_W, GRID_W)

    def assemble(tcol_ref, valid_ref, tab_ref):
        bias = jnp.concatenate([jnp.concatenate([tcol_ref[k - q + NA_WIN_H - 5] for k in range(12)], axis=1)
                                for q in range(4)], axis=0)
        for var in range(NA_VARIANTS):
            tab_ref[var] = jnp.where(valid_ref[var] > 0, bias, NEG).astype(BF16)

    return pl.pallas_call(
        assemble,
        grid=(DEPTH * NA_HEADS,),
        in_specs=[pl.BlockSpec((None, 2 * NA_WIN_H - 1, GRID_W, GRID_W), lambda i: (i, 0, 0, 0)),
                  pl.BlockSpec((NA_VARIANTS, NA_BLK, 12 * GRID_W), lambda i: (0, 0, 0))],
        out_specs=pl.BlockSpec((NA_VARIANTS, None, NA_BLK, 12 * GRID_W), lambda i: (i // NA_HEADS, i % NA_HEADS, 0, 0)),
        out_shape=jax.ShapeDtypeStruct((DEPTH * NA_VARIANTS, NA_HEADS, NA_BLK, 12 * GRID_W), BF16),
        compiler_params=_cparams(("arbitrary",)),
        name="na_tables",
    )(tcol, jnp.asarray(valid, F32))


def _na_kernel(q_ref, k0_ref, k1_ref, k2_ref, v0_ref, v1_ref, v2_ref, kc_ref, vc_ref, tab_ref, o_ref):
    q = q_ref[...]
    kl = jnp.concatenate([k0_ref[...], k1_ref[...], k2_ref[...]], axis=0)
    vl = jnp.concatenate([v0_ref[...], v1_ref[...], v2_ref[...]], axis=0)
    kc = kc_ref[...]
    vc = vc_ref[...]
    lane_l = lax.broadcasted_iota(jnp.int32, (vl.shape[0], 2 * NA_DH), 1)
    lane_c = lax.broadcasted_iota(jnp.int32, (vc.shape[0], 2 * NA_DH), 1)
    one = jnp.ones((), BF16)
    outs = []
    for h0 in range(0, NA_HEADS, NA_HEAD_GROUP):
        heads = range(h0, h0 + NA_HEAD_GROUP)
        sls = [slice(h * NA_DH, (h + 1) * NA_DH) for h in heads]
        pls = [slice((h // 2) * 2 * NA_DH, (h // 2 + 1) * 2 * NA_DH) for h in heads]
        own = [(lambda ln, h=h: (ln < NA_DH) if h % 2 == 0 else (ln >= NA_DH)) for h in heads]
        qs = [q[:, sl] * (NA_DH ** -0.5) for sl in sls]
        z_loc = [_dot_nt(qh, kl[:, sl]).astype(BF16) + tab_ref[0, h] for h, qh, sl in zip(heads, qs, sls)]
        z_ctx = [_dot_nt(qh, kc[:, sl]).astype(BF16) for qh, sl in zip(qs, sls)]
        m = [jnp.maximum(jnp.max(a, axis=-1, keepdims=True), jnp.max(b, axis=-1, keepdims=True))
             for a, b in zip(z_loc, z_ctx)]
        p_loc = [jnp.exp(a - mm) for a, mm in zip(z_loc, m)]
        p_ctx = [jnp.exp(b - mm) for b, mm in zip(z_ctx, m)]
        oa = [_dot(a, jnp.where(f(lane_l), vl[:, ps], one)) + _dot(b, jnp.where(f(lane_c), vc[:, ps], one))
              for a, b, ps, f in zip(p_loc, p_ctx, pls, own)]
        outs += [(x[:, :NA_DH] / x[:, NA_DH:NA_DH + 1]) if h % 2 == 0 else (x[:, NA_DH:] / x[:, 0:1])
                 for h, x in zip(heads, oa)]
    o_ref[...] = jnp.concatenate(outs, axis=-1).astype(BF16)


def _na(p, tab, layer, with_ctx):
    nlat = T_LAT // NA_BLK
    nb = nlat + (1 if with_ctx else 0)
    ctx_blk = T_LAT // NA_BLK

    def kidx(b, part):
        return jnp.clip(jnp.minimum(b, nlat - 1) - 1 + part, 0, nlat - 1)

    def variant(b):
        return layer * NA_VARIANTS + jnp.where(b == 0, 0, jnp.where(b == nlat - 1, 2, jnp.where(b >= nlat, 3, 1)))

    blk = (NA_BLK, NA_W)
    in_specs = [pl.BlockSpec(blk, lambda b: (b, P_NAQ))]
    in_specs += [pl.BlockSpec(blk, functools.partial(lambda b, part: (kidx(b, part), P_NAK), part=part))
                 for part in range(3)]
    in_specs += [pl.BlockSpec(blk, functools.partial(lambda b, part: (kidx(b, part), P_NAV), part=part))
                 for part in range(3)]
    in_specs += [pl.BlockSpec(blk, lambda b: (ctx_blk, P_NAK)),
                 pl.BlockSpec(blk, lambda b: (ctx_blk, P_NAV)),
                 pl.BlockSpec((1, NA_HEADS, NA_BLK, 3 * NA_BLK), lambda b: (variant(b), 0, 0, 0))]
    return pl.pallas_call(
        _na_kernel,
        grid=(nb,),
        in_specs=in_specs,
        out_specs=pl.BlockSpec(blk, lambda b: (b, 0)),
        out_shape=jax.ShapeDtypeStruct((nb * NA_BLK, NA_W), BF16),
        compiler_params=_cparams(("arbitrary",)),
        name="na_attn",
    )(p, p, p, p, p, p, p, p, p, tab)


def _rope_tables():
    half = DN_DH // 2
    inv = 1.0 / (ROPE_BASE ** (np.arange(0, half, 2, dtype=np.float64) / half))
    inv = np.concatenate([inv, inv])
    low = np.arange(half) < half // 2

    def tabs(n):
        ang = np.arange(n, dtype=np.float64)[:, None] * inv[None, :]
        return np.stack([np.cos(ang), np.where(low, -np.sin(ang), 0.0), np.where(low, 0.0, np.sin(ang))])

    rows = tabs(GRID_H).reshape(3, GRID_H // 4, 4, half)
    return jnp.asarray(rows, F32), jnp.asarray(tabs(GRID_W), F32)


def _dn_prep_kernel(x_ref, prev_ref, next_ref, g_ref, cw_ref, alog_ref, dtb_ref, rt_ref, ct_ref,
                    qkv_ref, gb_ref, *, tm):
    b = pl.program_id(0)
    nlat = T_LAT // tm
    first = jnp.logical_or(b == 0, b == nlat)
    last = b >= nlat - 1
    is_ctx = b >= nlat
    cur = x_ref[...].astype(F32)
    prev = jnp.where(first, 0.0, prev_ref[...][14:16].astype(F32))
    nxt = jnp.where(last, 0.0, next_ref[...][0:2].astype(F32))
    ext = jnp.concatenate([prev, cur, nxt], axis=0)
    cw = cw_ref[...]
    y = ext[0:tm] * cw[0:1]
    for k in range(1, DN_CONV):
        y = y + ext[k:k + tm] * cw[k:k + 1]
    y = _silu(y)
    def rope_factor(k):
        col = ct_ref[k]
        return jnp.concatenate([jnp.concatenate([jnp.broadcast_to(rt_ref[k, r:r + 1, :], col.shape), col], axis=1)
                                for r in range(tm // GRID_W)], axis=0)

    cos = jnp.where(is_ctx, 1.0, rope_factor(0))
    sa = jnp.where(is_ctx, 0.0, rope_factor(1))
    sb = jnp.where(is_ctx, 0.0, rope_factor(2))
    quarter = DN_DH // 4
    for s in range(2):
        for h in range(DN_HEADS):
            c0 = s * DN_W + h * DN_DH
            xh = y[:, c0:c0 + DN_DH]
            xh = xh * lax.rsqrt(jnp.sum(xh * xh, axis=-1, keepdims=True) + EPS)
            xh = (xh * cos + pltpu.roll(xh, DN_DH - quarter, axis=1) * sa + pltpu.roll(xh, quarter, axis=1) * sb)
            if s == 0:
                xh = xh * (DN_DH ** -0.5)
            qkv_ref[:, c0:c0 + DN_DH] = xh.astype(BF16)
    qkv_ref[:, 2 * DN_W:] = y[:, 2 * DN_W:].astype(BF16)
    g = g_ref[...]
    lane = lax.broadcasted_iota(jnp.int32, g.shape, 1)
    decay = -jnp.exp(alog_ref[...]) * jax.nn.softplus(g + dtb_ref[...])
    gb_ref[...] = jnp.where(lane < 2 * DN_HEADS, decay, jax.nn.sigmoid(g))


def _dn_prep(p, g, conv_w, a_log, dt_bias, rope):
    tm = 256
    nb = T_ALL // tm
    nlat = T_LAT // tm
    hb = 16
    nh = T_ALL // hb
    pad = lambda v: jnp.pad(v.reshape(1, 2 * DN_HEADS).astype(F32), ((0, 0), (0, GP - 2 * DN_HEADS)))
    half = DN_DH // 2
    rt_spec = pl.BlockSpec((3, None, tm // GRID_W, half), lambda b: (0, jnp.minimum(b, nlat - 1), 0, 0))
    ct_spec = pl.BlockSpec((3, GRID_W, half), lambda b: (0, 0, 0))
    return pl.pallas_call(
        functools.partial(_dn_prep_kernel, tm=tm),
        grid=(nb,),
        in_specs=[pl.BlockSpec((tm, 3 * DN_W), lambda b: (b, P_DN)),
                  pl.BlockSpec((hb, 3 * DN_W), lambda b: (jnp.maximum(b * (tm // hb) - 1, 0), P_DN)),
                  pl.BlockSpec((hb, 3 * DN_W), lambda b: (jnp.minimum((b + 1) * (tm // hb), nh - 1), P_DN)),
                  pl.BlockSpec((tm, GP), lambda b: (b, 0)),
                  pl.BlockSpec((DN_CONV, 3 * DN_W), lambda b: (0, 0)),
                  pl.BlockSpec((1, GP), lambda b: (0, 0)),
                  pl.BlockSpec((1, GP), lambda b: (0, 0)),
                  rt_spec, ct_spec],
        out_specs=[pl.BlockSpec((tm, 3 * DN_W), lambda b: (b, 0)),
                   pl.BlockSpec((tm, GP), lambda b: (b, 0))],
        out_shape=[jax.ShapeDtypeStruct((T_ALL, 3 * DN_W), BF16),
                   jax.ShapeDtypeStruct((T_ALL, GP), F32)],
        compiler_params=_cparams(("arbitrary",)),
        name="dn_prep",
    )(p, p, p, g, conv_w, pad(a_log), pad(dt_bias), *rope)


def _pair_blockdiag(y):
    c = y.shape[0]
    lane = lax.broadcasted_iota(jnp.int32, y.shape, 1)
    zero = jnp.zeros((), y.dtype)
    return jnp.concatenate([jnp.where(lane < c, y, zero), jnp.where(lane >= c, y, zero)], axis=0)


def _unit_tri_inverses_minus_identity(mats):
    c = mats[0].shape[0]
    ri = lax.broadcasted_iota(jnp.int32, (c, 2 * c), 0)
    ci = lax.broadcasted_iota(jnp.int32, (c, 2 * c), 1) % c

    def same_block(s):
        return (ri // s) == (ci // s)

    def mm(xs, ys):
        return [_dot(x.astype(BF16), _pair_blockdiag(y.astype(BF16))) for x, y in zip(xs, ys)]

    diag = same_block(8)
    b1 = [jnp.where(diag, -a, 0.0) for a in mats]
    b2 = mm(b1, b1)
    b3 = mm(b1, b2)
    b4 = mm(b2, b2)
    n2 = [x + y + z for x, y, z in zip(b1, b2, b3)]
    n = [x + y + z for x, y, z in zip(n2, b4, mm(n2, b4))]
    for s in (8, 16, 32):
        off = jnp.logical_and(same_block(2 * s), jnp.logical_not(same_block(s)))
        lo = [jnp.where(off, a, 0.0) for a in mats]
        x = [p + q for p, q in zip(lo, mm(n, lo))]
        n = [p - (q + r) for p, q, r in zip(n, x, mm(x, n))]
    return n


def _dn_wy_kernel(qkv_ref, gb_ref, u_ref, wq_ref, kq_ref, dl_ref, *, nchunk):
    c = CHUNK
    tm = nchunk * c
    gb = gb_ref[...]
    qkv = qkv_ref[...]
    rt = lax.broadcasted_iota(jnp.int32, (tm, tm), 0)
    ct = lax.broadcasted_iota(jnp.int32, (tm, tm), 1)
    same_chunk = (rt // c) == (ct // c)
    ri = lax.broadcasted_iota(jnp.int32, (c, 2 * c), 0)
    lane2 = lax.broadcasted_iota(jnp.int32, (c, 2 * c), 1)
    ci = lane2 % c
    first = lane2 < c
    lane = lax.broadcasted_iota(jnp.int32, (8, GP), 1)
    lane_k = lax.broadcasted_iota(jnp.int32, (c, 2 * DN_DH), 1)
    zk = jnp.zeros((), BF16)
    eye_p = (ri == ci).astype(F32)
    eye2 = (lax.broadcasted_iota(jnp.int32, (DN_DH, 2 * DN_DH), 1) % DN_DH
            == lax.broadcasted_iota(jnp.int32, (DN_DH, 2 * DN_DH), 0)).astype(BF16)
    gb_t = gb.T
    dl_rows, gcs = [], []
    for d in range(2):
        cum = jnp.logical_and(same_chunk, (ct <= rt) if d == 0 else (ct >= rt))
        gc_all = jnp.dot(cum.astype(F32), gb, preferred_element_type=F32, precision=lax.Precision.HIGHEST)
        gcs.append((gc_all, gc_all.T))
        dl_rows.append([jnp.exp(gc_all[j * c + (c - 1 if d == 0 else 0):j * c + (c if d == 0 else 1), :])
                        for j in range(nchunk)])
    for j0 in range(0, nchunk, WY_GROUP_CHUNKS):
        shared = {}
        for j in range(j0, j0 + WY_GROUP_CHUNKS):
            for pr in range(DN_HEADS // 2):
                r0 = j * c
                hs = slice(2 * pr * DN_DH, (2 * pr + 2) * DN_DH)
                bd = lambda x: jnp.concatenate([jnp.where(lane_k < DN_DH, x, zk), jnp.where(lane_k >= DN_DH, x, zk)],
                                               axis=0)
                q2 = qkv[r0:r0 + c, hs]
                k2 = qkv[r0:r0 + c, DN_W + hs.start:DN_W + hs.stop]
                v2 = qkv[r0:r0 + c, 2 * DN_W + hs.start:2 * DN_W + hs.stop]
                bd_k = bd(k2)
                shared[(j, pr)] = dict(q2=q2, bd_k=bd_k, bd_v=bd(v2),
                                       gqt=_dot_nt(jnp.concatenate([k2, q2, eye2], axis=0), bd_k))
        units = [(d, j, pr) for j in range(j0, j0 + WY_GROUP_CHUNKS) for d in range(2) for pr in range(DN_HEADS // 2)]
        pre = []
        for d, j, pr in units:
            gc_all, gc_all_t = gcs[d]
            incl = (ci <= ri) if d == 0 else (ci >= ri)
            strict = (ci < ri) if d == 0 else (ci > ri)
            r0 = j * c
            last = r0 + (c - 1 if d == 0 else 0)
            i0 = d * DN_HEADS + 2 * pr
            col = lambda x, a: x[r0:r0 + c, a:a + 1]
            row = lambda x, a: jnp.concatenate([x[a:a + 1, r0:r0 + c], x[a + 1:a + 2, r0:r0 + c]], axis=1)
            beta_p = jnp.where(first, col(gb, 2 * DN_HEADS + i0), col(gb, 2 * DN_HEADS + i0 + 1))
            gc_p = jnp.where(first, col(gc_all, i0), col(gc_all, i0 + 1))
            gc_row = row(gc_all_t, i0)
            beta_row = row(gb_t, 2 * DN_HEADS + i0)
            g_last = jnp.where(first[0:1], gc_all[last:last + 1, i0:i0 + 1], gc_all[last:last + 1, i0 + 1:i0 + 2])
            e = jnp.exp(jnp.where(incl, gc_p - gc_row, 0.0))
            eg_row = jnp.exp(gc_row)
            pre.append(dict(beta_p=beta_p, e_incl=jnp.where(incl, e, 0.0), e_strict=jnp.where(strict, e, 0.0),
                            beta_row=beta_row, eg_row=eg_row, tail_row=jnp.exp(g_last - gc_row)))
        gqt = [shared[(j, pr)]["gqt"] for d, j, pr in units]
        ns = _unit_tri_inverses_minus_identity([p["beta_p"] * x[:c] * p["e_strict"] for x, p in zip(gqt, pre)])
        tmat = {u_: n + eye_p for u_, n in zip(units, ns)}
        prd = dict(zip(units, pre))
        both = [(j, pr) for j in range(j0, j0 + WY_GROUP_CHUNKS) for pr in range(DN_HEADS // 2)]
        lhs_u = [jnp.concatenate([(tmat[(d, j, pr)] * prd[(d, j, pr)]["beta_row"]).astype(BF16) for d in range(2)], axis=0)
                 for j, pr in both]
        lhs_w = [jnp.concatenate([(tmat[(d, j, pr)] * (prd[(d, j, pr)]["beta_row"] * prd[(d, j, pr)]["eg_row"])
                                   ).astype(BF16) for d in range(2)], axis=0) for j, pr in both]
        u2 = dict(zip(both, [_dot(a, shared[jp]["bd_v"]) for a, jp in zip(lhs_u, both)]))
        w2 = dict(zip(both, [_dot(a, shared[jp]["bd_k"]) for a, jp in zip(lhs_w, both)]))
        for (d, j, pr), p, x in zip(units, pre, gqt):
            r0 = j * c
            i0 = d * DN_HEADS + 2 * pr
            cs = slice(i0 * DN_DH, (i0 + 2) * DN_DH)
            ks = slice(i0 * c, (i0 + 2) * c)
            q2 = shared[(j, pr)]["q2"].astype(F32)
            gcol = gcs[d][0][r0:r0 + c]
            eg = jnp.where(lane_k < DN_DH, jnp.exp(gcol[:, i0:i0 + 1]), jnp.exp(gcol[:, i0 + 1:i0 + 2]))
            u_ref[r0:r0 + c, cs] = u2[(j, pr)][d * c:(d + 1) * c].astype(BF16)
            wq_ref[2 * r0:2 * r0 + c, cs] = w2[(j, pr)][d * c:(d + 1) * c].astype(BF16)
            wq_ref[2 * r0 + c:2 * r0 + 2 * c, cs] = (q2 * eg).astype(BF16)
            kq_ref[3 * r0:3 * r0 + DN_DH, ks] = (x[2 * c:] * p["tail_row"]).astype(BF16)
            kq_ref[3 * r0 + DN_DH:3 * r0 + DN_DH + c, ks] = (x[c:2 * c] * p["e_incl"]).astype(BF16)
    for j in range(nchunk):
        row = jnp.where(lane < DN_HEADS, dl_rows[0][j], dl_rows[1][j])
        dl_ref[8 * j:8 * j + 8, :] = row


def _dn_wy(qkv, gb, nchunk):
    tm = nchunk * CHUNK
    nb = T_ALL // tm
    nch = 2 * DN_HEADS
    return pl.pallas_call(
        functools.partial(_dn_wy_kernel, nchunk=nchunk),
        grid=(nb,),
        in_specs=[pl.BlockSpec((tm, 3 * DN_W), lambda i: (i, 0)),
                  pl.BlockSpec((tm, GP), lambda i: (i, 0))],
        out_specs=[pl.BlockSpec((tm, nch * DN_DH), lambda i: (i, 0)),
                   pl.BlockSpec((2 * tm, nch * DN_DH), lambda i: (i, 0)),
                   pl.BlockSpec((3 * tm, nch * CHUNK), lambda i: (i, 0)),
                   pl.BlockSpec((8 * nchunk, GP), lambda i: (i, 0))],
        out_shape=[jax.ShapeDtypeStruct((T_ALL, nch * DN_DH), BF16),
                   jax.ShapeDtypeStruct((2 * T_ALL, nch * DN_DH), BF16),
                   jax.ShapeDtypeStruct((3 * T_ALL, nch * CHUNK), BF16),
                   jax.ShapeDtypeStruct((8 * T_ALL // CHUNK, GP), F32)],
        compiler_params=_cparams(("arbitrary",)),
        name="dn_wy",
    )(qkv, gb)


def _dn_scan_kernel(u_f_ref, wq_f_ref, kq_f_ref, dl_f_ref, u_b_ref, wq_b_ref, kq_b_ref, dl_b_ref,
                    of_ref, ob_ref, s_scr, *, nchunk):
    i = pl.program_id(0)

    @pl.when(i == 0)
    def _():
        s_scr[...] = jnp.zeros_like(s_scr)

    c = CHUNK
    states = [s_scr[idx] for idx in range(2 * DN_HEADS)]
    zero = jnp.zeros((c, DN_DH), BF16)
    refs = ((u_f_ref, wq_f_ref, kq_f_ref, dl_f_ref, of_ref), (u_b_ref, wq_b_ref, kq_b_ref, dl_b_ref, ob_ref))
    chains = [(d, h) for d in range(2) for h in range(DN_HEADS)]
    pairs = [(d, p) for d in range(2) for p in range(DN_HEADS // 2)]
    for step in range(nchunk):
        blk = lambda d: step if d == 0 else nchunk - 1 - step
        hs = lambda h: slice(h * DN_DH, (h + 1) * DN_DH)
        r1 = [_dot(refs[d][1][2 * blk(d) * c:2 * (blk(d) + 1) * c, hs(h)], states[d * DN_HEADS + h].astype(BF16))
              for d, h in chains]
        v_new = [(refs[d][0][blk(d) * c:(blk(d) + 1) * c, hs(h)].astype(F32) - r[:c]).astype(BF16)
                 for (d, h), r in zip(chains, r1)]
        r2 = [_dot(refs[d][2][3 * blk(d) * c:3 * (blk(d) + 1) * c, p * 2 * c:(p + 1) * 2 * c],
                   jnp.concatenate([jnp.concatenate([v_new[d * DN_HEADS + 2 * p], zero], axis=1),
                                    jnp.concatenate([zero, v_new[d * DN_HEADS + 2 * p + 1]], axis=1)], axis=0))
              for d, p in pairs]
        for d, h in chains:
            idx = d * DN_HEADS + h
            r = r2[d * (DN_HEADS // 2) + h // 2]
            ts = slice((h % 2) * DN_DH, (h % 2 + 1) * DN_DH)
            dl_row = refs[d][3][8 * blk(d):8 * blk(d) + 1, :]
            states[idx] = states[idx] * dl_row[:, idx:idx + 1] + r[:DN_DH, ts]
            refs[d][4][blk(d) * c:(blk(d) + 1) * c, hs(h)] = (r1[idx][c:] + r[DN_DH:, ts]).astype(BF16)
    for idx in range(2 * DN_HEADS):
        s_scr[idx] = states[idx]


def _dn_scan(u, wq, kq, dl, nchunk):
    tm = nchunk * CHUNK
    nb = T_ALL // tm
    nlat = T_LAT // tm
    fwd = lambda i: jnp.where(i == 0, nlat, i - 1)
    bwd = lambda i: nb - 1 - i
    specs = lambda f, col: [pl.BlockSpec((tm, DN_W), lambda i: (f(i), col)),
                            pl.BlockSpec((2 * tm, DN_W), lambda i: (f(i), col)),
                            pl.BlockSpec((3 * tm, DN_HEADS * CHUNK), lambda i: (f(i), col)),
                            pl.BlockSpec((8 * nchunk, GP), lambda i: (f(i), 0))]
    return pl.pallas_call(
        functools.partial(_dn_scan_kernel, nchunk=nchunk),
        grid=(nb,),
        in_specs=specs(fwd, 0) + specs(bwd, 1),
        out_specs=[pl.BlockSpec((tm, DN_W), lambda i: (fwd(i), 0)),
                   pl.BlockSpec((tm, DN_W), lambda i: (bwd(i), 0))],
        out_shape=[jax.ShapeDtypeStruct((T_ALL, DN_W), BF16),
                   jax.ShapeDtypeStruct((T_ALL, DN_W), BF16)],
        scratch_shapes=[pltpu.VMEM((2 * DN_HEADS, DN_DH, DN_DH), F32)],
        compiler_params=_cparams(("arbitrary",)),
        name="dn_scan",
    )(u, wq, kq, dl, u, wq, kq, dl)


def _dft_cos_sin(n):
    k = np.arange(n)
    ang = 2.0 * np.pi * ((k[:, None] * k[None, :]) % n) / n
    return np.cos(ang), np.sin(ang)


def _fft_consts():
    n = FFT_N
    c, s = _dft_cos_sin(n)
    cs_ch = np.concatenate([c, s], axis=1)
    w1 = np.block([[c, -s], [-s, -c]])
    k2 = np.arange(n)[None, :, None]
    t2 = np.arange(n)[:, None, None]
    ang = 2.0 * np.pi * ((k2 * t2) % (n * n)) / (n * n)
    scale = 1.0 / math.sqrt(T_LAT * FN_DG)
    cc, sc = _dft_cos_sin(T_CTX)
    scale_c = 1.0 / math.sqrt(T_CTX * FN_DG)
    f32 = lambda a: jnp.asarray(a, F32)
    bf = lambda a: f32(a).astype(BF16)
    return dict(cs_ch=bf(cs_ch), w1=bf(w1), twc=f32(np.cos(ang)), tws=f32(np.sin(ang)),
                c2=bf(c * scale), s2=bf(s * scale), cc=bf(cc * scale_c), sc=bf(sc * scale_c))


def _fft1_kernel(u_ref, cs_ref, w1_ref, twc_ref, tws_ref, y_ref, *, n_t2):
    n = FFT_N
    cs = cs_ref[...]
    w1 = w1_ref[...]
    for t in range(n_t2):
        twc = twc_ref[t]
        tws = tws_ref[t]
        for g in range(0, FN_GROUPS, 2):
            c0 = (t * FN_GROUPS + g) * FN_DG
            ab = [_dot(u_ref[:, c0 + s * FN_DG:c0 + (s + 1) * FN_DG], cs) for s in range(2)]
            rhs = jnp.concatenate([jnp.concatenate([ab[0][:, :FN_DG], ab[1][:, :FN_DG]], axis=1),
                                   jnp.concatenate([ab[0][:, FN_DG:], ab[1][:, FN_DG:]], axis=1)], axis=0)
            y = _dot(w1, rhs.astype(BF16))
            yr, yi = y[:n], y[n:]
            y_ref[0:n, c0:c0 + 2 * FN_DG] = (yr * twc + yi * tws).astype(BF16)
            y_ref[n:2 * n, c0:c0 + 2 * FN_DG] = (yi * twc - yr * tws).astype(BF16)


def _fft2_kernel(y_ref, c2_ref, s2_ref, o_ref, *, n_k2):
    c2 = c2_ref[...]
    s2 = s2_ref[...]
    for j in range(n_k2):
        o_ref[:, j * FN_W:(j + 1) * FN_W] = (_dot(c2, y_ref[0, j]) + _dot(s2, y_ref[1, j])).astype(BF16)


def _fft_ctx_kernel(u_ref, cs_ref, cc_ref, sc_ref, o_in_ref, o_ref):
    del o_in_ref
    for g in range(FN_GROUPS):
        ab = _dot(u_ref[:, g * FN_DG:(g + 1) * FN_DG], cs_ref[...])
        a = ab[:, :FN_DG].astype(BF16)
        b = ab[:, FN_DG:].astype(BF16)
        o_ref[:, g * FN_DG:(g + 1) * FN_DG] = (_dot(cc_ref[...], a) - _dot(sc_ref[...], b)).astype(BF16)


def _fnet(u, fc, with_ctx):
    n = FFT_N
    row_w = n * FN_W
    n_t2 = 8
    tc = n_t2 * FN_W
    y = pl.pallas_call(
        functools.partial(_fft1_kernel, n_t2=n_t2),
        grid=(row_w // tc,),
        in_specs=[pl.BlockSpec((n, tc), lambda j: (0, j)),
                  pl.BlockSpec((n, 2 * n), lambda j: (0, 0)),
                  pl.BlockSpec((2 * n, 2 * n), lambda j: (0, 0)),
                  pl.BlockSpec((n_t2, n, 1), lambda j: (j, 0, 0)),
                  pl.BlockSpec((n_t2, n, 1), lambda j: (j, 0, 0))],
        out_specs=pl.BlockSpec((2 * n, tc), lambda j: (0, j)),
        out_shape=jax.ShapeDtypeStruct((2 * n, row_w), BF16),
        compiler_params=_cparams(("arbitrary",)),
        name="fnet_stage1",
    )(u.reshape(T_ALL // n, row_w), fc["cs_ch"], fc["w1"], fc["twc"], fc["tws"])
    n_k2 = 8
    o = pl.pallas_call(
        functools.partial(_fft2_kernel, n_k2=n_k2),
        grid=(n // n_k2,),
        in_specs=[pl.BlockSpec((2, n_k2, n, FN_W), lambda j: (0, j, 0, 0)),
                  pl.BlockSpec((n, n), lambda j: (0, 0)),
                  pl.BlockSpec((n, n), lambda j: (0, 0))],
        out_specs=pl.BlockSpec((n, n_k2 * FN_W), lambda j: (0, j)),
        out_shape=jax.ShapeDtypeStruct(((T_ALL if with_ctx else T_LAT) // n, row_w), BF16),
        compiler_params=_cparams(("arbitrary",)),
        name="fnet_stage2",
    )(y.reshape(2, n, n, FN_W), fc["c2"], fc["s2"])
    if not with_ctx:
        return o.reshape(T_LAT, FN_W)
    o = o.reshape(T_ALL, FN_W)
    cb = T_LAT // T_CTX
    return pl.pallas_call(
        _fft_ctx_kernel,
        grid=(1,),
        in_specs=[pl.BlockSpec((T_CTX, FN_W), lambda j: (cb, 0)),
                  pl.BlockSpec((n, 2 * n), lambda j: (0, 0)),
                  pl.BlockSpec((T_CTX, T_CTX), lambda j: (0, 0)),
                  pl.BlockSpec((T_CTX, T_CTX), lambda j: (0, 0)),
                  pl.BlockSpec(memory_space=pl.ANY)],
        out_specs=pl.BlockSpec((T_CTX, FN_W), lambda j: (cb, 0)),
        out_shape=jax.ShapeDtypeStruct((T_ALL, FN_W), BF16),
        input_output_aliases={4: 0},
        compiler_params=_cparams(("arbitrary",)),
        name="fnet_ctx",
    )(u, fc["cs_ch"], fc["cc"], fc["sc"], o)


def _merge_kernel(*refs, tm, has_ctx, split):
    if split:
        (x_ref, ctx_ref, mod_ref, n1_ref, ona_ref, of_ref, ob_ref, z_ref, ofn_ref, dnn_ref,
         wg_ref, wna_ref, wdn_ref, wfn_ref, wout_ref, o_ref, tail_scr) = refs
    else:
        (x_ref, mod_ref, n1_ref, ona_ref, of_ref, ob_ref, z_ref, ofn_ref, dnn_ref,
         wg_ref, wna_ref, wdn_ref, wfn_ref, wout_ref, o_ref) = refs
        ctx_ref = tail_scr = None
    i = pl.program_id(0)
    tail_is_ctx = i == pl.num_programs(0) - 1
    x = _token_tile(x_ref, ctx_ref, tail_scr, tail_is_ctx)
    h = _modnorm_tile(x, n1_ref[...], mod_ref, 0, 1, tail_is_ctx, has_ctx).astype(BF16)
    o = of_ref[...].astype(F32) + ob_ref[...].astype(F32)
    z = z_ref[...].astype(F32)
    parts = []
    for hd in range(DN_HEADS):
        sl = slice(hd * DN_DH, (hd + 1) * DN_DH)
        oh = o[:, sl]
        oh = oh * lax.rsqrt(jnp.mean(oh * oh, axis=-1, keepdims=True) + EPS) * dnn_ref[...]
        parts.append(oh * _silu(z[:, sl]))
    odn = jnp.concatenate(parts, axis=-1).astype(BF16)
    branches = ((ona_ref[...], wna_ref), (odn, wdn_ref), (ofn_ref[...], wfn_ref))
    logits = [_dot(h, wg_ref[:, b * D:(b + 1) * D]) for b in range(N_BRANCH)]
    proj = [_dot(br, w_ref[...]) for br, w_ref in branches]
    y = jax.nn.sigmoid(logits[0]) * proj[0]
    for b in range(1, N_BRANCH):
        y = y + jax.nn.sigmoid(logits[b]) * proj[b]
    y = _dot(y.astype(BF16), wout_ref[...])
    o_ref[...] = _gated_residual(x, y, mod_ref, 2, tail_is_ctx, has_ctx)


def _merge(x, ctx, mod, norm1, p, o_na, o_f, o_b, o_fn, dn_norm, w_gate, layer, w_na_o, w_dn_o, w_fn, w_out, has_ctx):
    rows, tm = (T_ALL, TM_MERGE_ALL) if has_ctx else (T_LAT, TM_MERGE_LAT)
    split = ctx is not None
    row = lambda w: pl.BlockSpec((tm, w), lambda i: (i, 0))
    full = lambda a: pl.BlockSpec(a.shape, lambda i: (0, 0), pipeline_mode=pl.Buffered(1))
    ctx_specs = [pl.BlockSpec((T_CTX, D), lambda i: (0, 0))] if split else []
    return pl.pallas_call(
        functools.partial(_merge_kernel, tm=tm, has_ctx=has_ctx, split=split),
        grid=(rows // tm,),
        in_specs=[row(D)] + ctx_specs + [
                  pl.BlockSpec((8, 6 * D), lambda i: (0, 0)), pl.BlockSpec((1, D), lambda i: (0, 0)),
                  row(NA_W), row(DN_W), row(DN_W),
                  pl.BlockSpec((tm, DN_W), lambda i: (i, P_Z)),
                  row(FN_W), pl.BlockSpec((1, DN_DH), lambda i: (0, 0)),
                  pl.BlockSpec((None,) + w_gate.shape[1:], lambda i: (layer, 0, 0), pipeline_mode=pl.Buffered(1)),
                  full(w_na_o), full(w_dn_o), full(w_fn), full(w_out)],
        out_specs=row(D),
        out_shape=jax.ShapeDtypeStruct((rows, D), F32),
        scratch_shapes=[pltpu.VMEM((T_CTX, D), F32)] if split else [],
        compiler_params=_cparams(("arbitrary",)),
        name="merge",
    )(*([x, ctx] if split else [x]), mod, norm1.reshape(1, D), o_na, o_f, o_b, p, o_fn, dn_norm.reshape(1, DN_DH),
      w_gate, w_na_o, w_dn_o, w_fn, w_out)


def _mlp_kernel(x_ref, xnext_ref, mod_ref, n_ref, w1_ref, w2_ref, nf_ref, o_ref, h_scr, acc_scr, *,
                tm, has_ctx, final):
    i = pl.program_id(0)
    j = pl.program_id(1)
    nt = pl.num_programs(0)
    nj = pl.num_programs(1)
    slot = i % 2

    @pl.when(jnp.logical_and(i == 0, j == 0))
    def _():
        h_scr[0] = _modnorm_tile(x_ref[...], n_ref[...], mod_ref, 3, 4, nt == 1, has_ctx).astype(BF16)

    def hidden_step(first):
        a = jnp.maximum(_dot(h_scr[slot], w1_ref[...]), 0.0)
        upd = _dot((a * a).astype(BF16), w2_ref[...])
        if first:
            acc_scr[...] = upd
        else:
            acc_scr[...] += upd

    @pl.when(j == 0)
    def _():
        hidden_step(True)

    @pl.when(jnp.logical_and(j > 0, j < nj - 1))
    def _():
        hidden_step(False)

    @pl.when(j == nj - 1)
    def _():
        h_scr[1 - slot] = _modnorm_tile(xnext_ref[...], n_ref[...], mod_ref, 3, 4, i + 1 == nt - 1, has_ctx).astype(BF16)
        hidden_step(False)
        xn = _gated_residual(x_ref[...], acc_scr[...], mod_ref, 5, i == nt - 1, has_ctx)
        if final:
            xn = xn * lax.rsqrt(jnp.mean(xn * xn, axis=-1, keepdims=True) + EPS) * nf_ref[...]
        o_ref[...] = xn


def _mlp(xs, mod, norm, w1, w2, norm_f, has_ctx, final):
    rows, tm = (T_ALL, TM_ALL) if has_ctx else (T_LAT, TM_LAT)
    th = 1024
    return pl.pallas_call(
        functools.partial(_mlp_kernel, tm=tm, has_ctx=has_ctx, final=final),
        grid=(rows // tm, HID // th),
        in_specs=[pl.BlockSpec((tm, D), lambda i, j: (i, 0)),
                  pl.BlockSpec((tm, D), lambda i, j: (jnp.minimum(i + 1, rows // tm - 1), 0)),
                  pl.BlockSpec((8, 6 * D), lambda i, j: (0, 0)),
                  pl.BlockSpec((1, D), lambda i, j: (0, 0)),
                  pl.BlockSpec((D, th), lambda i, j: (0, j)),
                  pl.BlockSpec((th, D), lambda i, j: (j, 0)),
                  pl.BlockSpec((1, D), lambda i, j: (0, 0))],
        out_specs=pl.BlockSpec((tm, D), lambda i, j: (i, 0)),
        out_shape=jax.ShapeDtypeStruct((rows, D), F32),
        scratch_shapes=[pltpu.VMEM((2, tm, D), BF16), pltpu.VMEM((tm, D), F32)],
        compiler_params=_cparams(("arbitrary", "arbitrary")),
        name="mlp",
    )(xs, xs, mod, norm.reshape(1, D), w1, w2, norm_f.reshape(1, D))


IN_W = 2 * NA_W + 2 * DN_W + 4 * DN_HEADS + NA_W + 2 * DN_W + FN_W + N_BRANCH * D


W_PREP_BLK = 512
_W_SRC = [2 * NA_W + 2 * DN_W + 4 * DN_HEADS, 0, NA_W, 3 * NA_W + 2 * DN_W + 4 * DN_HEADS, 2 * NA_W, 2 * NA_W + DN_W,
          3 * NA_W + 3 * DN_W + 4 * DN_HEADS, 3 * NA_W + 4 * DN_W + 4 * DN_HEADS] + [
          IN_W - N_BRANCH * D + i * W_PREP_BLK for i in range(N_BRANCH * D // W_PREP_BLK)]
_W_AB = 2 * NA_W + 2 * DN_W


def _w_prep_kernel(w_ref, ab_ref, wcat_ref, wgate_ref, wg_ref):
    c = pl.program_id(1)
    ncat = P_W // W_PREP_BLK
    blk = w_ref[0].T.astype(BF16)

    @pl.when(c < ncat)
    def _():
        wcat_ref[...] = blk

    @pl.when(c >= ncat)
    def _():
        wgate_ref[...] = blk

    @pl.when(c == 0)
    def _():
        ab = jnp.concatenate([ab_ref[0], jnp.zeros((GP - ab_ref.shape[1], D), F32)], axis=0)
        wg_ref[...] = ab.T.astype(BF16)


def _split_in_weights(w_in):
    ncat = P_W // W_PREP_BLK
    nblk = len(_W_SRC)

    def src(c):
        off = 0
        for k, o in enumerate(_W_SRC):
            off = off + jnp.where(c == k, o // 16, 0)
        return off * 16

    return pl.pallas_call(
        _w_prep_kernel,
        grid=(DEPTH, nblk),
        in_specs=[pl.BlockSpec((pl.Element(1), pl.Element(W_PREP_BLK), pl.Element(D)), lambda l, c: (l, src(c), 0)),
                  pl.BlockSpec((pl.Element(1), pl.Element(4 * DN_HEADS), pl.Element(D)), lambda l, c: (l, _W_AB, 0))],
        out_specs=[pl.BlockSpec((None, D, W_PREP_BLK), lambda l, c: (l, 0, jnp.minimum(c, ncat - 1))),
                   pl.BlockSpec((None, D, W_PREP_BLK), lambda l, c: (l, 0, jnp.maximum(c - ncat, 0))),
                   pl.BlockSpec((None, D, GP), lambda l, c: (l, 0, 0))],
        out_shape=[jax.ShapeDtypeStruct((DEPTH, D, P_W), BF16),
                   jax.ShapeDtypeStruct((DEPTH, D, N_BRANCH * D), BF16),
                   jax.ShapeDtypeStruct((DEPTH, D, GP), BF16)],
        compiler_params=_cparams(("arbitrary", "arbitrary")),
        name="w_prep",
    )(jnp.swapaxes(w_in, 1, 2), jnp.swapaxes(w_in, 1, 2))


def kernel(x, c, ctx, c_ctx, w_ada, b_ada, norm1, w_in, conv_w, a_log, dt_bias, dn_norm, rpb,
           w_na_o, w_dn_o, w_fn, w_out, norm2, w_mlp1, w_mlp2, norm_f):
    xs, xc = x[0], ctx[0]
    cc = jnp.concatenate([c, c_ctx[None, :], jnp.zeros((6, D), F32)], axis=0)
    mods = _ada(cc, w_ada, b_ada)
    rope = _rope_tables()
    fc = _fft_consts()
    na_tab = _na_tables(rpb)
    wcat, w_gate, wg = _split_in_weights(w_in)
    for l in range(DEPTH):
        has_ctx = l < DEPTH - 1
        final = l == DEPTH - 1
        u, p, g = _in_proj(xs, xc, mods[l], norm1[l], wcat, wg, l)
        o_na = _na(p, na_tab, l, has_ctx)
        qkv, gb = _dn_prep(p, g, conv_w[l], a_log[l], dt_bias[l], rope)
        o_f, o_b = _dn_scan(*_dn_wy(qkv, gb, DN_BLK_CHUNKS), DN_BLK_CHUNKS)
        o_fn = _fnet(u, fc, has_ctx)
        xs, xc = _merge(xs, xc, mods[l], norm1[l], p, o_na, o_f, o_b, o_fn, dn_norm[l], w_gate, l,
                    w_na_o[l].astype(BF16), w_dn_o[l].astype(BF16), w_fn[l].astype(BF16), w_out[l].astype(BF16),
                    has_ctx), None
        xs = _mlp(xs, mods[l], norm2[l], w_mlp1[l].astype(BF16), w_mlp2[l].astype(BF16), norm_f, has_ctx, final)
    return xs[None]
```

```python
import functools
import math

import numpy as np
import jax
import jax.numpy as jnp
from jax import lax
from jax.experimental import pallas as pl
from jax.experimental.pallas import tpu as pltpu

F32 = jnp.float32
BF16 = jnp.bfloat16

D = 1024
T_LAT = 16384
T_CTX = 256
T_ALL = T_LAT + T_CTX
DEPTH = 2
GRID_W = 64
GRID_H = T_LAT // GRID_W
NA_HEADS = 8
NA_DH = 64
NA_WIN_H = 8
NA_WIN_W = 16
NA_W = NA_HEADS * NA_DH
DN_HEADS = 4
DN_DH = 128
DN_W = DN_HEADS * DN_DH
DN_CONV = 5
CHUNK = 64
FN_GROUPS = 4
FN_DG = 128
FN_W = FN_GROUPS * FN_DG
N_BRANCH = 3
HID = 4 * D
ROPE_BASE = 10000.0
EPS = 1e-6
NEG = -1e30

P_W = 3 * NA_W + 4 * DN_W + FN_W
P_NAQ, P_NAK, P_NAV = 0, 1, 2
P_DN = 1
P_Z = 6
TN_IN = P_W // 2
GP = 128

TM_ALL = 1280
TM_LAT = 1024
TM_MERGE_ALL = 640
TM_MERGE_LAT = 512
NA_BLK = 4 * GRID_W
NA_HEAD_GROUP = 4
NA_VARIANTS = 4
FFT_N = 128
DN_BLK_CHUNKS = 4
WY_GROUP_CHUNKS = 4
VMEM_LIMIT = 56 * 1024 * 1024


def _cparams(sem):
    return pltpu.CompilerParams(dimension_semantics=sem, vmem_limit_bytes=VMEM_LIMIT)


def _dot(a, b):
    return jnp.dot(a, b, preferred_element_type=F32)


def _dot_nt(a, b):
    return lax.dot_general(a, b, (((1,), (1,)), ((), ())), preferred_element_type=F32)


def _silu(x):
    return x * jax.nn.sigmoid(x)


def _ada_kernel(c_ref, w_ref, b_ref, o_ref):
    s = _silu(c_ref[...])
    s_hi = s.astype(BF16)
    s_lo = (s - s_hi.astype(F32)).astype(BF16)
    w = w_ref[0].astype(BF16)
    o_ref[0] = _dot(s_hi, w) + _dot(s_lo, w) + b_ref[0]


def _ada(cc, w_ada, b_ada):
    tn = 1536
    return pl.pallas_call(
        _ada_kernel,
        grid=(DEPTH, 6 * D // tn),
        in_specs=[pl.BlockSpec((8, D), lambda l, j: (0, 0)),
                  pl.BlockSpec((1, D, tn), lambda l, j: (l, 0, j)),
                  pl.BlockSpec((1, 1, tn), lambda l, j: (l, 0, j))],
        out_specs=pl.BlockSpec((1, 8, tn), lambda l, j: (l, 0, j)),
        out_shape=jax.ShapeDtypeStruct((DEPTH, 8, 6 * D), F32),
        compiler_params=_cparams(("arbitrary", "arbitrary")),
        name="ada_mod",
    )(cc, w_ada, b_ada.reshape(DEPTH, 1, 6 * D))


def _row_slices(tm, has_ctx):
    return [slice(0, tm)] if not has_ctx else [slice(0, tm - T_CTX), slice(tm - T_CTX, tm)]


def _mod_vecs(mod_ref, k, tail_is_ctx, has_ctx):
    lat = mod_ref[0:1, k * D:(k + 1) * D]
    if not has_ctx:
        return [lat]
    return [lat, jnp.where(tail_is_ctx, mod_ref[1:2, k * D:(k + 1) * D], lat)]


def _modnorm(x, g, shift, scale):
    y = x * lax.rsqrt(jnp.mean(x * x, axis=-1, keepdims=True) + EPS)
    return y * (g * (1.0 + scale)) + shift


def _modnorm_tile(x, g, mod_ref, k_shift, k_scale, tail_is_ctx, has_ctx):
    tm = x.shape[0]
    parts = [_modnorm(x[sl], g, sh, sc) for sl, sh, sc in zip(_row_slices(tm, has_ctx),
                                                            _mod_vecs(mod_ref, k_shift, tail_is_ctx, has_ctx),
                                                            _mod_vecs(mod_ref, k_scale, tail_is_ctx, has_ctx))]
    return parts[0] if len(parts) == 1 else jnp.concatenate(parts, axis=0)


def _gated_residual(x, y, mod_ref, k_gate, tail_is_ctx, has_ctx):
    tm = x.shape[0]
    parts = [x[sl] + gt * y[sl] for sl, gt in zip(_row_slices(tm, has_ctx), _mod_vecs(mod_ref, k_gate, tail_is_ctx, has_ctx))]
    return parts[0] if len(parts) == 1 else jnp.concatenate(parts, axis=0)


def _token_tile(x_ref, ctx_ref, tail_scr, is_last):
    if ctx_ref is None:
        return x_ref[...]
    top = x_ref.shape[0] - T_CTX

    @pl.when(jnp.logical_not(is_last))
    def _():
        tail_scr[...] = x_ref[top:, :]

    @pl.when(is_last)
    def _():
        tail_scr[...] = ctx_ref[...]

    return jnp.concatenate([x_ref[0:top, :], tail_scr[...]], axis=0)


def _in_proj_kernel(*refs, tm, split):
    if split:
        x_ref, ctx_ref, mod_ref, n_ref, w_ref, wg_ref, u_ref, p_ref, g_ref, h_scr, tail_scr = refs
    else:
        x_ref, mod_ref, n_ref, w_ref, wg_ref, u_ref, p_ref, g_ref, h_scr = refs
        ctx_ref = tail_scr = None
    i = pl.program_id(0)
    j = pl.program_id(1)

    @pl.when(j == 0)
    def _():
        is_last = i == pl.num_programs(0) - 1
        x = _token_tile(x_ref, ctx_ref, tail_scr, is_last)
        h = _modnorm_tile(x, n_ref[...], mod_ref, 0, 1, is_last, True).astype(BF16)
        h_scr[...] = h
        g_ref[...] = _dot(h, wg_ref[...])

    p = _dot(h_scr[...], w_ref[...]).astype(BF16)
    p_ref[...] = p

    @pl.when(j == pl.num_programs(1) - 1)
    def _():
        u_ref[...] = p[:, TN_IN - FN_W:]


def _in_proj(x, ctx, mod, norm, wcat, wg, layer):
    tm = TM_ALL
    nj = P_W // TN_IN
    split = ctx is not None
    ctx_specs = [pl.BlockSpec((T_CTX, D), lambda i, j: (0, 0))] if split else []
    return pl.pallas_call(
        functools.partial(_in_proj_kernel, tm=tm, split=split),
        grid=(T_ALL // tm, nj),
        in_specs=[pl.BlockSpec((tm, D), lambda i, j: (i, 0))] + ctx_specs + [
                  pl.BlockSpec((8, 6 * D), lambda i, j: (0, 0)),
                  pl.BlockSpec((1, D), lambda i, j: (0, 0)),
                  pl.BlockSpec((None, D, TN_IN), lambda i, j: (layer, 0, j)),
                  pl.BlockSpec((None, D, GP), lambda i, j: (layer, 0, 0))],
        out_specs=[pl.BlockSpec((tm, FN_W), lambda i, j: (i, 0)),
                   pl.BlockSpec((tm, TN_IN), lambda i, j: (i, j)),
                   pl.BlockSpec((tm, GP), lambda i, j: (i, 0))],
        out_shape=[jax.ShapeDtypeStruct((T_ALL, FN_W), BF16),
                   jax.ShapeDtypeStruct((T_ALL, P_W), BF16),
                   jax.ShapeDtypeStruct((T_ALL, GP), F32)],
        scratch_shapes=[pltpu.VMEM((tm, D), BF16)] + ([pltpu.VMEM((T_CTX, D), F32)] if split else []),
        compiler_params=_cparams(("arbitrary", "arbitrary")),
        name="in_proj",
    )(*([x, ctx] if split else [x]), mod, norm.reshape(1, D), wcat, wg)


def _na_tables(rpb):
    qr = np.arange(4)[:, None, None, None]
    qc = np.arange(GRID_W)[None, :, None, None]
    kk = np.arange(12)[None, None, :, None]
    kc = np.arange(GRID_W)[None, None, None, :]
    valid = []
    for b in (0, 1, GRID_H // 4 - 1):
        r = 4 * b + qr
        kr = 4 * (b - 1) + kk
        rs = np.clip(r - NA_WIN_H // 2, 0, GRID_H - NA_WIN_H)
        cs = np.clip(qc - NA_WIN_W // 2, 0, GRID_W - NA_WIN_W)
        ok = (kr >= 0) & (kr < GRID_H) & (kr >= rs) & (kr < rs + NA_WIN_H) & (kc >= cs) & (kc < cs + NA_WIN_W)
        valid.append(np.broadcast_to(ok, (4, GRID_W, 12, GRID_W)).reshape(NA_BLK, 12 * GRID_W))
    valid.append(np.zeros_like(valid[0]))
    valid = np.stack(valid)
    col = np.arange(GRID_W)
    onehot = ((col[None, None, :] - col[None, :, None] + NA_WIN_W - 1)
              == np.arange(2 * NA_WIN_W - 1)[:, None, None]).astype(np.float32)
    tcol = jnp.einsum('lhrd,dqk->lhrqk', rpb, onehot, precision=lax.Precision.HIGHEST)
    tcol = tcol.reshape(DEPTH * NA_HEADS, 2 * NA_WIN_H - 1, GRID_W, GRID_W)

    def assemble(tcol_ref, valid_ref, tab_ref):
        bias = jnp.concatenate([jnp.concatenate([tcol_ref[k - q + NA_WIN_H - 5] for k in range(12)], axis=1)
                                for q in range(4)], axis=0)
        for var in range(NA_VARIANTS):
            tab_ref[var] = jnp.where(valid_ref[var] > 0, bias, NEG).astype(BF16)

    return pl.pallas_call(
        assemble,
        grid=(DEPTH * NA_HEADS,),
        in_specs=[pl.BlockSpec((None, 2 * NA_WIN_H - 1, GRID_W, GRID_W), lambda i: (i, 0, 0, 0)),
                  pl.BlockSpec((NA_VARIANTS, NA_BLK, 12 * GRID_W), lambda i: (0, 0, 0))],
        out_specs=pl.BlockSpec((NA_VARIANTS, None, NA_BLK, 12 * GRID_W), lambda i: (i // NA_HEADS, i % NA_HEADS, 0, 0)),
        out_shape=jax.ShapeDtypeStruct((DEPTH * NA_VARIANTS, NA_HEADS, NA_BLK, 12 * GRID_W), BF16),
        compiler_params=_cparams(("arbitrary",)),
        name="na_tables",
    )(tcol, jnp.asarray(valid, F32))


def _na_kernel(q_ref, k0_ref, k1_ref, k2_ref, v0_ref, v1_ref, v2_ref, kc_ref, vc_ref, tab_ref, o_ref):
    q = q_ref[...]
    kl = jnp.concatenate([k0_ref[...], k1_ref[...], k2_ref[...]], axis=0)
    vl = jnp.concatenate([v0_ref[...], v1_ref[...], v2_ref[...]], axis=0)
    kc = kc_ref[...]
    vc = vc_ref[...]
    lane_l = lax.broadcasted_iota(jnp.int32, (vl.shape[0], 2 * NA_DH), 1)
    lane_c = lax.broadcasted_iota(jnp.int32, (vc.shape[0], 2 * NA_DH), 1)
    one = jnp.ones((), BF16)
    outs = []
    for h0 in range(0, NA_HEADS, NA_HEAD_GROUP):
        heads = range(h0, h0 + NA_HEAD_GROUP)
        sls = [slice(h * NA_DH, (h + 1) * NA_DH) for h in heads]
        pls = [slice((h // 2) * 2 * NA_DH, (h // 2 + 1) * 2 * NA_DH) for h in heads]
        own = [(lambda ln, h=h: (ln < NA_DH) if h % 2 == 0 else (ln >= NA_DH)) for h in heads]
        qs = [q[:, sl] * (NA_DH ** -0.5) for sl in sls]
        z_loc = [_dot_nt(qh, kl[:, sl]).astype(BF16) + tab_ref[0, h] for h, qh, sl in zip(heads, qs, sls)]
        z_ctx = [_dot_nt(qh, kc[:, sl]).astype(BF16) for qh, sl in zip(qs, sls)]
        m = [jnp.maximum(jnp.max(a, axis=-1, keepdims=True), jnp.max(b, axis=-1, keepdims=True))
             for a, b in zip(z_loc, z_ctx)]
        p_loc = [jnp.exp(a - mm) for a, mm in zip(z_loc, m)]
        p_ctx = [jnp.exp(b - mm) for b, mm in zip(z_ctx, m)]
        oa = [_dot(a, jnp.where(f(lane_l), vl[:, ps], one)) + _dot(b, jnp.where(f(lane_c), vc[:, ps], one))
              for a, b, ps, f in zip(p_loc, p_ctx, pls, own)]
        outs += [(x[:, :NA_DH] / x[:, NA_DH:NA_DH + 1]) if h % 2 == 0 else (x[:, NA_DH:] / x[:, 0:1])
                 for h, x in zip(heads, oa)]
    o_ref[...] = jnp.concatenate(outs, axis=-1).astype(BF16)


def _na(p, tab, layer, with_ctx):
    nlat = T_LAT // NA_BLK
    nb = nlat + (1 if with_ctx else 0)
    ctx_blk = T_LAT // NA_BLK

    def kidx(b, part):
        return jnp.clip(jnp.minimum(b, nlat - 1) - 1 + part, 0, nlat - 1)

    def variant(b):
        return layer * NA_VARIANTS + jnp.where(b == 0, 0, jnp.where(b == nlat - 1, 2, jnp.where(b >= nlat, 3, 1)))

    blk = (NA_BLK, NA_W)
    in_specs = [pl.BlockSpec(blk, lambda b: (b, P_NAQ))]
    in_specs += [pl.BlockSpec(blk, functools.partial(lambda b, part: (kidx(b, part), P_NAK), part=part))
                 for part in range(3)]
    in_specs += [pl.BlockSpec(blk, functools.partial(lambda b, part: (kidx(b, part), P_NAV), part=part))
                 for part in range(3)]
    in_specs += [pl.BlockSpec(blk, lambda b: (ctx_blk, P_NAK)),
                 pl.BlockSpec(blk, lambda b: (ctx_blk, P_NAV)),
                 pl.BlockSpec((1, NA_HEADS, NA_BLK, 3 * NA_BLK), lambda b: (variant(b), 0, 0, 0))]
    return pl.pallas_call(
        _na_kernel,
        grid=(nb,),
        in_specs=in_specs,
        out_specs=pl.BlockSpec(blk, lambda b: (b, 0)),
        out_shape=jax.ShapeDtypeStruct((nb * NA_BLK, NA_W), BF16),
        compiler_params=_cparams(("arbitrary",)),
        name="na_attn",
    )(p, p, p, p, p, p, p, p, p, tab)


def _rope_tables():
    half = DN_DH // 2
    inv = 1.0 / (ROPE_BASE ** (np.arange(0, half, 2, dtype=np.float64) / half))
    inv = np.concatenate([inv, inv])
    low = np.arange(half) < half // 2

    def tabs(n):
        ang = np.arange(n, dtype=np.float64)[:, None] * inv[None, :]
        return np.stack([np.cos(ang), np.where(low, -np.sin(ang), 0.0), np.where(low, 0.0, np.sin(ang))])

    rows = tabs(GRID_H).reshape(3, GRID_H // 4, 4, half)
    return jnp.asarray(rows, F32), jnp.asarray(tabs(GRID_W), F32)


def _dn_prep_kernel(x_ref, prev_ref, next_ref, g_ref, cw_ref, alog_ref, dtb_ref, rt_ref, ct_ref,
                    qkv_ref, gb_ref, *, tm):
    b = pl.program_id(0)
    nlat = T_LAT // tm
    first = jnp.logical_or(b == 0, b == nlat)
    last = b >= nlat - 1
    is_ctx = b >= nlat
    cur = x_ref[...].astype(F32)
    prev = jnp.where(first, 0.0, prev_ref[...][14:16].astype(F32))
    nxt = jnp.where(last, 0.0, next_ref[...][0:2].astype(F32))
    ext = jnp.concatenate([prev, cur, nxt], axis=0)
    cw = cw_ref[...]
    y = ext[0:tm] * cw[0:1]
    for k in range(1, DN_CONV):
        y = y + ext[k:k + tm] * cw[k:k + 1]
    y = _silu(y)
    def rope_factor(k):
        col = ct_ref[k]
        return jnp.concatenate([jnp.concatenate([jnp.broadcast_to(rt_ref[k, r:r + 1, :], col.shape), col], axis=1)
                                for r in range(tm // GRID_W)], axis=0)

    cos = jnp.where(is_ctx, 1.0, rope_factor(0))
    sa = jnp.where(is_ctx, 0.0, rope_factor(1))
    sb = jnp.where(is_ctx, 0.0, rope_factor(2))
    quarter = DN_DH // 4
    for s in range(2):
        for h in range(DN_HEADS):
            c0 = s * DN_W + h * DN_DH
            xh = y[:, c0:c0 + DN_DH]
            xh = xh * lax.rsqrt(jnp.sum(xh * xh, axis=-1, keepdims=True) + EPS)
            xh = (xh * cos + pltpu.roll(xh, DN_DH - quarter, axis=1) * sa + pltpu.roll(xh, quarter, axis=1) * sb)
            if s == 0:
                xh = xh * (DN_DH ** -0.5)
            qkv_ref[:, c0:c0 + DN_DH] = xh.astype(BF16)
    qkv_ref[:, 2 * DN_W:] = y[:, 2 * DN_W:].astype(BF16)
    g = g_ref[...]
    lane = lax.broadcasted_iota(jnp.int32, g.shape, 1)
    decay = -jnp.exp(alog_ref[...]) * jax.nn.softplus(g + dtb_ref[...])
    gb_ref[...] = jnp.where(lane < 2 * DN_HEADS, decay, jax.nn.sigmoid(g))


def _dn_prep(p, g, conv_w, a_log, dt_bias, rope):
    tm = 256
    nb = T_ALL // tm
    nlat = T_LAT // tm
    hb = 16
    nh = T_ALL // hb
    pad = lambda v: jnp.pad(v.reshape(1, 2 * DN_HEADS).astype(F32), ((0, 0), (0, GP - 2 * DN_HEADS)))
    half = DN_DH // 2
    rt_spec = pl.BlockSpec((3, None, tm // GRID_W, half), lambda b: (0, jnp.minimum(b, nlat - 1), 0, 0))
    ct_spec = pl.BlockSpec((3, GRID_W, half), lambda b: (0, 0, 0))
    return pl.pallas_call(
        functools.partial(_dn_prep_kernel, tm=tm),
        grid=(nb,),
        in_specs=[pl.BlockSpec((tm, 3 * DN_W), lambda b: (b, P_DN)),
                  pl.BlockSpec((hb, 3 * DN_W), lambda b: (jnp.maximum(b * (tm // hb) - 1, 0), P_DN)),
                  pl.BlockSpec((hb, 3 * DN_W), lambda b: (jnp.minimum((b + 1) * (tm // hb), nh - 1), P_DN)),
                  pl.BlockSpec((tm, GP), lambda b: (b, 0)),
                  pl.BlockSpec((DN_CONV, 3 * DN_W), lambda b: (0, 0)),
                  pl.BlockSpec((1, GP), lambda b: (0, 0)),
                  pl.BlockSpec((1, GP), lambda b: (0, 0)),
                  rt_spec, ct_spec],
        out_specs=[pl.BlockSpec((tm, 3 * DN_W), lambda b: (b, 0)),
                   pl.BlockSpec((tm, GP), lambda b: (b, 0))],
        out_shape=[jax.ShapeDtypeStruct((T_ALL, 3 * DN_W), BF16),
                   jax.ShapeDtypeStruct((T_ALL, GP), F32)],
        compiler_params=_cparams(("arbitrary",)),
        name="dn_prep",
    )(p, p, p, g, conv_w, pad(a_log), pad(dt_bias), *rope)


def _pair_blockdiag(y):
    c = y.shape[0]
    lane = lax.broadcasted_iota(jnp.int32, y.shape, 1)
    zero = jnp.zeros((), y.dtype)
    return jnp.concatenate([jnp.where(lane < c, y, zero), jnp.where(lane >= c, y, zero)], axis=0)


def _unit_tri_inverses_minus_identity(mats):
    c = mats[0].shape[0]
    ri = lax.broadcasted_iota(jnp.int32, (c, 2 * c), 0)
    ci = lax.broadcasted_iota(jnp.int32, (c, 2 * c), 1) % c

    def same_block(s):
        return (ri // s) == (ci // s)

    def mm(xs, ys):
        return [_dot(x.astype(BF16), _pair_blockdiag(y.astype(BF16))) for x, y in zip(xs, ys)]

    diag = same_block(8)
    b1 = [jnp.where(diag, -a, 0.0) for a in mats]
    b2 = mm(b1, b1)
    b3 = mm(b1, b2)
    b4 = mm(b2, b2)
    n2 = [x + y + z for x, y, z in zip(b1, b2, b3)]
    n = [x + y + z for x, y, z in zip(n2, b4, mm(n2, b4))]
    for s in (8, 16, 32):
        off = jnp.logical_and(same_block(2 * s), jnp.logical_not(same_block(s)))
        lo = [jnp.where(off, a, 0.0) for a in mats]
        x = [p + q for p, q in zip(lo, mm(n, lo))]
        n = [p - (q + r) for p, q, r in zip(n, x, mm(x, n))]
    return n


def _dn_wy_kernel(qkv_ref, gb_ref, u_ref, wq_ref, kq_ref, dl_ref, *, nchunk):
    c = CHUNK
    tm = nchunk * c
    gb = gb_ref[...]
    qkv = qkv_ref[...]
    rt = lax.broadcasted_iota(jnp.int32, (tm, tm), 0)
    ct = lax.broadcasted_iota(jnp.int32, (tm, tm), 1)
    same_chunk = (rt // c) == (ct // c)
    ri = lax.broadcasted_iota(jnp.int32, (c, 2 * c), 0)
    lane2 = lax.broadcasted_iota(jnp.int32, (c, 2 * c), 1)
    ci = lane2 % c
    first = lane2 < c
    lane = lax.broadcasted_iota(jnp.int32, (8, GP), 1)
    lane_k = lax.broadcasted_iota(jnp.int32, (c, 2 * DN_DH), 1)
    zk = jnp.zeros((), BF16)
    eye_p = (ri == ci).astype(F32)
    eye2 = (lax.broadcasted_iota(jnp.int32, (DN_DH, 2 * DN_DH), 1) % DN_DH
            == lax.broadcasted_iota(jnp.int32, (DN_DH, 2 * DN_DH), 0)).astype(BF16)
    gb_t = gb.T
    dl_rows, gcs = [], []
    for d in range(2):
        cum = jnp.logical_and(same_chunk, (ct <= rt) if d == 0 else (ct >= rt))
        gc_all = jnp.dot(cum.astype(F32), gb, preferred_element_type=F32, precision=lax.Precision.HIGHEST)
        gcs.append((gc_all, gc_all.T))
        dl_rows.append([jnp.exp(gc_all[j * c + (c - 1 if d == 0 else 0):j * c + (c if d == 0 else 1), :])
                        for j in range(nchunk)])
    for j0 in range(0, nchunk, WY_GROUP_CHUNKS):
        shared = {}
        for j in range(j0, j0 + WY_GROUP_CHUNKS):
            for pr in range(DN_HEADS // 2):
                r0 = j * c
                hs = slice(2 * pr * DN_DH, (2 * pr + 2) * DN_DH)
                bd = lambda x: jnp.concatenate([jnp.where(lane_k < DN_DH, x, zk), jnp.where(lane_k >= DN_DH, x, zk)],
                                               axis=0)
                q2 = qkv[r0:r0 + c, hs]
                k2 = qkv[r0:r0 + c, DN_W + hs.start:DN_W + hs.stop]
                v2 = qkv[r0:r0 + c, 2 * DN_W + hs.start:2 * DN_W + hs.stop]
                bd_k = bd(k2)
                shared[(j, pr)] = dict(q2=q2, bd_k=bd_k, bd_v=bd(v2),
                                       gqt=_dot_nt(jnp.concatenate([k2, q2, eye2], axis=0), bd_k))
        units = [(d, j, pr) for j in range(j0, j0 + WY_GROUP_CHUNKS) for d in range(2) for pr in range(DN_HEADS // 2)]
        pre = []
        for d, j, pr in units:
            gc_all, gc_all_t = gcs[d]
            incl = (ci <= ri) if d == 0 else (ci >= ri)
            strict = (ci < ri) if d == 0 else (ci > ri)
            r0 = j * c
            last = r0 + (c - 1 if d == 0 else 0)
            i0 = d * DN_HEADS + 2 * pr
            col = lambda x, a: x[r0:r0 + c, a:a + 1]
            row = lambda x, a: jnp.concatenate([x[a:a + 1, r0:r0 + c], x[a + 1:a + 2, r0:r0 + c]], axis=1)
            beta_p = jnp.where(first, col(gb, 2 * DN_HEADS + i0), col(gb, 2 * DN_HEADS + i0 + 1))
            gc_p = jnp.where(first, col(gc_all, i0), col(gc_all, i0 + 1))
            gc_row = row(gc_all_t, i0)
            beta_row = row(gb_t, 2 * DN_HEADS + i0)
            g_last = jnp.where(first[0:1], gc_all[last:last + 1, i0:i0 + 1], gc_all[last:last + 1, i0 + 1:i0 + 2])
            e = jnp.exp(jnp.where(incl, gc_p - gc_row, 0.0))
            eg_row = jnp.exp(gc_row)
            pre.append(dict(beta_p=beta_p, e_incl=jnp.where(incl, e, 0.0), e_strict=jnp.where(strict, e, 0.0),
                            beta_row=beta_row, eg_row=eg_row, tail_row=jnp.exp(g_last - gc_row)))
        gqt = [shared[(j, pr)]["gqt"] for d, j, pr in units]
        ns = _unit_tri_inverses_minus_identity([p["beta_p"] * x[:c] * p["e_strict"] for x, p in zip(gqt, pre)])
        tmat = {u_: n + eye_p for u_, n in zip(units, ns)}
        prd = dict(zip(units, pre))
        both = [(j, pr) for j in range(j0, j0 + WY_GROUP_CHUNKS) for pr in range(DN_HEADS // 2)]
        lhs_u = [jnp.concatenate([(tmat[(d, j, pr)] * prd[(d, j, pr)]["beta_row"]).astype(BF16) for d in range(2)], axis=0)
                 for j, pr in both]
        lhs_w = [jnp.concatenate([(tmat[(d, j, pr)] * (prd[(d, j, pr)]["beta_row"] * prd[(d, j, pr)]["eg_row"])
                                   ).astype(BF16) for d in range(2)], axis=0) for j, pr in both]
        u2 = dict(zip(both, [_dot(a, shared[jp]["bd_v"]) for a, jp in zip(lhs_u, both)]))
        w2 = dict(zip(both, [_dot(a, shared[jp]["bd_k"]) for a, jp in zip(lhs_w, both)]))
        for (d, j, pr), p, x in zip(units, pre, gqt):
            r0 = j * c
            i0 = d * DN_HEADS + 2 * pr
            cs = slice(i0 * DN_DH, (i0 + 2) * DN_DH)
            ks = slice(i0 * c, (i0 + 2) * c)
            q2 = shared[(j, pr)]["q2"].astype(F32)
            gcol = gcs[d][0][r0:r0 + c]
            eg = jnp.where(lane_k < DN_DH, jnp.exp(gcol[:, i0:i0 + 1]), jnp.exp(gcol[:, i0 + 1:i0 + 2]))
            u_ref[r0:r0 + c, cs] = u2[(j, pr)][d * c:(d + 1) * c].astype(BF16)
            wq_ref[2 * r0:2 * r0 + c, cs] = w2[(j, pr)][d * c:(d + 1) * c].astype(BF16)
            wq_ref[2 * r0 + c:2 * r0 + 2 * c, cs] = (q2 * eg).astype(BF16)
            kq_ref[3 * r0:3 * r0 + DN_DH, ks] = (x[2 * c:] * p["tail_row"]).astype(BF16)
            kq_ref[3 * r0 + DN_DH:3 * r0 + DN_DH + c, ks] = (x[c:2 * c] * p["e_incl"]).astype(BF16)
    for j in range(nchunk):
        row = jnp.where(lane < DN_HEADS, dl_rows[0][j], dl_rows[1][j])
        dl_ref[8 * j:8 * j + 8, :] = row


def _dn_wy(qkv, gb, nchunk):
    tm = nchunk * CHUNK
    nb = T_ALL // tm
    nch = 2 * DN_HEADS
    return pl.pallas_call(
        functools.partial(_dn_wy_kernel, nchunk=nchunk),
        grid=(nb,),
        in_specs=[pl.BlockSpec((tm, 3 * DN_W), lambda i: (i, 0)),
                  pl.BlockSpec((tm, GP), lambda i: (i, 0))],
        out_specs=[pl.BlockSpec((tm, nch * DN_DH), lambda i: (i, 0)),
                   pl.BlockSpec((2 * tm, nch * DN_DH), lambda i: (i, 0)),
                   pl.BlockSpec((3 * tm, nch * CHUNK), lambda i: (i, 0)),
                   pl.BlockSpec((8 * nchunk, GP), lambda i: (i, 0))],
        out_shape=[jax.ShapeDtypeStruct((T_ALL, nch * DN_DH), BF16),
                   jax.ShapeDtypeStruct((2 * T_ALL, nch * DN_DH), BF16),
                   jax.ShapeDtypeStruct((3 * T_ALL, nch * CHUNK), BF16),
                   jax.ShapeDtypeStruct((8 * T_ALL // CHUNK, GP), F32)],
        compiler_params=_cparams(("arbitrary",)),
        name="dn_wy",
    )(qkv, gb)


def _dn_scan_kernel(u_f_ref, wq_f_ref, kq_f_ref, dl_f_ref, u_b_ref, wq_b_ref, kq_b_ref, dl_b_ref,
                    of_ref, ob_ref, s_scr, *, nchunk):
    i = pl.program_id(0)

    @pl.when(i == 0)
    def _():
        s_scr[...] = jnp.zeros_like(s_scr)

    c = CHUNK
    states = [s_scr[idx] for idx in range(2 * DN_HEADS)]
    zero = jnp.zeros((c, DN_DH), BF16)
    refs = ((u_f_ref, wq_f_ref, kq_f_ref, dl_f_ref, of_ref), (u_b_ref, wq_b_ref, kq_b_ref, dl_b_ref, ob_ref))
    chains = [(d, h) for d in range(2) for h in range(DN_HEADS)]
    pairs = [(d, p) for d in range(2) for p in range(DN_HEADS // 2)]
    for step in range(nchunk):
        blk = lambda d: step if d == 0 else nchunk - 1 - step
        hs = lambda h: slice(h * DN_DH, (h + 1) * DN_DH)
        r1 = [_dot(refs[d][1][2 * blk(d) * c:2 * (blk(d) + 1) * c, hs(h)], states[d * DN_HEADS + h].astype(BF16))
              for d, h in chains]
        v_new = [(refs[d][0][blk(d) * c:(blk(d) + 1) * c, hs(h)].astype(F32) - r[:c]).astype(BF16)
                 for (d, h), r in zip(chains, r1)]
        r2 = [_dot(refs[d][2][3 * blk(d) * c:3 * (blk(d) + 1) * c, p * 2 * c:(p + 1) * 2 * c],
                   jnp.concatenate([jnp.concatenate([v_new[d * DN_HEADS + 2 * p], zero], axis=1),
                                    jnp.concatenate([zero, v_new[d * DN_HEADS + 2 * p + 1]], axis=1)], axis=0))
              for d, p in pairs]
        for d, h in chains:
            idx = d * DN_HEADS + h
            r = r2[d * (DN_HEADS // 2) + h // 2]
            ts = slice((h % 2) * DN_DH, (h % 2 + 1) * DN_DH)
            dl_row = refs[d][3][8 * blk(d):8 * blk(d) + 1, :]
            states[idx] = states[idx] * dl_row[:, idx:idx + 1] + r[:DN_DH, ts]
            refs[d][4][blk(d) * c:(blk(d) + 1) * c, hs(h)] = (r1[idx][c:] + r[DN_DH:, ts]).astype(BF16)
    for idx in range(2 * DN_HEADS):
        s_scr[idx] = states[idx]


def _dn_scan(u, wq, kq, dl, nchunk):
    tm = nchunk * CHUNK
    nb = T_ALL // tm
    nlat = T_LAT // tm
    fwd = lambda i: jnp.where(i == 0, nlat, i - 1)
    bwd = lambda i: nb - 1 - i
    specs = lambda f, col: [pl.BlockSpec((tm, DN_W), lambda i: (f(i), col)),
                            pl.BlockSpec((2 * tm, DN_W), lambda i: (f(i), col)),
                            pl.BlockSpec((3 * tm, DN_HEADS * CHUNK), lambda i: (f(i), col)),
                            pl.BlockSpec((8 * nchunk, GP), lambda i: (f(i), 0))]
    return pl.pallas_call(
        functools.partial(_dn_scan_kernel, nchunk=nchunk),
        grid=(nb,),
        in_specs=specs(fwd, 0) + specs(bwd, 1),
        out_specs=[pl.BlockSpec((tm, DN_W), lambda i: (fwd(i), 0)),
                   pl.BlockSpec((tm, DN_W), lambda i: (bwd(i), 0))],
        out_shape=[jax.ShapeDtypeStruct((T_ALL, DN_W), BF16),
                   jax.ShapeDtypeStruct((T_ALL, DN_W), BF16)],
        scratch_shapes=[pltpu.VMEM((2 * DN_HEADS, DN_DH, DN_DH), F32)],
        compiler_params=_cparams(("arbitrary",)),
        name="dn_scan",
    )(u, wq, kq, dl, u, wq, kq, dl)


def _dft_cos_sin(n):
    k = np.arange(n)
    ang = 2.0 * np.pi * ((k[:, None] * k[None, :]) % n) / n
    return np.cos(ang), np.sin(ang)


def _fft_consts():
    n = FFT_N
    c, s = _dft_cos_sin(n)
    cs_ch = np.concatenate([c, s], axis=1)
    w1 = np.block([[c, -s], [-s, -c]])
    k2 = np.arange(n)[None, :, None]
    t2 = np.arange(n)[:, None, None]
    ang = 2.0 * np.pi * ((k2 * t2) % (n * n)) / (n * n)
    scale = 1.0 / math.sqrt(T_LAT * FN_DG)
    cc, sc = _dft_cos_sin(T_CTX)
    scale_c = 1.0 / math.sqrt(T_CTX * FN_DG)
    f32 = lambda a: jnp.asarray(a, F32)
    bf = lambda a: f32(a).astype(BF16)
    return dict(cs_ch=bf(cs_ch), w1=bf(w1), twc=f32(np.cos(ang)), tws=f32(np.sin(ang)),
                c2=bf(c * scale), s2=bf(s * scale), cc=bf(cc * scale_c), sc=bf(sc * scale_c))


def _fft1_kernel(u_ref, cs_ref, w1_ref, twc_ref, tws_ref, y_ref, *, n_t2):
    n = FFT_N
    cs = cs_ref[...]
    w1 = w1_ref[...]
    for t in range(n_t2):
        twc = twc_ref[t]
        tws = tws_ref[t]
        for g in range(0, FN_GROUPS, 2):
            c0 = (t * FN_GROUPS + g) * FN_DG
            ab = [_dot(u_ref[:, c0 + s * FN_DG:c0 + (s + 1) * FN_DG], cs) for s in range(2)]
            rhs = jnp.concatenate([jnp.concatenate([ab[0][:, :FN_DG], ab[1][:, :FN_DG]], axis=1),
                                   jnp.concatenate([ab[0][:, FN_DG:], ab[1][:, FN_DG:]], axis=1)], axis=0)
            y = _dot(w1, rhs.astype(BF16))
            yr, yi = y[:n], y[n:]
            y_ref[0:n, c0:c0 + 2 * FN_DG] = (yr * twc + yi * tws).astype(BF16)
            y_ref[n:2 * n, c0:c0 + 2 * FN_DG] = (yi * twc - yr * tws).astype(BF16)


def _fft2_kernel(y_ref, c2_ref, s2_ref, o_ref, *, n_k2):
    c2 = c2_ref[...]
    s2 = s2_ref[...]
    for j in range(n_k2):
        o_ref[:, j * FN_W:(j + 1) * FN_W] = (_dot(c2, y_ref[0, j]) + _dot(s2, y_ref[1, j])).astype(BF16)


def _fft_ctx_kernel(u_ref, cs_ref, cc_ref, sc_ref, o_in_ref, o_ref):
    del o_in_ref
    for g in range(FN_GROUPS):
        ab = _dot(u_ref[:, g * FN_DG:(g + 1) * FN_DG], cs_ref[...])
        a = ab[:, :FN_DG].astype(BF16)
        b = ab[:, FN_DG:].astype(BF16)
        o_ref[:, g * FN_DG:(g + 1) * FN_DG] = (_dot(cc_ref[...], a) - _dot(sc_ref[...], b)).astype(BF16)


def _fnet(u, fc, with_ctx):
    n = FFT_N
    row_w = n * FN_W
    n_t2 = 8
    tc = n_t2 * FN_W
    y = pl.pallas_call(
        functools.partial(_fft1_kernel, n_t2=n_t2),
        grid=(row_w // tc,),
        in_specs=[pl.BlockSpec((n, tc), lambda j: (0, j)),
                  pl.BlockSpec((n, 2 * n), lambda j: (0, 0)),
                  pl.BlockSpec((2 * n, 2 * n), lambda j: (0, 0)),
                  pl.BlockSpec((n_t2, n, 1), lambda j: (j, 0, 0)),
                  pl.BlockSpec((n_t2, n, 1), lambda j: (j, 0, 0))],
        out_specs=pl.BlockSpec((2 * n, tc), lambda j: (0, j)),
        out_shape=jax.ShapeDtypeStruct((2 * n, row_w), BF16),
        compiler_params=_cparams(("arbitrary",)),
        name="fnet_stage1",
    )(u.reshape(T_ALL // n, row_w), fc["cs_ch"], fc["w1"], fc["twc"], fc["tws"])
    n_k2 = 8
    o = pl.pallas_call(
        functools.partial(_fft2_kernel, n_k2=n_k2),
        grid=(n // n_k2,),
        in_specs=[pl.BlockSpec((2, n_k2, n, FN_W), lambda j: (0, j, 0, 0)),
                  pl.BlockSpec((n, n), lambda j: (0, 0)),
                  pl.BlockSpec((n, n), lambda j: (0, 0))],
        out_specs=pl.BlockSpec((n, n_k2 * FN_W), lambda j: (0, j)),
        out_shape=jax.ShapeDtypeStruct(((T_ALL if with_ctx else T_LAT) // n, row_w), BF16),
        compiler_params=_cparams(("arbitrary",)),
        name="fnet_stage2",
    )(y.reshape(2, n, n, FN_W), fc["c2"], fc["s2"])
    if not with_ctx:
        return o.reshape(T_LAT, FN_W)
    o = o.reshape(T_ALL, FN_W)
    cb = T_LAT // T_CTX
    return pl.pallas_call(
        _fft_ctx_kernel,
        grid=(1,),
        in_specs=[pl.BlockSpec((T_CTX, FN_W), lambda j: (cb, 0)),
                  pl.BlockSpec((n, 2 * n), lambda j: (0, 0)),
                  pl.BlockSpec((T_CTX, T_CTX), lambda j: (0, 0)),
                  pl.BlockSpec((T_CTX, T_CTX), lambda j: (0, 0)),
                  pl.BlockSpec(memory_space=pl.ANY)],
        out_specs=pl.BlockSpec((T_CTX, FN_W), lambda j: (cb, 0)),
        out_shape=jax.ShapeDtypeStruct((T_ALL, FN_W), BF16),
        input_output_aliases={4: 0},
        compiler_params=_cparams(("arbitrary",)),
        name="fnet_ctx",
    )(u, fc["cs_ch"], fc["cc"], fc["sc"], o)


def _merge_kernel(*refs, tm, has_ctx, split):
    if split:
        (x_ref, ctx_ref, mod_ref, n1_ref, ona_ref, of_ref, ob_ref, z_ref, ofn_ref, dnn_ref,
         wg_ref, wna_ref, wdn_ref, wfn_ref, wout_ref, o_ref, tail_scr) = refs
    else:
        (x_ref, mod_ref, n1_ref, ona_ref, of_ref, ob_ref, z_ref, ofn_ref, dnn_ref,
         wg_ref, wna_ref, wdn_ref, wfn_ref, wout_ref, o_ref) = refs
        ctx_ref = tail_scr = None
    i = pl.program_id(0)
    tail_is_ctx = i == pl.num_programs(0) - 1
    x = _token_tile(x_ref, ctx_ref, tail_scr, tail_is_ctx)
    h = _modnorm_tile(x, n1_ref[...], mod_ref, 0, 1, tail_is_ctx, has_ctx).astype(BF16)
    o = of_ref[...].astype(F32) + ob_ref[...].astype(F32)
    z = z_ref[...].astype(F32)
    parts = []
    for hd in range(DN_HEADS):
        sl = slice(hd * DN_DH, (hd + 1) * DN_DH)
        oh = o[:, sl]
        oh = oh * lax.rsqrt(jnp.mean(oh * oh, axis=-1, keepdims=True) + EPS) * dnn_ref[...]
        parts.append(oh * _silu(z[:, sl]))
    odn = jnp.concatenate(parts, axis=-1).astype(BF16)
    branches = ((ona_ref[...], wna_ref), (odn, wdn_ref), (ofn_ref[...], wfn_ref))
    logits = [_dot(h, wg_ref[:, b * D:(b + 1) * D]) for b in range(N_BRANCH)]
    proj = [_dot(br, w_ref[...]) for br, w_ref in branches]
    y = jax.nn.sigmoid(logits[0]) * proj[0]
    for b in range(1, N_BRANCH):
        y = y + jax.nn.sigmoid(logits[b]) * proj[b]
    y = _dot(y.astype(BF16), wout_ref[...])
    o_ref[...] = _gated_residual(x, y, mod_ref, 2, tail_is_ctx, has_ctx)


def _merge(x, ctx, mod, norm1, p, o_na, o_f, o_b, o_fn, dn_norm, w_gate, layer, w_na_o, w_dn_o, w_fn, w_out, has_ctx):
    rows, tm = (T_ALL, TM_MERGE_ALL) if has_ctx else (T_LAT, TM_MERGE_LAT)
    split = ctx is not None
    row = lambda w: pl.BlockSpec((tm, w), lambda i: (i, 0))
    full = lambda a: pl.BlockSpec(a.shape, lambda i: (0, 0), pipeline_mode=pl.Buffered(1))
    ctx_specs = [pl.BlockSpec((T_CTX, D), lambda i: (0, 0))] if split else []
    return pl.pallas_call(
        functools.partial(_merge_kernel, tm=tm, has_ctx=has_ctx, split=split),
        grid=(rows // tm,),
        in_specs=[row(D)] + ctx_specs + [
                  pl.BlockSpec((8, 6 * D), lambda i: (0, 0)), pl.BlockSpec((1, D), lambda i: (0, 0)),
                  row(NA_W), row(DN_W), row(DN_W),
                  pl.BlockSpec((tm, DN_W), lambda i: (i, P_Z)),
                  row(FN_W), pl.BlockSpec((1, DN_DH), lambda i: (0, 0)),
                  pl.BlockSpec((None,) + w_gate.shape[1:], lambda i: (layer, 0, 0), pipeline_mode=pl.Buffered(1)),
                  full(w_na_o), full(w_dn_o), full(w_fn), full(w_out)],
        out_specs=row(D),
        out_shape=jax.ShapeDtypeStruct((rows, D), F32),
        scratch_shapes=[pltpu.VMEM((T_CTX, D), F32)] if split else [],
        compiler_params=_cparams(("arbitrary",)),
        name="merge",
    )(*([x, ctx] if split else [x]), mod, norm1.reshape(1, D), o_na, o_f, o_b, p, o_fn, dn_norm.reshape(1, DN_DH),
      w_gate, w_na_o, w_dn_o, w_fn, w_out)


def _mlp_kernel(x_ref, xnext_ref, mod_ref, n_ref, w1_ref, w2_ref, nf_ref, o_ref, h_scr, acc_scr, *,
                tm, has_ctx, final):
    i = pl.program_id(0)
    j = pl.program_id(1)
    nt = pl.num_programs(0)
    nj = pl.num_programs(1)
    slot = i % 2

    @pl.when(jnp.logical_and(i == 0, j == 0))
    def _():
        h_scr[0] = _modnorm_tile(x_ref[...], n_ref[...], mod_ref, 3, 4, nt == 1, has_ctx).astype(BF16)

    def hidden_step(first):
        a = jnp.maximum(_dot(h_scr[slot], w1_ref[...]), 0.0)
        upd = _dot((a * a).astype(BF16), w2_ref[...])
        if first:
            acc_scr[...] = upd
        else:
            acc_scr[...] += upd

    @pl.when(j == 0)
    def _():
        hidden_step(True)

    @pl.when(jnp.logical_and(j > 0, j < nj - 1))
    def _():
        hidden_step(False)

    @pl.when(j == nj - 1)
    def _():
        h_scr[1 - slot] = _modnorm_tile(xnext_ref[...], n_ref[...], mod_ref, 3, 4, i + 1 == nt - 1, has_ctx).astype(BF16)
        hidden_step(False)
        xn = _gated_residual(x_ref[...], acc_scr[...], mod_ref, 5, i == nt - 1, has_ctx)
        if final:
            xn = xn * lax.rsqrt(jnp.mean(xn * xn, axis=-1, keepdims=True) + EPS) * nf_ref[...]
        o_ref[...] = xn


def _mlp(xs, mod, norm, w1, w2, norm_f, has_ctx, final):
    rows, tm = (T_ALL, TM_ALL) if has_ctx else (T_LAT, TM_LAT)
    th = 1024
    return pl.pallas_call(
        functools.partial(_mlp_kernel, tm=tm, has_ctx=has_ctx, final=final),
        grid=(rows // tm, HID // th),
        in_specs=[pl.BlockSpec((tm, D), lambda i, j: (i, 0)),
                  pl.BlockSpec((tm, D), lambda i, j: (jnp.minimum(i + 1, rows // tm - 1), 0)),
                  pl.BlockSpec((8, 6 * D), lambda i, j: (0, 0)),
                  pl.BlockSpec((1, D), lambda i, j: (0, 0)),
                  pl.BlockSpec((D, th), lambda i, j: (0, j)),
                  pl.BlockSpec((th, D), lambda i, j: (j, 0)),
                  pl.BlockSpec((1, D), lambda i, j: (0, 0))],
        out_specs=pl.BlockSpec((tm, D), lambda i, j: (i, 0)),
        out_shape=jax.ShapeDtypeStruct((rows, D), F32),
        scratch_shapes=[pltpu.VMEM((2, tm, D), BF16), pltpu.VMEM((tm, D), F32)],
        compiler_params=_cparams(("arbitrary", "arbitrary")),
        name="mlp",
    )(xs, xs, mod, norm.reshape(1, D), w1, w2, norm_f.reshape(1, D))


IN_W = 2 * NA_W + 2 * DN_W + 4 * DN_HEADS + NA_W + 2 * DN_W + FN_W + N_BRANCH * D


W_PREP_BLK = 512
_W_SRC = [2 * NA_W + 2 * DN_W + 4 * DN_HEADS, 0, NA_W, 3 * NA_W + 2 * DN_W + 4 * DN_HEADS, 2 * NA_W, 2 * NA_W + DN_W,
          3 * NA_W + 3 * DN_W + 4 * DN_HEADS, 3 * NA_W + 4 * DN_W + 4 * DN_HEADS] + [
          IN_W - N_BRANCH * D + i * W_PREP_BLK for i in range(N_BRANCH * D // W_PREP_BLK)]
_W_AB = 2 * NA_W + 2 * DN_W


def _w_prep_kernel(w_ref, ab_ref, wcat_ref, wgate_ref, wg_ref):
    c = pl.program_id(1)
    ncat = P_W // W_PREP_BLK
    blk = w_ref[0].T.astype(BF16)

    @pl.when(c < ncat)
    def _():
        wcat_ref[...] = blk

    @pl.when(c >= ncat)
    def _():
        wgate_ref[...] = blk

    @pl.when(c == 0)
    def _():
        ab = jnp.concatenate([ab_ref[0], jnp.zeros((GP - ab_ref.shape[1], D), F32)], axis=0)
        wg_ref[...] = ab.T.astype(BF16)


def _split_in_weights(w_in):
    ncat = P_W // W_PREP_BLK
    nblk = len(_W_SRC)

    def src(c):
        off = 0
        for k, o in enumerate(_W_SRC):
            off = off + jnp.where(c == k, o // 16, 0)
        return off * 16

    return pl.pallas_call(
        _w_prep_kernel,
        grid=(DEPTH, nblk),
        in_specs=[pl.BlockSpec((pl.Element(1), pl.Element(W_PREP_BLK), pl.Element(D)), lambda l, c: (l, src(c), 0)),
                  pl.BlockSpec((pl.Element(1), pl.Element(4 * DN_HEADS), pl.Element(D)), lambda l, c: (l, _W_AB, 0))],
        out_specs=[pl.BlockSpec((None, D, W_PREP_BLK), lambda l, c: (l, 0, jnp.minimum(c, ncat - 1))),
                   pl.BlockSpec((None, D, W_PREP_BLK), lambda l, c: (l, 0, jnp.maximum(c - ncat, 0))),
                   pl.BlockSpec((None, D, GP), lambda l, c: (l, 0, 0))],
        out_shape=[jax.ShapeDtypeStruct((DEPTH, D, P_W), BF16),
                   jax.ShapeDtypeStruct((DEPTH, D, N_BRANCH * D), BF16),
                   jax.ShapeDtypeStruct((DEPTH, D, GP), BF16)],
        compiler_params=_cparams(("arbitrary", "arbitrary")),
        name="w_prep",
    )(jnp.swapaxes(w_in, 1, 2), jnp.swapaxes(w_in, 1, 2))


def kernel(x, c, ctx, c_ctx, w_ada, b_ada, norm1, w_in, conv_w, a_log, dt_bias, dn_norm, rpb,
           w_na_o, w_dn_o, w_fn, w_out, norm2, w_mlp1, w_mlp2, norm_f):
    xs, xc = x[0], ctx[0]
    cc = jnp.concatenate([c, c_ctx[None, :], jnp.zeros((6, D), F32)], axis=0)
    mods = _ada(cc, w_ada, b_ada)
    rope = _rope_tables()
    fc = _fft_consts()
    na_tab = _na_tables(rpb)
    wcat, w_gate, wg = _split_in_weights(w_in)
    for l in range(DEPTH):
        has_ctx = l < DEPTH - 1
        final = l == DEPTH - 1
        u, p, g = _in_proj(xs, xc, mods[l], norm1[l], wcat, wg, l)
        o_na = _na(p, na_tab, l, has_ctx)
        qkv, gb = _dn_prep(p, g, conv_w[l], a_log[l], dt_bias[l], rope)
        o_f, o_b = _dn_scan(*_dn_wy(qkv, gb, DN_BLK_CHUNKS), DN_BLK_CHUNKS)
        o_fn = _fnet(u, fc, has_ctx)
        xs, xc = _merge(xs, xc, mods[l], norm1[l], p, o_na, o_f, o_b, o_fn, dn_norm[l], w_gate, l,
                    w_na_o[l].astype(BF16), w_dn_o[l].astype(BF16), w_fn[l].astype(BF16), w_out[l].astype(BF16),
                    has_ctx), None
        xs = _mlp(xs, mods[l], norm2[l], w_mlp1[l].astype(BF16), w_mlp2[l].astype(BF16), norm_f, has_ctx, final)
    return xs[None]
```

```python
import functools
import math

import numpy as np
import jax
import jax.numpy as jnp
from jax import lax
from jax.experimental import pallas as pl
from jax.experimental.pallas import tpu as pltpu

F32 = jnp.float32
BF16 = jnp.bfloat16

D = 1024
T_LAT = 16384
T_CTX = 256
T_ALL = T_LAT + T_CTX
DEPTH = 2
GRID_W = 64
GRID_H = T_LAT // GRID_W
NA_HEADS = 8
NA_DH = 64
NA_WIN_H = 8
NA_WIN_W = 16
NA_W = NA_HEADS * NA_DH
DN_HEADS = 4
DN_DH = 128
DN_W = DN_HEADS * DN_DH
DN_CONV = 5
CHUNK = 64
FN_GROUPS = 4
FN_DG = 128
FN_W = FN_GROUPS * FN_DG
N_BRANCH = 3
HID = 4 * D
ROPE_BASE = 10000.0
EPS = 1e-6
NEG = -1e30

P_W = 3 * NA_W + 4 * DN_W + FN_W
P_NAQ, P_NAK, P_NAV = 0, 1, 2
P_DN = 1
P_Z = 6
TN_IN = P_W // 2
GP = 128

TM_ALL = 1280
TM_LAT = 1024
TM_MERGE_ALL = 640
TM_MERGE_LAT = 512
NA_BLK = 4 * GRID_W
NA_HEAD_GROUP = 4
NA_VARIANTS = 4
FFT_N = 128
DN_BLK_CHUNKS = 4
WY_GROUP_CHUNKS = 4
VMEM_LIMIT = 56 * 1024 * 1024


def _cparams(sem):
    return pltpu.CompilerParams(dimension_semantics=sem, vmem_limit_bytes=VMEM_LIMIT)


def _dot(a, b):
    return jnp.dot(a, b, preferred_element_type=F32)


def _dot_nt(a, b):
    return lax.dot_general(a, b, (((1,), (1,)), ((), ())), preferred_element_type=F32)


def _silu(x):
    return x * jax.nn.sigmoid(x)


def _ada_kernel(c_ref, w_ref, b_ref, o_ref):
    s = _silu(c_ref[...])
    s_hi = s.astype(BF16)
    s_lo = (s - s_hi.astype(F32)).astype(BF16)
    w = w_ref[0].astype(BF16)
    o_ref[0] = _dot(s_hi, w) + _dot(s_lo, w) + b_ref[0]


def _ada(cc, w_ada, b_ada):
    tn = 1536
    return pl.pallas_call(
        _ada_kernel,
        grid=(DEPTH, 6 * D // tn),
        in_specs=[pl.BlockSpec((8, D), lambda l, j: (0, 0)),
                  pl.BlockSpec((1, D, tn), lambda l, j: (l, 0, j)),
                  pl.BlockSpec((1, 1, tn), lambda l, j: (l, 0, j))],
        out_specs=pl.BlockSpec((1, 8, tn), lambda l, j: (l, 0, j)),
        out_shape=jax.ShapeDtypeStruct((DEPTH, 8, 6 * D), F32),
        compiler_params=_cparams(("arbitrary", "arbitrary")),
        name="ada_mod",
    )(cc, w_ada, b_ada.reshape(DEPTH, 1, 6 * D))


def _row_slices(tm, has_ctx):
    return [slice(0, tm)] if not has_ctx else [slice(0, tm - T_CTX), slice(tm - T_CTX, tm)]


def _mod_vecs(mod_ref, k, tail_is_ctx, has_ctx):
    lat = mod_ref[0:1, k * D:(k + 1) * D]
    if not has_ctx:
        return [lat]
    return [lat, jnp.where(tail_is_ctx, mod_ref[1:2, k * D:(k + 1) * D], lat)]


def _modnorm(x, g, shift, scale):
    y = x * lax.rsqrt(jnp.mean(x * x, axis=-1, keepdims=True) + EPS)
    return y * (g * (1.0 + scale)) + shift


def _modnorm_tile(x, g, mod_ref, k_shift, k_scale, tail_is_ctx, has_ctx):
    tm = x.shape[0]
    parts = [_modnorm(x[sl], g, sh, sc) for sl, sh, sc in zip(_row_slices(tm, has_ctx),
                                                            _mod_vecs(mod_ref, k_shift, tail_is_ctx, has_ctx),
                                                            _mod_vecs(mod_ref, k_scale, tail_is_ctx, has_ctx))]
    return parts[0] if len(parts) == 1 else jnp.concatenate(parts, axis=0)


def _gated_residual(x, y, mod_ref, k_gate, tail_is_ctx, has_ctx):
    tm = x.shape[0]
    parts = [x[sl] + gt * y[sl] for sl, gt in zip(_row_slices(tm, has_ctx), _mod_vecs(mod_ref, k_gate, tail_is_ctx, has_ctx))]
    return parts[0] if len(parts) == 1 else jnp.concatenate(parts, axis=0)


def _token_tile(x_ref, ctx_ref, tail_scr, is_last):
    if ctx_ref is None:
        return x_ref[...]
    top = x_ref.shape[0] - T_CTX

    @pl.when(jnp.logical_not(is_last))
    def _():
        tail_scr[...] = x_ref[top:, :]

    @pl.when(is_last)
    def _():
        tail_scr[...] = ctx_ref[...]

    return jnp.concatenate([x_ref[0:top, :], tail_scr[...]], axis=0)


def _in_proj_kernel(*refs, tm, split):
    if split:
        x_ref, ctx_ref, mod_ref, n_ref, w_ref, wg_ref, u_ref, p_ref, g_ref, h_scr, tail_scr = refs
    else:
        x_ref, mod_ref, n_ref, w_ref, wg_ref, u_ref, p_ref, g_ref, h_scr = refs
        ctx_ref = tail_scr = None
    i = pl.program_id(0)
    j = pl.program_id(1)

    @pl.when(j == 0)
    def _():
        is_last = i == pl.num_programs(0) - 1
        x = _token_tile(x_ref, ctx_ref, tail_scr, is_last)
        h = _modnorm_tile(x, n_ref[...], mod_ref, 0, 1, is_last, True).astype(BF16)
        h_scr[...] = h
        g_ref[...] = _dot(h, wg_ref[...])

    p = _dot(h_scr[...], w_ref[...]).astype(BF16)
    p_ref[...] = p

    @pl.when(j == pl.num_programs(1) - 1)
    def _():
        u_ref[...] = p[:, TN_IN - FN_W:]


def _in_proj(x, ctx, mod, norm, wcat, wg, layer):
    tm = TM_ALL
    nj = P_W // TN_IN
    split = ctx is not None
    ctx_specs = [pl.BlockSpec((T_CTX, D), lambda i, j: (0, 0))] if split else []
    return pl.pallas_call(
        functools.partial(_in_proj_kernel, tm=tm, split=split),
        grid=(T_ALL // tm, nj),
        in_specs=[pl.BlockSpec((tm, D), lambda i, j: (i, 0))] + ctx_specs + [
                  pl.BlockSpec((8, 6 * D), lambda i, j: (0, 0)),
                  pl.BlockSpec((1, D), lambda i, j: (0, 0)),
                  pl.BlockSpec((None, D, TN_IN), lambda i, j: (layer, 0, j)),
                  pl.BlockSpec((None, D, GP), lambda i, j: (layer, 0, 0))],
        out_specs=[pl.BlockSpec((tm, FN_W), lambda i, j: (i, 0)),
                   pl.BlockSpec((tm, TN_IN), lambda i, j: (i, j)),
                   pl.BlockSpec((tm, GP), lambda i, j: (i, 0))],
        out_shape=[jax.ShapeDtypeStruct((T_ALL, FN_W), BF16),
                   jax.ShapeDtypeStruct((T_ALL, P_W), BF16),
                   jax.ShapeDtypeStruct((T_ALL, GP), F32)],
        scratch_shapes=[pltpu.VMEM((tm, D), BF16)] + ([pltpu.VMEM((T_CTX, D), F32)] if split else []),
        compiler_params=_cparams(("arbitrary", "arbitrary")),
        name="in_proj",
    )(*([x, ctx] if split else [x]), mod, norm.reshape(1, D), wcat, wg)


def _na_tables(rpb):
    qr = np.arange(4)[:, None, None, None]
    qc = np.arange(GRID_W)[None, :, None, None]
    kk = np.arange(12)[None, None, :, None]
    kc = np.arange(GRID_W)[None, None, None, :]
    valid = []
    for b in (0, 1, GRID_H // 4 - 1):
        r = 4 * b + qr
        kr = 4 * (b - 1) + kk
        rs = np.clip(r - NA_WIN_H // 2, 0, GRID_H - NA_WIN_H)
        cs = np.clip(qc - NA_WIN_W // 2, 0, GRID_W - NA_WIN_W)
        ok = (kr >= 0) & (kr < GRID_H) & (kr >= rs) & (kr < rs + NA_WIN_H) & (kc >= cs) & (kc < cs + NA_WIN_W)
        valid.append(np.broadcast_to(ok, (4, GRID_W, 12, GRID_W)).reshape(NA_BLK, 12 * GRID_W))
    valid.append(np.zeros_like(valid[0]))
    valid = np.stack(valid)
    col = np.arange(GRID_W)
    onehot = ((col[None, None, :] - col[None, :, None] + NA_WIN_W - 1)
              == np.arange(2 * NA_WIN_W - 1)[:, None, None]).astype(np.float32)
    tcol = jnp.einsum('lhrd,dqk->lhrqk', rpb, onehot, precision=lax.Precision.HIGHEST)
    tcol = tcol.reshape(DEPTH * NA_HEADS, 2 * NA_WIN_H - 1, GRID_W, GRID_W)

    def assemble(tcol_ref, valid_ref, tab_ref):
        bias = jnp.concatenate([jnp.concatenate([tcol_ref[k - q + NA_WIN_H - 5] for k in range(12)], axis=1)
                                for q in range(4)], axis=0)
        for var in range(NA_VARIANTS):
            tab_ref[var] = jnp.where(valid_ref[var] > 0, bias, NEG).astype(BF16)

    return pl.pallas_call(
        assemble,
        grid=(DEPTH * NA_HEADS,),
        in_specs=[pl.BlockSpec((None, 2 * NA_WIN_H - 1, GRID_W, GRID_W), lambda i: (i, 0, 0, 0)),
                  pl.BlockSpec((NA_VARIANTS, NA_BLK, 12 * GRID_W), lambda i: (0, 0, 0))],
        out_specs=pl.BlockSpec((NA_VARIANTS, None, NA_BLK, 12 * GRID_W), lambda i: (i // NA_HEADS, i % NA_HEADS, 0, 0)),
        out_shape=jax.ShapeDtypeStruct((DEPTH * NA_VARIANTS, NA_HEADS, NA_BLK, 12 * GRID_W), BF16),
        compiler_params=_cparams(("arbitrary",)),
        name="na_tables",
    )(tcol, jnp.asarray(valid, F32))


def _na_kernel(q_ref, k0_ref, k1_ref, k2_ref, v0_ref, v1_ref, v2_ref, kc_ref, vc_ref, tab_ref, o_ref):
    q = q_ref[...]
    kl = jnp.concatenate([k0_ref[...], k1_ref[...], k2_ref[...]], axis=0)
    vl = jnp.concatenate([v0_ref[...], v1_ref[...], v2_ref[...]], axis=0)
    kc = kc_ref[...]
    vc = vc_ref[...]
    lane_l = lax.broadcasted_iota(jnp.int32, (vl.shape[0], 2 * NA_DH), 1)
    lane_c = lax.broadcasted_iota(jnp.int32, (vc.shape[0], 2 * NA_DH), 1)
    one = jnp.ones((), BF16)
    outs = []
    for h0 in range(0, NA_HEADS, NA_HEAD_GROUP):
        heads = range(h0, h0 + NA_HEAD_GROUP)
        sls = [slice(h * NA_DH, (h + 1) * NA_DH) for h in heads]
        pls = [slice((h // 2) * 2 * NA_DH, (h // 2 + 1) * 2 * NA_DH) for h in heads]
        own = [(lambda ln, h=h: (ln < NA_DH) if h % 2 == 0 else (ln >= NA_DH)) for h in heads]
        qs = [q[:, sl] * (NA_DH ** -0.5) for sl in sls]
        z_loc = [_dot_nt(qh, kl[:, sl]).astype(BF16) + tab_ref[0, h] for h, qh, sl in zip(heads, qs, sls)]
        z_ctx = [_dot_nt(qh, kc[:, sl]).astype(BF16) for qh, sl in zip(qs, sls)]
        m = [jnp.maximum(jnp.max(a, axis=-1, keepdims=True), jnp.max(b, axis=-1, keepdims=True))
             for a, b in zip(z_loc, z_ctx)]
        p_loc = [jnp.exp(a - mm) for a, mm in zip(z_loc, m)]
        p_ctx = [jnp.exp(b - mm) for b, mm in zip(z_ctx, m)]
        oa = [_dot(a, jnp.where(f(lane_l), vl[:, ps], one)) + _dot(b, jnp.where(f(lane_c), vc[:, ps], one))
              for a, b, ps, f in zip(p_loc, p_ctx, pls, own)]
        outs += [(x[:, :NA_DH] / x[:, NA_DH:NA_DH + 1]) if h % 2 == 0 else (x[:, NA_DH:] / x[:, 0:1])
                 for h, x in zip(heads, oa)]
    o_ref[...] = jnp.concatenate(outs, axis=-1).astype(BF16)


def _na(p, tab, layer, with_ctx):
    nlat = T_LAT // NA_BLK
    nb = nlat + (1 if with_ctx else 0)
    ctx_blk = T_LAT // NA_BLK

    def kidx(b, part):
        return jnp.clip(jnp.minimum(b, nlat - 1) - 1 + part, 0, nlat - 1)

    def variant(b):
        return layer * NA_VARIANTS + jnp.where(b == 0, 0, jnp.where(b == nlat - 1, 2, jnp.where(b >= nlat, 3, 1)))

    blk = (NA_BLK, NA_W)
    in_specs = [pl.BlockSpec(blk, lambda b: (b, P_NAQ))]
    in_specs += [pl.BlockSpec(blk, functools.partial(lambda b, part: (kidx(b, part), P_NAK), part=part))
                 for part in range(3)]
    in_specs += [pl.BlockSpec(blk, functools.partial(lambda b, part: (kidx(b, part), P_NAV), part=part))
                 for part in range(3)]
    in_specs += [pl.BlockSpec(blk, lambda b: (ctx_blk, P_NAK)),
                 pl.BlockSpec(blk, lambda b: (ctx_blk, P_NAV)),
                 pl.BlockSpec((1, NA_HEADS, NA_BLK, 3 * NA_BLK), lambda b: (variant(b), 0, 0, 0))]
    return pl.pallas_call(
        _na_kernel,
        grid=(nb,),
        in_specs=in_specs,
        out_specs=pl.BlockSpec(blk, lambda b: (b, 0)),
        out_shape=jax.ShapeDtypeStruct((nb * NA_BLK, NA_W), BF16),
        compiler_params=_cparams(("arbitrary",)),
        name="na_attn",
    )(p, p, p, p, p, p, p, p, p, tab)


def _rope_tables():
    half = DN_DH // 2
    inv = 1.0 / (ROPE_BASE ** (np.arange(0, half, 2, dtype=np.float64) / half))
    inv = np.concatenate([inv, inv])
    low = np.arange(half) < half // 2

    def tabs(n):
        ang = np.arange(n, dtype=np.float64)[:, None] * inv[None, :]
        return np.stack([np.cos(ang), np.where(low, -np.sin(ang), 0.0), np.where(low, 0.0, np.sin(ang))])

    rows = tabs(GRID_H).reshape(3, GRID_H // 4, 4, half)
    return jnp.asarray(rows, F32), jnp.asarray(tabs(GRID_W), F32)


def _dn_prep_kernel(x_ref, prev_ref, next_ref, g_ref, cw_ref, alog_ref, dtb_ref, rt_ref, ct_ref,
                    qkv_ref, gb_ref, *, tm):
    b = pl.program_id(0)
    nlat = T_LAT // tm
    first = jnp.logical_or(b == 0, b == nlat)
    last = b >= nlat - 1
    is_ctx = b >= nlat
    cur = x_ref[...].astype(F32)
    prev = jnp.where(first, 0.0, prev_ref[...][14:16].astype(F32))
    nxt = jnp.where(last, 0.0, next_ref[...][0:2].astype(F32))
    ext = jnp.concatenate([prev, cur, nxt], axis=0)
    cw = cw_ref[...]
    y = ext[0:tm] * cw[0:1]
    for k in range(1, DN_CONV):
        y = y + ext[k:k + tm] * cw[k:k + 1]
    y = _silu(y)
    def rope_factor(k):
        col = ct_ref[k]
        return jnp.concatenate([jnp.concatenate([jnp.broadcast_to(rt_ref[k, r:r + 1, :], col.shape), col], axis=1)
                                for r in range(tm // GRID_W)], axis=0)

    cos = jnp.where(is_ctx, 1.0, rope_factor(0))
    sa = jnp.where(is_ctx, 0.0, rope_factor(1))
    sb = jnp.where(is_ctx, 0.0, rope_factor(2))
    quarter = DN_DH // 4
    for s in range(2):
        for h in range(DN_HEADS):
            c0 = s * DN_W + h * DN_DH
            xh = y[:, c0:c0 + DN_DH]
            xh = xh * lax.rsqrt(jnp.sum(xh * xh, axis=-1, keepdims=True) + EPS)
            xh = (xh * cos + pltpu.roll(xh, DN_DH - quarter, axis=1) * sa + pltpu.roll(xh, quarter, axis=1) * sb)
            if s == 0:
                xh = xh * (DN_DH ** -0.5)
            qkv_ref[:, c0:c0 + DN_DH] = xh.astype(BF16)
    qkv_ref[:, 2 * DN_W:] = y[:, 2 * DN_W:].astype(BF16)
    g = g_ref[...]
    lane = lax.broadcasted_iota(jnp.int32, g.shape, 1)
    decay = -jnp.exp(alog_ref[...]) * jax.nn.softplus(g + dtb_ref[...])
    gb_ref[...] = jnp.where(lane < 2 * DN_HEADS, decay, jax.nn.sigmoid(g))


def _dn_prep(p, g, conv_w, a_log, dt_bias, rope):
    tm = 256
    nb = T_ALL // tm
    nlat = T_LAT // tm
    hb = 16
    nh = T_ALL // hb
    pad = lambda v: jnp.pad(v.reshape(1, 2 * DN_HEADS).astype(F32), ((0, 0), (0, GP - 2 * DN_HEADS)))
    half = DN_DH // 2
    rt_spec = pl.BlockSpec((3, None, tm // GRID_W, half), lambda b: (0, jnp.minimum(b, nlat - 1), 0, 0))
    ct_spec = pl.BlockSpec((3, GRID_W, half), lambda b: (0, 0, 0))
    return pl.pallas_call(
        functools.partial(_dn_prep_kernel, tm=tm),
        grid=(nb,),
        in_specs=[pl.BlockSpec((tm, 3 * DN_W), lambda b: (b, P_DN)),
                  pl.BlockSpec((hb, 3 * DN_W), lambda b: (jnp.maximum(b * (tm // hb) - 1, 0), P_DN)),
                  pl.BlockSpec((hb, 3 * DN_W), lambda b: (jnp.minimum((b + 1) * (tm // hb), nh - 1), P_DN)),
                  pl.BlockSpec((tm, GP), lambda b: (b, 0)),
                  pl.BlockSpec((DN_CONV, 3 * DN_W), lambda b: (0, 0)),
                  pl.BlockSpec((1, GP), lambda b: (0, 0)),
                  pl.BlockSpec((1, GP), lambda b: (0, 0)),
                  rt_spec, ct_spec],
        out_specs=[pl.BlockSpec((tm, 3 * DN_W), lambda b: (b, 0)),
                   pl.BlockSpec((tm, GP), lambda b: (b, 0))],
        out_shape=[jax.ShapeDtypeStruct((T_ALL, 3 * DN_W), BF16),
                   jax.ShapeDtypeStruct((T_ALL, GP), F32)],
        compiler_params=_cparams(("arbitrary",)),
        name="dn_prep",
    )(p, p, p, g, conv_w, pad(a_log), pad(dt_bias), *rope)


def _pair_blockdiag(y):
    c = y.shape[0]
    lane = lax.broadcasted_iota(jnp.int32, y.shape, 1)
    zero = jnp.zeros((), y.dtype)
    return jnp.concatenate([jnp.where(lane < c, y, zero), jnp.where(lane >= c, y, zero)], axis=0)


def _unit_tri_inverses_minus_identity(mats):
    c = mats[0].shape[0]
    ri = lax.broadcasted_iota(jnp.int32, (c, 2 * c), 0)
    ci = lax.broadcasted_iota(jnp.int32, (c, 2 * c), 1) % c

    def same_block(s):
        return (ri // s) == (ci // s)

    def mm(xs, ys):
        return [_dot(x.astype(BF16), _pair_blockdiag(y.astype(BF16))) for x, y in zip(xs, ys)]

    diag = same_block(8)
    b1 = [jnp.where(diag, -a, 0.0) for a in mats]
    b2 = mm(b1, b1)
    b3 = mm(b1, b2)
    b4 = mm(b2, b2)
    n2 = [x + y + z for x, y, z in zip(b1, b2, b3)]
    n = [x + y + z for x, y, z in zip(n2, b4, mm(n2, b4))]
    for s in (8, 16, 32):
        off = jnp.logical_and(same_block(2 * s), jnp.logical_not(same_block(s)))
        lo = [jnp.where(off, a, 0.0) for a in mats]
        x = [p + q for p, q in zip(lo, mm(n, lo))]
        n = [p - (q + r) for p, q, r in zip(n, x, mm(x, n))]
    return n


def _wy_pack_rows(nchunk):
    tm = nchunk * CHUNK
    return tm, 3 * tm, 3 * tm + 2 * 3 * CHUNK


def _wy_kq_slices(j, d_lane0, pair):
    _, kq0, _ = _wy_pack_rows(DN_BLK_CHUNKS)
    r = kq0 + (j % 2) * 3 * CHUNK
    l = d_lane0 + (j // 2) * 4 * CHUNK + pair * 2 * CHUNK
    return r, slice(l, l + 2 * CHUNK)


def _dn_wy_kernel(qkv_ref, gb_ref, r_ref, dl_ref, *, nchunk):
    c = CHUNK
    tm = nchunk * c
    gb = gb_ref[...]
    qkv = qkv_ref[...]
    rt = lax.broadcasted_iota(jnp.int32, (tm, tm), 0)
    ct = lax.broadcasted_iota(jnp.int32, (tm, tm), 1)
    same_chunk = (rt // c) == (ct // c)
    ri = lax.broadcasted_iota(jnp.int32, (c, 2 * c), 0)
    lane2 = lax.broadcasted_iota(jnp.int32, (c, 2 * c), 1)
    ci = lane2 % c
    first = lane2 < c
    lane = lax.broadcasted_iota(jnp.int32, (8, GP), 1)
    lane_k = lax.broadcasted_iota(jnp.int32, (c, 2 * DN_DH), 1)
    zk = jnp.zeros((), BF16)
    eye_p = (ri == ci).astype(F32)
    eye2 = (lax.broadcasted_iota(jnp.int32, (DN_DH, 2 * DN_DH), 1) % DN_DH
            == lax.broadcasted_iota(jnp.int32, (DN_DH, 2 * DN_DH), 0)).astype(BF16)
    gb_t = gb.T
    dl_rows, gcs = [], []
    for d in range(2):
        cum = jnp.logical_and(same_chunk, (ct <= rt) if d == 0 else (ct >= rt))
        gc_all = jnp.dot(cum.astype(F32), gb, preferred_element_type=F32, precision=lax.Precision.HIGHEST)
        gcs.append((gc_all, gc_all.T))
        dl_rows.append([jnp.exp(gc_all[j * c + (c - 1 if d == 0 else 0):j * c + (c if d == 0 else 1), :])
                        for j in range(nchunk)])
    for j0 in range(0, nchunk, WY_GROUP_CHUNKS):
        shared = {}
        for j in range(j0, j0 + WY_GROUP_CHUNKS):
            for pr in range(DN_HEADS // 2):
                r0 = j * c
                hs = slice(2 * pr * DN_DH, (2 * pr + 2) * DN_DH)
                bd = lambda x: jnp.concatenate([jnp.where(lane_k < DN_DH, x, zk), jnp.where(lane_k >= DN_DH, x, zk)],
                                               axis=0)
                q2 = qkv[r0:r0 + c, hs]
                k2 = qkv[r0:r0 + c, DN_W + hs.start:DN_W + hs.stop]
                v2 = qkv[r0:r0 + c, 2 * DN_W + hs.start:2 * DN_W + hs.stop]
                bd_k = bd(k2)
                shared[(j, pr)] = dict(q2=q2, bd_k=bd_k, bd_v=bd(v2),
                                       gqt=_dot_nt(jnp.concatenate([k2, q2, eye2], axis=0), bd_k))
        units = [(d, j, pr) for j in range(j0, j0 + WY_GROUP_CHUNKS) for d in range(2) for pr in range(DN_HEADS // 2)]
        pre = []
        for d, j, pr in units:
            gc_all, gc_all_t = gcs[d]
            incl = (ci <= ri) if d == 0 else (ci >= ri)
            strict = (ci < ri) if d == 0 else (ci > ri)
            r0 = j * c
            last = r0 + (c - 1 if d == 0 else 0)
            i0 = d * DN_HEADS + 2 * pr
            col = lambda x, a: x[r0:r0 + c, a:a + 1]
            row = lambda x, a: jnp.concatenate([x[a:a + 1, r0:r0 + c], x[a + 1:a + 2, r0:r0 + c]], axis=1)
            beta_p = jnp.where(first, col(gb, 2 * DN_HEADS + i0), col(gb, 2 * DN_HEADS + i0 + 1))
            gc_p = jnp.where(first, col(gc_all, i0), col(gc_all, i0 + 1))
            gc_row = row(gc_all_t, i0)
            beta_row = row(gb_t, 2 * DN_HEADS + i0)
            g_last = jnp.where(first[0:1], gc_all[last:last + 1, i0:i0 + 1], gc_all[last:last + 1, i0 + 1:i0 + 2])
            e = jnp.exp(jnp.where(incl, gc_p - gc_row, 0.0))
            eg_row = jnp.exp(gc_row)
            pre.append(dict(beta_p=beta_p, e_incl=jnp.where(incl, e, 0.0), e_strict=jnp.where(strict, e, 0.0),
                            beta_row=beta_row, eg_row=eg_row, tail_row=jnp.exp(g_last - gc_row)))
        gqt = [shared[(j, pr)]["gqt"] for d, j, pr in units]
        ns = _unit_tri_inverses_minus_identity([p["beta_p"] * x[:c] * p["e_strict"] for x, p in zip(gqt, pre)])
        tmat = {u_: n + eye_p for u_, n in zip(units, ns)}
        prd = dict(zip(units, pre))
        both = [(j, pr) for j in range(j0, j0 + WY_GROUP_CHUNKS) for pr in range(DN_HEADS // 2)]
        lhs_u = [jnp.concatenate([(tmat[(d, j, pr)] * prd[(d, j, pr)]["beta_row"]).astype(BF16) for d in range(2)], axis=0)
                 for j, pr in both]
        lhs_w = [jnp.concatenate([(tmat[(d, j, pr)] * (prd[(d, j, pr)]["beta_row"] * prd[(d, j, pr)]["eg_row"])
                                   ).astype(BF16) for d in range(2)], axis=0) for j, pr in both]
        u2 = dict(zip(both, [_dot(a, shared[jp]["bd_v"]) for a, jp in zip(lhs_u, both)]))
        w2 = dict(zip(both, [_dot(a, shared[jp]["bd_k"]) for a, jp in zip(lhs_w, both)]))
        for (d, j, pr), p, x in zip(units, pre, gqt):
            r0 = j * c
            i0 = d * DN_HEADS + 2 * pr
            cs = slice(i0 * DN_DH, (i0 + 2) * DN_DH)
            ks = slice(i0 * c, (i0 + 2) * c)
            q2 = shared[(j, pr)]["q2"].astype(F32)
            gcol = gcs[d][0][r0:r0 + c]
            eg = jnp.where(lane_k < DN_DH, jnp.exp(gcol[:, i0:i0 + 1]), jnp.exp(gcol[:, i0 + 1:i0 + 2]))
            wq0 = _wy_pack_rows(nchunk)[0]
            kr, kl = _wy_kq_slices(j, d * DN_W, pr)
            r_ref[r0:r0 + c, cs] = u2[(j, pr)][d * c:(d + 1) * c].astype(BF16)
            r_ref[wq0 + 2 * r0:wq0 + 2 * r0 + c, cs] = w2[(j, pr)][d * c:(d + 1) * c].astype(BF16)
            r_ref[wq0 + 2 * r0 + c:wq0 + 2 * r0 + 2 * c, cs] = (q2 * eg).astype(BF16)
            r_ref[kr:kr + DN_DH, kl] = (x[2 * c:] * p["tail_row"]).astype(BF16)
            r_ref[kr + DN_DH:kr + DN_DH + c, kl] = (x[c:2 * c] * p["e_incl"]).astype(BF16)
    for j in range(nchunk):
        row = jnp.where(lane < DN_HEADS, dl_rows[0][j], dl_rows[1][j])
        dl_ref[8 * j:8 * j + 8, :] = row


def _dn_wy(qkv, gb, nchunk):
    tm = nchunk * CHUNK
    nb = T_ALL // tm
    nch = 2 * DN_HEADS
    return pl.pallas_call(
        functools.partial(_dn_wy_kernel, nchunk=nchunk),
        grid=(nb,),
        in_specs=[pl.BlockSpec((tm, 3 * DN_W), lambda i: (i, 0)),
                  pl.BlockSpec((tm, GP), lambda i: (i, 0))],
        out_specs=[pl.BlockSpec((_wy_pack_rows(nchunk)[2], nch * DN_DH), lambda i: (i, 0)),
                   pl.BlockSpec((8 * nchunk, GP), lambda i: (i, 0))],
        out_shape=[jax.ShapeDtypeStruct((nb * _wy_pack_rows(nchunk)[2], nch * DN_DH), BF16),
                   jax.ShapeDtypeStruct((8 * T_ALL // CHUNK, GP), F32)],
        compiler_params=_cparams(("arbitrary",)),
        name="dn_wy",
    )(qkv, gb)


def _dn_scan_kernel(r_f_ref, dl_f_ref, r_b_ref, dl_b_ref, of_ref, ob_ref, s_scr, *, nchunk):
    i = pl.program_id(0)

    @pl.when(i == 0)
    def _():
        s_scr[...] = jnp.zeros_like(s_scr)

    c = CHUNK
    states = [s_scr[idx] for idx in range(2 * DN_HEADS)]
    zero = jnp.zeros((c, DN_DH), BF16)
    tm = nchunk * c
    wq0 = _wy_pack_rows(nchunk)[0]
    refs = tuple((r.at[0:tm, :], r.at[wq0:wq0 + 2 * tm, :], r, dl, o)
                 for r, dl, o in ((r_f_ref, dl_f_ref, of_ref), (r_b_ref, dl_b_ref, ob_ref)))

    def kq(d, p):
        kr, kl = _wy_kq_slices(blk(d), 0, p)
        return refs[d][2][kr:kr + 3 * c, kl]

    chains = [(d, h) for d in range(2) for h in range(DN_HEADS)]
    pairs = [(d, p) for d in range(2) for p in range(DN_HEADS // 2)]
    for step in range(nchunk):
        blk = lambda d: step if d == 0 else nchunk - 1 - step
        hs = lambda h: slice(h * DN_DH, (h + 1) * DN_DH)
        r1 = [_dot(refs[d][1][2 * blk(d) * c:2 * (blk(d) + 1) * c, hs(h)], states[d * DN_HEADS + h].astype(BF16))
              for d, h in chains]
        v_new = [(refs[d][0][blk(d) * c:(blk(d) + 1) * c, hs(h)].astype(F32) - r[:c]).astype(BF16)
                 for (d, h), r in zip(chains, r1)]
        r2 = [_dot(kq(d, p),
                   jnp.concatenate([jnp.concatenate([v_new[d * DN_HEADS + 2 * p], zero], axis=1),
                                    jnp.concatenate([zero, v_new[d * DN_HEADS + 2 * p + 1]], axis=1)], axis=0))
              for d, p in pairs]
        for d, h in chains:
            idx = d * DN_HEADS + h
            r = r2[d * (DN_HEADS // 2) + h // 2]
            ts = slice((h % 2) * DN_DH, (h % 2 + 1) * DN_DH)
            dl_row = refs[d][3][8 * blk(d):8 * blk(d) + 1, :]
            states[idx] = states[idx] * dl_row[:, idx:idx + 1] + r[:DN_DH, ts]
            refs[d][4][blk(d) * c:(blk(d) + 1) * c, hs(h)] = (r1[idx][c:] + r[DN_DH:, ts]).astype(BF16)
    for idx in range(2 * DN_HEADS):
        s_scr[idx] = states[idx]


def _dn_scan(r, dl, nchunk):
    tm = nchunk * CHUNK
    nb = T_ALL // tm
    nlat = T_LAT // tm
    fwd = lambda i: jnp.where(i == 0, nlat, i - 1)
    bwd = lambda i: nb - 1 - i
    specs = lambda f, col: [pl.BlockSpec((_wy_pack_rows(nchunk)[2], DN_W), lambda i: (f(i), col)),
                            pl.BlockSpec((8 * nchunk, GP), lambda i: (f(i), 0))]
    return pl.pallas_call(
        functools.partial(_dn_scan_kernel, nchunk=nchunk),
        grid=(nb,),
        in_specs=specs(fwd, 0) + specs(bwd, 1),
        out_specs=[pl.BlockSpec((tm, DN_W), lambda i: (fwd(i), 0)),
                   pl.BlockSpec((tm, DN_W), lambda i: (bwd(i), 0))],
        out_shape=[jax.ShapeDtypeStruct((T_ALL, DN_W), BF16),
                   jax.ShapeDtypeStruct((T_ALL, DN_W), BF16)],
        scratch_shapes=[pltpu.VMEM((2 * DN_HEADS, DN_DH, DN_DH), F32)],
        compiler_params=_cparams(("arbitrary",)),
        name="dn_scan",
    )(r, dl, r, dl)


def _dft_cos_sin(n):
    k = np.arange(n)
    ang = 2.0 * np.pi * ((k[:, None] * k[None, :]) % n) / n
    return np.cos(ang), np.sin(ang)


def _fft_consts():
    n = FFT_N
    c, s = _dft_cos_sin(n)
    cs_ch = np.concatenate([c, s], axis=1)
    w1 = np.block([[c, -s], [-s, -c]])
    k2 = np.arange(n)[None, :, None]
    t2 = np.arange(n)[:, None, None]
    ang = 2.0 * np.pi * ((k2 * t2) % (n * n)) / (n * n)
    scale = 1.0 / math.sqrt(T_LAT * FN_DG)
    cc, sc = _dft_cos_sin(T_CTX)
    scale_c = 1.0 / math.sqrt(T_CTX * FN_DG)
    f32 = lambda a: jnp.asarray(a, F32)
    bf = lambda a: f32(a).astype(BF16)
    return dict(cs_ch=bf(cs_ch), w1=bf(w1), twc=f32(np.cos(ang)), tws=f32(np.sin(ang)),
                c2=bf(c * scale), s2=bf(s * scale), cc=bf(cc * scale_c), sc=bf(sc * scale_c))


def _fft1_kernel(u_ref, cs_ref, w1_ref, twc_ref, tws_ref, y_ref, *, n_t2):
    n = FFT_N
    cs = cs_ref[...]
    w1 = w1_ref[...]
    for t in range(n_t2):
        twc = twc_ref[t]
        tws = tws_ref[t]
        for g in range(0, FN_GROUPS, 2):
            c0 = (t * FN_GROUPS + g) * FN_DG
            ab = [_dot(u_ref[:, c0 + s * FN_DG:c0 + (s + 1) * FN_DG], cs) for s in range(2)]
            rhs = jnp.concatenate([jnp.concatenate([ab[0][:, :FN_DG], ab[1][:, :FN_DG]], axis=1),
                                   jnp.concatenate([ab[0][:, FN_DG:], ab[1][:, FN_DG:]], axis=1)], axis=0)
            y = _dot(w1, rhs.astype(BF16))
            yr, yi = y[:n], y[n:]
            y_ref[0:n, c0:c0 + 2 * FN_DG] = (yr * twc + yi * tws).astype(BF16)
            y_ref[n:2 * n, c0:c0 + 2 * FN_DG] = (yi * twc - yr * tws).astype(BF16)


def _fft2_kernel(y_ref, c2_ref, s2_ref, o_ref, *, n_k2):
    c2 = c2_ref[...]
    s2 = s2_ref[...]
    for j in range(n_k2):
        o_ref[:, j * FN_W:(j + 1) * FN_W] = (_dot(c2, y_ref[0, j]) + _dot(s2, y_ref[1, j])).astype(BF16)


def _fft_ctx_kernel(u_ref, cs_ref, cc_ref, sc_ref, o_in_ref, o_ref):
    del o_in_ref
    for g in range(FN_GROUPS):
        ab = _dot(u_ref[:, g * FN_DG:(g + 1) * FN_DG], cs_ref[...])
        a = ab[:, :FN_DG].astype(BF16)
        b = ab[:, FN_DG:].astype(BF16)
        o_ref[:, g * FN_DG:(g + 1) * FN_DG] = (_dot(cc_ref[...], a) - _dot(sc_ref[...], b)).astype(BF16)


def _fnet(u, fc, with_ctx):
    n = FFT_N
    row_w = n * FN_W
    n_t2 = 8
    tc = n_t2 * FN_W
    y = pl.pallas_call(
        functools.partial(_fft1_kernel, n_t2=n_t2),
        grid=(row_w // tc,),
        in_specs=[pl.BlockSpec((n, tc), lambda j: (0, j)),
                  pl.BlockSpec((n, 2 * n), lambda j: (0, 0)),
                  pl.BlockSpec((2 * n, 2 * n), lambda j: (0, 0)),
                  pl.BlockSpec((n_t2, n, 1), lambda j: (j, 0, 0)),
                  pl.BlockSpec((n_t2, n, 1), lambda j: (j, 0, 0))],
        out_specs=pl.BlockSpec((2 * n, tc), lambda j: (0, j)),
        out_shape=jax.ShapeDtypeStruct((2 * n, row_w), BF16),
        compiler_params=_cparams(("arbitrary",)),
        name="fnet_stage1",
    )(u.reshape(T_ALL // n, row_w), fc["cs_ch"], fc["w1"], fc["twc"], fc["tws"])
    n_k2 = 8
    o = pl.pallas_call(
        functools.partial(_fft2_kernel, n_k2=n_k2),
        grid=(n // n_k2,),
        in_specs=[pl.BlockSpec((2, n_k2, n, FN_W), lambda j: (0, j, 0, 0)),
                  pl.BlockSpec((n, n), lambda j: (0, 0)),
                  pl.BlockSpec((n, n), lambda j: (0, 0))],
        out_specs=pl.BlockSpec((n, n_k2 * FN_W), lambda j: (0, j)),
        out_shape=jax.ShapeDtypeStruct(((T_ALL if with_ctx else T_LAT) // n, row_w), BF16),
        compiler_params=_cparams(("arbitrary",)),
        name="fnet_stage2",
    )(y.reshape(2, n, n, FN_W), fc["c2"], fc["s2"])
    if not with_ctx:
        return o.reshape(T_LAT, FN_W)
    o = o.reshape(T_ALL, FN_W)
    cb = T_LAT // T_CTX
    return pl.pallas_call(
        _fft_ctx_kernel,
        grid=(1,),
        in_specs=[pl.BlockSpec((T_CTX, FN_W), lambda j: (cb, 0)),
                  pl.BlockSpec((n, 2 * n), lambda j: (0, 0)),
                  pl.BlockSpec((T_CTX, T_CTX), lambda j: (0, 0)),
                  pl.BlockSpec((T_CTX, T_CTX), lambda j: (0, 0)),
                  pl.BlockSpec(memory_space=pl.ANY)],
        out_specs=pl.BlockSpec((T_CTX, FN_W), lambda j: (cb, 0)),
        out_shape=jax.ShapeDtypeStruct((T_ALL, FN_W), BF16),
        input_output_aliases={4: 0},
        compiler_params=_cparams(("arbitrary",)),
        name="fnet_ctx",
    )(u, fc["cs_ch"], fc["cc"], fc["sc"], o)


def _merge_kernel(*refs, tm, has_ctx, split):
    if split:
        (x_ref, ctx_ref, mod_ref, n1_ref, ona_ref, of_ref, ob_ref, z_ref, ofn_ref, dnn_ref,
         wg_ref, wna_ref, wdn_ref, wfn_ref, wout_ref, o_ref, tail_scr) = refs
    else:
        (x_ref, mod_ref, n1_ref, ona_ref, of_ref, ob_ref, z_ref, ofn_ref, dnn_ref,
         wg_ref, wna_ref, wdn_ref, wfn_ref, wout_ref, o_ref) = refs
        ctx_ref = tail_scr = None
    i = pl.program_id(0)
    tail_is_ctx = i == pl.num_programs(0) - 1
    x = _token_tile(x_ref, ctx_ref, tail_scr, tail_is_ctx)
    h = _modnorm_tile(x, n1_ref[...], mod_ref, 0, 1, tail_is_ctx, has_ctx).astype(BF16)
    o = of_ref[...].astype(F32) + ob_ref[...].astype(F32)
    z = z_ref[...].astype(F32)
    parts = []
    for hd in range(DN_HEADS):
        sl = slice(hd * DN_DH, (hd + 1) * DN_DH)
        oh = o[:, sl]
        oh = oh * lax.rsqrt(jnp.mean(oh * oh, axis=-1, keepdims=True) + EPS) * dnn_ref[...]
        parts.append(oh * _silu(z[:, sl]))
    odn = jnp.concatenate(parts, axis=-1).astype(BF16)
    branches = ((ona_ref[...], wna_ref), (odn, wdn_ref), (ofn_ref[...], wfn_ref))
    logits = [_dot(h, wg_ref[:, b * D:(b + 1) * D]) for b in range(N_BRANCH)]
    proj = [_dot(br, w_ref[...]) for br, w_ref in branches]
    y = jax.nn.sigmoid(logits[0]) * proj[0]
    for b in range(1, N_BRANCH):
        y = y + jax.nn.sigmoid(logits[b]) * proj[b]
    y = _dot(y.astype(BF16), wout_ref[...])
    o_ref[...] = _gated_residual(x, y, mod_ref, 2, tail_is_ctx, has_ctx)


def _merge(x, ctx, mod, norm1, p, o_na, o_f, o_b, o_fn, dn_norm, w_gate, layer, w_na_o, w_dn_o, w_fn, w_out, has_ctx):
    rows, tm = (T_ALL, TM_MERGE_ALL) if has_ctx else (T_LAT, TM_MERGE_LAT)
    split = ctx is not None
    row = lambda w: pl.BlockSpec((tm, w), lambda i: (i, 0))
    full = lambda a: pl.BlockSpec(a.shape, lambda i: (0, 0), pipeline_mode=pl.Buffered(1))
    ctx_specs = [pl.BlockSpec((T_CTX, D), lambda i: (0, 0))] if split else []
    return pl.pallas_call(
        functools.partial(_merge_kernel, tm=tm, has_ctx=has_ctx, split=split),
        grid=(rows // tm,),
        in_specs=[row(D)] + ctx_specs + [
                  pl.BlockSpec((8, 6 * D), lambda i: (0, 0)), pl.BlockSpec((1, D), lambda i: (0, 0)),
                  row(NA_W), row(DN_W), row(DN_W),
                  pl.BlockSpec((tm, DN_W), lambda i: (i, P_Z)),
                  row(FN_W), pl.BlockSpec((1, DN_DH), lambda i: (0, 0)),
                  pl.BlockSpec((None,) + w_gate.shape[1:], lambda i: (layer, 0, 0), pipeline_mode=pl.Buffered(1)),
                  full(w_na_o), full(w_dn_o), full(w_fn), full(w_out)],
        out_specs=row(D),
        out_shape=jax.ShapeDtypeStruct((rows, D), F32),
        scratch_shapes=[pltpu.VMEM((T_CTX, D), F32)] if split else [],
        compiler_params=_cparams(("arbitrary",)),
        name="merge",
    )(*([x, ctx] if split else [x]), mod, norm1.reshape(1, D), o_na, o_f, o_b, p, o_fn, dn_norm.reshape(1, DN_DH),
      w_gate, w_na_o, w_dn_o, w_fn, w_out)


def _mlp_kernel(x_ref, xnext_ref, mod_ref, n_ref, w1_ref, w2_ref, nf_ref, o_ref, h_scr, acc_scr, *,
                tm, has_ctx, final):
    i = pl.program_id(0)
    j = pl.program_id(1)
    nt = pl.num_programs(0)
    nj = pl.num_programs(1)
    slot = i % 2

    @pl.when(jnp.logical_and(i == 0, j == 0))
    def _():
        h_scr[0] = _modnorm_tile(x_ref[...], n_ref[...], mod_ref, 3, 4, nt == 1, has_ctx).astype(BF16)

    def hidden_step(first):
        a = jnp.maximum(_dot(h_scr[slot], w1_ref[...]), 0.0)
        upd = _dot((a * a).astype(BF16), w2_ref[...])
        if first:
            acc_scr[...] = upd
        else:
            acc_scr[...] += upd

    @pl.when(j == 0)
    def _():
        hidden_step(True)

    @pl.when(jnp.logical_and(j > 0, j < nj - 1))
    def _():
        hidden_step(False)

    @pl.when(j == nj - 1)
    def _():
        h_scr[1 - slot] = _modnorm_tile(xnext_ref[...], n_ref[...], mod_ref, 3, 4, i + 1 == nt - 1, has_ctx).astype(BF16)
        hidden_step(False)
        xn = _gated_residual(x_ref[...], acc_scr[...], mod_ref, 5, i == nt - 1, has_ctx)
        if final:
            xn = xn * lax.rsqrt(jnp.mean(xn * xn, axis=-1, keepdims=True) + EPS) * nf_ref[...]
        o_ref[...] = xn


def _mlp(xs, mod, norm, w1, w2, norm_f, has_ctx, final):
    rows, tm = (T_ALL, TM_ALL) if has_ctx else (T_LAT, TM_LAT)
    th = 1024
    return pl.pallas_call(
        functools.partial(_mlp_kernel, tm=tm, has_ctx=has_ctx, final=final),
        grid=(rows // tm, HID // th),
        in_specs=[pl.BlockSpec((tm, D), lambda i, j: (i, 0)),
                  pl.BlockSpec((tm, D), lambda i, j: (jnp.minimum(i + 1, rows // tm - 1), 0)),
                  pl.BlockSpec((8, 6 * D), lambda i, j: (0, 0)),
                  pl.BlockSpec((1, D), lambda i, j: (0, 0)),
                  pl.BlockSpec((D, th), lambda i, j: (0, j)),
                  pl.BlockSpec((th, D), lambda i, j: (j, 0)),
                  pl.BlockSpec((1, D), lambda i, j: (0, 0))],
        out_specs=pl.BlockSpec((tm, D), lambda i, j: (i, 0)),
        out_shape=jax.ShapeDtypeStruct((rows, D), F32),
        scratch_shapes=[pltpu.VMEM((2, tm, D), BF16), pltpu.VMEM((tm, D), F32)],
        compiler_params=_cparams(("arbitrary", "arbitrary")),
        name="mlp",
    )(xs, xs, mod, norm.reshape(1, D), w1, w2, norm_f.reshape(1, D))


IN_W = 2 * NA_W + 2 * DN_W + 4 * DN_HEADS + NA_W + 2 * DN_W + FN_W + N_BRANCH * D


W_PREP_BLK = 512
_W_SRC = [2 * NA_W + 2 * DN_W + 4 * DN_HEADS, 0, NA_W, 3 * NA_W + 2 * DN_W + 4 * DN_HEADS, 2 * NA_W, 2 * NA_W + DN_W,
          3 * NA_W + 3 * DN_W + 4 * DN_HEADS, 3 * NA_W + 4 * DN_W + 4 * DN_HEADS] + [
          IN_W - N_BRANCH * D + i * W_PREP_BLK for i in range(N_BRANCH * D // W_PREP_BLK)]
_W_AB = 2 * NA_W + 2 * DN_W


def _w_prep_kernel(w_ref, ab_ref, wcat_ref, wgate_ref, wg_ref):
    c = pl.program_id(1)
    ncat = P_W // W_PREP_BLK
    blk = w_ref[0].T.astype(BF16)

    @pl.when(c < ncat)
    def _():
        wcat_ref[...] = blk

    @pl.when(c >= ncat)
    def _():
        wgate_ref[...] = blk

    @pl.when(c == 0)
    def _():
        ab = jnp.concatenate([ab_ref[0], jnp.zeros((GP - ab_ref.shape[1], D), F32)], axis=0)
        wg_ref[...] = ab.T.astype(BF16)


def _split_in_weights(w_in):
    ncat = P_W // W_PREP_BLK
    nblk = len(_W_SRC)

    def src(c):
        off = 0
        for k, o in enumerate(_W_SRC):
            off = off + jnp.where(c == k, o // 16, 0)
        return off * 16

    return pl.pallas_call(
        _w_prep_kernel,
        grid=(DEPTH, nblk),
        in_specs=[pl.BlockSpec((pl.Element(1), pl.Element(W_PREP_BLK), pl.Element(D)), lambda l, c: (l, src(c), 0)),
                  pl.BlockSpec((pl.Element(1), pl.Element(4 * DN_HEADS), pl.Element(D)), lambda l, c: (l, _W_AB, 0))],
        out_specs=[pl.BlockSpec((None, D, W_PREP_BLK), lambda l, c: (l, 0, jnp.minimum(c, ncat - 1))),
                   pl.BlockSpec((None, D, W_PREP_BLK), lambda l, c: (l, 0, jnp.maximum(c - ncat, 0))),
                   pl.BlockSpec((None, D, GP), lambda l, c: (l, 0, 0))],
        out_shape=[jax.ShapeDtypeStruct((DEPTH, D, P_W), BF16),
                   jax.ShapeDtypeStruct((DEPTH, D, N_BRANCH * D), BF16),
                   jax.ShapeDtypeStruct((DEPTH, D, GP), BF16)],
        compiler_params=_cparams(("arbitrary", "arbitrary")),
        name="w_prep",
    )(jnp.swapaxes(w_in, 1, 2), jnp.swapaxes(w_in, 1, 2))


def kernel(x, c, ctx, c_ctx, w_ada, b_ada, norm1, w_in, conv_w, a_log, dt_bias, dn_norm, rpb,
           w_na_o, w_dn_o, w_fn, w_out, norm2, w_mlp1, w_mlp2, norm_f):
    xs, xc = x[0], ctx[0]
    cc = jnp.concatenate([c, c_ctx[None, :], jnp.zeros((6, D), F32)], axis=0)
    mods = _ada(cc, w_ada, b_ada)
    rope = _rope_tables()
    fc = _fft_consts()
    na_tab = _na_tables(rpb)
    wcat, w_gate, wg = _split_in_weights(w_in)
    for l in range(DEPTH):
        has_ctx = l < DEPTH - 1
        final = l == DEPTH - 1
        u, p, g = _in_proj(xs, xc, mods[l], norm1[l], wcat, wg, l)
        o_na = _na(p, na_tab, l, has_ctx)
        qkv, gb = _dn_prep(p, g, conv_w[l], a_log[l], dt_bias[l], rope)
        o_f, o_b = _dn_scan(*_dn_wy(qkv, gb, DN_BLK_CHUNKS), DN_BLK_CHUNKS)
        o_fn = _fnet(u, fc, has_ctx)
        xs, xc = _merge(xs, xc, mods[l], norm1[l], p, o_na, o_f, o_b, o_fn, dn_norm[l], w_gate, l,
                    w_na_o[l].astype(BF16), w_dn_o[l].astype(BF16), w_fn[l].astype(BF16), w_out[l].astype(BF16),
                    has_ctx), None
        xs = _mlp(xs, mods[l], norm2[l], w_mlp1[l].astype(BF16), w_mlp2[l].astype(BF16), norm_f, has_ctx, final)
    return xs[None]
```
